```python
import jax, jax.numpy as jnp
from jax import lax
import numpy as np

D_MODEL = 1024
BATCH = 4
SEQ = 8192
DEPTH = 4

GRID_W = 64
CTX_LEN = 256
CHUNK = 128
ROWS_PER_CHUNK = CHUNK // GRID_W
A_HEADS = 4
A_HEAD_DIM = 64
A_WIDTH = A_HEADS * A_HEAD_DIM
B_HEADS = 4
B_DK = 64
B_DV = 128
B_QK = B_HEADS * B_DK
B_WIDTH = B_HEADS * B_DV
GATE_RANK = 16
GATE_TAU = 16.0
GLA_CHUNK = 64
S5_GROUPS = 16
S5_GROUP_CH = 16
S5_STATE = 64
C_WIDTH = S5_GROUPS * S5_GROUP_CH
MIX_WIDTH = A_WIDTH + B_WIDTH + C_WIDTH
IN_SPLITS = (A_WIDTH, A_WIDTH, B_QK, B_QK, B_WIDTH, B_WIDTH, 2 * GATE_RANK, C_WIDTH)
N_IN = sum(IN_SPLITS)
N_EXPERTS = 32
TOP_K = 4
D_EXPERT = 1024
SWIGLU_LIMIT = 7.0
SWIGLU_ALPHA = 1.702
MOE_BLOCK = 128
DN_ALPHA = (2 * DEPTH) ** 0.25
DN_BETA = (8 * DEPTH) ** -0.25
LN_EPS = 1e-5

kernel_name = 'hybrid_gmlp_gla_s5_moe_deepnorm_dit'

F32 = jnp.float32


def layer_norm(x, g, b):
    xf = x.astype(F32)
    mu = xf.mean(-1, keepdims=True)
    var = jnp.square(xf - mu).mean(-1, keepdims=True)
    return ((xf - mu) * lax.rsqrt(var + LN_EPS) * g + b).astype(x.dtype)


def chunk_spatial_gate(u, v, n_chunks, ln_g, ln_b, w_s, b_s):
    bsz, L, _ = u.shape
    v = layer_norm(v, ln_g, ln_b).reshape(bsz, n_chunks, CHUNK, A_HEADS, A_HEAD_DIM)
    mixed = jnp.einsum('hts,bnshc->bnthc', w_s, v) + b_s.T[:, :, None]
    return u * mixed.reshape(bsz, L, A_WIDTH)


def gla_direction(q, k, v, log_a, s0, strict, need_out):
    bsz, nh, L, dk = q.shape
    dv = v.shape[-1]
    n = L // GLA_CHUNK
    shp = (bsz, nh, n, GLA_CHUNK)
    q = q.reshape(*shp, dk)
    k = k.reshape(*shp, dk)
    v = v.reshape(*shp, dv)
    b = jnp.cumsum(log_a.reshape(*shp, dk).astype(F32), axis=3)
    b_last = b[:, :, :, -1:, :]
    ds = jnp.einsum('bhncd,bhnce->bhnde', k * jnp.exp(b_last - b), v)
    decay = jnp.exp(b_last[:, :, :, 0, :])

    def step(s, inp):
        ds_n, dec_n = inp
        return dec_n[..., None] * s + ds_n, s

    s_final, s_prev = lax.scan(step, s0, (jnp.moveaxis(ds, 2, 0), jnp.moveaxis(decay, 2, 0)))
    if not need_out:
        return None, s_final
    s_prev = jnp.moveaxis(s_prev, 0, 2)
    b_mid = b[:, :, :, GLA_CHUNK // 2:GLA_CHUNK // 2 + 1, :]
    scores = jnp.einsum('bhncd,bhnsd->bhncs', q * jnp.exp(b - b_mid), k * jnp.exp(b_mid - b))
    mask = jnp.tril(jnp.ones((GLA_CHUNK, GLA_CHUNK), bool), -1 if strict else 0)
    o = (jnp.einsum('bhncs,bhnse->bhnce', jnp.where(mask, scores, 0.0), v)
         + jnp.einsum('bhncd,bhnde->bhnce', q * jnp.exp(b), s_prev))
    return o.reshape(bsz, nh, L, dv), s_final


def gla_mixer(q, k, v, g, gl, gate_up, gate_b, norm_g, s0, need_out):
    bsz, L, _ = q.shape

    def heads(t, d):
        return t.reshape(bsz, L, B_HEADS, d).transpose(0, 2, 1, 3)

    def flip(t):
        return jnp.flip(t, axis=2)

    qh = heads(q, B_DK) * B_DK ** -0.5
    kh = heads(k, B_DK)
    vh = heads(v, B_DV)
    gl = gl.reshape(bsz, L, 2, GATE_RANK).astype(F32)
    log_a = jax.nn.log_sigmoid(jnp.einsum('blzr,zrk->zblk', gl, gate_up.astype(F32))
                               + gate_b.astype(F32)[:, None, None, :]) / GATE_TAU
    o_f, s_f = gla_direction(qh, kh, vh, heads(log_a[0], B_DK), s0[0], False, need_out)
    o_b, s_b = gla_direction(flip(qh), flip(kh), flip(vh), flip(heads(log_a[1], B_DK)), s0[1], True, need_out)
    states = jnp.stack([s_f, s_b])
    if not need_out:
        return None, states
    o = o_f + flip(o_b)
    o = o * lax.rsqrt(jnp.mean(jnp.square(o), -1, keepdims=True) + LN_EPS)
    o = o.transpose(0, 2, 1, 3).reshape(bsz, L, B_WIDTH) * norm_g
    return (o * jax.nn.silu(g)).astype(q.dtype), states


def _linear_combine(e1, e2):
    a1, b1 = e1
    a2, b2 = e2
    return a1 * a2, a2 * b1 + b2


def s5_direction(u, lam, dt, b_mat, c_mat, h0, reverse, need_out):
    lam_bar = jnp.exp(lam * dt[:, None])
    b_bar = ((lam_bar - 1.0) / lam)[..., None] * b_mat
    bu = jnp.einsum('blgh,gph->blgp', u, b_bar)
    edge = -1 if reverse else 0
    bu = bu.at[:, edge].add(lam_bar * h0)
    a = jnp.broadcast_to(lam_bar, bu.shape)
    _, h = lax.associative_scan(_linear_combine, (a, bu), axis=1, reverse=reverse)
    h_final = h[:, 0] if reverse else h[:, -1]
    if not need_out:
        return None, h_final
    return jnp.real(jnp.einsum('blgp,ghp->blgh', h, c_mat)), h_final


def s5_mixer(u, lam_re, lam_im, log_dt, b_re, b_im, c_re, c_im, d_skip, glu_w, glu_b, h0, need_out):
    bsz, L, _ = u.shape
    uf = u.astype(F32).reshape(bsz, L, S5_GROUPS, S5_GROUP_CH)

    def cplx(re, im):
        return lax.complex(re.astype(F32), im.astype(F32))

    lam = cplx(lam_re, lam_im)
    bm = cplx(b_re, b_im)
    cm = cplx(c_re, c_im)
    dt = jnp.exp(log_dt.astype(F32))
    y_f, h_f = s5_direction(uf, lam[0], dt[0], bm[0], cm[0], h0[0], False, need_out)
    y_b, h_b = s5_direction(uf, lam[1], dt[1], bm[1], cm[1], h0[1], True, need_out)
    states = jnp.stack([h_f, h_b])
    if not need_out:
        return None, states
    y = jax.nn.gelu((y_f + y_b + d_skip.astype(F32) * uf).reshape(bsz, L, C_WIDTH))
    return (y * jax.nn.sigmoid(y @ glu_w.astype(F32) + glu_b.astype(F32))).astype(u.dtype), states


def token_mixer(h, n_chunks, w_in, sgu_ln_g, sgu_ln_b, sgu_w, sgu_b, gla_gate_up, gla_gate_b, gla_norm_g,
                s5_lam_re, s5_lam_im, s5_log_dt, s5_b_re, s5_b_im, s5_c_re, s5_c_im, s5_d, s5_glu_w, s5_glu_b,
                gla_s0, s5_h0, need_out):
    proj = h @ w_in
    offsets = [int(o) for o in np.cumsum(IN_SPLITS)[:-1]]
    a_u, a_v, q, k, v, g, gl, s5_u = jnp.split(proj, offsets, axis=-1)
    gla_out, gla_state = gla_mixer(q, k, v, g, gl, gla_gate_up, gla_gate_b, gla_norm_g, gla_s0, need_out)
    s5_out, s5_state = s5_mixer(s5_u, s5_lam_re, s5_lam_im, s5_log_dt, s5_b_re, s5_b_im, s5_c_re, s5_c_im,
                                s5_d, s5_glu_w, s5_glu_b, s5_h0, need_out)
    if not need_out:
        return None, gla_state, s5_state
    a_out = chunk_spatial_gate(jax.nn.gelu(a_u), jax.nn.gelu(a_v), n_chunks, sgu_ln_g, sgu_ln_b, sgu_w, sgu_b)
    mixed = jnp.concatenate([a_out, gla_out.astype(a_out.dtype), s5_out.astype(a_out.dtype)], axis=-1)
    return mixed, gla_state, s5_state


def moe(h, router_w, router_b, w_up, b_up, w_down, b_down):
    n_tok, d = h.shape
    logits = h.astype(F32) @ router_w.astype(F32) + router_b.astype(F32)
    top_logit, top_idx = lax.top_k(logits, TOP_K)
    gates = jax.nn.softmax(top_logit, axis=-1)
    n_assign = n_tok * TOP_K
    flat_e = top_idx.reshape(-1).astype(jnp.int32)
    order = jnp.argsort(flat_e)
    sorted_e = flat_e[order]
    counts = jnp.bincount(flat_e, length=N_EXPERTS).astype(jnp.int32)
    padded = (counts + MOE_BLOCK - 1) // MOE_BLOCK * MOE_BLOCK
    padded_end = jnp.cumsum(padded)
    group_start = jnp.cumsum(counts) - counts
    dest = (padded_end - padded)[sorted_e] + jnp.arange(n_assign, dtype=jnp.int32) - group_start[sorted_e]
    n_blocks = (n_assign + N_EXPERTS * (MOE_BLOCK - 1) + MOE_BLOCK - 1) // MOE_BLOCK
    slot_token = jnp.zeros(n_blocks * MOE_BLOCK, jnp.int32).at[dest].set((order // TOP_K).astype(jnp.int32))
    block_expert = jnp.minimum(
        jnp.searchsorted(padded_end, jnp.arange(n_blocks, dtype=jnp.int32) * MOE_BLOCK, side='right'),
        N_EXPERTS - 1)

    def expert_block(args):
        tok, e = args
        up = h[tok] @ w_up[e] + b_up[e]
        x_glu = jnp.minimum(up[:, :D_EXPERT], SWIGLU_LIMIT)
        x_lin = jnp.clip(up[:, D_EXPERT:], -SWIGLU_LIMIT, SWIGLU_LIMIT)
        act = x_glu * jax.nn.sigmoid(SWIGLU_ALPHA * x_glu) * (x_lin + 1.0)
        return act @ w_down[e] + b_down[e]

    ys = lax.map(expert_block, (slot_token.reshape(n_blocks, MOE_BLOCK), block_expert))
    dest_of_assign = jnp.zeros(n_assign, jnp.int32).at[order].set(dest)
    y = ys.reshape(-1, d)[dest_of_assign].reshape(n_tok, TOP_K, d)
    return jnp.einsum('tk,tkd->td', gates.astype(y.dtype), y)


def setup_inputs(seed: int = 0) -> dict:
    key = jax.random.key(seed)
    ks = jax.random.split(key, 40)

    def nrm(k, shape, scale):
        return jax.random.normal(k, shape, F32) * scale

    n_idx = jnp.arange(S5_STATE, dtype=F32)
    return {
        'x': nrm(ks[0], (BATCH, SEQ, D_MODEL), 1.0),
        'c': nrm(ks[1], (BATCH, D_MODEL), 1.0),
        'ctx': nrm(ks[2], (BATCH, CTX_LEN, D_MODEL), 1.0),
        'c_ctx': nrm(ks[3], (D_MODEL,), 1.0),
        'w_mod': nrm(ks[4], (DEPTH, D_MODEL, 6 * D_MODEL), D_MODEL ** -0.5),
        'b_mod': nrm(ks[5], (DEPTH, 6 * D_MODEL), 0.02),
        'w_in': nrm(ks[6], (DEPTH, D_MODEL, N_IN), D_MODEL ** -0.5),
        'sgu_ln_g': 1.0 + nrm(ks[7], (DEPTH, A_WIDTH), 0.02),
        'sgu_ln_b': nrm(ks[8], (DEPTH, A_WIDTH), 0.02),
        'sgu_w': nrm(ks[9], (DEPTH, A_HEADS, CHUNK, CHUNK), CHUNK ** -0.5),
        'sgu_b': 1.0 + nrm(ks[10], (DEPTH, A_HEADS, CHUNK), 0.1),
        'gla_gate_up': nrm(ks[11], (DEPTH, 2, GATE_RANK, B_QK), GATE_RANK ** -0.5),
        'gla_gate_b': nrm(ks[12], (DEPTH, 2, B_QK), 0.1),
        'gla_norm_g': 1.0 + nrm(ks[13], (DEPTH, B_WIDTH), 0.02),
        's5_lam_re': -0.5 + nrm(ks[14], (DEPTH, 2, S5_GROUPS, S5_STATE), 0.01),
        's5_lam_im': jnp.pi * n_idx + nrm(ks[15], (DEPTH, 2, S5_GROUPS, S5_STATE), 0.01),
        's5_log_dt': jax.random.uniform(ks[16], (DEPTH, 2, S5_GROUPS), F32, float(np.log(1e-3)), float(np.log(1e-1))),
        's5_b_re': nrm(ks[17], (DEPTH, 2, S5_GROUPS, S5_STATE, S5_GROUP_CH), (2 * S5_GROUP_CH) ** -0.5),
        's5_b_im': nrm(ks[18], (DEPTH, 2, S5_GROUPS, S5_STATE, S5_GROUP_CH), (2 * S5_GROUP_CH) ** -0.5),
        's5_c_re': nrm(ks[19], (DEPTH, 2, S5_GROUPS, S5_GROUP_CH, S5_STATE), (2 * S5_STATE) ** -0.5),
        's5_c_im': nrm(ks[20], (DEPTH, 2, S5_GROUPS, S5_GROUP_CH, S5_STATE), (2 * S5_STATE) ** -0.5),
        's5_d': nrm(ks[21], (DEPTH, S5_GROUPS, S5_GROUP_CH), 1.0),
        's5_glu_w': nrm(ks[22], (DEPTH, C_WIDTH, C_WIDTH), C_WIDTH ** -0.5),
        's5_glu_b': nrm(ks[23], (DEPTH, C_WIDTH), 0.02),
        'w_out': nrm(ks[24], (DEPTH, MIX_WIDTH, D_MODEL), MIX_WIDTH ** -0.5 * DN_BETA),
        'ln_g': 1.0 + nrm(ks[25], (DEPTH, 2, D_MODEL), 0.02),
        'ln_b': nrm(ks[26], (DEPTH, 2, D_MODEL), 0.02),
        'router_w': nrm(ks[27], (DEPTH, D_MODEL, N_EXPERTS), D_MODEL ** -0.5),
        'router_b': nrm(ks[28], (DEPTH, N_EXPERTS), 0.01),
        'w_up': nrm(ks[29], (DEPTH, N_EXPERTS, D_MODEL, 2 * D_EXPERT), D_MODEL ** -0.5),
        'b_up': nrm(ks[30], (DEPTH, N_EXPERTS, 2 * D_EXPERT), 0.02),
        'w_down': nrm(ks[31], (DEPTH, N_EXPERTS, D_EXPERT, D_MODEL), D_EXPERT ** -0.5 * DN_BETA),
        'b_down': nrm(ks[32], (DEPTH, N_EXPERTS, D_MODEL), 0.02),
    }


def reference(x, c, ctx, c_ctx, w_mod, b_mod, w_in, sgu_ln_g, sgu_ln_b, sgu_w, sgu_b, gla_gate_up, gla_gate_b,
              gla_norm_g, s5_lam_re, s5_lam_im, s5_log_dt, s5_b_re, s5_b_im, s5_c_re, s5_c_im, s5_d, s5_glu_w,
              s5_glu_b, w_out, ln_g, ln_b, router_w, router_b, w_up, b_up, w_down, b_down):
    bsz, n_lat, d = x.shape
    rows = n_lat // GRID_W
    lat_chunks = rows // ROWS_PER_CHUNK
    ctx_chunks = ctx.shape[1] // CHUNK
    xc = ctx
    for l in range(DEPTH):
        last = l == DEPTH - 1
        mod = jax.nn.silu(c) @ w_mod[l] + b_mod[l]
        mod_c = jax.nn.silu(c_ctx) @ w_mod[l] + b_mod[l]
        sh1, sc1, g1, sh2, sc2, g2 = jnp.split(mod[:, None, :], 6, axis=-1)
        cmod = jnp.split(mod_c, 6)
        lp = (w_in[l], sgu_ln_g[l], sgu_ln_b[l], sgu_w[l], sgu_b[l], gla_gate_up[l], gla_gate_b[l], gla_norm_g[l],
              s5_lam_re[l], s5_lam_im[l], s5_log_dt[l], s5_b_re[l], s5_b_im[l], s5_c_re[l], s5_c_im[l], s5_d[l],
              s5_glu_w[l], s5_glu_b[l])
        gla0 = jnp.zeros((2, bsz, B_HEADS, B_DK, B_DV), F32)
        s50 = jnp.zeros((2, bsz, S5_GROUPS, S5_STATE), jnp.complex64)
        mix_c, gla_sc, s5_sc = token_mixer(xc * (1.0 + cmod[1]) + cmod[0], ctx_chunks, *lp, gla0, s50, not last)
        mix_x, _, _ = token_mixer(x * (1.0 + sc1) + sh1, lat_chunks, *lp, gla_sc, s5_sc, True)
        x = layer_norm(DN_ALPHA * x + g1 * (mix_x @ w_out[l]), ln_g[l, 0], ln_b[l, 0])
        h2 = (x * (1.0 + sc2) + sh2).reshape(-1, d)
        if not last:
            xc = layer_norm(DN_ALPHA * xc + cmod[2] * (mix_c @ w_out[l]), ln_g[l, 0], ln_b[l, 0])
            hc2 = (xc * (1.0 + cmod[4]) + cmod[3]).reshape(-1, d)
            f = moe(jnp.concatenate([h2, hc2], axis=0), router_w[l], router_b[l], w_up[l], b_up[l], w_down[l], b_down[l])
            fx = f[:bsz * n_lat].reshape(x.shape)
            xc = layer_norm(DN_ALPHA * xc + cmod[5] * f[bsz * n_lat:].reshape(xc.shape), ln_g[l, 1], ln_b[l, 1])
        else:
            fx = moe(h2, router_w[l], router_b[l], w_up[l], b_up[l], w_down[l], b_down[l]).reshape(x.shape)
        x = layer_norm(DN_ALPHA * x + g2 * fx, ln_g[l, 1], ln_b[l, 1])
    return x
```

```python
import functools

import numpy as np
import jax
import jax.numpy as jnp
from jax import lax
from jax.experimental import pallas as pl
from jax.experimental.pallas import tpu as pltpu

F32 = jnp.float32
BF16 = jnp.bfloat16
HIGHEST = lax.Precision.HIGHEST

D_MODEL = 1024
CHUNK = 128
A_HEADS = 4
A_HEAD_DIM = 64
A_WIDTH = 256
B_HEADS = 4
B_DK = 64
B_DV = 128
B_QK = 256
B_WIDTH = 512
GATE_RANK = 16
GATE_TAU = 16.0
GLA_CHUNK = 64
S5_GROUPS = 16
S5_GROUP_CH = 16
S5_STATE = 64
C_WIDTH = 256
N_EXPERTS = 32
TOP_K = 4
D_EXPERT = 1024
SWIGLU_LIMIT = 7.0
SWIGLU_ALPHA = 1.702
LN_EPS = 1e-5

N_IN_PAD = 2432
COL_GL = 2304
COL_S5 = 2048

ROW_TILE = 512
SEQ_TILE = 256
S5_T = 16
S5_QUARTER = 4
MOE_BLOCK = 512
COMBINE_TILE = 256
VMEM_LIMIT = 56 * 1024 * 1024


def _layer_norm(x, g, b):
    mu = jnp.mean(x, axis=-1, keepdims=True)
    xc = x - mu
    var = jnp.mean(xc * xc, axis=-1, keepdims=True)
    return xc * lax.rsqrt(var + LN_EPS) * g + b


def _segment(row0, n_ctx_rows, seq):
    return jnp.where(row0 < n_ctx_rows, 0, 1 + (row0 - n_ctx_rows) // seq)


def _mod_kernel(c_ref, w_ref, b_ref, o_ref):
    c = c_ref[...]
    s = c * jax.nn.sigmoid(c)
    o_ref[0] = jnp.dot(s, w_ref[0], precision=HIGHEST, preferred_element_type=F32) + b_ref[0]


def _modulation(cvec, w_mod, b_mod):
    depth = w_mod.shape[0]
    n6 = w_mod.shape[2]
    tn = 1024
    return pl.pallas_call(
        _mod_kernel,
        grid=(depth, n6 // tn),
        in_specs=[
            pl.BlockSpec((8, D_MODEL), lambda l, j: (0, 0)),
            pl.BlockSpec((1, D_MODEL, tn), lambda l, j: (l, 0, j)),
            pl.BlockSpec((1, 1, tn), lambda l, j: (l, 0, j)),
        ],
        out_specs=pl.BlockSpec((1, 8, tn), lambda l, j: (l, 0, j)),
        out_shape=jax.ShapeDtypeStruct((depth, 8, n6), F32),
        compiler_params=pltpu.CompilerParams(
            dimension_semantics=("arbitrary", "arbitrary"), vmem_limit_bytes=VMEM_LIMIT),
    )(cvec, w_mod, b_mod.reshape(depth, 1, n6))


def _inproj_kernel(x_ref, mod_ref, w_ref, o_ref):
    m = mod_ref[0]
    h = x_ref[...] * (1.0 + m[1:2]) + m[0:1]
    o_ref[...] = jnp.dot(h.astype(BF16), w_ref[...], preferred_element_type=F32)


def _in_projection(x, mod_l, w_in_b, n_ctx_rows, seq):
    t = x.shape[0]
    tm = ROW_TILE
    return pl.pallas_call(
        _inproj_kernel,
        grid=(t // tm,),
        in_specs=[
            pl.BlockSpec((tm, D_MODEL), lambda i: (i, 0)),
            pl.BlockSpec((1, 6, D_MODEL), lambda i: (_segment(i * tm, n_ctx_rows, seq), 0, 0)),
            pl.BlockSpec((D_MODEL, N_IN_PAD), lambda i: (0, 0)),
        ],
        out_specs=pl.BlockSpec((tm, N_IN_PAD), lambda i: (i, 0)),
        out_shape=jax.ShapeDtypeStruct((t, N_IN_PAD), F32),
        compiler_params=pltpu.CompilerParams(
            dimension_semantics=("arbitrary",), vmem_limit_bytes=VMEM_LIMIT),
    )(x, mod_l, w_in_b)


def _sgu_kernel(uv_ref, g_ref, b_ref, w_ref, bias_ref, o_ref):
    tm = uv_ref.shape[0]
    u = jax.nn.gelu(uv_ref[:, :A_WIDTH])
    v = _layer_norm(jax.nn.gelu(uv_ref[:, A_WIDTH:]), g_ref[...], b_ref[...]).astype(BF16)
    head = lax.broadcasted_iota(jnp.int32, (1, A_WIDTH), 1) // A_HEAD_DIM
    for c in range(tm // CHUNK):
        rows = slice(c * CHUNK, (c + 1) * CHUNK)
        vc = v[rows]
        acc = bias_ref[...]
        for h in range(A_HEADS):
            r = jnp.dot(w_ref[h], vc, preferred_element_type=F32)
            acc = acc + jnp.where(head == h, r, 0.0)
        o_ref[rows, :] = (u[rows] * acc).astype(BF16)


def _spatial_gate(proj, ln_g, ln_b, w_s, b_s):
    t = proj.shape[0]
    tm = ROW_TILE
    bias = jnp.repeat(b_s.T, A_HEAD_DIM, axis=1)
    return pl.pallas_call(
        _sgu_kernel,
        grid=(t // tm,),
        in_specs=[
            pl.BlockSpec((tm, 2 * A_WIDTH), lambda i: (i, 0)),
            pl.BlockSpec((1, A_WIDTH), lambda i: (0, 0)),
            pl.BlockSpec((1, A_WIDTH), lambda i: (0, 0)),
            pl.BlockSpec((A_HEADS, CHUNK, CHUNK), lambda i: (0, 0, 0)),
            pl.BlockSpec((CHUNK, A_WIDTH), lambda i: (0, 0)),
        ],
        out_specs=pl.BlockSpec((tm, A_WIDTH), lambda i: (i, 0)),
        out_shape=jax.ShapeDtypeStruct((t, A_WIDTH), BF16),
        compiler_params=pltpu.CompilerParams(
            dimension_semantics=("arbitrary",), vmem_limit_bytes=VMEM_LIMIT),
    )(proj, ln_g.reshape(1, -1), ln_b.reshape(1, -1), w_s.astype(BF16), bias)


_NT = (((1,), (1,)), ((), ()))
_TN = (((0,), (0,)), ((), ()))


def _gla_kernel(backward, *refs):
    if backward:
        q_ref, k_ref, v_ref, gl_ref, gup_ref, gb_ref, of_ref, g_ref, ng_ref, o_ref, s_ref = refs
    else:
        q_ref, k_ref, v_ref, gl_ref, gup_ref, gb_ref, o_ref, s_ref = refs
    c_len = GLA_CHUNK

    @pl.when(pl.program_id(1) == 0)
    def _():
        s_ref[...] = jnp.zeros_like(s_ref)

    row = lax.broadcasted_iota(jnp.int32, (c_len, c_len), 0)
    col = lax.broadcasted_iota(jnp.int32, (c_len, c_len), 1)
    if backward:
        tri = (col >= row).astype(F32)
        keep = col > row
        i_last, i_mid = 0, c_len - 1 - c_len // 2
        lo = GATE_RANK
    else:
        tri = (col <= row).astype(F32)
        keep = col <= row
        i_last, i_mid = c_len - 1, c_len // 2
        lo = 0
    n_chunks = q_ref.shape[0] // c_len
    order = range(n_chunks - 1, -1, -1) if backward else range(n_chunks)
    for c in order:
        rows = slice(c * c_len, (c + 1) * c_len)
        gl = gl_ref[rows, lo:lo + GATE_RANK]
        z = jnp.dot(gl, gup_ref[...], precision=HIGHEST, preferred_element_type=F32) + gb_ref[...]
        la = jax.nn.log_sigmoid(z) / GATE_TAU
        b = jnp.dot(tri, la, precision=HIGHEST, preferred_element_type=F32)
        b_last = b[i_last:i_last + 1]
        b_mid = b[i_mid:i_mid + 1]
        q = q_ref[rows, :] * (B_DK ** -0.5)
        k = k_ref[rows, :]
        q_mid = (q * jnp.exp(b - b_mid)).astype(BF16)
        k_mid = (k * jnp.exp(b_mid - b)).astype(BF16)
        q_in = (q * jnp.exp(b)).astype(BF16)
        k_out = (k * jnp.exp(b_last - b)).astype(BF16)
        decay = jnp.exp(b_last)
        for h in range(B_HEADS):
            hk = slice(h * B_DK, (h + 1) * B_DK)
            hv = slice(h * B_DV, (h + 1) * B_DV)
            vh = v_ref[rows, hv].astype(BF16)
            s_prev = s_ref[h]
            sc = lax.dot_general(q_mid[:, hk], k_mid[:, hk], _NT, preferred_element_type=F32)
            sc = jnp.where(keep, sc, 0.0).astype(BF16)
            o = (jnp.dot(sc, vh, preferred_element_type=F32)
                 + lax.dot_general(q_in[:, hk], s_prev.astype(BF16), _NT, preferred_element_type=F32))
            s_ref[h] = s_prev * decay[:, hk] + lax.dot_general(vh, k_out[:, hk], _TN,
                                                               preferred_element_type=F32)
            if backward:
                o = o + of_ref[rows, hv]
                o = o * lax.rsqrt(jnp.mean(o * o, axis=-1, keepdims=True) + LN_EPS)
                gate = g_ref[rows, hv]
                o = o * ng_ref[:, hv] * (gate * jax.nn.sigmoid(gate))
                o_ref[rows, hv] = o.astype(BF16)
            else:
                o_ref[rows, hv] = o


def _gla_block(backward, bsz, nctx_blk, nlat_blk, b, j):
    if backward:
        ctx_i = b * nctx_blk + (nctx_blk - 1 - j)
        lat_i = bsz * nctx_blk + b * nlat_blk + (nlat_blk - 1 - (j - nctx_blk))
    else:
        ctx_i = b * nctx_blk + j
        lat_i = bsz * nctx_blk + b * nlat_blk + (j - nctx_blk)
    return jnp.where(j < nctx_blk, ctx_i, lat_i)


def _gla_sweep(backward, proj, gate_up, gate_b, bsz, ctx_len, seq, o_fwd=None, norm_g=None):
    t = proj.shape[0]
    r = SEQ_TILE
    nctx_blk, nlat_blk = ctx_len // r, seq // r
    blk = functools.partial(_gla_block, backward, bsz, nctx_blk, nlat_blk)
    d = 1 if backward else 0
    in_specs = [
        pl.BlockSpec((r, B_QK), lambda b, j: (blk(b, j), 2)),
        pl.BlockSpec((r, B_QK), lambda b, j: (blk(b, j), 3)),
        pl.BlockSpec((r, B_WIDTH), lambda b, j: (blk(b, j), 2)),
        pl.BlockSpec((r, 128), lambda b, j: (blk(b, j), COL_GL // 128)),
        pl.BlockSpec((GATE_RANK, B_QK), lambda b, j: (0, 0)),
        pl.BlockSpec((1, B_QK), lambda b, j: (0, 0)),
    ]
    args = [proj, proj, proj, proj, gate_up[d], gate_b[d].reshape(1, -1)]
    if backward:
        in_specs += [
            pl.BlockSpec((r, B_WIDTH), lambda b, j: (blk(b, j), 0)),
            pl.BlockSpec((r, B_WIDTH), lambda b, j: (blk(b, j), 3)),
            pl.BlockSpec((1, B_WIDTH), lambda b, j: (0, 0)),
        ]
        args += [o_fwd, proj, norm_g.reshape(1, -1)]
    return pl.pallas_call(
        functools.partial(_gla_kernel, backward),
        grid=(bsz, nctx_blk + nlat_blk),
        in_specs=in_specs,
        out_specs=pl.BlockSpec((r, B_WIDTH), lambda b, j: (blk(b, j), 0)),
        out_shape=jax.ShapeDtypeStruct((t, B_WIDTH), BF16 if backward else F32),
        scratch_shapes=[pltpu.VMEM((B_HEADS, B_DV, B_DK), F32)],
        compiler_params=pltpu.CompilerParams(
            dimension_semantics=("arbitrary", "arbitrary"), vmem_limit_bytes=VMEM_LIMIT),
    )(*args)


def _s5_matrices(lam_re, lam_im, log_dt, b_re, b_im, c_re, c_im):
    tc = S5_T
    lam = lax.complex(lam_re.astype(F32), lam_im.astype(F32))
    dt = jnp.exp(log_dt.astype(F32))
    bm = lax.complex(b_re.astype(F32), b_im.astype(F32))
    cm = lax.complex(c_re.astype(F32), c_im.astype(F32))
    ldt = lam * dt[..., None]
    lam_bar = jnp.exp(ldt)
    b_bar = ((lam_bar - 1.0) / lam)[..., None] * bm
    steps = jnp.arange(tc + 1, dtype=F32)
    pw = jnp.exp(ldt[:, None] * steps[None, :, None, None])
    kern = jnp.real(jnp.einsum('dgip,dtgp,dgpj->dgtij', cm, pw[:, :tc], b_bar))
    s_i = jnp.arange(tc)[:, None]
    t_i = jnp.arange(tc)[None, :]
    g_n = lam.shape[1]
    hh = S5_GROUP_CH

    def toeplitz(kd, lag, ok):
        m = kd[:, jnp.clip(lag, 0, tc - 1)]
        m = jnp.where(ok[None, :, :, None, None], m, 0.0)
        return m.transpose(0, 1, 4, 2, 3).reshape(g_n, tc * hh, tc * hh)

    mt = toeplitz(kern[0], t_i - s_i, t_i >= s_i) + toeplitz(kern[1], s_i - t_i, s_i >= t_i)

    def state_in(d, powers):
        w = pw[d][powers][:, :, :, None] * b_bar[d][None]
        w = w.transpose(1, 0, 3, 2).reshape(g_n, tc * hh, S5_STATE)
        re, im = jnp.real(w), jnp.imag(w)
        return jnp.concatenate([re, im, im, re], axis=-1)

    qt = jnp.concatenate([state_in(0, tc - 1 - jnp.arange(tc)), state_in(1, jnp.arange(tc))], axis=-1)

    def state_out(d, powers):
        w = cm[d][:, None] * pw[d][powers].transpose(1, 0, 2)[:, :, None, :]
        w = w.reshape(g_n, tc * hh, S5_STATE).transpose(0, 2, 1)
        return jnp.concatenate([jnp.real(w), -jnp.imag(w)], axis=1)

    pt = jnp.concatenate([state_out(0, 1 + jnp.arange(tc)), state_out(1, tc - jnp.arange(tc))], axis=1)

    a = pw[:, tc]
    ar, ai = jnp.real(a), jnp.imag(a)
    a1 = jnp.concatenate([ar, ar], axis=-1).reshape(2, -1)
    a2 = jnp.concatenate([-ai, ai], axis=-1).reshape(2, -1)
    a3 = jnp.concatenate([ai, -ai], axis=-1).reshape(2, -1)
    ac = jnp.stack([a1, a2, a3], axis=1)
    return mt.astype(BF16), qt.astype(BF16), pt.astype(BF16), ac.reshape(2, 3, 1, -1)


def _s5_kernel(uc_ref, ul_ref, mt_ref, qt_ref, pt_ref, ac_ref, yc_ref, yl_ref, ef, esf, eb, esb):
    nq = uc_ref.shape[0]
    nc_ctx, nc_lat = uc_ref.shape[1], ul_ref.shape[1]
    n = nc_ctx + nc_lat
    w = 2 * S5_STATE
    for gi in range(nq):
        lanes = slice(gi * w, (gi + 1) * w)
        rc = jnp.dot(uc_ref[gi], qt_ref[gi], preferred_element_type=F32)
        rl = jnp.dot(ul_ref[gi], qt_ref[gi], preferred_element_type=F32)
        ef[0:nc_ctx, lanes] = rc[:, 0:w]
        esf[0:nc_ctx, lanes] = rc[:, w:2 * w]
        ef[nc_ctx:n, lanes] = rl[:, 0:w]
        esf[nc_ctx:n, lanes] = rl[:, w:2 * w]
        eb[0:nc_lat, lanes] = rl[:, 2 * w:3 * w]
        esb[0:nc_lat, lanes] = rl[:, 3 * w:4 * w]
        eb[nc_lat:n, lanes] = rc[:, 2 * w:3 * w]
        esb[nc_lat:n, lanes] = rc[:, 3 * w:4 * w]

    a1f, a2f, a3f = ac_ref[0, 0], ac_ref[0, 1], ac_ref[0, 2]
    a1b, a2b, a3b = ac_ref[1, 0], ac_ref[1, 1], ac_ref[1, 2]

    def body(i, carry):
        hf, hsf, hb, hsb = carry
        rf = pl.ds(i, 1)
        rb = pl.ds(n - 1 - i, 1)
        e_f, es_f = ef[rf, :], esf[rf, :]
        e_b, es_b = eb[rb, :], esb[rb, :]
        ef[rf, :] = hf
        eb[rb, :] = hb
        return (a1f * hf + a2f * hsf + e_f, a1f * hsf + a3f * hf + es_f,
                a1b * hb + a2b * hsb + e_b, a1b * hsb + a3b * hb + es_b)

    zero = jnp.zeros((1, nq * w), F32)
    lax.fori_loop(0, n, body, (zero, zero, zero, zero))

    for gi in range(nq):
        lanes = slice(gi * w, (gi + 1) * w)
        hc = jnp.concatenate([ef[0:nc_ctx, lanes], eb[nc_lat:n, lanes]], axis=1).astype(BF16)
        hl = jnp.concatenate([ef[nc_ctx:n, lanes], eb[0:nc_lat, lanes]], axis=1).astype(BF16)
        yc_ref[gi] = (jnp.dot(uc_ref[gi], mt_ref[gi], preferred_element_type=F32)
                      + jnp.dot(hc, pt_ref[gi], preferred_element_type=F32))
        yl_ref[gi] = (jnp.dot(ul_ref[gi], mt_ref[gi], preferred_element_type=F32)
                      + jnp.dot(hl, pt_ref[gi], preferred_element_type=F32))


def _s5_scan(proj, mats, bsz, ctx_len, seq):
    mt, qt, pt, ac = mats
    g_n, tc, hh = S5_GROUPS, S5_T, S5_GROUP_CH
    n_ctx_rows = bsz * ctx_len
    u = proj[:, COL_S5:COL_S5 + C_WIDTH].astype(BF16)

    def to_rows(part):
        r = part.shape[0]
        return part.reshape(r // tc, tc, g_n, hh).transpose(2, 0, 1, 3).reshape(g_n, r // tc, tc * hh)

    uc, ul = to_rows(u[:n_ctx_rows]), to_rows(u[n_ctx_rows:])
    nc_ctx, nc_lat = ctx_len // tc, seq // tc
    nq = S5_QUARTER
    wq = nq * 2 * S5_STATE
    n = nc_ctx + nc_lat
    yc, yl = pl.pallas_call(
        _s5_kernel,
        grid=(bsz, g_n // nq),
        in_specs=[
            pl.BlockSpec((nq, nc_ctx, tc * hh), lambda b, qi: (qi, b, 0)),
            pl.BlockSpec((nq, nc_lat, tc * hh), lambda b, qi: (qi, b, 0)),
            pl.BlockSpec((nq, tc * hh, tc * hh), lambda b, qi: (qi, 0, 0)),
            pl.BlockSpec((nq, tc * hh, 8 * S5_STATE), lambda b, qi: (qi, 0, 0)),
            pl.BlockSpec((nq, 4 * S5_STATE, tc * hh), lambda b, qi: (qi, 0, 0)),
            pl.BlockSpec((2, 3, 1, wq), lambda b, qi: (0, 0, 0, qi)),
        ],
        out_specs=[
            pl.BlockSpec((nq, nc_ctx, tc * hh), lambda b, qi: (qi, b, 0)),
            pl.BlockSpec((nq, nc_lat, tc * hh), lambda b, qi: (qi, b, 0)),
        ],
        out_shape=[
            jax.ShapeDtypeStruct((g_n, bsz * nc_ctx, tc * hh), F32),
            jax.ShapeDtypeStruct((g_n, bsz * nc_lat, tc * hh), F32),
        ],
        scratch_shapes=[pltpu.VMEM((n, wq), F32) for _ in range(4)],
        compiler_params=pltpu.CompilerParams(
            dimension_semantics=("arbitrary", "arbitrary"), vmem_limit_bytes=VMEM_LIMIT),
    )(uc, ul, mt, qt, pt, ac)

    def from_rows(y):
        r = y.shape[1]
        return y.reshape(g_n, r, tc, hh).transpose(1, 2, 0, 3).reshape(r * tc, g_n * hh)

    return jnp.concatenate([from_rows(yc), from_rows(yl)], axis=0)


def _outproj_kernel(alpha, a_ref, gla_ref, y_ref, u_ref, x_ref, mod_ref, d_ref, gw_ref, gb_ref, wo_ref,
                    lng_ref, lnb_ref, rw_ref, rb_ref, x1_ref, h2_ref, lg_ref):
    m = mod_ref[0]
    y = jax.nn.gelu(y_ref[...] + d_ref[...] * u_ref[...])
    s5 = y * jax.nn.sigmoid(jnp.dot(y.astype(BF16), gw_ref[...], preferred_element_type=F32) + gb_ref[...])
    mix = (jnp.dot(a_ref[...], wo_ref[0:A_WIDTH, :], preferred_element_type=F32)
           + jnp.dot(gla_ref[...], wo_ref[A_WIDTH:A_WIDTH + B_WIDTH, :], preferred_element_type=F32)
           + jnp.dot(s5.astype(BF16), wo_ref[A_WIDTH + B_WIDTH:, :], preferred_element_type=F32))
    x1 = _layer_norm(alpha * x_ref[...] + m[2:3] * mix, lng_ref[...], lnb_ref[...])
    x1_ref[...] = x1
    h2 = x1 * (1.0 + m[4:5]) + m[3:4]
    h2_ref[...] = h2
    lg_ref[...] = jnp.dot(h2, rw_ref[...], precision=HIGHEST, preferred_element_type=F32) + rb_ref[...]


def _out_projection(alpha, a_out, gla_out, y_s5, proj, x, mod_l, s5_d, glu_w, glu_b, w_out_b, ln_g, ln_b,
                    router_w, router_b, n_ctx_rows, seq):
    t = x.shape[0]
    tm = ROW_TILE
    rw = jnp.zeros((D_MODEL, 128), F32).at[:, :N_EXPERTS].set(router_w)
    rb = jnp.zeros((1, 128), F32).at[0, :N_EXPERTS].set(router_b)
    row = lambda i: (i, 0)
    fixed = lambda i: (0, 0)
    return pl.pallas_call(
        functools.partial(_outproj_kernel, alpha),
        grid=(t // tm,),
        in_specs=[
            pl.BlockSpec((tm, A_WIDTH), row),
            pl.BlockSpec((tm, B_WIDTH), row),
            pl.BlockSpec((tm, C_WIDTH), row),
            pl.BlockSpec((tm, C_WIDTH), lambda i: (i, COL_S5 // C_WIDTH)),
            pl.BlockSpec((tm, D_MODEL), row),
            pl.BlockSpec((1, 6, D_MODEL), lambda i: (_segment(i * tm, n_ctx_rows, seq), 0, 0)),
            pl.BlockSpec((1, C_WIDTH), fixed),
            pl.BlockSpec((C_WIDTH, C_WIDTH), fixed),
            pl.BlockSpec((1, C_WIDTH), fixed),
            pl.BlockSpec((D_MODEL, D_MODEL), fixed),
            pl.BlockSpec((1, D_MODEL), fixed),
            pl.BlockSpec((1, D_MODEL), fixed),
            pl.BlockSpec((D_MODEL, 128), fixed),
            pl.BlockSpec((1, 128), fixed),
        ],
        out_specs=[
            pl.BlockSpec((tm, D_MODEL), row),
            pl.BlockSpec((tm, D_MODEL), row),
            pl.BlockSpec((tm, 128), row),
        ],
        out_shape=[
            jax.ShapeDtypeStruct((t, D_MODEL), F32),
            jax.ShapeDtypeStruct((t, D_MODEL), F32),
            jax.ShapeDtypeStruct((t, 128), F32),
        ],
        compiler_params=pltpu.CompilerParams(
            dimension_semantics=("arbitrary",), vmem_limit_bytes=VMEM_LIMIT),
    )(a_out, gla_out, y_s5, proj, x, mod_l, s5_d.reshape(1, -1), glu_w.astype(BF16), glu_b.reshape(1, -1),
      w_out_b, ln_g.reshape(1, -1), ln_b.reshape(1, -1), rw, rb)


def _routing(logits, n_blocks):
    t = logits.shape[0]
    top_logit, top_idx = lax.top_k(logits, TOP_K)
    gates = jax.nn.softmax(top_logit, axis=-1)
    member = (top_idx[:, :, None] == jnp.arange(N_EXPERTS, dtype=jnp.int32)[None, None, :]).any(axis=1)
    member = member.astype(jnp.int32)
    csum = jnp.cumsum(member, axis=0)
    counts = csum[-1]
    rank = jnp.take_along_axis(csum - member, top_idx, axis=1)
    padded = (counts + MOE_BLOCK - 1) // MOE_BLOCK * MOE_BLOCK
    padded_end = jnp.cumsum(padded)
    dest = (padded_end - padded)[top_idx] + rank
    tok = jnp.broadcast_to(jnp.arange(t, dtype=jnp.int32)[:, None], (t, TOP_K))
    slot_token = jnp.zeros((n_blocks * MOE_BLOCK,), jnp.int32).at[dest.reshape(-1)].set(tok.reshape(-1))
    block_expert = jnp.minimum(
        jnp.searchsorted(padded_end, jnp.arange(n_blocks, dtype=jnp.int32) * MOE_BLOCK, side='right'),
        N_EXPERTS - 1).astype(jnp.int32)
    n_valid = (padded_end[-1] // MOE_BLOCK).astype(jnp.int32).reshape(1)
    return gates, dest.astype(jnp.int32), slot_token, block_expert, n_valid


def _moe_kernel(be_ref, nv_ref, tok_ref, tokn_ref, h_ref, wu_ref, bu_ref, wd_ref, bd_ref, o_ref,
                xbuf, x2d, wu_b, wd_b, sem):
    i = pl.program_id(0)
    n_valid = nv_ref[0]
    slot = i % 2

    def gather(idx_ref, s):
        def issue(r, carry):
            pltpu.make_async_copy(h_ref.at[idx_ref[0, 0, r]], xbuf.at[s, r], sem.at[s]).start()
            return carry
        lax.fori_loop(0, MOE_BLOCK, issue, 0, unroll=8)

    @pl.when(i == 0)
    def _():
        gather(tok_ref, 0)

    @pl.when(i + 1 < n_valid)
    def _():
        gather(tokn_ref, 1 - slot)

    @pl.when(i < n_valid)
    def _():
        first = jnp.logical_or(i == 0, be_ref[i] != be_ref[jnp.maximum(i - 1, 0)])

        @pl.when(first)
        def _():
            rows = 64

            def cast(r, carry):
                rs = pl.ds(pl.multiple_of(r * rows, rows), rows)
                wu_b[rs, :] = wu_ref[0, rs, :].astype(BF16)
                wd_b[rs, :] = wd_ref[0, rs, :].astype(BF16)
                return carry
            lax.fori_loop(0, D_MODEL // rows, cast, 0)

        pltpu.make_async_copy(xbuf.at[slot], xbuf.at[slot], sem.at[slot]).wait()
        x2d[...] = xbuf[slot].reshape(MOE_BLOCK, D_MODEL)
        x = x2d[...].astype(BF16)
        acc = jnp.zeros((MOE_BLOCK, D_MODEL), F32) + bd_ref[0]
        cw = 512
        for jc in range(D_EXPERT // cw):
            cg = slice(jc * cw, (jc + 1) * cw)
            cl = slice(D_EXPERT + jc * cw, D_EXPERT + (jc + 1) * cw)
            ug = jnp.dot(x, wu_b[:, cg], preferred_element_type=F32) + bu_ref[0, :, cg]
            ul = jnp.dot(x, wu_b[:, cl], preferred_element_type=F32) + bu_ref[0, :, cl]
            xg = jnp.minimum(ug, SWIGLU_LIMIT)
            xl = jnp.clip(ul, -SWIGLU_LIMIT, SWIGLU_LIMIT)
            act = xg * jax.nn.sigmoid(SWIGLU_ALPHA * xg) * (xl + 1.0)
            acc = acc + jnp.dot(act.astype(BF16), wd_b[cg, :], preferred_element_type=F32)
        o_ref[...] = acc

    @pl.when(i >= n_valid)
    def _():
        o_ref[...] = jnp.zeros_like(o_ref)


def _moe_experts(h3, slot_token, block_expert, n_valid, w_up, b_up, w_down, b_down):
    n_blocks = block_expert.shape[0]
    tok3 = slot_token.reshape(n_blocks, 1, MOE_BLOCK)
    last = n_blocks - 1
    grid_spec = pltpu.PrefetchScalarGridSpec(
        num_scalar_prefetch=2,
        grid=(n_blocks,),
        in_specs=[
            pl.BlockSpec((1, 1, MOE_BLOCK), lambda i, be, nv: (i, 0, 0), memory_space=pltpu.SMEM),
            pl.BlockSpec((1, 1, MOE_BLOCK), lambda i, be, nv: (jnp.minimum(i + 1, last), 0, 0),
                         memory_space=pltpu.SMEM),
            pl.BlockSpec(memory_space=pl.ANY),
            pl.BlockSpec((1, D_MODEL, 2 * D_EXPERT), lambda i, be, nv: (be[i], 0, 0)),
            pl.BlockSpec((1, 1, 2 * D_EXPERT), lambda i, be, nv: (be[i], 0, 0)),
            pl.BlockSpec((1, D_EXPERT, D_MODEL), lambda i, be, nv: (be[i], 0, 0)),
            pl.BlockSpec((1, 1, D_MODEL), lambda i, be, nv: (be[i], 0, 0)),
        ],
        out_specs=pl.BlockSpec((MOE_BLOCK, D_MODEL), lambda i, be, nv: (i, 0)),
        scratch_shapes=[
            pltpu.VMEM((2, MOE_BLOCK, 1, D_MODEL), F32),
            pltpu.VMEM((MOE_BLOCK, D_MODEL), F32),
            pltpu.VMEM((D_MODEL, 2 * D_EXPERT), BF16),
            pltpu.VMEM((D_EXPERT, D_MODEL), BF16),
            pltpu.SemaphoreType.DMA((2,)),
        ],
    )
    return pl.pallas_call(
        _moe_kernel,
        grid_spec=grid_spec,
        out_shape=jax.ShapeDtypeStruct((n_blocks * MOE_BLOCK, D_MODEL), F32),
        compiler_params=pltpu.CompilerParams(
            dimension_semantics=("arbitrary",), vmem_limit_bytes=VMEM_LIMIT),
    )(block_expert, n_valid, tok3, tok3, h3, w_up, b_up.reshape(N_EXPERTS, 1, -1), w_down,
      b_down.reshape(N_EXPERTS, 1, -1))


def _combine_kernel(alpha, dst_ref, dstn_ref, gate_ref, y_ref, x_ref, mod_ref, lng_ref, lnb_ref, o_ref,
                    buf, y2d, sem):
    i = pl.program_id(0)
    n = pl.num_programs(0)
    tm = COMBINE_TILE
    slot = i % 2

    def gather(idx_ref, s):
        def issue(r, carry):
            for kk in range(TOP_K):
                pltpu.make_async_copy(y_ref.at[idx_ref[0, 0, r * TOP_K + kk]], buf.at[s, kk, r],
                                      sem.at[s]).start()
            return carry
        lax.fori_loop(0, tm, issue, 0, unroll=4)

    @pl.when(i == 0)
    def _():
        gather(dst_ref, 0)

    @pl.when(i + 1 < n)
    def _():
        gather(dstn_ref, 1 - slot)

    pltpu.make_async_copy(buf.at[slot], buf.at[slot], sem.at[slot]).wait()
    gates = gate_ref[...]
    f = jnp.zeros((tm, D_MODEL), F32)
    for kk in range(TOP_K):
        y2d[...] = buf[slot, kk].reshape(tm, D_MODEL)
        f = f + gates[:, kk:kk + 1] * y2d[...]
    m = mod_ref[0]
    o_ref[...] = _layer_norm(alpha * x_ref[...] + m[5:6] * f, lng_ref[...], lnb_ref[...])


def _moe_combine(alpha, dest, gates, ys3, x1, mod_l, ln_g, ln_b, n_ctx_rows, seq):
    t = x1.shape[0]
    tm = COMBINE_TILE
    nt = t // tm
    dst3 = dest.reshape(nt, 1, tm * TOP_K)
    return pl.pallas_call(
        functools.partial(_combine_kernel, alpha),
        grid=(nt,),
        in_specs=[
            pl.BlockSpec((1, 1, tm * TOP_K), lambda i: (i, 0, 0), memory_space=pltpu.SMEM),
            pl.BlockSpec((1, 1, tm * TOP_K), lambda i: (jnp.minimum(i + 1, nt - 1), 0, 0),
                         memory_space=pltpu.SMEM),
            pl.BlockSpec((tm, TOP_K), lambda i: (i, 0)),
            pl.BlockSpec(memory_space=pl.ANY),
            pl.BlockSpec((tm, D_MODEL), lambda i: (i, 0)),
            pl.BlockSpec((1, 6, D_MODEL), lambda i: (_segment(i * tm, n_ctx_rows, seq), 0, 0)),
            pl.BlockSpec((1, D_MODEL), lambda i: (0, 0)),
            pl.BlockSpec((1, D_MODEL), lambda i: (0, 0)),
        ],
        out_specs=pl.BlockSpec((tm, D_MODEL), lambda i: (i, 0)),
        out_shape=jax.ShapeDtypeStruct((t, D_MODEL), F32),
        scratch_shapes=[
            pltpu.VMEM((2, TOP_K, tm, 1, D_MODEL), F32),
            pltpu.VMEM((tm, D_MODEL), F32),
            pltpu.SemaphoreType.DMA((2,)),
        ],
        compiler_params=pltpu.CompilerParams(
            dimension_semantics=("arbitrary",), vmem_limit_bytes=VMEM_LIMIT),
    )(dst3, dst3, gates, ys3, x1, mod_l, ln_g.reshape(1, -1), ln_b.reshape(1, -1))


def kernel(x, c, ctx, c_ctx, w_mod, b_mod, w_in, sgu_ln_g, sgu_ln_b, sgu_w, sgu_b, gla_gate_up, gla_gate_b,
           gla_norm_g, s5_lam_re, s5_lam_im, s5_log_dt, s5_b_re, s5_b_im, s5_c_re, s5_c_im, s5_d, s5_glu_w,
           s5_glu_b, w_out, ln_g, ln_b, router_w, router_b, w_up, b_up, w_down, b_down):
    bsz, seq, d = x.shape
    ctx_len = ctx.shape[1]
    depth = w_in.shape[0]
    alpha = float((2 * depth) ** 0.25)
    n_ctx_rows = bsz * ctx_len
    t = n_ctx_rows + bsz * seq
    assert d == D_MODEL and bsz + 1 <= 8
    assert n_ctx_rows % ROW_TILE == 0 and seq % ROW_TILE == 0
    assert ctx_len % SEQ_TILE == 0 and seq % SEQ_TILE == 0 and t % COMBINE_TILE == 0

    xa = jnp.concatenate([ctx.reshape(n_ctx_rows, d), x.reshape(bsz * seq, d)], axis=0)
    cvec = jnp.zeros((8, d), F32).at[0].set(c_ctx).at[1:1 + bsz].set(c)
    mod = _modulation(cvec, w_mod, b_mod).reshape(depth, 8, 6, d)

    o = np.cumsum((0, A_WIDTH, A_WIDTH, B_QK, B_QK, B_WIDTH, B_WIDTH, 2 * GATE_RANK, C_WIDTH))
    w_in_r = jnp.concatenate(
        [w_in[:, :, o[0]:o[6]], w_in[:, :, o[7]:o[8]], w_in[:, :, o[6]:o[7]],
         jnp.zeros((depth, d, N_IN_PAD - int(o[8])), w_in.dtype)], axis=-1).astype(BF16)
    w_out_b = w_out.astype(BF16)

    n_assign = t * TOP_K
    n_blocks = -(-(n_assign + N_EXPERTS * (MOE_BLOCK - 1)) // MOE_BLOCK)

    for l in range(depth):
        mod_l = mod[l]
        proj = _in_projection(xa, mod_l, w_in_r[l], n_ctx_rows, seq)
        a_out = _spatial_gate(proj, sgu_ln_g[l], sgu_ln_b[l], sgu_w[l], sgu_b[l])
        o_fwd = _gla_sweep(False, proj, gla_gate_up[l], gla_gate_b[l], bsz, ctx_len, seq)
        gla_out = _gla_sweep(True, proj, gla_gate_up[l], gla_gate_b[l], bsz, ctx_len, seq,
                             o_fwd=o_fwd, norm_g=gla_norm_g[l])
        mats = _s5_matrices(s5_lam_re[l], s5_lam_im[l], s5_log_dt[l], s5_b_re[l], s5_b_im[l],
                            s5_c_re[l], s5_c_im[l])
        y_s5 = _s5_scan(proj, mats, bsz, ctx_len, seq)
        x1, h2, logits = _out_projection(alpha, a_out, gla_out, y_s5, proj, xa, mod_l, s5_d[l], s5_glu_w[l],
                                         s5_glu_b[l], w_out_b[l], ln_g[l, 0], ln_b[l, 0], router_w[l],
                                         router_b[l], n_ctx_rows, seq)
        gates, dest, slot_token, block_expert, n_valid = _routing(logits[:, :N_EXPERTS], n_blocks)
        ys = _moe_experts(h2.reshape(t, 1, d), slot_token, block_expert, n_valid,
                          w_up[l], b_up[l], w_down[l], b_down[l])
        xa = _moe_combine(alpha, dest, gates, ys.reshape(n_blocks * MOE_BLOCK, 1, d), x1, mod_l,
                          ln_g[l, 1], ln_b[l, 1], n_ctx_rows, seq)
    return xa[n_ctx_rows:].reshape(bsz, seq, d)
```

```python
import functools

import numpy as np
import jax
import jax.numpy as jnp
from jax import lax
from jax.experimental import pallas as pl
from jax.experimental.pallas import tpu as pltpu

F32 = jnp.float32
BF16 = jnp.bfloat16
HIGHEST = lax.Precision.HIGHEST

D_MODEL = 1024
CHUNK = 128
A_HEADS = 4
A_HEAD_DIM = 64
A_WIDTH = 256
B_HEADS = 4
B_DK = 64
B_DV = 128
B_QK = 256
B_WIDTH = 512
GATE_RANK = 16
GATE_TAU = 16.0
GLA_CHUNK = 64
S5_GROUPS = 16
S5_GROUP_CH = 16
S5_STATE = 64
C_WIDTH = 256
N_EXPERTS = 32
TOP_K = 4
D_EXPERT = 1024
SWIGLU_LIMIT = 7.0
SWIGLU_ALPHA = 1.702
LN_EPS = 1e-5

N_IN_PAD = 2432
COL_GL = 2304
COL_S5 = 2048

ROW_TILE = 512
SEQ_TILE = 256
S5_T = 16
S5_QUARTER = 4
MOE_BLOCK = 512
COMBINE_TILE = 256
VMEM_LIMIT = 56 * 1024 * 1024


def _layer_norm(x, g, b):
    mu = jnp.mean(x, axis=-1, keepdims=True)
    xc = x - mu
    var = jnp.mean(xc * xc, axis=-1, keepdims=True)
    return xc * lax.rsqrt(var + LN_EPS) * g + b


LANES = 128
ROW_SUB = D_MODEL // LANES


def _store_row_tiles(ref, lead, val):
    n = val.shape[0]
    for j in range(ROW_SUB):
        ref[lead + (pl.ds(j, n, stride=ROW_SUB), slice(None))] = val[:, j * LANES:(j + 1) * LANES]


def _load_row_tiles(ref, lead, n):
    return jnp.concatenate(
        [ref[lead + (pl.ds(j, n, stride=ROW_SUB), slice(None))] for j in range(ROW_SUB)], axis=1)


def _row_tile(ref, lead, r):
    return ref.at[lead + (pl.ds(pl.multiple_of(r * ROW_SUB, ROW_SUB), ROW_SUB), slice(None))]


def _segment(row0, n_ctx_rows, seq):
    return jnp.where(row0 < n_ctx_rows, 0, 1 + (row0 - n_ctx_rows) // seq)


def _mod_kernel(c_ref, w_ref, b_ref, o_ref):
    c = c_ref[...]
    s = c * jax.nn.sigmoid(c)
    o_ref[0] = jnp.dot(s, w_ref[0], precision=HIGHEST, preferred_element_type=F32) + b_ref[0]


def _modulation(cvec, w_mod, b_mod):
    depth = w_mod.shape[0]
    n6 = w_mod.shape[2]
    tn = 1024
    return pl.pallas_call(
        _mod_kernel,
        grid=(depth, n6 // tn),
        in_specs=[
            pl.BlockSpec((8, D_MODEL), lambda l, j: (0, 0)),
            pl.BlockSpec((1, D_MODEL, tn), lambda l, j: (l, 0, j)),
            pl.BlockSpec((1, 1, tn), lambda l, j: (l, 0, j)),
        ],
        out_specs=pl.BlockSpec((1, 8, tn), lambda l, j: (l, 0, j)),
        out_shape=jax.ShapeDtypeStruct((depth, 8, n6), F32),
        compiler_params=pltpu.CompilerParams(
            dimension_semantics=("arbitrary", "arbitrary"), vmem_limit_bytes=VMEM_LIMIT),
    )(cvec, w_mod, b_mod.reshape(depth, 1, n6))


def _inproj_kernel(x_ref, mod_ref, w_ref, o_ref):
    m = mod_ref[0]
    h = x_ref[...] * (1.0 + m[1:2]) + m[0:1]
    o_ref[...] = jnp.dot(h.astype(BF16), w_ref[...], preferred_element_type=F32)


def _in_projection(x, mod_l, w_in_b, n_ctx_rows, seq):
    t = x.shape[0]
    tm = ROW_TILE
    return pl.pallas_call(
        _inproj_kernel,
        grid=(t // tm,),
        in_specs=[
            pl.BlockSpec((tm, D_MODEL), lambda i: (i, 0)),
            pl.BlockSpec((1, 6, D_MODEL), lambda i: (_segment(i * tm, n_ctx_rows, seq), 0, 0)),
            pl.BlockSpec((D_MODEL, N_IN_PAD), lambda i: (0, 0)),
        ],
        out_specs=pl.BlockSpec((tm, N_IN_PAD), lambda i: (i, 0)),
        out_shape=jax.ShapeDtypeStruct((t, N_IN_PAD), F32),
        compiler_params=pltpu.CompilerParams(
            dimension_semantics=("arbitrary",), vmem_limit_bytes=VMEM_LIMIT),
    )(x, mod_l, w_in_b)


def _sgu_kernel(uv_ref, g_ref, b_ref, w_ref, bias_ref, o_ref):
    tm = uv_ref.shape[0]
    u = jax.nn.gelu(uv_ref[:, :A_WIDTH])
    v = _layer_norm(jax.nn.gelu(uv_ref[:, A_WIDTH:]), g_ref[...], b_ref[...]).astype(BF16)
    head = lax.broadcasted_iota(jnp.int32, (1, A_WIDTH), 1) // A_HEAD_DIM
    for c in range(tm // CHUNK):
        rows = slice(c * CHUNK, (c + 1) * CHUNK)
        vc = v[rows]
        acc = bias_ref[...]
        for h in range(A_HEADS):
            r = jnp.dot(w_ref[h], vc, preferred_element_type=F32)
            acc = acc + jnp.where(head == h, r, 0.0)
        o_ref[rows, :] = (u[rows] * acc).astype(BF16)


def _spatial_gate(proj, ln_g, ln_b, w_s, b_s):
    t = proj.shape[0]
    tm = ROW_TILE
    bias = jnp.repeat(b_s.T, A_HEAD_DIM, axis=1)
    return pl.pallas_call(
        _sgu_kernel,
        grid=(t // tm,),
        in_specs=[
            pl.BlockSpec((tm, 2 * A_WIDTH), lambda i: (i, 0)),
            pl.BlockSpec((1, A_WIDTH), lambda i: (0, 0)),
            pl.BlockSpec((1, A_WIDTH), lambda i: (0, 0)),
            pl.BlockSpec((A_HEADS, CHUNK, CHUNK), lambda i: (0, 0, 0)),
            pl.BlockSpec((CHUNK, A_WIDTH), lambda i: (0, 0)),
        ],
        out_specs=pl.BlockSpec((tm, A_WIDTH), lambda i: (i, 0)),
        out_shape=jax.ShapeDtypeStruct((t, A_WIDTH), BF16),
        compiler_params=pltpu.CompilerParams(
            dimension_semantics=("arbitrary",), vmem_limit_bytes=VMEM_LIMIT),
    )(proj, ln_g.reshape(1, -1), ln_b.reshape(1, -1), w_s.astype(BF16), bias)


_NT = (((1,), (1,)), ((), ()))
_TN = (((0,), (0,)), ((), ()))


def _gla_kernel(backward, *refs):
    if backward:
        q_ref, k_ref, v_ref, gl_ref, gup_ref, gb_ref, of_ref, g_ref, ng_ref, o_ref, s_ref = refs
    else:
        q_ref, k_ref, v_ref, gl_ref, gup_ref, gb_ref, o_ref, s_ref = refs
    c_len = GLA_CHUNK

    @pl.when(pl.program_id(1) == 0)
    def _():
        s_ref[...] = jnp.zeros_like(s_ref)

    row = lax.broadcasted_iota(jnp.int32, (c_len, c_len), 0)
    col = lax.broadcasted_iota(jnp.int32, (c_len, c_len), 1)
    if backward:
        tri = (col >= row).astype(F32)
        keep = col > row
        i_last, i_mid = 0, c_len - 1 - c_len // 2
        lo = GATE_RANK
    else:
        tri = (col <= row).astype(F32)
        keep = col <= row
        i_last, i_mid = c_len - 1, c_len // 2
        lo = 0
    n_chunks = q_ref.shape[0] // c_len
    order = range(n_chunks - 1, -1, -1) if backward else range(n_chunks)
    for c in order:
        rows = slice(c * c_len, (c + 1) * c_len)
        gl = gl_ref[rows, lo:lo + GATE_RANK]
        z = jnp.dot(gl, gup_ref[...], precision=HIGHEST, preferred_element_type=F32) + gb_ref[...]
        la = jax.nn.log_sigmoid(z) / GATE_TAU
        b = jnp.dot(tri, la, precision=HIGHEST, preferred_element_type=F32)
        b_last = b[i_last:i_last + 1]
        b_mid = b[i_mid:i_mid + 1]
        q = q_ref[rows, :] * (B_DK ** -0.5)
        k = k_ref[rows, :]
        q_mid = (q * jnp.exp(b - b_mid)).astype(BF16)
        k_mid = (k * jnp.exp(b_mid - b)).astype(BF16)
        q_in = (q * jnp.exp(b)).astype(BF16)
        k_out = (k * jnp.exp(b_last - b)).astype(BF16)
        decay = jnp.exp(b_last)
        for h in range(B_HEADS):
            hk = slice(h * B_DK, (h + 1) * B_DK)
            hv = slice(h * B_DV, (h + 1) * B_DV)
            vh = v_ref[rows, hv].astype(BF16)
            s_prev = s_ref[h]
            sc = lax.dot_general(q_mid[:, hk], k_mid[:, hk], _NT, preferred_element_type=F32)
            sc = jnp.where(keep, sc, 0.0).astype(BF16)
            o = (jnp.dot(sc, vh, preferred_element_type=F32)
                 + lax.dot_general(q_in[:, hk], s_prev.astype(BF16), _NT, preferred_element_type=F32))
            s_ref[h] = s_prev * decay[:, hk] + lax.dot_general(vh, k_out[:, hk], _TN,
                                                               preferred_element_type=F32)
            if backward:
                o = o + of_ref[rows, hv]
                o = o * lax.rsqrt(jnp.mean(o * o, axis=-1, keepdims=True) + LN_EPS)
                gate = g_ref[rows, hv]
                o = o * ng_ref[:, hv] * (gate * jax.nn.sigmoid(gate))
                o_ref[rows, hv] = o.astype(BF16)
            else:
                o_ref[rows, hv] = o


def _gla_block(backward, bsz, nctx_blk, nlat_blk, b, j):
    if backward:
        ctx_i = b * nctx_blk + (nctx_blk - 1 - j)
        lat_i = bsz * nctx_blk + b * nlat_blk + (nlat_blk - 1 - (j - nctx_blk))
    else:
        ctx_i = b * nctx_blk + j
        lat_i = bsz * nctx_blk + b * nlat_blk + (j - nctx_blk)
    return jnp.where(j < nctx_blk, ctx_i, lat_i)


def _gla_sweep(backward, proj, gate_up, gate_b, bsz, ctx_len, seq, o_fwd=None, norm_g=None):
    t = proj.shape[0]
    r = SEQ_TILE
    nctx_blk, nlat_blk = ctx_len // r, seq // r
    blk = functools.partial(_gla_block, backward, bsz, nctx_blk, nlat_blk)
    d = 1 if backward else 0
    in_specs = [
        pl.BlockSpec((r, B_QK), lambda b, j: (blk(b, j), 2)),
        pl.BlockSpec((r, B_QK), lambda b, j: (blk(b, j), 3)),
        pl.BlockSpec((r, B_WIDTH), lambda b, j: (blk(b, j), 2)),
        pl.BlockSpec((r, 128), lambda b, j: (blk(b, j), COL_GL // 128)),
        pl.BlockSpec((GATE_RANK, B_QK), lambda b, j: (0, 0)),
        pl.BlockSpec((1, B_QK), lambda b, j: (0, 0)),
    ]
    args = [proj, proj, proj, proj, gate_up[d], gate_b[d].reshape(1, -1)]
    if backward:
        in_specs += [
            pl.BlockSpec((r, B_WIDTH), lambda b, j: (blk(b, j), 0)),
            pl.BlockSpec((r, B_WIDTH), lambda b, j: (blk(b, j), 3)),
            pl.BlockSpec((1, B_WIDTH), lambda b, j: (0, 0)),
        ]
        args += [o_fwd, proj, norm_g.reshape(1, -1)]
    return pl.pallas_call(
        functools.partial(_gla_kernel, backward),
        grid=(bsz, nctx_blk + nlat_blk),
        in_specs=in_specs,
        out_specs=pl.BlockSpec((r, B_WIDTH), lambda b, j: (blk(b, j), 0)),
        out_shape=jax.ShapeDtypeStruct((t, B_WIDTH), BF16 if backward else F32),
        scratch_shapes=[pltpu.VMEM((B_HEADS, B_DV, B_DK), F32)],
        compiler_params=pltpu.CompilerParams(
            dimension_semantics=("arbitrary", "arbitrary"), vmem_limit_bytes=VMEM_LIMIT),
    )(*args)


def _s5_matrices(lam_re, lam_im, log_dt, b_re, b_im, c_re, c_im):
    tc = S5_T
    lam = lax.complex(lam_re.astype(F32), lam_im.astype(F32))
    dt = jnp.exp(log_dt.astype(F32))
    bm = lax.complex(b_re.astype(F32), b_im.astype(F32))
    cm = lax.complex(c_re.astype(F32), c_im.astype(F32))
    ldt = lam * dt[..., None]
    lam_bar = jnp.exp(ldt)
    b_bar = ((lam_bar - 1.0) / lam)[..., None] * bm
    steps = jnp.arange(tc + 1, dtype=F32)
    pw = jnp.exp(ldt[:, None] * steps[None, :, None, None])
    kern = jnp.real(jnp.einsum('dgip,dtgp,dgpj->dgtij', cm, pw[:, :tc], b_bar))
    s_i = jnp.arange(tc)[:, None]
    t_i = jnp.arange(tc)[None, :]
    g_n = lam.shape[1]
    hh = S5_GROUP_CH

    def toeplitz(kd, lag, ok):
        m = kd[:, jnp.clip(lag, 0, tc - 1)]
        m = jnp.where(ok[None, :, :, None, None], m, 0.0)
        return m.transpose(0, 1, 4, 2, 3).reshape(g_n, tc * hh, tc * hh)

    mt = toeplitz(kern[0], t_i - s_i, t_i >= s_i) + toeplitz(kern[1], s_i - t_i, s_i >= t_i)

    def state_in(d, powers):
        w = pw[d][powers][:, :, :, None] * b_bar[d][None]
        w = w.transpose(1, 0, 3, 2).reshape(g_n, tc * hh, S5_STATE)
        re, im = jnp.real(w), jnp.imag(w)
        return jnp.concatenate([re, im, im, re], axis=-1)

    qt = jnp.concatenate([state_in(0, tc - 1 - jnp.arange(tc)), state_in(1, jnp.arange(tc))], axis=-1)

    def state_out(d, powers):
        w = cm[d][:, None] * pw[d][powers].transpose(1, 0, 2)[:, :, None, :]
        w = w.reshape(g_n, tc * hh, S5_STATE).transpose(0, 2, 1)
        return jnp.concatenate([jnp.real(w), -jnp.imag(w)], axis=1)

    pt = jnp.concatenate([state_out(0, 1 + jnp.arange(tc)), state_out(1, tc - jnp.arange(tc))], axis=1)

    a = pw[:, tc]
    ar, ai = jnp.real(a), jnp.imag(a)
    a1 = jnp.concatenate([ar, ar], axis=-1).reshape(2, -1)
    a2 = jnp.concatenate([-ai, ai], axis=-1).reshape(2, -1)
    a3 = jnp.concatenate([ai, -ai], axis=-1).reshape(2, -1)
    ac = jnp.stack([a1, a2, a3], axis=1)
    return mt.astype(BF16), qt.astype(BF16), pt.astype(BF16), ac.reshape(2, 3, 1, -1)


def _s5_kernel(uc_ref, ul_ref, mt_ref, qt_ref, pt_ref, ac_ref, yc_ref, yl_ref, ef, esf, eb, esb):
    nq = uc_ref.shape[0]
    nc_ctx, nc_lat = uc_ref.shape[1], ul_ref.shape[1]
    n = nc_ctx + nc_lat
    w = 2 * S5_STATE
    for gi in range(nq):
        lanes = slice(gi * w, (gi + 1) * w)
        rc = jnp.dot(uc_ref[gi], qt_ref[gi], preferred_element_type=F32)
        rl = jnp.dot(ul_ref[gi], qt_ref[gi], preferred_element_type=F32)
        ef[0:nc_ctx, lanes] = rc[:, 0:w]
        esf[0:nc_ctx, lanes] = rc[:, w:2 * w]
        ef[nc_ctx:n, lanes] = rl[:, 0:w]
        esf[nc_ctx:n, lanes] = rl[:, w:2 * w]
        eb[0:nc_lat, lanes] = rl[:, 2 * w:3 * w]
        esb[0:nc_lat, lanes] = rl[:, 3 * w:4 * w]
        eb[nc_lat:n, lanes] = rc[:, 2 * w:3 * w]
        esb[nc_lat:n, lanes] = rc[:, 3 * w:4 * w]

    a1f, a2f, a3f = ac_ref[0, 0], ac_ref[0, 1], ac_ref[0, 2]
    a1b, a2b, a3b = ac_ref[1, 0], ac_ref[1, 1], ac_ref[1, 2]

    def body(i, carry):
        hf, hsf, hb, hsb = carry
        rf = pl.ds(i, 1)
        rb = pl.ds(n - 1 - i, 1)
        e_f, es_f = ef[rf, :], esf[rf, :]
        e_b, es_b = eb[rb, :], esb[rb, :]
        ef[rf, :] = hf
        eb[rb, :] = hb
        return (a1f * hf + a2f * hsf + e_f, a1f * hsf + a3f * hf + es_f,
                a1b * hb + a2b * hsb + e_b, a1b * hsb + a3b * hb + es_b)

    zero = jnp.zeros((1, nq * w), F32)
    lax.fori_loop(0, n, body, (zero, zero, zero, zero))

    for gi in range(nq):
        lanes = slice(gi * w, (gi + 1) * w)
        hc = jnp.concatenate([ef[0:nc_ctx, lanes], eb[nc_lat:n, lanes]], axis=1).astype(BF16)
        hl = jnp.concatenate([ef[nc_ctx:n, lanes], eb[0:nc_lat, lanes]], axis=1).astype(BF16)
        yc_ref[gi] = (jnp.dot(uc_ref[gi], mt_ref[gi], preferred_element_type=F32)
                      + jnp.dot(hc, pt_ref[gi], preferred_element_type=F32))
        yl_ref[gi] = (jnp.dot(ul_ref[gi], mt_ref[gi], preferred_element_type=F32)
                      + jnp.dot(hl, pt_ref[gi], preferred_element_type=F32))


def _s5_scan(proj, mats, bsz, ctx_len, seq):
    mt, qt, pt, ac = mats
    g_n, tc, hh = S5_GROUPS, S5_T, S5_GROUP_CH
    n_ctx_rows = bsz * ctx_len
    u = proj[:, COL_S5:COL_S5 + C_WIDTH].astype(BF16)

    def to_rows(part):
        r = part.shape[0]
        return part.reshape(r // tc, tc, g_n, hh).transpose(2, 0, 1, 3).reshape(g_n, r // tc, tc * hh)

    uc, ul = to_rows(u[:n_ctx_rows]), to_rows(u[n_ctx_rows:])
    nc_ctx, nc_lat = ctx_len // tc, seq // tc
    nq = S5_QUARTER
    wq = nq * 2 * S5_STATE
    n = nc_ctx + nc_lat
    yc, yl = pl.pallas_call(
        _s5_kernel,
        grid=(bsz, g_n // nq),
        in_specs=[
            pl.BlockSpec((nq, nc_ctx, tc * hh), lambda b, qi: (qi, b, 0)),
            pl.BlockSpec((nq, nc_lat, tc * hh), lambda b, qi: (qi, b, 0)),
            pl.BlockSpec((nq, tc * hh, tc * hh), lambda b, qi: (qi, 0, 0)),
            pl.BlockSpec((nq, tc * hh, 8 * S5_STATE), lambda b, qi: (qi, 0, 0)),
            pl.BlockSpec((nq, 4 * S5_STATE, tc * hh), lambda b, qi: (qi, 0, 0)),
            pl.BlockSpec((2, 3, 1, wq), lambda b, qi: (0, 0, 0, qi)),
        ],
        out_specs=[
            pl.BlockSpec((nq, nc_ctx, tc * hh), lambda b, qi: (qi, b, 0)),
            pl.BlockSpec((nq, nc_lat, tc * hh), lambda b, qi: (qi, b, 0)),
        ],
        out_shape=[
            jax.ShapeDtypeStruct((g_n, bsz * nc_ctx, tc * hh), F32),
            jax.ShapeDtypeStruct((g_n, bsz * nc_lat, tc * hh), F32),
        ],
        scratch_shapes=[pltpu.VMEM((n, wq), F32) for _ in range(4)],
        compiler_params=pltpu.CompilerParams(
            dimension_semantics=("arbitrary", "arbitrary"), vmem_limit_bytes=VMEM_LIMIT),
    )(uc, ul, mt, qt, pt, ac)

    def from_rows(y):
        r = y.shape[1]
        return y.reshape(g_n, r, tc, hh).transpose(1, 2, 0, 3).reshape(r * tc, g_n * hh)

    return jnp.concatenate([from_rows(yc), from_rows(yl)], axis=0)


def _outproj_kernel(alpha, a_ref, gla_ref, y_ref, u_ref, x_ref, mod_ref, d_ref, gw_ref, gb_ref, wo_ref,
                    lng_ref, lnb_ref, rwh_ref, rwl_ref, rb_ref, x1_ref, h2_ref, lg_ref):
    m = mod_ref[0]
    y = jax.nn.gelu(y_ref[...] + d_ref[...] * u_ref[...])
    s5 = y * jax.nn.sigmoid(jnp.dot(y.astype(BF16), gw_ref[...], preferred_element_type=F32) + gb_ref[...])
    mix = (jnp.dot(a_ref[...], wo_ref[0:A_WIDTH, :], preferred_element_type=F32)
           + jnp.dot(gla_ref[...], wo_ref[A_WIDTH:A_WIDTH + B_WIDTH, :], preferred_element_type=F32)
           + jnp.dot(s5.astype(BF16), wo_ref[A_WIDTH + B_WIDTH:, :], preferred_element_type=F32))
    x1 = _layer_norm(alpha * x_ref[...] + m[2:3] * mix, lng_ref[...], lnb_ref[...])
    x1_ref[...] = x1
    h2 = x1 * (1.0 + m[4:5]) + m[3:4]
    _store_row_tiles(h2_ref, (), h2)
    h_hi = h2.astype(BF16)
    h_lo = (h2 - h_hi.astype(F32)).astype(BF16)
    lg_ref[...] = (jnp.dot(h_hi, rwh_ref[...], preferred_element_type=F32)
                   + jnp.dot(h_lo, rwh_ref[...], preferred_element_type=F32)
                   + jnp.dot(h_hi, rwl_ref[...], preferred_element_type=F32) + rb_ref[...])


def _out_projection(alpha, a_out, gla_out, y_s5, proj, x, mod_l, s5_d, glu_w, glu_b, w_out_b, ln_g, ln_b,
                    router_w, router_b, n_ctx_rows, seq):
    t = x.shape[0]
    tm = ROW_TILE
    rw = jnp.zeros((D_MODEL, 128), F32).at[:, :N_EXPERTS].set(router_w)
    rw_hi = rw.astype(BF16)
    rw_lo = (rw - rw_hi.astype(F32)).astype(BF16)
    rb = jnp.zeros((1, 128), F32).at[0, :N_EXPERTS].set(router_b)
    row = lambda i: (i, 0)
    fixed = lambda i: (0, 0)
    return pl.pallas_call(
        functools.partial(_outproj_kernel, alpha),
        grid=(t // tm,),
        in_specs=[
            pl.BlockSpec((tm, A_WIDTH), row),
            pl.BlockSpec((tm, B_WIDTH), row),
            pl.BlockSpec((tm, C_WIDTH), row),
            pl.BlockSpec((tm, C_WIDTH), lambda i: (i, COL_S5 // C_WIDTH)),
            pl.BlockSpec((tm, D_MODEL), row),
            pl.BlockSpec((1, 6, D_MODEL), lambda i: (_segment(i * tm, n_ctx_rows, seq), 0, 0)),
            pl.BlockSpec((1, C_WIDTH), fixed),
            pl.BlockSpec((C_WIDTH, C_WIDTH), fixed),
            pl.BlockSpec((1, C_WIDTH), fixed),
            pl.BlockSpec((D_MODEL, D_MODEL), fixed),
            pl.BlockSpec((1, D_MODEL), fixed),
            pl.BlockSpec((1, D_MODEL), fixed),
            pl.BlockSpec((D_MODEL, 128), fixed),
            pl.BlockSpec((D_MODEL, 128), fixed),
            pl.BlockSpec((1, 128), fixed),
        ],
        out_specs=[
            pl.BlockSpec((tm, D_MODEL), row),
            pl.BlockSpec((tm * ROW_SUB, LANES), row),
            pl.BlockSpec((tm, 128), row),
        ],
        out_shape=[
            jax.ShapeDtypeStruct((t, D_MODEL), F32),
            jax.ShapeDtypeStruct((t * ROW_SUB, LANES), F32),
            jax.ShapeDtypeStruct((t, 128), F32),
        ],
        compiler_params=pltpu.CompilerParams(
            dimension_semantics=("arbitrary",), vmem_limit_bytes=VMEM_LIMIT),
    )(a_out, gla_out, y_s5, proj, x, mod_l, s5_d.reshape(1, -1), glu_w.astype(BF16), glu_b.reshape(1, -1),
      w_out_b, ln_g.reshape(1, -1), ln_b.reshape(1, -1), rw_hi, rw_lo, rb)


def _routing(logits, n_blocks):
    t = logits.shape[0]
    top_logit, top_idx = lax.top_k(logits, TOP_K)
    gates = jax.nn.softmax(top_logit, axis=-1)
    member = (top_idx[:, :, None] == jnp.arange(N_EXPERTS, dtype=jnp.int32)[None, None, :]).any(axis=1)
    member = member.astype(jnp.int32)
    csum = jnp.cumsum(member, axis=0)
    counts = csum[-1]
    rank = jnp.take_along_axis(csum - member, top_idx, axis=1)
    padded = (counts + MOE_BLOCK - 1) // MOE_BLOCK * MOE_BLOCK
    padded_end = jnp.cumsum(padded)
    dest = (padded_end - padded)[top_idx] + rank
    tok = jnp.broadcast_to(jnp.arange(t, dtype=jnp.int32)[:, None], (t, TOP_K))
    slot_token = jnp.zeros((n_blocks * MOE_BLOCK,), jnp.int32).at[dest.reshape(-1)].set(
        tok.reshape(-1), unique_indices=True, mode='promise_in_bounds')
    block_expert = jnp.minimum(
        jnp.searchsorted(padded_end, jnp.arange(n_blocks, dtype=jnp.int32) * MOE_BLOCK, side='right'),
        N_EXPERTS - 1).astype(jnp.int32)
    n_valid = (padded_end[-1] // MOE_BLOCK).astype(jnp.int32).reshape(1)
    return gates, dest.astype(jnp.int32), slot_token, block_expert, n_valid


def _moe_kernel(be_ref, nv_ref, tok_ref, tokn_ref, h_ref, wu_ref, bu_ref, wd_ref, bd_ref, o_ref,
                xbuf, wu_b, wd_b, sem):
    i = pl.program_id(0)
    n_valid = nv_ref[0]
    slot = i % 2

    def row_copy(idx_ref, s, r):
        return pltpu.make_async_copy(_row_tile(h_ref, (), idx_ref[0, 0, r]), _row_tile(xbuf, (s,), r), sem.at[s])

    def wait_block(s):
        pltpu.make_async_copy(xbuf.at[s], xbuf.at[s], sem.at[s]).wait()

    @pl.when(i == 0)
    def _():
        def issue(r, carry):
            row_copy(tok_ref, 0, r).start()
            return carry
        lax.fori_loop(0, MOE_BLOCK, issue, 0, unroll=8)

    @pl.when(i < n_valid)
    def _():
        first = jnp.logical_or(i == 0, be_ref[i] != be_ref[jnp.maximum(i - 1, 0)])

        @pl.when(first)
        def _():
            rows = 64

            def cast(r, carry):
                rs = pl.ds(pl.multiple_of(r * rows, rows), rows)
                wu_b[rs, :] = wu_ref[0, 0, rs, :].astype(BF16)
                wd_b[rs, :] = wd_ref[0, 0, rs, :].astype(BF16)
                return carry
            lax.fori_loop(0, D_MODEL // rows, cast, 0)

        wait_block(slot)
        x = _load_row_tiles(xbuf, (slot,), MOE_BLOCK).astype(BF16)
        for r in range(MOE_BLOCK):
            row_copy(tokn_ref, 1 - slot, r).start()
        acc = jnp.zeros((MOE_BLOCK, D_MODEL), F32) + bd_ref[0, 0]
        cw = 512
        for jc in range(D_EXPERT // cw):
            cg = slice(jc * cw, (jc + 1) * cw)
            cl = slice(D_EXPERT + jc * cw, D_EXPERT + (jc + 1) * cw)
            ug = jnp.dot(x, wu_b[:, cg], preferred_element_type=F32) + bu_ref[0, 0, :, cg]
            ul = jnp.dot(x, wu_b[:, cl], preferred_element_type=F32) + bu_ref[0, 0, :, cl]
            xg = jnp.minimum(ug, SWIGLU_LIMIT)
            xl = jnp.clip(ul, -SWIGLU_LIMIT, SWIGLU_LIMIT)
            act = xg * jax.nn.sigmoid(SWIGLU_ALPHA * xg) * (xl + 1.0)
            acc = acc + jnp.dot(act.astype(BF16), wd_b[cg, :], preferred_element_type=F32)
        _store_row_tiles(o_ref, (), acc)

        @pl.when(i == n_valid - 1)
        def _():
            wait_block(1 - slot)

    @pl.when(i >= n_valid)
    def _():
        o_ref[...] = jnp.zeros_like(o_ref)


def _moe_experts(layer, h3, slot_token, block_expert, n_valid, w_up, b_up, w_down, b_down):
    n_blocks = block_expert.shape[0]
    depth = w_up.shape[0]
    tok3 = slot_token.reshape(n_blocks, 1, MOE_BLOCK)
    last = n_blocks - 1
    grid_spec = pltpu.PrefetchScalarGridSpec(
        num_scalar_prefetch=2,
        grid=(n_blocks,),
        in_specs=[
            pl.BlockSpec((1, 1, MOE_BLOCK), lambda i, be, nv: (i, 0, 0), memory_space=pltpu.SMEM),
            pl.BlockSpec((1, 1, MOE_BLOCK), lambda i, be, nv: (jnp.minimum(i + 1, last), 0, 0),
                         memory_space=pltpu.SMEM),
            pl.BlockSpec(memory_space=pl.ANY),
            pl.BlockSpec((1, 1, D_MODEL, 2 * D_EXPERT), lambda i, be, nv: (layer, be[i], 0, 0)),
            pl.BlockSpec((1, 1, 1, 2 * D_EXPERT), lambda i, be, nv: (layer, be[i], 0, 0)),
            pl.BlockSpec((1, 1, D_EXPERT, D_MODEL), lambda i, be, nv: (layer, be[i], 0, 0)),
            pl.BlockSpec((1, 1, 1, D_MODEL), lambda i, be, nv: (layer, be[i], 0, 0)),
        ],
        out_specs=pl.BlockSpec((MOE_BLOCK * ROW_SUB, LANES), lambda i, be, nv: (i, 0)),
        scratch_shapes=[
            pltpu.VMEM((2, MOE_BLOCK * ROW_SUB, LANES), F32),
            pltpu.VMEM((D_MODEL, 2 * D_EXPERT), BF16),
            pltpu.VMEM((D_EXPERT, D_MODEL), BF16),
            pltpu.SemaphoreType.DMA((2,)),
        ],
    )
    return pl.pallas_call(
        _moe_kernel,
        grid_spec=grid_spec,
        out_shape=jax.ShapeDtypeStruct((n_blocks * MOE_BLOCK * ROW_SUB, LANES), F32),
        compiler_params=pltpu.CompilerParams(
            dimension_semantics=("arbitrary",), vmem_limit_bytes=VMEM_LIMIT),
    )(block_expert, n_valid, tok3, tok3, h3, w_up, b_up.reshape(depth, N_EXPERTS, 1, -1), w_down,
      b_down.reshape(depth, N_EXPERTS, 1, -1))


def _combine_kernel(alpha, dst_ref, dstn_ref, gate_ref, y_ref, x_ref, mod_ref, lng_ref, lnb_ref, o_ref,
                    buf, sem):
    i = pl.program_id(0)
    n = pl.num_programs(0)
    tm = COMBINE_TILE
    slot = i % 2

    def gather(idx_ref, s):
        def issue(r, carry):
            for kk in range(TOP_K):
                pltpu.make_async_copy(_row_tile(y_ref, (), idx_ref[0, 0, r * TOP_K + kk]),
                                      _row_tile(buf, (s, kk), r), sem.at[s]).start()
            return carry
        lax.fori_loop(0, tm, issue, 0, unroll=4)

    @pl.when(i == 0)
    def _():
        gather(dst_ref, 0)

    @pl.when(i + 1 < n)
    def _():
        gather(dstn_ref, 1 - slot)

    pltpu.make_async_copy(buf.at[slot], buf.at[slot], sem.at[slot]).wait()
    gates = gate_ref[...]
    f = jnp.zeros((tm, D_MODEL), F32)
    for kk in range(TOP_K):
        f = f + gates[:, kk:kk + 1] * _load_row_tiles(buf, (slot, kk), tm)
    m = mod_ref[0]
    o_ref[...] = _layer_norm(alpha * x_ref[...] + m[5:6] * f, lng_ref[...], lnb_ref[...])


def _moe_combine(alpha, dest, gates, ys3, x1, mod_l, ln_g, ln_b, n_ctx_rows, seq):
    t = x1.shape[0]
    tm = COMBINE_TILE
    nt = t // tm
    dst3 = dest.reshape(nt, 1, tm * TOP_K)
    return pl.pallas_call(
        functools.partial(_combine_kernel, alpha),
        grid=(nt,),
        in_specs=[
            pl.BlockSpec((1, 1, tm * TOP_K), lambda i: (i, 0, 0), memory_space=pltpu.SMEM),
            pl.BlockSpec((1, 1, tm * TOP_K), lambda i: (jnp.minimum(i + 1, nt - 1), 0, 0),
                         memory_space=pltpu.SMEM),
            pl.BlockSpec((tm, TOP_K), lambda i: (i, 0)),
            pl.BlockSpec(memory_space=pl.ANY),
            pl.BlockSpec((tm, D_MODEL), lambda i: (i, 0)),
            pl.BlockSpec((1, 6, D_MODEL), lambda i: (_segment(i * tm, n_ctx_rows, seq), 0, 0)),
            pl.BlockSpec((1, D_MODEL), lambda i: (0, 0)),
            pl.BlockSpec((1, D_MODEL), lambda i: (0, 0)),
        ],
        out_specs=pl.BlockSpec((tm, D_MODEL), lambda i: (i, 0)),
        out_shape=jax.ShapeDtypeStruct((t, D_MODEL), F32),
        scratch_shapes=[
            pltpu.VMEM((2, TOP_K, tm * ROW_SUB, LANES), F32),
            pltpu.SemaphoreType.DMA((2,)),
        ],
        compiler_params=pltpu.CompilerParams(
            dimension_semantics=("arbitrary",), vmem_limit_bytes=VMEM_LIMIT),
    )(dst3, dst3, gates, ys3, x1, mod_l, ln_g.reshape(1, -1), ln_b.reshape(1, -1))


def kernel(x, c, ctx, c_ctx, w_mod, b_mod, w_in, sgu_ln_g, sgu_ln_b, sgu_w, sgu_b, gla_gate_up, gla_gate_b,
           gla_norm_g, s5_lam_re, s5_lam_im, s5_log_dt, s5_b_re, s5_b_im, s5_c_re, s5_c_im, s5_d, s5_glu_w,
           s5_glu_b, w_out, ln_g, ln_b, router_w, router_b, w_up, b_up, w_down, b_down):
    bsz, seq, d = x.shape
    ctx_len = ctx.shape[1]
    depth = w_in.shape[0]
    alpha = float((2 * depth) ** 0.25)
    n_ctx_rows = bsz * ctx_len
    t = n_ctx_rows + bsz * seq
    assert d == D_MODEL and bsz + 1 <= 8
    assert n_ctx_rows % ROW_TILE == 0 and seq % ROW_TILE == 0
    assert ctx_len % SEQ_TILE == 0 and seq % SEQ_TILE == 0 and t % COMBINE_TILE == 0

    xa = jnp.concatenate([ctx.reshape(n_ctx_rows, d), x.reshape(bsz * seq, d)], axis=0)
    cvec = jnp.zeros((8, d), F32).at[0].set(c_ctx).at[1:1 + bsz].set(c)
    mod = _modulation(cvec, w_mod, b_mod).reshape(depth, 8, 6, d)

    o = np.cumsum((0, A_WIDTH, A_WIDTH, B_QK, B_QK, B_WIDTH, B_WIDTH, 2 * GATE_RANK, C_WIDTH))
    w_in_r = jnp.concatenate(
        [w_in[:, :, o[0]:o[6]], w_in[:, :, o[7]:o[8]], w_in[:, :, o[6]:o[7]],
         jnp.zeros((depth, d, N_IN_PAD - int(o[8])), w_in.dtype)], axis=-1).astype(BF16)
    w_out_b = w_out.astype(BF16)

    n_assign = t * TOP_K
    n_blocks = -(-(n_assign + N_EXPERTS * (MOE_BLOCK - 1)) // MOE_BLOCK)

    for l in range(depth):
        mod_l = mod[l]
        proj = _in_projection(xa, mod_l, w_in_r[l], n_ctx_rows, seq)
        a_out = _spatial_gate(proj, sgu_ln_g[l], sgu_ln_b[l], sgu_w[l], sgu_b[l])
        o_fwd = _gla_sweep(False, proj, gla_gate_up[l], gla_gate_b[l], bsz, ctx_len, seq)
        gla_out = _gla_sweep(True, proj, gla_gate_up[l], gla_gate_b[l], bsz, ctx_len, seq,
                             o_fwd=o_fwd, norm_g=gla_norm_g[l])
        mats = _s5_matrices(s5_lam_re[l], s5_lam_im[l], s5_log_dt[l], s5_b_re[l], s5_b_im[l],
                            s5_c_re[l], s5_c_im[l])
        y_s5 = _s5_scan(proj, mats, bsz, ctx_len, seq)
        x1, h2, logits = _out_projection(alpha, a_out, gla_out, y_s5, proj, xa, mod_l, s5_d[l], s5_glu_w[l],
                                         s5_glu_b[l], w_out_b[l], ln_g[l, 0], ln_b[l, 0], router_w[l],
                                         router_b[l], n_ctx_rows, seq)
        gates, dest, slot_token, block_expert, n_valid = _routing(logits[:, :N_EXPERTS], n_blocks)
        ys = _moe_experts(l, h2, slot_token, block_expert, n_valid, w_up, b_up, w_down, b_down)
        xa = _moe_combine(alpha, dest, gates, ys, x1, mod_l, ln_g[l, 1], ln_b[l, 1], n_ctx_rows, seq)
    return xa[n_ctx_rows:].reshape(bsz, seq, d)
```

```python
import functools

import numpy as np
import jax
import jax.numpy as jnp
from jax import lax
from jax.experimental import pallas as pl
from jax.experimental.pallas import tpu as pltpu

F32 = jnp.float32
BF16 = jnp.bfloat16
HIGHEST = lax.Precision.HIGHEST

D_MODEL = 1024
CHUNK = 128
A_HEADS = 4
A_HEAD_DIM = 64
A_WIDTH = 256
B_HEADS = 4
B_DK = 64
B_DV = 128
B_QK = 256
B_WIDTH = 512
GATE_RANK = 16
GATE_TAU = 16.0
GLA_CHUNK = 64
S5_GROUPS = 16
S5_GROUP_CH = 16
S5_STATE = 64
C_WIDTH = 256
N_EXPERTS = 32
TOP_K = 4
D_EXPERT = 1024
SWIGLU_LIMIT = 7.0
SWIGLU_ALPHA = 1.702
LN_EPS = 1e-5

N_IN_PAD = 2432
COL_GL = 2304
COL_S5 = 2048

ROW_TILE = 512
SEQ_TILE = 256
S5_T = 16
S5_QUARTER = 4
MOE_BLOCK = 512
COMBINE_TILE = 256
DISPATCH_TILE = 512
VMEM_LIMIT = 56 * 1024 * 1024


def _layer_norm(x, g, b):
    mu = jnp.mean(x, axis=-1, keepdims=True)
    xc = x - mu
    var = jnp.mean(xc * xc, axis=-1, keepdims=True)
    return xc * lax.rsqrt(var + LN_EPS) * g + b


LANES = 128
ROW_SUB = D_MODEL // LANES


def _store_row_tiles(ref, lead, val):
    n = val.shape[0]
    for j in range(ROW_SUB):
        ref[lead + (pl.ds(j, n, stride=ROW_SUB), slice(None))] = val[:, j * LANES:(j + 1) * LANES]


def _load_row_tiles(ref, lead, n):
    return jnp.concatenate(
        [ref[lead + (pl.ds(j, n, stride=ROW_SUB), slice(None))] for j in range(ROW_SUB)], axis=1)


def _row_tile(ref, lead, r):
    return ref.at[lead + (pl.ds(pl.multiple_of(r * ROW_SUB, ROW_SUB), ROW_SUB), slice(None))]


def _segment(row0, n_ctx_rows, seq):
    return jnp.where(row0 < n_ctx_rows, 0, 1 + (row0 - n_ctx_rows) // seq)


def _mod_kernel(c_ref, w_ref, b_ref, o_ref):
    c = c_ref[...]
    s = c * jax.nn.sigmoid(c)
    o_ref[0] = jnp.dot(s, w_ref[0], precision=HIGHEST, preferred_element_type=F32) + b_ref[0]


def _modulation(cvec, w_mod, b_mod):
    depth = w_mod.shape[0]
    n6 = w_mod.shape[2]
    tn = 1024
    return pl.pallas_call(
        _mod_kernel,
        grid=(depth, n6 // tn),
        in_specs=[
            pl.BlockSpec((8, D_MODEL), lambda l, j: (0, 0)),
            pl.BlockSpec((1, D_MODEL, tn), lambda l, j: (l, 0, j)),
            pl.BlockSpec((1, 1, tn), lambda l, j: (l, 0, j)),
        ],
        out_specs=pl.BlockSpec((1, 8, tn), lambda l, j: (l, 0, j)),
        out_shape=jax.ShapeDtypeStruct((depth, 8, n6), F32),
        compiler_params=pltpu.CompilerParams(
            dimension_semantics=("arbitrary", "arbitrary"), vmem_limit_bytes=VMEM_LIMIT),
    )(cvec, w_mod, b_mod.reshape(depth, 1, n6))


def _inproj_kernel(x_ref, mod_ref, w_ref, o_ref):
    m = mod_ref[0]
    h = x_ref[...] * (1.0 + m[1:2]) + m[0:1]
    o_ref[...] = jnp.dot(h.astype(BF16), w_ref[...], preferred_element_type=F32)


def _in_projection(x, mod_l, w_in_b, n_ctx_rows, seq):
    t = x.shape[0]
    tm = ROW_TILE
    return pl.pallas_call(
        _inproj_kernel,
        grid=(t // tm,),
        in_specs=[
            pl.BlockSpec((tm, D_MODEL), lambda i: (i, 0)),
            pl.BlockSpec((1, 6, D_MODEL), lambda i: (_segment(i * tm, n_ctx_rows, seq), 0, 0)),
            pl.BlockSpec((D_MODEL, N_IN_PAD), lambda i: (0, 0)),
        ],
        out_specs=pl.BlockSpec((tm, N_IN_PAD), lambda i: (i, 0)),
        out_shape=jax.ShapeDtypeStruct((t, N_IN_PAD), F32),
        compiler_params=pltpu.CompilerParams(
            dimension_semantics=("arbitrary",), vmem_limit_bytes=VMEM_LIMIT),
    )(x, mod_l, w_in_b)


def _sgu_kernel(uv_ref, g_ref, b_ref, w_ref, bias_ref, o_ref):
    tm = uv_ref.shape[0]
    u = jax.nn.gelu(uv_ref[:, :A_WIDTH])
    v = _layer_norm(jax.nn.gelu(uv_ref[:, A_WIDTH:]), g_ref[...], b_ref[...]).astype(BF16)
    head = lax.broadcasted_iota(jnp.int32, (1, A_WIDTH), 1) // A_HEAD_DIM
    for c in range(tm // CHUNK):
        rows = slice(c * CHUNK, (c + 1) * CHUNK)
        vc = v[rows]
        acc = bias_ref[...]
        for h in range(A_HEADS):
            r = jnp.dot(w_ref[h], vc, preferred_element_type=F32)
            acc = acc + jnp.where(head == h, r, 0.0)
        o_ref[rows, :] = (u[rows] * acc).astype(BF16)


def _spatial_gate(proj, ln_g, ln_b, w_s, b_s):
    t = proj.shape[0]
    tm = ROW_TILE
    bias = jnp.repeat(b_s.T, A_HEAD_DIM, axis=1)
    return pl.pallas_call(
        _sgu_kernel,
        grid=(t // tm,),
        in_specs=[
            pl.BlockSpec((tm, 2 * A_WIDTH), lambda i: (i, 0)),
            pl.BlockSpec((1, A_WIDTH), lambda i: (0, 0)),
            pl.BlockSpec((1, A_WIDTH), lambda i: (0, 0)),
            pl.BlockSpec((A_HEADS, CHUNK, CHUNK), lambda i: (0, 0, 0)),
            pl.BlockSpec((CHUNK, A_WIDTH), lambda i: (0, 0)),
        ],
        out_specs=pl.BlockSpec((tm, A_WIDTH), lambda i: (i, 0)),
        out_shape=jax.ShapeDtypeStruct((t, A_WIDTH), BF16),
        compiler_params=pltpu.CompilerParams(
            dimension_semantics=("arbitrary",), vmem_limit_bytes=VMEM_LIMIT),
    )(proj, ln_g.reshape(1, -1), ln_b.reshape(1, -1), w_s.astype(BF16), bias)


_NT = (((1,), (1,)), ((), ()))
_TN = (((0,), (0,)), ((), ()))


def _gla_kernel(backward, *refs):
    if backward:
        q_ref, k_ref, v_ref, gl_ref, gup_ref, gb_ref, of_ref, g_ref, ng_ref, o_ref, s_ref = refs
    else:
        q_ref, k_ref, v_ref, gl_ref, gup_ref, gb_ref, o_ref, s_ref = refs
    c_len = GLA_CHUNK
    n_rows = q_ref.shape[0]
    n_chunks = n_rows // c_len

    @pl.when(pl.program_id(1) == 0)
    def _():
        s_ref[...] = jnp.zeros_like(s_ref)

    row = lax.broadcasted_iota(jnp.int32, (n_rows, n_rows), 0)
    col = lax.broadcasted_iota(jnp.int32, (n_rows, n_rows), 1)
    same = (row // c_len) == (col // c_len)
    if backward:
        tri = jnp.where(same & (col >= row), 1.0, 0.0).astype(BF16)
        keep = same & (col > row)
        i_last, i_mid = 0, c_len - 1 - c_len // 2
        lo = GATE_RANK
    else:
        tri = jnp.where(same & (col <= row), 1.0, 0.0).astype(BF16)
        keep = same & (col <= row)
        i_last, i_mid = c_len - 1, c_len // 2
        lo = 0

    z = jnp.dot(gl_ref[:, lo:lo + GATE_RANK], gup_ref[...], precision=HIGHEST,
                preferred_element_type=F32) + gb_ref[...]
    la = jax.nn.log_sigmoid(z) / GATE_TAU
    l1 = la.astype(BF16)
    r1 = la - l1.astype(F32)
    l2 = r1.astype(BF16)
    l3 = (r1 - l2.astype(F32)).astype(BF16)
    b = (jnp.dot(tri, l1, preferred_element_type=F32) + jnp.dot(tri, l2, preferred_element_type=F32)
         + jnp.dot(tri, l3, preferred_element_type=F32))

    def per_chunk(index):
        return jnp.concatenate(
            [jnp.broadcast_to(b[c * c_len + index:c * c_len + index + 1], (c_len, B_QK))
             for c in range(n_chunks)], axis=0)

    b_last = per_chunk(i_last)
    b_mid = per_chunk(i_mid)
    q = q_ref[...] * (B_DK ** -0.5)
    k = k_ref[...]
    q_mid = (q * jnp.exp(b - b_mid)).astype(BF16)
    k_mid = (k * jnp.exp(b_mid - b)).astype(BF16)
    q_in = (q * jnp.exp(b)).astype(BF16)
    k_out = (k * jnp.exp(b_last - b)).astype(BF16)
    order = range(n_chunks - 1, -1, -1) if backward else range(n_chunks)
    for h in range(B_HEADS):
        hk = slice(h * B_DK, (h + 1) * B_DK)
        hv = slice(h * B_DV, (h + 1) * B_DV)
        vh = v_ref[:, hv].astype(BF16)
        sc = lax.dot_general(q_mid[:, hk], k_mid[:, hk], _NT, preferred_element_type=F32)
        sc = jnp.where(keep, sc, 0.0).astype(BF16)
        o_intra = jnp.dot(sc, vh, preferred_element_type=F32)
        state = s_ref[h]
        for c in order:
            rows = slice(c * c_len, (c + 1) * c_len)
            o = o_intra[rows] + lax.dot_general(q_in[rows, hk], state.astype(BF16), _NT,
                                                preferred_element_type=F32)
            decay = jnp.exp(b[c * c_len + i_last:c * c_len + i_last + 1, hk])
            state = state * decay + lax.dot_general(vh[rows], k_out[rows, hk], _TN,
                                                    preferred_element_type=F32)
            if backward:
                o = o + of_ref[rows, hv]
                o = o * lax.rsqrt(jnp.mean(o * o, axis=-1, keepdims=True) + LN_EPS)
                gate = g_ref[rows, hv]
                o = o * ng_ref[:, hv] * (gate * jax.nn.sigmoid(gate))
                o_ref[rows, hv] = o.astype(BF16)
            else:
                o_ref[rows, hv] = o
        s_ref[h] = state


def _gla_block(backward, bsz, nctx_blk, nlat_blk, b, j):
    if backward:
        ctx_i = b * nctx_blk + (nctx_blk - 1 - j)
        lat_i = bsz * nctx_blk + b * nlat_blk + (nlat_blk - 1 - (j - nctx_blk))
    else:
        ctx_i = b * nctx_blk + j
        lat_i = bsz * nctx_blk + b * nlat_blk + (j - nctx_blk)
    return jnp.where(j < nctx_blk, ctx_i, lat_i)


def _gla_sweep(backward, proj, gate_up, gate_b, bsz, ctx_len, seq, o_fwd=None, norm_g=None):
    t = proj.shape[0]
    r = SEQ_TILE
    nctx_blk, nlat_blk = ctx_len // r, seq // r
    blk = functools.partial(_gla_block, backward, bsz, nctx_blk, nlat_blk)
    d = 1 if backward else 0
    in_specs = [
        pl.BlockSpec((r, B_QK), lambda b, j: (blk(b, j), 2)),
        pl.BlockSpec((r, B_QK), lambda b, j: (blk(b, j), 3)),
        pl.BlockSpec((r, B_WIDTH), lambda b, j: (blk(b, j), 2)),
        pl.BlockSpec((r, 128), lambda b, j: (blk(b, j), COL_GL // 128)),
        pl.BlockSpec((GATE_RANK, B_QK), lambda b, j: (0, 0)),
        pl.BlockSpec((1, B_QK), lambda b, j: (0, 0)),
    ]
    args = [proj, proj, proj, proj, gate_up[d], gate_b[d].reshape(1, -1)]
    if backward:
        in_specs += [
            pl.BlockSpec((r, B_WIDTH), lambda b, j: (blk(b, j), 0)),
            pl.BlockSpec((r, B_WIDTH), lambda b, j: (blk(b, j), 3)),
            pl.BlockSpec((1, B_WIDTH), lambda b, j: (0, 0)),
        ]
        args += [o_fwd, proj, norm_g.reshape(1, -1)]
    return pl.pallas_call(
        functools.partial(_gla_kernel, backward),
        grid=(bsz, nctx_blk + nlat_blk),
        in_specs=in_specs,
        out_specs=pl.BlockSpec((r, B_WIDTH), lambda b, j: (blk(b, j), 0)),
        out_shape=jax.ShapeDtypeStruct((t, B_WIDTH), BF16 if backward else F32),
        scratch_shapes=[pltpu.VMEM((B_HEADS, B_DV, B_DK), F32)],
        compiler_params=pltpu.CompilerParams(
            dimension_semantics=("arbitrary", "arbitrary"), vmem_limit_bytes=VMEM_LIMIT),
    )(*args)


def _s5_matrices(lam_re, lam_im, log_dt, b_re, b_im, c_re, c_im):
    tc = S5_T
    lam = lax.complex(lam_re.astype(F32), lam_im.astype(F32))
    dt = jnp.exp(log_dt.astype(F32))
    bm = lax.complex(b_re.astype(F32), b_im.astype(F32))
    cm = lax.complex(c_re.astype(F32), c_im.astype(F32))
    ldt = lam * dt[..., None]
    lam_bar = jnp.exp(ldt)
    b_bar = ((lam_bar - 1.0) / lam)[..., None] * bm
    steps = jnp.arange(tc + 1, dtype=F32)
    pw = jnp.exp(ldt[:, None] * steps[None, :, None, None])
    kern = jnp.real(jnp.einsum('dgip,dtgp,dgpj->dgtij', cm, pw[:, :tc], b_bar))
    s_i = jnp.arange(tc)[:, None]
    t_i = jnp.arange(tc)[None, :]
    g_n = lam.shape[1]
    hh = S5_GROUP_CH

    def toeplitz(kd, lag, ok):
        m = kd[:, jnp.clip(lag, 0, tc - 1)]
        m = jnp.where(ok[None, :, :, None, None], m, 0.0)
        return m.transpose(0, 1, 4, 2, 3).reshape(g_n, tc * hh, tc * hh)

    mt = toeplitz(kern[0], t_i - s_i, t_i >= s_i) + toeplitz(kern[1], s_i - t_i, s_i >= t_i)

    def state_in(d, powers):
        w = pw[d][powers][:, :, :, None] * b_bar[d][None]
        w = w.transpose(1, 0, 3, 2).reshape(g_n, tc * hh, S5_STATE)
        re, im = jnp.real(w), jnp.imag(w)
        return jnp.concatenate([re, im, im, re], axis=-1)

    qt = jnp.concatenate([state_in(0, tc - 1 - jnp.arange(tc)), state_in(1, jnp.arange(tc))], axis=-1)

    def state_out(d, powers):
        w = cm[d][:, None] * pw[d][powers].transpose(1, 0, 2)[:, :, None, :]
        w = w.reshape(g_n, tc * hh, S5_STATE).transpose(0, 2, 1)
        return jnp.concatenate([jnp.real(w), -jnp.imag(w)], axis=1)

    pt = jnp.concatenate([state_out(0, 1 + jnp.arange(tc)), state_out(1, tc - jnp.arange(tc))], axis=1)

    a = pw[:, tc]
    ar, ai = jnp.real(a), jnp.imag(a)
    a1 = jnp.concatenate([ar, ar], axis=-1).reshape(2, -1)
    a2 = jnp.concatenate([-ai, ai], axis=-1).reshape(2, -1)
    a3 = jnp.concatenate([ai, -ai], axis=-1).reshape(2, -1)
    ac = jnp.stack([a1, a2, a3], axis=1)
    return mt.astype(BF16), qt.astype(BF16), pt.astype(BF16), ac.reshape(2, 3, 1, -1)


def _s5_kernel(uc_ref, ul_ref, mt_ref, qt_ref, pt_ref, ac_ref, yc_ref, yl_ref, ef, esf, eb, esb):
    nq = uc_ref.shape[0]
    nc_ctx, nc_lat = uc_ref.shape[1], ul_ref.shape[1]
    n = nc_ctx + nc_lat
    w = 2 * S5_STATE
    for gi in range(nq):
        lanes = slice(gi * w, (gi + 1) * w)
        rc = jnp.dot(uc_ref[gi], qt_ref[gi], preferred_element_type=F32)
        rl = jnp.dot(ul_ref[gi], qt_ref[gi], preferred_element_type=F32)
        ef[0:nc_ctx, lanes] = rc[:, 0:w]
        esf[0:nc_ctx, lanes] = rc[:, w:2 * w]
        ef[nc_ctx:n, lanes] = rl[:, 0:w]
        esf[nc_ctx:n, lanes] = rl[:, w:2 * w]
        eb[0:nc_lat, lanes] = rl[:, 2 * w:3 * w]
        esb[0:nc_lat, lanes] = rl[:, 3 * w:4 * w]
        eb[nc_lat:n, lanes] = rc[:, 2 * w:3 * w]
        esb[nc_lat:n, lanes] = rc[:, 3 * w:4 * w]

    a1f, a2f, a3f = ac_ref[0, 0], ac_ref[0, 1], ac_ref[0, 2]
    a1b, a2b, a3b = ac_ref[1, 0], ac_ref[1, 1], ac_ref[1, 2]

    def body(i, carry):
        hf, hsf, hb, hsb = carry
        rf = pl.ds(i, 1)
        rb = pl.ds(n - 1 - i, 1)
        e_f, es_f = ef[rf, :], esf[rf, :]
        e_b, es_b = eb[rb, :], esb[rb, :]
        ef[rf, :] = hf
        eb[rb, :] = hb
        return (a1f * hf + a2f * hsf + e_f, a1f * hsf + a3f * hf + es_f,
                a1b * hb + a2b * hsb + e_b, a1b * hsb + a3b * hb + es_b)

    zero = jnp.zeros((1, nq * w), F32)
    lax.fori_loop(0, n, body, (zero, zero, zero, zero))

    for gi in range(nq):
        lanes = slice(gi * w, (gi + 1) * w)
        hc = jnp.concatenate([ef[0:nc_ctx, lanes], eb[nc_lat:n, lanes]], axis=1).astype(BF16)
        hl = jnp.concatenate([ef[nc_ctx:n, lanes], eb[0:nc_lat, lanes]], axis=1).astype(BF16)
        yc_ref[gi] = (jnp.dot(uc_ref[gi], mt_ref[gi], preferred_element_type=F32)
                      + jnp.dot(hc, pt_ref[gi], preferred_element_type=F32))
        yl_ref[gi] = (jnp.dot(ul_ref[gi], mt_ref[gi], preferred_element_type=F32)
                      + jnp.dot(hl, pt_ref[gi], preferred_element_type=F32))


def _s5_scan(proj, mats, bsz, ctx_len, seq):
    mt, qt, pt, ac = mats
    g_n, tc, hh = S5_GROUPS, S5_T, S5_GROUP_CH
    n_ctx_rows = bsz * ctx_len
    u = proj[:, COL_S5:COL_S5 + C_WIDTH].astype(BF16)

    def to_rows(part):
        r = part.shape[0]
        return part.reshape(r // tc, tc, g_n, hh).transpose(2, 0, 1, 3).reshape(g_n, r // tc, tc * hh)

    uc, ul = to_rows(u[:n_ctx_rows]), to_rows(u[n_ctx_rows:])
    nc_ctx, nc_lat = ctx_len // tc, seq // tc
    nq = S5_QUARTER
    wq = nq * 2 * S5_STATE
    n = nc_ctx + nc_lat
    yc, yl = pl.pallas_call(
        _s5_kernel,
        grid=(bsz, g_n // nq),
        in_specs=[
            pl.BlockSpec((nq, nc_ctx, tc * hh), lambda b, qi: (qi, b, 0)),
            pl.BlockSpec((nq, nc_lat, tc * hh), lambda b, qi: (qi, b, 0)),
            pl.BlockSpec((nq, tc * hh, tc * hh), lambda b, qi: (qi, 0, 0)),
            pl.BlockSpec((nq, tc * hh, 8 * S5_STATE), lambda b, qi: (qi, 0, 0)),
            pl.BlockSpec((nq, 4 * S5_STATE, tc * hh), lambda b, qi: (qi, 0, 0)),
            pl.BlockSpec((2, 3, 1, wq), lambda b, qi: (0, 0, 0, qi)),
        ],
        out_specs=[
            pl.BlockSpec((nq, nc_ctx, tc * hh), lambda b, qi: (qi, b, 0)),
            pl.BlockSpec((nq, nc_lat, tc * hh), lambda b, qi: (qi, b, 0)),
        ],
        out_shape=[
            jax.ShapeDtypeStruct((g_n, bsz * nc_ctx, tc * hh), F32),
            jax.ShapeDtypeStruct((g_n, bsz * nc_lat, tc * hh), F32),
        ],
        scratch_shapes=[pltpu.VMEM((n, wq), F32) for _ in range(4)],
        compiler_params=pltpu.CompilerParams(
            dimension_semantics=("arbitrary", "arbitrary"), vmem_limit_bytes=VMEM_LIMIT),
    )(uc, ul, mt, qt, pt, ac)

    def from_rows(y):
        r = y.shape[1]
        return y.reshape(g_n, r, tc, hh).transpose(1, 2, 0, 3).reshape(r * tc, g_n * hh)

    return jnp.concatenate([from_rows(yc), from_rows(yl)], axis=0)


def _outproj_kernel(alpha, a_ref, gla_ref, y_ref, u_ref, x_ref, mod_ref, d_ref, gw_ref, gb_ref, wo_ref,
                    lng_ref, lnb_ref, rwh_ref, rwl_ref, rb_ref, x1_ref, h2_ref, lg_ref):
    m = mod_ref[0]
    y = jax.nn.gelu(y_ref[...] + d_ref[...] * u_ref[...])
    s5 = y * jax.nn.sigmoid(jnp.dot(y.astype(BF16), gw_ref[...], preferred_element_type=F32) + gb_ref[...])
    mix = (jnp.dot(a_ref[...], wo_ref[0:A_WIDTH, :], preferred_element_type=F32)
           + jnp.dot(gla_ref[...], wo_ref[A_WIDTH:A_WIDTH + B_WIDTH, :], preferred_element_type=F32)
           + jnp.dot(s5.astype(BF16), wo_ref[A_WIDTH + B_WIDTH:, :], preferred_element_type=F32))
    x1 = _layer_norm(alpha * x_ref[...] + m[2:3] * mix, lng_ref[...], lnb_ref[...])
    x1_ref[...] = x1
    h2 = x1 * (1.0 + m[4:5]) + m[3:4]
    _store_row_tiles(h2_ref, (), h2)
    h_hi = h2.astype(BF16)
    h_lo = (h2 - h_hi.astype(F32)).astype(BF16)
    lg_ref[...] = (jnp.dot(h_hi, rwh_ref[...], preferred_element_type=F32)
                   + jnp.dot(h_lo, rwh_ref[...], preferred_element_type=F32)
                   + jnp.dot(h_hi, rwl_ref[...], preferred_element_type=F32) + rb_ref[...])


def _out_projection(alpha, a_out, gla_out, y_s5, proj, x, mod_l, s5_d, glu_w, glu_b, w_out_b, ln_g, ln_b,
                    router_w, router_b, n_ctx_rows, seq):
    t = x.shape[0]
    tm = ROW_TILE
    rw = jnp.zeros((D_MODEL, 128), F32).at[:, :N_EXPERTS].set(router_w)
    rw_hi = rw.astype(BF16)
    rw_lo = (rw - rw_hi.astype(F32)).astype(BF16)
    rb = jnp.zeros((1, 128), F32).at[0, :N_EXPERTS].set(router_b)
    row = lambda i: (i, 0)
    fixed = lambda i: (0, 0)
    return pl.pallas_call(
        functools.partial(_outproj_kernel, alpha),
        grid=(t // tm,),
        in_specs=[
            pl.BlockSpec((tm, A_WIDTH), row),
            pl.BlockSpec((tm, B_WIDTH), row),
            pl.BlockSpec((tm, C_WIDTH), row),
            pl.BlockSpec((tm, C_WIDTH), lambda i: (i, COL_S5 // C_WIDTH)),
            pl.BlockSpec((tm, D_MODEL), row),
            pl.BlockSpec((1, 6, D_MODEL), lambda i: (_segment(i * tm, n_ctx_rows, seq), 0, 0)),
            pl.BlockSpec((1, C_WIDTH), fixed),
            pl.BlockSpec((C_WIDTH, C_WIDTH), fixed),
            pl.BlockSpec((1, C_WIDTH), fixed),
            pl.BlockSpec((D_MODEL, D_MODEL), fixed),
            pl.BlockSpec((1, D_MODEL), fixed),
            pl.BlockSpec((1, D_MODEL), fixed),
            pl.BlockSpec((D_MODEL, 128), fixed),
            pl.BlockSpec((D_MODEL, 128), fixed),
            pl.BlockSpec((1, 128), fixed),
        ],
        out_specs=[
            pl.BlockSpec((tm, D_MODEL), row),
            pl.BlockSpec((tm * ROW_SUB, LANES), row),
            pl.BlockSpec((tm, 128), row),
        ],
        out_shape=[
            jax.ShapeDtypeStruct((t, D_MODEL), F32),
            jax.ShapeDtypeStruct((t * ROW_SUB, LANES), F32),
            jax.ShapeDtypeStruct((t, 128), F32),
        ],
        compiler_params=pltpu.CompilerParams(
            dimension_semantics=("arbitrary",), vmem_limit_bytes=VMEM_LIMIT),
    )(a_out, gla_out, y_s5, proj, x, mod_l, s5_d.reshape(1, -1), glu_w.astype(BF16), glu_b.reshape(1, -1),
      w_out_b, ln_g.reshape(1, -1), ln_b.reshape(1, -1), rw_hi, rw_lo, rb)


def _route_kernel(lg_ref, idx_ref, gate_ref, rank_ref, cnt_ref, base, before):
    tm = lg_ref.shape[0]

    @pl.when(pl.program_id(0) == 0)
    def _():
        base[...] = jnp.zeros_like(base)
        r = lax.broadcasted_iota(jnp.int32, (tm, tm), 0)
        c = lax.broadcasted_iota(jnp.int32, (tm, tm), 1)
        before[...] = jnp.where(r < c, 1.0, 0.0).astype(BF16)

    logit = jnp.transpose(lg_ref[...])[:N_EXPERTS]
    eid = lax.broadcasted_iota(jnp.int32, (N_EXPERTS, tm), 0)
    vals, hots = [], []
    work = logit
    for kk in range(TOP_K):
        m = jnp.max(work, axis=0, keepdims=True)
        ix = jnp.min(jnp.where(work == m, eid, N_EXPERTS), axis=0, keepdims=True)
        hot = eid == ix
        idx_ref[kk:kk + 1, :] = ix
        vals.append(m)
        hots.append(hot)
        work = jnp.where(hot, -jnp.inf, work)
    ex = [jnp.exp(v - vals[0]) for v in vals]
    den = ex[0] + ex[1] + ex[2] + ex[3]
    member = jnp.zeros((N_EXPERTS, tm), F32)
    for kk in range(TOP_K):
        gate_ref[kk:kk + 1, :] = ex[kk] / den
        member = member + jnp.where(hots[kk], 1.0, 0.0)
    earlier = jnp.dot(member.astype(BF16), before[...], preferred_element_type=F32) + base[...]
    for kk in range(TOP_K):
        rank_ref[kk:kk + 1, :] = jnp.sum(jnp.where(hots[kk], earlier, 0.0), axis=0,
                                         keepdims=True).astype(jnp.int32)
    total = base[...] + jnp.sum(member, axis=1, keepdims=True)
    base[...] = total
    cnt_ref[...] = jnp.broadcast_to(total, cnt_ref.shape)


def _routing(logits, n_blocks):
    t = logits.shape[0]
    tm = ROW_TILE
    idx, gates, rank, cnt = pl.pallas_call(
        _route_kernel,
        grid=(t // tm,),
        in_specs=[pl.BlockSpec((tm, LANES), lambda i: (i, 0))],
        out_specs=[
            pl.BlockSpec((TOP_K, tm), lambda i: (0, i)),
            pl.BlockSpec((TOP_K, tm), lambda i: (0, i)),
            pl.BlockSpec((TOP_K, tm), lambda i: (0, i)),
            pl.BlockSpec((N_EXPERTS, LANES), lambda i: (0, 0)),
        ],
        out_shape=[
            jax.ShapeDtypeStruct((TOP_K, t), jnp.int32),
            jax.ShapeDtypeStruct((TOP_K, t), F32),
            jax.ShapeDtypeStruct((TOP_K, t), jnp.int32),
            jax.ShapeDtypeStruct((N_EXPERTS, LANES), F32),
        ],
        scratch_shapes=[pltpu.VMEM((N_EXPERTS, 1), F32), pltpu.VMEM((tm, tm), BF16)],
        compiler_params=pltpu.CompilerParams(
            dimension_semantics=("arbitrary",), vmem_limit_bytes=VMEM_LIMIT),
    )(logits)
    counts = cnt[:, 0].astype(jnp.int32)
    padded = (counts + MOE_BLOCK - 1) // MOE_BLOCK * MOE_BLOCK
    padded_end = jnp.cumsum(padded)
    padded_start = padded_end - padded
    dest = (padded_start[idx] + rank).T
    block_expert = jnp.minimum(
        jnp.searchsorted(padded_end, jnp.arange(n_blocks, dtype=jnp.int32) * MOE_BLOCK, side='right'),
        N_EXPERTS - 1).astype(jnp.int32)
    n_valid = (padded_end[-1] // MOE_BLOCK).astype(jnp.int32).reshape(1)
    pad_lo = (padded_start + counts).astype(jnp.int32)
    return gates.T, dest.astype(jnp.int32), block_expert, n_valid, pad_lo, padded_end.astype(jnp.int32)


def _dispatch_kernel(lo_ref, hi_ref, dst_ref, h_ref, o_ref, zbuf, sem, zsem):
    i = pl.program_id(0)
    tm = DISPATCH_TILE

    def issue(r, carry):
        for kk in range(TOP_K):
            pltpu.make_async_copy(_row_tile(h_ref, (), r), _row_tile(o_ref, (), dst_ref[0, 0, r * TOP_K + kk]),
                                  sem).start()
        return carry
    lax.fori_loop(0, tm, issue, 0, unroll=4)

    @pl.when(i == pl.num_programs(0) - 1)
    def _():
        zbuf[...] = jnp.zeros_like(zbuf)
        for e in range(N_EXPERTS):
            def fill(s, carry):
                pltpu.make_async_copy(zbuf, _row_tile(o_ref, (), s), zsem).start()
                return carry
            lax.fori_loop(lo_ref[e], hi_ref[e], fill, 0)
        for e in range(N_EXPERTS):
            def drain(s, carry):
                pltpu.make_async_copy(zbuf, zbuf, zsem).wait()
                return carry
            lax.fori_loop(lo_ref[e], hi_ref[e], drain, 0)

    for _ in range(TOP_K):
        pltpu.make_async_copy(h_ref, h_ref, sem).wait()


def _moe_dispatch(h2t, dest, pad_lo, pad_hi, n_blocks):
    t = dest.shape[0]
    tm = DISPATCH_TILE
    nt = t // tm
    grid_spec = pltpu.PrefetchScalarGridSpec(
        num_scalar_prefetch=2,
        grid=(nt,),
        in_specs=[
            pl.BlockSpec((1, 1, tm * TOP_K), lambda i, lo, hi: (i, 0, 0), memory_space=pltpu.SMEM),
            pl.BlockSpec((tm * ROW_SUB, LANES), lambda i, lo, hi: (i, 0)),
        ],
        out_specs=pl.BlockSpec(memory_space=pl.ANY),
        scratch_shapes=[
            pltpu.VMEM((ROW_SUB, LANES), F32),
            pltpu.SemaphoreType.DMA(()),
            pltpu.SemaphoreType.DMA(()),
        ],
    )
    return pl.pallas_call(
        _dispatch_kernel,
        grid_spec=grid_spec,
        out_shape=jax.ShapeDtypeStruct((n_blocks * MOE_BLOCK * ROW_SUB, LANES), F32),
        compiler_params=pltpu.CompilerParams(
            dimension_semantics=("arbitrary",), vmem_limit_bytes=VMEM_LIMIT),
    )(pad_lo, pad_hi, dest.reshape(nt, 1, tm * TOP_K), h2t)


def _moe_kernel(be_ref, nv_ref, x_ref, wu_ref, bu_ref, wd_ref, bd_ref, o_ref, wu_b, wd_b):
    i = pl.program_id(0)
    n_valid = nv_ref[0]

    @pl.when(i < n_valid)
    def _():
        first = jnp.logical_or(i == 0, be_ref[i] != be_ref[jnp.maximum(i - 1, 0)])

        @pl.when(first)
        def _():
            rows = 64

            def cast(r, carry):
                rs = pl.ds(pl.multiple_of(r * rows, rows), rows)
                wu_b[rs, :] = wu_ref[0, 0, rs, :].astype(BF16)
                wd_b[rs, :] = wd_ref[0, 0, rs, :].astype(BF16)
                return carry
            lax.fori_loop(0, D_MODEL // rows, cast, 0)

        x = _load_row_tiles(x_ref, (), MOE_BLOCK).astype(BF16)
        acc = jnp.zeros((MOE_BLOCK, D_MODEL), F32) + bd_ref[0, 0]
        cw = 512
        for jc in range(D_EXPERT // cw):
            cg = slice(jc * cw, (jc + 1) * cw)
            cl = slice(D_EXPERT + jc * cw, D_EXPERT + (jc + 1) * cw)
            ug = jnp.dot(x, wu_b[:, cg], preferred_element_type=F32) + bu_ref[0, 0, :, cg]
            ul = jnp.dot(x, wu_b[:, cl], preferred_element_type=F32) + bu_ref[0, 0, :, cl]
            xg = jnp.minimum(ug, SWIGLU_LIMIT)
            xl = jnp.clip(ul, -SWIGLU_LIMIT, SWIGLU_LIMIT)
            act = xg * jax.nn.sigmoid(SWIGLU_ALPHA * xg) * (xl + 1.0)
            acc = acc + jnp.dot(act.astype(BF16), wd_b[cg, :], preferred_element_type=F32)
        _store_row_tiles(o_ref, (), acc)

    @pl.when(i >= n_valid)
    def _():
        o_ref[...] = jnp.zeros_like(o_ref)


def _moe_experts(layer, xs, block_expert, n_valid, w_up, b_up, w_down, b_down):
    n_blocks = block_expert.shape[0]
    depth = w_up.shape[0]
    grid_spec = pltpu.PrefetchScalarGridSpec(
        num_scalar_prefetch=2,
        grid=(n_blocks,),
        in_specs=[
            pl.BlockSpec((MOE_BLOCK * ROW_SUB, LANES),
                         lambda i, be, nv: (jnp.minimum(i, jnp.maximum(nv[0] - 1, 0)), 0)),
            pl.BlockSpec((1, 1, D_MODEL, 2 * D_EXPERT), lambda i, be, nv: (layer, be[i], 0, 0)),
            pl.BlockSpec((1, 1, 1, 2 * D_EXPERT), lambda i, be, nv: (layer, be[i], 0, 0)),
            pl.BlockSpec((1, 1, D_EXPERT, D_MODEL), lambda i, be, nv: (layer, be[i], 0, 0)),
            pl.BlockSpec((1, 1, 1, D_MODEL), lambda i, be, nv: (layer, be[i], 0, 0)),
        ],
        out_specs=pl.BlockSpec((MOE_BLOCK * ROW_SUB, LANES), lambda i, be, nv: (i, 0)),
        scratch_shapes=[
            pltpu.VMEM((D_MODEL, 2 * D_EXPERT), BF16),
            pltpu.VMEM((D_EXPERT, D_MODEL), BF16),
        ],
    )
    return pl.pallas_call(
        _moe_kernel,
        grid_spec=grid_spec,
        out_shape=jax.ShapeDtypeStruct((n_blocks * MOE_BLOCK * ROW_SUB, LANES), F32),
        compiler_params=pltpu.CompilerParams(
            dimension_semantics=("arbitrary",), vmem_limit_bytes=VMEM_LIMIT),
    )(block_expert, n_valid, xs, w_up, b_up.reshape(depth, N_EXPERTS, 1, -1), w_down,
      b_down.reshape(depth, N_EXPERTS, 1, -1))


def _combine_kernel(alpha, dst_ref, dstn_ref, gate_ref, y_ref, x_ref, mod_ref, lng_ref, lnb_ref, o_ref,
                    buf, sem):
    i = pl.program_id(0)
    n = pl.num_programs(0)
    tm = COMBINE_TILE
    slot = i % 2

    def gather(idx_ref, s):
        def issue(r, carry):
            for kk in range(TOP_K):
                pltpu.make_async_copy(_row_tile(y_ref, (), idx_ref[0, 0, r * TOP_K + kk]),
                                      _row_tile(buf, (s, kk), r), sem.at[s]).start()
            return carry
        lax.fori_loop(0, tm, issue, 0, unroll=4)

    @pl.when(i == 0)
    def _():
        gather(dst_ref, 0)

    @pl.when(i + 1 < n)
    def _():
        gather(dstn_ref, 1 - slot)

    pltpu.make_async_copy(buf.at[slot], buf.at[slot], sem.at[slot]).wait()
    gates = gate_ref[...]
    f = jnp.zeros((tm, D_MODEL), F32)
    for kk in range(TOP_K):
        f = f + gates[:, kk:kk + 1] * _load_row_tiles(buf, (slot, kk), tm)
    m = mod_ref[0]
    o_ref[...] = _layer_norm(alpha * x_ref[...] + m[5:6] * f, lng_ref[...], lnb_ref[...])


def _moe_combine(alpha, dest, gates, ys3, x1, mod_l, ln_g, ln_b, n_ctx_rows, seq):
    t = x1.shape[0]
    tm = COMBINE_TILE
    nt = t // tm
    dst3 = dest.reshape(nt, 1, tm * TOP_K)
    return pl.pallas_call(
        functools.partial(_combine_kernel, alpha),
        grid=(nt,),
        in_specs=[
            pl.BlockSpec((1, 1, tm * TOP_K), lambda i: (i, 0, 0), memory_space=pltpu.SMEM),
            pl.BlockSpec((1, 1, tm * TOP_K), lambda i: (jnp.minimum(i + 1, nt - 1), 0, 0),
                         memory_space=pltpu.SMEM),
            pl.BlockSpec((tm, TOP_K), lambda i: (i, 0)),
            pl.BlockSpec(memory_space=pl.ANY),
            pl.BlockSpec((tm, D_MODEL), lambda i: (i, 0)),
            pl.BlockSpec((1, 6, D_MODEL), lambda i: (_segment(i * tm, n_ctx_rows, seq), 0, 0)),
            pl.BlockSpec((1, D_MODEL), lambda i: (0, 0)),
            pl.BlockSpec((1, D_MODEL), lambda i: (0, 0)),
        ],
        out_specs=pl.BlockSpec((tm, D_MODEL), lambda i: (i, 0)),
        out_shape=jax.ShapeDtypeStruct((t, D_MODEL), F32),
        scratch_shapes=[
            pltpu.VMEM((2, TOP_K, tm * ROW_SUB, LANES), F32),
            pltpu.SemaphoreType.DMA((2,)),
        ],
        compiler_params=pltpu.CompilerParams(
            dimension_semantics=("arbitrary",), vmem_limit_bytes=VMEM_LIMIT),
    )(dst3, dst3, gates, ys3, x1, mod_l, ln_g.reshape(1, -1), ln_b.reshape(1, -1))


def kernel(x, c, ctx, c_ctx, w_mod, b_mod, w_in, sgu_ln_g, sgu_ln_b, sgu_w, sgu_b, gla_gate_up, gla_gate_b,
           gla_norm_g, s5_lam_re, s5_lam_im, s5_log_dt, s5_b_re, s5_b_im, s5_c_re, s5_c_im, s5_d, s5_glu_w,
           s5_glu_b, w_out, ln_g, ln_b, router_w, router_b, w_up, b_up, w_down, b_down):
    bsz, seq, d = x.shape
    ctx_len = ctx.shape[1]
    depth = w_in.shape[0]
    alpha = float((2 * depth) ** 0.25)
    n_ctx_rows = bsz * ctx_len
    t = n_ctx_rows + bsz * seq
    assert d == D_MODEL and bsz + 1 <= 8
    assert n_ctx_rows % ROW_TILE == 0 and seq % ROW_TILE == 0
    assert ctx_len % SEQ_TILE == 0 and seq % SEQ_TILE == 0 and t % COMBINE_TILE == 0 and t % DISPATCH_TILE == 0

    xa = jnp.concatenate([ctx.reshape(n_ctx_rows, d), x.reshape(bsz * seq, d)], axis=0)
    cvec = jnp.zeros((8, d), F32).at[0].set(c_ctx).at[1:1 + bsz].set(c)
    mod = _modulation(cvec, w_mod, b_mod).reshape(depth, 8, 6, d)

    o = np.cumsum((0, A_WIDTH, A_WIDTH, B_QK, B_QK, B_WIDTH, B_WIDTH, 2 * GATE_RANK, C_WIDTH))
    w_in_r = jnp.concatenate(
        [w_in[:, :, o[0]:o[6]], w_in[:, :, o[7]:o[8]], w_in[:, :, o[6]:o[7]],
         jnp.zeros((depth, d, N_IN_PAD - int(o[8])), w_in.dtype)], axis=-1).astype(BF16)
    w_out_b = w_out.astype(BF16)

    n_assign = t * TOP_K
    n_blocks = -(-(n_assign + N_EXPERTS * (MOE_BLOCK - 1)) // MOE_BLOCK)

    for l in range(depth):
        mod_l = mod[l]
        proj = _in_projection(xa, mod_l, w_in_r[l], n_ctx_rows, seq)
        a_out = _spatial_gate(proj, sgu_ln_g[l], sgu_ln_b[l], sgu_w[l], sgu_b[l])
        o_fwd = _gla_sweep(False, proj, gla_gate_up[l], gla_gate_b[l], bsz, ctx_len, seq)
        gla_out = _gla_sweep(True, proj, gla_gate_up[l], gla_gate_b[l], bsz, ctx_len, seq,
                             o_fwd=o_fwd, norm_g=gla_norm_g[l])
        mats = _s5_matrices(s5_lam_re[l], s5_lam_im[l], s5_log_dt[l], s5_b_re[l], s5_b_im[l],
                            s5_c_re[l], s5_c_im[l])
        y_s5 = _s5_scan(proj, mats, bsz, ctx_len, seq)
        x1, h2, logits = _out_projection(alpha, a_out, gla_out, y_s5, proj, xa, mod_l, s5_d[l], s5_glu_w[l],
                                         s5_glu_b[l], w_out_b[l], ln_g[l, 0], ln_b[l, 0], router_w[l],
                                         router_b[l], n_ctx_rows, seq)
        gates, dest, block_expert, n_valid, pad_lo, pad_hi = _routing(logits, n_blocks)
        xs = _moe_dispatch(h2, dest, pad_lo, pad_hi, n_blocks)
        ys = _moe_experts(l, xs, block_expert, n_valid, w_up, b_up, w_down, b_down)
        xa = _moe_combine(alpha, dest, gates, ys, x1, mod_l, ln_g[l, 1], ln_b[l, 1], n_ctx_rows, seq)
    return xa[n_ctx_rows:].reshape(bsz, seq, d)
```

```python
import functools

import numpy as np
import jax
import jax.numpy as jnp
from jax import lax
from jax.experimental import pallas as pl
from jax.experimental.pallas import tpu as pltpu

F32 = jnp.float32
BF16 = jnp.bfloat16
HIGHEST = lax.Precision.HIGHEST

D_MODEL = 1024
CHUNK = 128
A_HEADS = 4
A_HEAD_DIM = 64
A_WIDTH = 256
B_HEADS = 4
B_DK = 64
B_DV = 128
B_QK = 256
B_WIDTH = 512
GATE_RANK = 16
GATE_TAU = 16.0
GLA_CHUNK = 64
S5_GROUPS = 16
S5_GROUP_CH = 16
S5_STATE = 64
C_WIDTH = 256
N_EXPERTS = 32
TOP_K = 4
D_EXPERT = 1024
SWIGLU_LIMIT = 7.0
SWIGLU_ALPHA = 1.702
LN_EPS = 1e-5

N_IN_PAD = 2432
COL_GL = 2304
COL_S5 = 2048

ROW_TILE = 512
SEQ_TILE = 256
S5_T = 16
S5_QUARTER = 4
MOE_BLOCK = 512
COMBINE_TILE = 256
DISPATCH_TILE = 1024
VMEM_LIMIT = 56 * 1024 * 1024


def _layer_norm(x, g, b):
    mu = jnp.mean(x, axis=-1, keepdims=True)
    xc = x - mu
    var = jnp.mean(xc * xc, axis=-1, keepdims=True)
    return xc * lax.rsqrt(var + LN_EPS) * g + b


LANES = 128
ROW_SUB = D_MODEL // LANES


def _store_row_tiles(ref, lead, val):
    n = val.shape[0]
    for j in range(ROW_SUB):
        ref[lead + (pl.ds(j, n, stride=ROW_SUB), slice(None))] = val[:, j * LANES:(j + 1) * LANES]


def _load_row_tiles(ref, lead, n):
    return jnp.concatenate(
        [ref[lead + (pl.ds(j, n, stride=ROW_SUB), slice(None))] for j in range(ROW_SUB)], axis=1)


def _row_tile(ref, lead, r):
    return ref.at[lead + (pl.ds(pl.multiple_of(r * ROW_SUB, ROW_SUB), ROW_SUB), slice(None))]


def _segment(row0, n_ctx_rows, seq):
    return jnp.where(row0 < n_ctx_rows, 0, 1 + (row0 - n_ctx_rows) // seq)


def _mod_kernel(c_ref, w_ref, b_ref, o_ref):
    c = c_ref[...]
    s = c * jax.nn.sigmoid(c)
    o_ref[0] = jnp.dot(s, w_ref[0], precision=HIGHEST, preferred_element_type=F32) + b_ref[0]


def _modulation(cvec, w_mod, b_mod):
    depth = w_mod.shape[0]
    n6 = w_mod.shape[2]
    tn = 1024
    return pl.pallas_call(
        _mod_kernel,
        grid=(depth, n6 // tn),
        in_specs=[
            pl.BlockSpec((8, D_MODEL), lambda l, j: (0, 0)),
            pl.BlockSpec((1, D_MODEL, tn), lambda l, j: (l, 0, j)),
            pl.BlockSpec((1, 1, tn), lambda l, j: (l, 0, j)),
        ],
        out_specs=pl.BlockSpec((1, 8, tn), lambda l, j: (l, 0, j)),
        out_shape=jax.ShapeDtypeStruct((depth, 8, n6), F32),
        compiler_params=pltpu.CompilerParams(
            dimension_semantics=("arbitrary", "arbitrary"), vmem_limit_bytes=VMEM_LIMIT),
    )(cvec, w_mod, b_mod.reshape(depth, 1, n6))


def _inproj_kernel(x_ref, mod_ref, w_ref, o_ref):
    m = mod_ref[0]
    h = x_ref[...] * (1.0 + m[1:2]) + m[0:1]
    o_ref[...] = jnp.dot(h.astype(BF16), w_ref[...], preferred_element_type=F32)


def _in_projection(x, mod_l, w_in_b, n_ctx_rows, seq):
    t = x.shape[0]
    tm = ROW_TILE
    return pl.pallas_call(
        _inproj_kernel,
        grid=(t // tm,),
        in_specs=[
            pl.BlockSpec((tm, D_MODEL), lambda i: (i, 0)),
            pl.BlockSpec((1, 6, D_MODEL), lambda i: (_segment(i * tm, n_ctx_rows, seq), 0, 0)),
            pl.BlockSpec((D_MODEL, N_IN_PAD), lambda i: (0, 0)),
        ],
        out_specs=pl.BlockSpec((tm, N_IN_PAD), lambda i: (i, 0)),
        out_shape=jax.ShapeDtypeStruct((t, N_IN_PAD), F32),
        compiler_params=pltpu.CompilerParams(
            dimension_semantics=("arbitrary",), vmem_limit_bytes=VMEM_LIMIT),
    )(x, mod_l, w_in_b)


def _sgu_kernel(uv_ref, g_ref, b_ref, w_ref, bias_ref, o_ref):
    tm = uv_ref.shape[0]
    u = jax.nn.gelu(uv_ref[:, :A_WIDTH])
    v = _layer_norm(jax.nn.gelu(uv_ref[:, A_WIDTH:]), g_ref[...], b_ref[...]).astype(BF16)
    head = lax.broadcasted_iota(jnp.int32, (1, A_WIDTH), 1) // A_HEAD_DIM
    for c in range(tm // CHUNK):
        rows = slice(c * CHUNK, (c + 1) * CHUNK)
        vc = v[rows]
        acc = bias_ref[...]
        for h in range(A_HEADS):
            r = jnp.dot(w_ref[h], vc, preferred_element_type=F32)
            acc = acc + jnp.where(head == h, r, 0.0)
        o_ref[rows, :] = (u[rows] * acc).astype(BF16)


def _spatial_gate(proj, ln_g, ln_b, w_s, b_s):
    t = proj.shape[0]
    tm = ROW_TILE
    bias = jnp.repeat(b_s.T, A_HEAD_DIM, axis=1)
    return pl.pallas_call(
        _sgu_kernel,
        grid=(t // tm,),
        in_specs=[
            pl.BlockSpec((tm, 2 * A_WIDTH), lambda i: (i, 0)),
            pl.BlockSpec((1, A_WIDTH), lambda i: (0, 0)),
            pl.BlockSpec((1, A_WIDTH), lambda i: (0, 0)),
            pl.BlockSpec((A_HEADS, CHUNK, CHUNK), lambda i: (0, 0, 0)),
            pl.BlockSpec((CHUNK, A_WIDTH), lambda i: (0, 0)),
        ],
        out_specs=pl.BlockSpec((tm, A_WIDTH), lambda i: (i, 0)),
        out_shape=jax.ShapeDtypeStruct((t, A_WIDTH), BF16),
        compiler_params=pltpu.CompilerParams(
            dimension_semantics=("arbitrary",), vmem_limit_bytes=VMEM_LIMIT),
    )(proj, ln_g.reshape(1, -1), ln_b.reshape(1, -1), w_s.astype(BF16), bias)


_NT = (((1,), (1,)), ((), ()))
_TN = (((0,), (0,)), ((), ()))


def _gla_direction(backward, q_ref, k_ref, v_ref, gl_ref, gup_ref, gb_ref, o_ref, s_ref):
    c_len = GLA_CHUNK
    n_rows = q_ref.shape[0]
    n_chunks = n_rows // c_len

    row = lax.broadcasted_iota(jnp.int32, (n_rows, n_rows), 0)
    col = lax.broadcasted_iota(jnp.int32, (n_rows, n_rows), 1)
    same = (row // c_len) == (col // c_len)
    if backward:
        tri = jnp.where(same & (col >= row), 1.0, 0.0).astype(BF16)
        keep = same & (col > row)
        i_last, i_mid = 0, c_len - 1 - c_len // 2
        lo = GATE_RANK
    else:
        tri = jnp.where(same & (col <= row), 1.0, 0.0).astype(BF16)
        keep = same & (col <= row)
        i_last, i_mid = c_len - 1, c_len // 2
        lo = 0

    z = jnp.dot(gl_ref[:, lo:lo + GATE_RANK], gup_ref[...], precision=HIGHEST,
                preferred_element_type=F32) + gb_ref[...]
    la = jax.nn.log_sigmoid(z) / GATE_TAU
    l1 = la.astype(BF16)
    r1 = la - l1.astype(F32)
    l2 = r1.astype(BF16)
    l3 = (r1 - l2.astype(F32)).astype(BF16)
    b = (jnp.dot(tri, l1, preferred_element_type=F32) + jnp.dot(tri, l2, preferred_element_type=F32)
         + jnp.dot(tri, l3, preferred_element_type=F32))

    def per_chunk(index):
        return jnp.concatenate(
            [jnp.broadcast_to(b[c * c_len + index:c * c_len + index + 1], (c_len, B_QK))
             for c in range(n_chunks)], axis=0)

    b_last = per_chunk(i_last)
    b_mid = per_chunk(i_mid)
    q = q_ref[...] * (B_DK ** -0.5)
    k = k_ref[...]
    q_mid = (q * jnp.exp(b - b_mid)).astype(BF16)
    k_mid = (k * jnp.exp(b_mid - b)).astype(BF16)
    q_in = (q * jnp.exp(b)).astype(BF16)
    k_out = (k * jnp.exp(b_last - b)).astype(BF16)
    order = range(n_chunks - 1, -1, -1) if backward else range(n_chunks)
    for h in range(B_HEADS):
        hk = slice(h * B_DK, (h + 1) * B_DK)
        hv = slice(h * B_DV, (h + 1) * B_DV)
        vh = v_ref[:, hv].astype(BF16)
        sc = lax.dot_general(q_mid[:, hk], k_mid[:, hk], _NT, preferred_element_type=F32)
        sc = jnp.where(keep, sc, 0.0).astype(BF16)
        o_intra = jnp.dot(sc, vh, preferred_element_type=F32)
        state = s_ref[h]
        for c in order:
            rows = slice(c * c_len, (c + 1) * c_len)
            o_ref[rows, hv] = o_intra[rows] + lax.dot_general(q_in[rows, hk], state.astype(BF16), _NT,
                                                              preferred_element_type=F32)
            decay = jnp.exp(b[c * c_len + i_last:c * c_len + i_last + 1, hk])
            state = state * decay + lax.dot_general(vh[rows], k_out[rows, hk], _TN,
                                                    preferred_element_type=F32)
        s_ref[h] = state


def _gla_kernel(qf, kf, vf, glf, qb, kb, vb, glb, gup_ref, gb_ref, of_ref, ob_ref, s_ref):
    @pl.when(pl.program_id(1) == 0)
    def _():
        s_ref[...] = jnp.zeros_like(s_ref)

    _gla_direction(False, qf, kf, vf, glf, gup_ref.at[0], gb_ref.at[0], of_ref, s_ref.at[0])
    _gla_direction(True, qb, kb, vb, glb, gup_ref.at[1], gb_ref.at[1], ob_ref, s_ref.at[1])


def _gla_block(backward, bsz, nctx_blk, nlat_blk, b, j):
    if backward:
        ctx_i = b * nctx_blk + (nctx_blk - 1 - j)
        lat_i = bsz * nctx_blk + b * nlat_blk + (nlat_blk - 1 - (j - nctx_blk))
    else:
        ctx_i = b * nctx_blk + j
        lat_i = bsz * nctx_blk + b * nlat_blk + (j - nctx_blk)
    return jnp.where(j < nctx_blk, ctx_i, lat_i)


def _gla_sweep(proj, gate_up, gate_b, bsz, ctx_len, seq):
    t = proj.shape[0]
    r = SEQ_TILE
    nctx_blk, nlat_blk = ctx_len // r, seq // r
    in_specs = []
    for backward in (False, True):
        blk = functools.partial(_gla_block, backward, bsz, nctx_blk, nlat_blk)
        in_specs += [
            pl.BlockSpec((r, B_QK), lambda b, j, blk=blk: (blk(b, j), 2)),
            pl.BlockSpec((r, B_QK), lambda b, j, blk=blk: (blk(b, j), 3)),
            pl.BlockSpec((r, B_WIDTH), lambda b, j, blk=blk: (blk(b, j), 2)),
            pl.BlockSpec((r, 128), lambda b, j, blk=blk: (blk(b, j), COL_GL // 128)),
        ]
    in_specs += [
        pl.BlockSpec((2, GATE_RANK, B_QK), lambda b, j: (0, 0, 0)),
        pl.BlockSpec((2, 1, B_QK), lambda b, j: (0, 0, 0)),
    ]
    fwd = functools.partial(_gla_block, False, bsz, nctx_blk, nlat_blk)
    bwd = functools.partial(_gla_block, True, bsz, nctx_blk, nlat_blk)
    return pl.pallas_call(
        _gla_kernel,
        grid=(bsz, nctx_blk + nlat_blk),
        in_specs=in_specs,
        out_specs=[
            pl.BlockSpec((r, B_WIDTH), lambda b, j: (fwd(b, j), 0)),
            pl.BlockSpec((r, B_WIDTH), lambda b, j: (bwd(b, j), 0)),
        ],
        out_shape=[jax.ShapeDtypeStruct((t, B_WIDTH), F32), jax.ShapeDtypeStruct((t, B_WIDTH), F32)],
        scratch_shapes=[pltpu.VMEM((2, B_HEADS, B_DV, B_DK), F32)],
        compiler_params=pltpu.CompilerParams(
            dimension_semantics=("arbitrary", "arbitrary"), vmem_limit_bytes=VMEM_LIMIT),
    )(*([proj] * 8), gate_up, gate_b.reshape(2, 1, -1))


def _s5_matrices(lam_re, lam_im, log_dt, b_re, b_im, c_re, c_im):
    tc = S5_T
    lam = lax.complex(lam_re.astype(F32), lam_im.astype(F32))
    dt = jnp.exp(log_dt.astype(F32))
    bm = lax.complex(b_re.astype(F32), b_im.astype(F32))
    cm = lax.complex(c_re.astype(F32), c_im.astype(F32))
    ldt = lam * dt[..., None]
    lam_bar = jnp.exp(ldt)
    b_bar = ((lam_bar - 1.0) / lam)[..., None] * bm
    steps = jnp.arange(tc + 1, dtype=F32)
    pw = jnp.exp(ldt[:, None] * steps[None, :, None, None])
    kern = jnp.real(jnp.einsum('dgip,dtgp,dgpj->dgtij', cm, pw[:, :tc], b_bar))
    s_i = jnp.arange(tc)[:, None]
    t_i = jnp.arange(tc)[None, :]
    g_n = lam.shape[1]
    hh = S5_GROUP_CH

    def toeplitz(kd, lag, ok):
        m = kd[:, jnp.clip(lag, 0, tc - 1)]
        m = jnp.where(ok[None, :, :, None, None], m, 0.0)
        return m.transpose(0, 1, 4, 2, 3).reshape(g_n, tc * hh, tc * hh)

    mt = toeplitz(kern[0], t_i - s_i, t_i >= s_i) + toeplitz(kern[1], s_i - t_i, s_i >= t_i)

    def state_in(d, powers):
        w = pw[d][powers][:, :, :, None] * b_bar[d][None]
        w = w.transpose(1, 0, 3, 2).reshape(g_n, tc * hh, S5_STATE)
        re, im = jnp.real(w), jnp.imag(w)
        return jnp.concatenate([re, im, im, re], axis=-1)

    qt = jnp.concatenate([state_in(0, tc - 1 - jnp.arange(tc)), state_in(1, jnp.arange(tc))], axis=-1)

    def state_out(d, powers):
        w = cm[d][:, None] * pw[d][powers].transpose(1, 0, 2)[:, :, None, :]
        w = w.reshape(g_n, tc * hh, S5_STATE).transpose(0, 2, 1)
        return jnp.concatenate([jnp.real(w), -jnp.imag(w)], axis=1)

    pt = jnp.concatenate([state_out(0, 1 + jnp.arange(tc)), state_out(1, tc - jnp.arange(tc))], axis=1)

    a = pw[:, tc]
    ar, ai = jnp.real(a), jnp.imag(a)
    a1 = jnp.concatenate([ar, ar], axis=-1).reshape(2, -1)
    a2 = jnp.concatenate([-ai, ai], axis=-1).reshape(2, -1)
    a3 = jnp.concatenate([ai, -ai], axis=-1).reshape(2, -1)
    ac = jnp.stack([a1, a2, a3], axis=1)
    return mt.astype(BF16), qt.astype(BF16), pt.astype(BF16), ac.reshape(2, 3, 1, -1)


def _s5_kernel(uc_ref, ul_ref, mt_ref, qt_ref, pt_ref, ac_ref, yc_ref, yl_ref, ef, esf, eb, esb):
    nq = uc_ref.shape[0]
    nc_ctx, nc_lat = uc_ref.shape[1], ul_ref.shape[1]
    n = nc_ctx + nc_lat
    w = 2 * S5_STATE
    for gi in range(nq):
        lanes = slice(gi * w, (gi + 1) * w)
        rc = jnp.dot(uc_ref[gi], qt_ref[gi], preferred_element_type=F32)
        rl = jnp.dot(ul_ref[gi], qt_ref[gi], preferred_element_type=F32)
        ef[0:nc_ctx, lanes] = rc[:, 0:w]
        esf[0:nc_ctx, lanes] = rc[:, w:2 * w]
        ef[nc_ctx:n, lanes] = rl[:, 0:w]
        esf[nc_ctx:n, lanes] = rl[:, w:2 * w]
        eb[0:nc_lat, lanes] = rl[:, 2 * w:3 * w]
        esb[0:nc_lat, lanes] = rl[:, 3 * w:4 * w]
        eb[nc_lat:n, lanes] = rc[:, 2 * w:3 * w]
        esb[nc_lat:n, lanes] = rc[:, 3 * w:4 * w]

    a1f, a2f, a3f = ac_ref[0, 0], ac_ref[0, 1], ac_ref[0, 2]
    a1b, a2b, a3b = ac_ref[1, 0], ac_ref[1, 1], ac_ref[1, 2]

    def body(i, carry):
        hf, hsf, hb, hsb = carry
        rf = pl.ds(i, 1)
        rb = pl.ds(n - 1 - i, 1)
        e_f, es_f = ef[rf, :], esf[rf, :]
        e_b, es_b = eb[rb, :], esb[rb, :]
        ef[rf, :] = hf
        eb[rb, :] = hb
        return (a1f * hf + a2f * hsf + e_f, a1f * hsf + a3f * hf + es_f,
                a1b * hb + a2b * hsb + e_b, a1b * hsb + a3b * hb + es_b)

    zero = jnp.zeros((1, nq * w), F32)
    lax.fori_loop(0, n, body, (zero, zero, zero, zero))

    for gi in range(nq):
        lanes = slice(gi * w, (gi + 1) * w)
        hc = jnp.concatenate([ef[0:nc_ctx, lanes], eb[nc_lat:n, lanes]], axis=1).astype(BF16)
        hl = jnp.concatenate([ef[nc_ctx:n, lanes], eb[0:nc_lat, lanes]], axis=1).astype(BF16)
        yc_ref[gi] = (jnp.dot(uc_ref[gi], mt_ref[gi], preferred_element_type=F32)
                      + jnp.dot(hc, pt_ref[gi], preferred_element_type=F32))
        yl_ref[gi] = (jnp.dot(ul_ref[gi], mt_ref[gi], preferred_element_type=F32)
                      + jnp.dot(hl, pt_ref[gi], preferred_element_type=F32))


def _s5_scan(proj, mats, bsz, ctx_len, seq):
    mt, qt, pt, ac = mats
    g_n, tc, hh = S5_GROUPS, S5_T, S5_GROUP_CH
    n_ctx_rows = bsz * ctx_len
    u = proj[:, COL_S5:COL_S5 + C_WIDTH].astype(BF16)

    def to_rows(part):
        r = part.shape[0]
        return part.reshape(r // tc, tc, g_n, hh).transpose(2, 0, 1, 3).reshape(g_n, r // tc, tc * hh)

    uc, ul = to_rows(u[:n_ctx_rows]), to_rows(u[n_ctx_rows:])
    nc_ctx, nc_lat = ctx_len // tc, seq // tc
    nq = S5_QUARTER
    wq = nq * 2 * S5_STATE
    n = nc_ctx + nc_lat
    yc, yl = pl.pallas_call(
        _s5_kernel,
        grid=(bsz, g_n // nq),
        in_specs=[
            pl.BlockSpec((nq, nc_ctx, tc * hh), lambda b, qi: (qi, b, 0)),
            pl.BlockSpec((nq, nc_lat, tc * hh), lambda b, qi: (qi, b, 0)),
            pl.BlockSpec((nq, tc * hh, tc * hh), lambda b, qi: (qi, 0, 0)),
            pl.BlockSpec((nq, tc * hh, 8 * S5_STATE), lambda b, qi: (qi, 0, 0)),
            pl.BlockSpec((nq, 4 * S5_STATE, tc * hh), lambda b, qi: (qi, 0, 0)),
            pl.BlockSpec((2, 3, 1, wq), lambda b, qi: (0, 0, 0, qi)),
        ],
        out_specs=[
            pl.BlockSpec((nq, nc_ctx, tc * hh), lambda b, qi: (qi, b, 0)),
            pl.BlockSpec((nq, nc_lat, tc * hh), lambda b, qi: (qi, b, 0)),
        ],
        out_shape=[
            jax.ShapeDtypeStruct((g_n, bsz * nc_ctx, tc * hh), F32),
            jax.ShapeDtypeStruct((g_n, bsz * nc_lat, tc * hh), F32),
        ],
        scratch_shapes=[pltpu.VMEM((n, wq), F32) for _ in range(4)],
        compiler_params=pltpu.CompilerParams(
            dimension_semantics=("arbitrary", "arbitrary"), vmem_limit_bytes=VMEM_LIMIT),
    )(uc, ul, mt, qt, pt, ac)

    def from_rows(y):
        r = y.shape[1]
        return y.reshape(g_n, r, tc, hh).transpose(1, 2, 0, 3).reshape(r * tc, g_n * hh)

    return jnp.concatenate([from_rows(yc), from_rows(yl)], axis=0)


def _outproj_kernel(alpha, a_ref, of_ref, ob_ref, g_ref, ng_ref, y_ref, u_ref, x_ref, mod_ref, d_ref, gw_ref,
                    gb_ref, wo_ref, lng_ref, lnb_ref, rwh_ref, rwl_ref, rb_ref, x1_ref, h2_ref, lg_ref):
    m = mod_ref[0]
    heads = []
    for h in range(B_HEADS):
        hv = slice(h * B_DV, (h + 1) * B_DV)
        o = of_ref[:, hv] + ob_ref[:, hv]
        heads.append(o * lax.rsqrt(jnp.mean(o * o, axis=-1, keepdims=True) + LN_EPS))
    gate = g_ref[...]
    gla = (jnp.concatenate(heads, axis=1) * ng_ref[...] * (gate * jax.nn.sigmoid(gate))).astype(BF16)
    y = jax.nn.gelu(y_ref[...] + d_ref[...] * u_ref[...])
    s5 = y * jax.nn.sigmoid(jnp.dot(y.astype(BF16), gw_ref[...], preferred_element_type=F32) + gb_ref[...])
    mix = (jnp.dot(a_ref[...], wo_ref[0:A_WIDTH, :], preferred_element_type=F32)
           + jnp.dot(gla, wo_ref[A_WIDTH:A_WIDTH + B_WIDTH, :], preferred_element_type=F32)
           + jnp.dot(s5.astype(BF16), wo_ref[A_WIDTH + B_WIDTH:, :], preferred_element_type=F32))
    x1 = _layer_norm(alpha * x_ref[...] + m[2:3] * mix, lng_ref[...], lnb_ref[...])
    x1_ref[...] = x1
    h2 = x1 * (1.0 + m[4:5]) + m[3:4]
    _store_row_tiles(h2_ref, (), h2)
    h_hi = h2.astype(BF16)
    h_lo = (h2 - h_hi.astype(F32)).astype(BF16)
    lg_ref[...] = (jnp.dot(h_hi, rwh_ref[...], preferred_element_type=F32)
                   + jnp.dot(h_lo, rwh_ref[...], preferred_element_type=F32)
                   + jnp.dot(h_hi, rwl_ref[...], preferred_element_type=F32) + rb_ref[...])


def _out_projection(alpha, a_out, o_fwd, o_bwd, norm_g, y_s5, proj, x, mod_l, s5_d, glu_w, glu_b, w_out_b, ln_g,
                    ln_b, router_w, router_b, n_ctx_rows, seq):
    t = x.shape[0]
    tm = ROW_TILE
    rw = jnp.zeros((D_MODEL, 128), F32).at[:, :N_EXPERTS].set(router_w)
    rw_hi = rw.astype(BF16)
    rw_lo = (rw - rw_hi.astype(F32)).astype(BF16)
    rb = jnp.zeros((1, 128), F32).at[0, :N_EXPERTS].set(router_b)
    row = lambda i: (i, 0)
    fixed = lambda i: (0, 0)
    return pl.pallas_call(
        functools.partial(_outproj_kernel, alpha),
        grid=(t // tm,),
        in_specs=[
            pl.BlockSpec((tm, A_WIDTH), row),
            pl.BlockSpec((tm, B_WIDTH), row),
            pl.BlockSpec((tm, B_WIDTH), row),
            pl.BlockSpec((tm, B_WIDTH), lambda i: (i, 3)),
            pl.BlockSpec((1, B_WIDTH), fixed),
            pl.BlockSpec((tm, C_WIDTH), row),
            pl.BlockSpec((tm, C_WIDTH), lambda i: (i, COL_S5 // C_WIDTH)),
            pl.BlockSpec((tm, D_MODEL), row),
            pl.BlockSpec((1, 6, D_MODEL), lambda i: (_segment(i * tm, n_ctx_rows, seq), 0, 0)),
            pl.BlockSpec((1, C_WIDTH), fixed),
            pl.BlockSpec((C_WIDTH, C_WIDTH), fixed),
            pl.BlockSpec((1, C_WIDTH), fixed),
            pl.BlockSpec((D_MODEL, D_MODEL), fixed),
            pl.BlockSpec((1, D_MODEL), fixed),
            pl.BlockSpec((1, D_MODEL), fixed),
            pl.BlockSpec((D_MODEL, 128), fixed),
            pl.BlockSpec((D_MODEL, 128), fixed),
            pl.BlockSpec((1, 128), fixed),
        ],
        out_specs=[
            pl.BlockSpec((tm, D_MODEL), row),
            pl.BlockSpec((tm * ROW_SUB, LANES), row),
            pl.BlockSpec((tm, 128), row),
        ],
        out_shape=[
            jax.ShapeDtypeStruct((t, D_MODEL), F32),
            jax.ShapeDtypeStruct((t * ROW_SUB, LANES), F32),
            jax.ShapeDtypeStruct((t, 128), F32),
        ],
        compiler_params=pltpu.CompilerParams(
            dimension_semantics=("arbitrary",), vmem_limit_bytes=VMEM_LIMIT),
    )(a_out, o_fwd, o_bwd, proj, norm_g.reshape(1, -1), y_s5, proj, x, mod_l, s5_d.reshape(1, -1), glu_w.astype(BF16), glu_b.reshape(1, -1),
      w_out_b, ln_g.reshape(1, -1), ln_b.reshape(1, -1), rw_hi, rw_lo, rb)


def _route_kernel(lg_ref, idx_ref, gate_ref, rank_ref, cnt_ref, base, before):
    tm = lg_ref.shape[0]

    @pl.when(pl.program_id(0) == 0)
    def _():
        base[...] = jnp.zeros_like(base)
        r = lax.broadcasted_iota(jnp.int32, (tm, tm), 0)
        c = lax.broadcasted_iota(jnp.int32, (tm, tm), 1)
        before[...] = jnp.where(r < c, 1.0, 0.0).astype(BF16)

    logit = jnp.transpose(lg_ref[...])[:N_EXPERTS]
    eid = lax.broadcasted_iota(jnp.int32, (N_EXPERTS, tm), 0)
    vals, hots = [], []
    work = logit
    for kk in range(TOP_K):
        m = jnp.max(work, axis=0, keepdims=True)
        ix = jnp.min(jnp.where(work == m, eid, N_EXPERTS), axis=0, keepdims=True)
        hot = eid == ix
        idx_ref[kk:kk + 1, :] = ix
        vals.append(m)
        hots.append(hot)
        work = jnp.where(hot, -jnp.inf, work)
    ex = [jnp.exp(v - vals[0]) for v in vals]
    den = ex[0] + ex[1] + ex[2] + ex[3]
    member = jnp.zeros((N_EXPERTS, tm), F32)
    for kk in range(TOP_K):
        gate_ref[kk:kk + 1, :] = ex[kk] / den
        member = member + jnp.where(hots[kk], 1.0, 0.0)
    earlier = jnp.dot(member.astype(BF16), before[...], preferred_element_type=F32) + base[...]
    for kk in range(TOP_K):
        rank_ref[kk:kk + 1, :] = jnp.sum(jnp.where(hots[kk], earlier, 0.0), axis=0,
                                         keepdims=True).astype(jnp.int32)
    total = base[...] + jnp.sum(member, axis=1, keepdims=True)
    base[...] = total
    cnt_ref[...] = jnp.broadcast_to(total, cnt_ref.shape)


def _routing(logits, n_blocks):
    t = logits.shape[0]
    tm = ROW_TILE
    idx, gates, rank, cnt = pl.pallas_call(
        _route_kernel,
        grid=(t // tm,),
        in_specs=[pl.BlockSpec((tm, LANES), lambda i: (i, 0))],
        out_specs=[
            pl.BlockSpec((TOP_K, tm), lambda i: (0, i)),
            pl.BlockSpec((TOP_K, tm), lambda i: (0, i)),
            pl.BlockSpec((TOP_K, tm), lambda i: (0, i)),
            pl.BlockSpec((N_EXPERTS, LANES), lambda i: (0, 0)),
        ],
        out_shape=[
            jax.ShapeDtypeStruct((TOP_K, t), jnp.int32),
            jax.ShapeDtypeStruct((TOP_K, t), F32),
            jax.ShapeDtypeStruct((TOP_K, t), jnp.int32),
            jax.ShapeDtypeStruct((N_EXPERTS, LANES), F32),
        ],
        scratch_shapes=[pltpu.VMEM((N_EXPERTS, 1), F32), pltpu.VMEM((tm, tm), BF16)],
        compiler_params=pltpu.CompilerParams(
            dimension_semantics=("arbitrary",), vmem_limit_bytes=VMEM_LIMIT),
    )(logits)
    counts = cnt[:, 0].astype(jnp.int32)
    padded = (counts + MOE_BLOCK - 1) // MOE_BLOCK * MOE_BLOCK
    padded_end = jnp.cumsum(padded)
    padded_start = padded_end - padded
    experts = jnp.arange(N_EXPERTS, dtype=jnp.int32)
    start_of = jnp.sum(jnp.where(idx[:, :, None] == experts, padded_start, 0), axis=-1)
    dest = (start_of + rank).T
    first_slot = jnp.arange(n_blocks, dtype=jnp.int32) * MOE_BLOCK
    block_expert = jnp.minimum(jnp.sum((padded_end[None, :] <= first_slot[:, None]).astype(jnp.int32), axis=1),
                               N_EXPERTS - 1).astype(jnp.int32)
    n_valid = (padded_end[-1] // MOE_BLOCK).astype(jnp.int32).reshape(1)
    pad_lo = (padded_start + counts).astype(jnp.int32)
    return gates.T, dest.astype(jnp.int32), block_expert, n_valid, pad_lo, padded_end.astype(jnp.int32)


def _dispatch_kernel(lo_ref, hi_ref, dst_ref, h_ref, o_ref, zbuf, sem, zsem):
    i = pl.program_id(0)
    tm = DISPATCH_TILE
    blk_rows = MOE_BLOCK * ROW_SUB
    n_blocks = o_ref.shape[0] // blk_rows

    def issue(r, carry):
        for kk in range(TOP_K):
            pltpu.make_async_copy(_row_tile(h_ref, (), r), _row_tile(o_ref, (), dst_ref[0, 0, r * TOP_K + kk]),
                                  sem).start()
        return carry
    lax.fori_loop(0, tm, issue, 0, unroll=4)

    @pl.when(i == pl.num_programs(0) - 1)
    def _():
        zbuf[...] = jnp.zeros_like(zbuf)
        zrow = zbuf.at[pl.ds(0, ROW_SUB), :]
        for e in range(N_EXPERTS):
            def fill(s, carry):
                pltpu.make_async_copy(zrow, _row_tile(o_ref, (), s), zsem).start()
                return carry
            lax.fori_loop(lo_ref[e], hi_ref[e], fill, 0)
        for e in range(N_EXPERTS):
            def drain(s, carry):
                pltpu.make_async_copy(zrow, zrow, zsem).wait()
                return carry
            lax.fori_loop(lo_ref[e], hi_ref[e], drain, 0)
        used = hi_ref[N_EXPERTS - 1] // MOE_BLOCK

        def fill_block(j, carry):
            rows = pl.ds(pl.multiple_of(j * blk_rows, blk_rows), blk_rows)
            pltpu.make_async_copy(zbuf, o_ref.at[rows, :], zsem).start()
            return carry
        lax.fori_loop(used, n_blocks, fill_block, 0)

        def drain_block(j, carry):
            pltpu.make_async_copy(zbuf, zbuf, zsem).wait()
            return carry
        lax.fori_loop(used, n_blocks, drain_block, 0)

    for _ in range(TOP_K):
        pltpu.make_async_copy(h_ref, h_ref, sem).wait()


def _moe_dispatch(h2t, dest, pad_lo, pad_hi, n_blocks):
    t = dest.shape[0]
    tm = DISPATCH_TILE
    nt = t // tm
    grid_spec = pltpu.PrefetchScalarGridSpec(
        num_scalar_prefetch=2,
        grid=(nt,),
        in_specs=[
            pl.BlockSpec((1, 1, tm * TOP_K), lambda i, lo, hi: (i, 0, 0), memory_space=pltpu.SMEM),
            pl.BlockSpec((tm * ROW_SUB, LANES), lambda i, lo, hi: (i, 0)),
        ],
        out_specs=pl.BlockSpec(memory_space=pl.ANY),
        scratch_shapes=[
            pltpu.VMEM((MOE_BLOCK * ROW_SUB, LANES), F32),
            pltpu.SemaphoreType.DMA(()),
            pltpu.SemaphoreType.DMA(()),
        ],
    )
    return pl.pallas_call(
        _dispatch_kernel,
        grid_spec=grid_spec,
        out_shape=jax.ShapeDtypeStruct((n_blocks * MOE_BLOCK * ROW_SUB, LANES), F32),
        compiler_params=pltpu.CompilerParams(
            dimension_semantics=("arbitrary",), vmem_limit_bytes=VMEM_LIMIT),
    )(pad_lo, pad_hi, dest.reshape(nt, 1, tm * TOP_K), h2t)


def _moe_kernel(be_ref, nv_ref, x_ref, wu_ref, bu_ref, wd_ref, bd_ref, o_ref, wu_b, wd_b):
    i = pl.program_id(0)
    n_valid = nv_ref[0]

    @pl.when(i < n_valid)
    def _():
        first = jnp.logical_or(i == 0, be_ref[i] != be_ref[jnp.maximum(i - 1, 0)])

        @pl.when(first)
        def _():
            rows = 64

            def cast(r, carry):
                rs = pl.ds(pl.multiple_of(r * rows, rows), rows)
                wu_b[rs, :] = wu_ref[0, 0, rs, :].astype(BF16)
                wd_b[rs, :] = wd_ref[0, 0, rs, :].astype(BF16)
                return carry
            lax.fori_loop(0, D_MODEL // rows, cast, 0)

        x = _load_row_tiles(x_ref, (), MOE_BLOCK).astype(BF16)
        acc = jnp.zeros((MOE_BLOCK, D_MODEL), F32) + bd_ref[0, 0]
        cw = 512
        for jc in range(D_EXPERT // cw):
            cg = slice(jc * cw, (jc + 1) * cw)
            cl = slice(D_EXPERT + jc * cw, D_EXPERT + (jc + 1) * cw)
            ug = jnp.dot(x, wu_b[:, cg], preferred_element_type=F32) + bu_ref[0, 0, :, cg]
            ul = jnp.dot(x, wu_b[:, cl], preferred_element_type=F32) + bu_ref[0, 0, :, cl]
            xg = jnp.minimum(ug, SWIGLU_LIMIT)
            xl = jnp.clip(ul, -SWIGLU_LIMIT, SWIGLU_LIMIT)
            act = xg * jax.nn.sigmoid(SWIGLU_ALPHA * xg) * (xl + 1.0)
            acc = acc + jnp.dot(act.astype(BF16), wd_b[cg, :], preferred_element_type=F32)
        _store_row_tiles(o_ref, (), acc)

    @pl.when(i >= n_valid)
    def _():
        o_ref[...] = jnp.zeros_like(o_ref)


def _moe_experts(layer, xs, block_expert, n_valid, w_up, b_up, w_down, b_down):
    n_blocks = block_expert.shape[0]
    depth = w_up.shape[0]
    grid_spec = pltpu.PrefetchScalarGridSpec(
        num_scalar_prefetch=2,
        grid=(n_blocks,),
        in_specs=[
            pl.BlockSpec((MOE_BLOCK * ROW_SUB, LANES),
                         lambda i, be, nv: (jnp.minimum(i, jnp.maximum(nv[0] - 1, 0)), 0)),
            pl.BlockSpec((1, 1, D_MODEL, 2 * D_EXPERT), lambda i, be, nv: (layer, be[i], 0, 0)),
            pl.BlockSpec((1, 1, 1, 2 * D_EXPERT), lambda i, be, nv: (layer, be[i], 0, 0)),
            pl.BlockSpec((1, 1, D_EXPERT, D_MODEL), lambda i, be, nv: (layer, be[i], 0, 0)),
            pl.BlockSpec((1, 1, 1, D_MODEL), lambda i, be, nv: (layer, be[i], 0, 0)),
        ],
        out_specs=pl.BlockSpec((MOE_BLOCK * ROW_SUB, LANES), lambda i, be, nv: (i, 0)),
        scratch_shapes=[
            pltpu.VMEM((D_MODEL, 2 * D_EXPERT), BF16),
            pltpu.VMEM((D_EXPERT, D_MODEL), BF16),
        ],
    )
    return pl.pallas_call(
        _moe_kernel,
        grid_spec=grid_spec,
        out_shape=jax.ShapeDtypeStruct((n_blocks * MOE_BLOCK * ROW_SUB, LANES), F32),
        compiler_params=pltpu.CompilerParams(
            dimension_semantics=("arbitrary",), vmem_limit_bytes=VMEM_LIMIT),
    )(block_expert, n_valid, xs, w_up, b_up.reshape(depth, N_EXPERTS, 1, -1), w_down,
      b_down.reshape(depth, N_EXPERTS, 1, -1))


def _combine_kernel(alpha, dst_ref, dstn_ref, gate_ref, y_ref, x_ref, mod_ref, lng_ref, lnb_ref, o_ref,
                    buf, sem):
    i = pl.program_id(0)
    n = pl.num_programs(0)
    tm = COMBINE_TILE
    slot = i % 2

    def gather(idx_ref, s):
        def issue(r, carry):
            for kk in range(TOP_K):
                pltpu.make_async_copy(_row_tile(y_ref, (), idx_ref[0, 0, r * TOP_K + kk]),
                                      _row_tile(buf, (s, kk), r), sem.at[s]).start()
            return carry
        lax.fori_loop(0, tm, issue, 0, unroll=4)

    @pl.when(i == 0)
    def _():
        gather(dst_ref, 0)

    @pl.when(i + 1 < n)
    def _():
        gather(dstn_ref, 1 - slot)

    pltpu.make_async_copy(buf.at[slot], buf.at[slot], sem.at[slot]).wait()
    gates = gate_ref[...]
    f = jnp.zeros((tm, D_MODEL), F32)
    for kk in range(TOP_K):
        f = f + gates[:, kk:kk + 1] * _load_row_tiles(buf, (slot, kk), tm)
    m = mod_ref[0]
    o_ref[...] = _layer_norm(alpha * x_ref[...] + m[5:6] * f, lng_ref[...], lnb_ref[...])


def _moe_combine(alpha, dest, gates, ys3, x1, mod_l, ln_g, ln_b, n_ctx_rows, seq):
    t = x1.shape[0]
    tm = COMBINE_TILE
    nt = t // tm
    dst3 = dest.reshape(nt, 1, tm * TOP_K)
    return pl.pallas_call(
        functools.partial(_combine_kernel, alpha),
        grid=(nt,),
        in_specs=[
            pl.BlockSpec((1, 1, tm * TOP_K), lambda i: (i, 0, 0), memory_space=pltpu.SMEM),
            pl.BlockSpec((1, 1, tm * TOP_K), lambda i: (jnp.minimum(i + 1, nt - 1), 0, 0),
                         memory_space=pltpu.SMEM),
            pl.BlockSpec((tm, TOP_K), lambda i: (i, 0)),
            pl.BlockSpec(memory_space=pl.ANY),
            pl.BlockSpec((tm, D_MODEL), lambda i: (i, 0)),
            pl.BlockSpec((1, 6, D_MODEL), lambda i: (_segment(i * tm, n_ctx_rows, seq), 0, 0)),
            pl.BlockSpec((1, D_MODEL), lambda i: (0, 0)),
            pl.BlockSpec((1, D_MODEL), lambda i: (0, 0)),
        ],
        out_specs=pl.BlockSpec((tm, D_MODEL), lambda i: (i, 0)),
        out_shape=jax.ShapeDtypeStruct((t, D_MODEL), F32),
        scratch_shapes=[
            pltpu.VMEM((2, TOP_K, tm * ROW_SUB, LANES), F32),
            pltpu.SemaphoreType.DMA((2,)),
        ],
        compiler_params=pltpu.CompilerParams(
            dimension_semantics=("arbitrary",), vmem_limit_bytes=VMEM_LIMIT),
    )(dst3, dst3, gates, ys3, x1, mod_l, ln_g.reshape(1, -1), ln_b.reshape(1, -1))


def kernel(x, c, ctx, c_ctx, w_mod, b_mod, w_in, sgu_ln_g, sgu_ln_b, sgu_w, sgu_b, gla_gate_up, gla_gate_b,
           gla_norm_g, s5_lam_re, s5_lam_im, s5_log_dt, s5_b_re, s5_b_im, s5_c_re, s5_c_im, s5_d, s5_glu_w,
           s5_glu_b, w_out, ln_g, ln_b, router_w, router_b, w_up, b_up, w_down, b_down):
    bsz, seq, d = x.shape
    ctx_len = ctx.shape[1]
    depth = w_in.shape[0]
    alpha = float((2 * depth) ** 0.25)
    n_ctx_rows = bsz * ctx_len
    t = n_ctx_rows + bsz * seq
    assert d == D_MODEL and bsz + 1 <= 8
    assert n_ctx_rows % ROW_TILE == 0 and seq % ROW_TILE == 0
    assert ctx_len % SEQ_TILE == 0 and seq % SEQ_TILE == 0 and t % COMBINE_TILE == 0 and t % DISPATCH_TILE == 0

    xa = jnp.concatenate([ctx.reshape(n_ctx_rows, d), x.reshape(bsz * seq, d)], axis=0)
    cvec = jnp.zeros((8, d), F32).at[0].set(c_ctx).at[1:1 + bsz].set(c)
    mod = _modulation(cvec, w_mod, b_mod).reshape(depth, 8, 6, d)

    o = np.cumsum((0, A_WIDTH, A_WIDTH, B_QK, B_QK, B_WIDTH, B_WIDTH, 2 * GATE_RANK, C_WIDTH))
    w_in_r = jnp.concatenate(
        [w_in[:, :, o[0]:o[6]], w_in[:, :, o[7]:o[8]], w_in[:, :, o[6]:o[7]],
         jnp.zeros((depth, d, N_IN_PAD - int(o[8])), w_in.dtype)], axis=-1).astype(BF16)
    w_out_b = w_out.astype(BF16)

    n_assign = t * TOP_K
    n_blocks = -(-(n_assign + N_EXPERTS * (MOE_BLOCK - 1)) // MOE_BLOCK)

    for l in range(depth):
        mod_l = mod[l]
        proj = _in_projection(xa, mod_l, w_in_r[l], n_ctx_rows, seq)
        a_out = _spatial_gate(proj, sgu_ln_g[l], sgu_ln_b[l], sgu_w[l], sgu_b[l])
        o_fwd, o_bwd = _gla_sweep(proj, gla_gate_up[l], gla_gate_b[l], bsz, ctx_len, seq)
        mats = _s5_matrices(s5_lam_re[l], s5_lam_im[l], s5_log_dt[l], s5_b_re[l], s5_b_im[l],
                            s5_c_re[l], s5_c_im[l])
        y_s5 = _s5_scan(proj, mats, bsz, ctx_len, seq)
        x1, h2, logits = _out_projection(alpha, a_out, o_fwd, o_bwd, gla_norm_g[l], y_s5, proj, xa, mod_l, s5_d[l], s5_glu_w[l],
                                         s5_glu_b[l], w_out_b[l], ln_g[l, 0], ln_b[l, 0], router_w[l],
                                         router_b[l], n_ctx_rows, seq)
        gates, dest, block_expert, n_valid, pad_lo, pad_hi = _routing(logits, n_blocks)
        xs = _moe_dispatch(h2, dest, pad_lo, pad_hi, n_blocks)
        ys = _moe_experts(l, xs, block_expert, n_valid, w_up, b_up, w_down, b_down)
        xa = _moe_combine(alpha, dest, gates, ys, x1, mod_l, ln_g[l, 1], ln_b[l, 1], n_ctx_rows, seq)
    return xa[n_ctx_rows:].reshape(bsz, seq, d)
```

```python
import functools

import numpy as np
import jax
import jax.numpy as jnp
from jax import lax
from jax.experimental import pallas as pl
from jax.experimental.pallas import tpu as pltpu

F32 = jnp.float32
BF16 = jnp.bfloat16
HIGHEST = lax.Precision.HIGHEST

D_MODEL = 1024
CHUNK = 128
A_HEADS = 4
A_HEAD_DIM = 64
A_WIDTH = 256
B_HEADS = 4
B_DK = 64
B_DV = 128
B_QK = 256
B_WIDTH = 512
GATE_RANK = 16
GATE_TAU = 16.0
GLA_CHUNK = 64
S5_GROUPS = 16
S5_GROUP_CH = 16
S5_STATE = 64
C_WIDTH = 256
N_EXPERTS = 32
TOP_K = 4
D_EXPERT = 1024
SWIGLU_LIMIT = 7.0
SWIGLU_ALPHA = 1.702
LN_EPS = 1e-5

N_IN_PAD = 2432
COL_GL = 2304
COL_S5 = 2048

ROW_TILE = 512
SEQ_TILE = 256
S5_T = 16
S5_QUARTER = 4
MOE_BLOCK = 512
VMEM_LIMIT = 56 * 1024 * 1024


def _layer_norm(x, g, b):
    mu = jnp.mean(x, axis=-1, keepdims=True)
    xc = x - mu
    var = jnp.mean(xc * xc, axis=-1, keepdims=True)
    return xc * lax.rsqrt(var + LN_EPS) * g + b


LANES = 128
ROW_SUB = D_MODEL // LANES


def _store_row_tiles(ref, lead, val):
    n = val.shape[0]
    for j in range(ROW_SUB):
        ref[lead + (pl.ds(j, n, stride=ROW_SUB), slice(None))] = val[:, j * LANES:(j + 1) * LANES]


def _load_row_tiles(ref, lead, n):
    return jnp.concatenate(
        [ref[lead + (pl.ds(j, n, stride=ROW_SUB), slice(None))] for j in range(ROW_SUB)], axis=1)


def _row_tile(ref, lead, r):
    return ref.at[lead + (pl.ds(pl.multiple_of(r * ROW_SUB, ROW_SUB), ROW_SUB), slice(None))]


def _segment(row0, n_ctx_rows, seq):
    return jnp.where(row0 < n_ctx_rows, 0, 1 + (row0 - n_ctx_rows) // seq)


def _mod_kernel(c_ref, w_ref, b_ref, o_ref):
    c = c_ref[...]
    s = c * jax.nn.sigmoid(c)
    o_ref[0] = jnp.dot(s, w_ref[0], precision=HIGHEST, preferred_element_type=F32) + b_ref[0]


def _modulation(cvec, w_mod, b_mod):
    depth = w_mod.shape[0]
    n6 = w_mod.shape[2]
    tn = 1024
    return pl.pallas_call(
        _mod_kernel,
        grid=(depth, n6 // tn),
        in_specs=[
            pl.BlockSpec((8, D_MODEL), lambda l, j: (0, 0)),
            pl.BlockSpec((1, D_MODEL, tn), lambda l, j: (l, 0, j)),
            pl.BlockSpec((1, 1, tn), lambda l, j: (l, 0, j)),
        ],
        out_specs=pl.BlockSpec((1, 8, tn), lambda l, j: (l, 0, j)),
        out_shape=jax.ShapeDtypeStruct((depth, 8, n6), F32),
        compiler_params=pltpu.CompilerParams(
            dimension_semantics=("arbitrary", "arbitrary"), vmem_limit_bytes=VMEM_LIMIT),
    )(cvec, w_mod, b_mod.reshape(depth, 1, n6))


def _inproj_kernel(x_ref, mod_ref, w_ref, o_ref):
    m = mod_ref[0]
    h = x_ref[...] * (1.0 + m[1:2]) + m[0:1]
    o_ref[...] = jnp.dot(h.astype(BF16), w_ref[...], preferred_element_type=F32)


def _in_projection(x, mod_l, w_in_b, n_ctx_rows, seq):
    t = x.shape[0]
    tm = ROW_TILE
    return pl.pallas_call(
        _inproj_kernel,
        grid=(t // tm,),
        in_specs=[
            pl.BlockSpec((tm, D_MODEL), lambda i: (i, 0)),
            pl.BlockSpec((1, 6, D_MODEL), lambda i: (_segment(i * tm, n_ctx_rows, seq), 0, 0)),
            pl.BlockSpec((D_MODEL, N_IN_PAD), lambda i: (0, 0)),
        ],
        out_specs=pl.BlockSpec((tm, N_IN_PAD), lambda i: (i, 0)),
        out_shape=jax.ShapeDtypeStruct((t, N_IN_PAD), F32),
        compiler_params=pltpu.CompilerParams(
            dimension_semantics=("arbitrary",), vmem_limit_bytes=VMEM_LIMIT),
    )(x, mod_l, w_in_b)


def _sgu_kernel(uv_ref, g_ref, b_ref, w_ref, bias_ref, o_ref):
    tm = uv_ref.shape[0]
    u = jax.nn.gelu(uv_ref[:, :A_WIDTH])
    v = _layer_norm(jax.nn.gelu(uv_ref[:, A_WIDTH:]), g_ref[...], b_ref[...]).astype(BF16)
    head = lax.broadcasted_iota(jnp.int32, (1, A_WIDTH), 1) // A_HEAD_DIM
    for c in range(tm // CHUNK):
        rows = slice(c * CHUNK, (c + 1) * CHUNK)
        vc = v[rows]
        acc = bias_ref[...]
        for h in range(A_HEADS):
            r = jnp.dot(w_ref[h], vc, preferred_element_type=F32)
            acc = acc + jnp.where(head == h, r, 0.0)
        o_ref[rows, :] = (u[rows] * acc).astype(BF16)


def _spatial_gate(proj, ln_g, ln_b, w_s, b_s):
    t = proj.shape[0]
    tm = ROW_TILE
    bias = jnp.repeat(b_s.T, A_HEAD_DIM, axis=1)
    return pl.pallas_call(
        _sgu_kernel,
        grid=(t // tm,),
        in_specs=[
            pl.BlockSpec((tm, 2 * A_WIDTH), lambda i: (i, 0)),
            pl.BlockSpec((1, A_WIDTH), lambda i: (0, 0)),
            pl.BlockSpec((1, A_WIDTH), lambda i: (0, 0)),
            pl.BlockSpec((A_HEADS, CHUNK, CHUNK), lambda i: (0, 0, 0)),
            pl.BlockSpec((CHUNK, A_WIDTH), lambda i: (0, 0)),
        ],
        out_specs=pl.BlockSpec((tm, A_WIDTH), lambda i: (i, 0)),
        out_shape=jax.ShapeDtypeStruct((t, A_WIDTH), BF16),
        compiler_params=pltpu.CompilerParams(
            dimension_semantics=("arbitrary",), vmem_limit_bytes=VMEM_LIMIT),
    )(proj, ln_g.reshape(1, -1), ln_b.reshape(1, -1), w_s.astype(BF16), bias)


_NT = (((1,), (1,)), ((), ()))
_TN = (((0,), (0,)), ((), ()))


def _gla_direction(backward, q_ref, k_ref, v_ref, gl_ref, gup_ref, gb_ref, o_ref, s_ref):
    c_len = GLA_CHUNK
    n_rows = q_ref.shape[0]
    n_chunks = n_rows // c_len

    row = lax.broadcasted_iota(jnp.int32, (n_rows, n_rows), 0)
    col = lax.broadcasted_iota(jnp.int32, (n_rows, n_rows), 1)
    same = (row // c_len) == (col // c_len)
    if backward:
        tri = jnp.where(same & (col >= row), 1.0, 0.0).astype(BF16)
        keep = same & (col > row)
        i_last, i_mid = 0, c_len - 1 - c_len // 2
        lo = GATE_RANK
    else:
        tri = jnp.where(same & (col <= row), 1.0, 0.0).astype(BF16)
        keep = same & (col <= row)
        i_last, i_mid = c_len - 1, c_len // 2
        lo = 0

    z = jnp.dot(gl_ref[:, lo:lo + GATE_RANK], gup_ref[...], precision=HIGHEST,
                preferred_element_type=F32) + gb_ref[...]
    la = jax.nn.log_sigmoid(z) / GATE_TAU
    l1 = la.astype(BF16)
    r1 = la - l1.astype(F32)
    l2 = r1.astype(BF16)
    l3 = (r1 - l2.astype(F32)).astype(BF16)
    b = (jnp.dot(tri, l1, preferred_element_type=F32) + jnp.dot(tri, l2, preferred_element_type=F32)
         + jnp.dot(tri, l3, preferred_element_type=F32))

    def per_chunk(index):
        return jnp.concatenate(
            [jnp.broadcast_to(b[c * c_len + index:c * c_len + index + 1], (c_len, B_QK))
             for c in range(n_chunks)], axis=0)

    b_last = per_chunk(i_last)
    b_mid = per_chunk(i_mid)
    q = q_ref[...] * (B_DK ** -0.5)
    k = k_ref[...]
    q_mid = (q * jnp.exp(b - b_mid)).astype(BF16)
    k_mid = (k * jnp.exp(b_mid - b)).astype(BF16)
    q_in = (q * jnp.exp(b)).astype(BF16)
    k_out = (k * jnp.exp(b_last - b)).astype(BF16)
    order = range(n_chunks - 1, -1, -1) if backward else range(n_chunks)
    for h in range(B_HEADS):
        hk = slice(h * B_DK, (h + 1) * B_DK)
        hv = slice(h * B_DV, (h + 1) * B_DV)
        vh = v_ref[:, hv].astype(BF16)
        sc = lax.dot_general(q_mid[:, hk], k_mid[:, hk], _NT, preferred_element_type=F32)
        sc = jnp.where(keep, sc, 0.0).astype(BF16)
        o_intra = jnp.dot(sc, vh, preferred_element_type=F32)
        state = s_ref[h]
        for c in order:
            rows = slice(c * c_len, (c + 1) * c_len)
            o_ref[rows, hv] = o_intra[rows] + lax.dot_general(q_in[rows, hk], state.astype(BF16), _NT,
                                                              preferred_element_type=F32)
            decay = jnp.exp(b[c * c_len + i_last:c * c_len + i_last + 1, hk])
            state = state * decay + lax.dot_general(vh[rows], k_out[rows, hk], _TN,
                                                    preferred_element_type=F32)
        s_ref[h] = state


def _gla_kernel(qf, kf, vf, glf, qb, kb, vb, glb, gup_ref, gb_ref, of_ref, ob_ref, s_ref):
    @pl.when(pl.program_id(1) == 0)
    def _():
        s_ref[...] = jnp.zeros_like(s_ref)

    _gla_direction(False, qf, kf, vf, glf, gup_ref.at[0], gb_ref.at[0], of_ref, s_ref.at[0])
    _gla_direction(True, qb, kb, vb, glb, gup_ref.at[1], gb_ref.at[1], ob_ref, s_ref.at[1])


def _gla_block(backward, bsz, nctx_blk, nlat_blk, b, j):
    if backward:
        ctx_i = b * nctx_blk + (nctx_blk - 1 - j)
        lat_i = bsz * nctx_blk + b * nlat_blk + (nlat_blk - 1 - (j - nctx_blk))
    else:
        ctx_i = b * nctx_blk + j
        lat_i = bsz * nctx_blk + b * nlat_blk + (j - nctx_blk)
    return jnp.where(j < nctx_blk, ctx_i, lat_i)


def _gla_sweep(proj, gate_up, gate_b, bsz, ctx_len, seq):
    t = proj.shape[0]
    r = SEQ_TILE
    nctx_blk, nlat_blk = ctx_len // r, seq // r
    in_specs = []
    for backward in (False, True):
        blk = functools.partial(_gla_block, backward, bsz, nctx_blk, nlat_blk)
        in_specs += [
            pl.BlockSpec((r, B_QK), lambda b, j, blk=blk: (blk(b, j), 2)),
            pl.BlockSpec((r, B_QK), lambda b, j, blk=blk: (blk(b, j), 3)),
            pl.BlockSpec((r, B_WIDTH), lambda b, j, blk=blk: (blk(b, j), 2)),
            pl.BlockSpec((r, 128), lambda b, j, blk=blk: (blk(b, j), COL_GL // 128)),
        ]
    in_specs += [
        pl.BlockSpec((2, GATE_RANK, B_QK), lambda b, j: (0, 0, 0)),
        pl.BlockSpec((2, 1, B_QK), lambda b, j: (0, 0, 0)),
    ]
    fwd = functools.partial(_gla_block, False, bsz, nctx_blk, nlat_blk)
    bwd = functools.partial(_gla_block, True, bsz, nctx_blk, nlat_blk)
    return pl.pallas_call(
        _gla_kernel,
        grid=(bsz, nctx_blk + nlat_blk),
        in_specs=in_specs,
        out_specs=[
            pl.BlockSpec((r, B_WIDTH), lambda b, j: (fwd(b, j), 0)),
            pl.BlockSpec((r, B_WIDTH), lambda b, j: (bwd(b, j), 0)),
        ],
        out_shape=[jax.ShapeDtypeStruct((t, B_WIDTH), F32), jax.ShapeDtypeStruct((t, B_WIDTH), F32)],
        scratch_shapes=[pltpu.VMEM((2, B_HEADS, B_DV, B_DK), F32)],
        compiler_params=pltpu.CompilerParams(
            dimension_semantics=("arbitrary", "arbitrary"), vmem_limit_bytes=VMEM_LIMIT),
    )(*([proj] * 8), gate_up, gate_b.reshape(2, 1, -1))


def _s5_matrices(lam_re, lam_im, log_dt, b_re, b_im, c_re, c_im):
    tc = S5_T
    lam = lax.complex(lam_re.astype(F32), lam_im.astype(F32))
    dt = jnp.exp(log_dt.astype(F32))
    bm = lax.complex(b_re.astype(F32), b_im.astype(F32))
    cm = lax.complex(c_re.astype(F32), c_im.astype(F32))
    ldt = lam * dt[..., None]
    lam_bar = jnp.exp(ldt)
    b_bar = ((lam_bar - 1.0) / lam)[..., None] * bm
    steps = jnp.arange(tc + 1, dtype=F32)
    pw = jnp.exp(ldt[:, None] * steps[None, :, None, None])
    kern = jnp.real(jnp.einsum('dgip,dtgp,dgpj->dgtij', cm, pw[:, :tc], b_bar))
    s_i = jnp.arange(tc)[:, None]
    t_i = jnp.arange(tc)[None, :]
    g_n = lam.shape[1]
    hh = S5_GROUP_CH

    def toeplitz(kd, lag, ok):
        m = kd[:, jnp.clip(lag, 0, tc - 1)]
        m = jnp.where(ok[None, :, :, None, None], m, 0.0)
        return m.transpose(0, 1, 4, 2, 3).reshape(g_n, tc * hh, tc * hh)

    mt = toeplitz(kern[0], t_i - s_i, t_i >= s_i) + toeplitz(kern[1], s_i - t_i, s_i >= t_i)

    def state_in(d, powers):
        w = pw[d][powers][:, :, :, None] * b_bar[d][None]
        w = w.transpose(1, 0, 3, 2).reshape(g_n, tc * hh, S5_STATE)
        re, im = jnp.real(w), jnp.imag(w)
        return jnp.concatenate([re, im, im, re], axis=-1)

    qt = jnp.concatenate([state_in(0, tc - 1 - jnp.arange(tc)), state_in(1, jnp.arange(tc))], axis=-1)

    def state_out(d, powers):
        w = cm[d][:, None] * pw[d][powers].transpose(1, 0, 2)[:, :, None, :]
        w = w.reshape(g_n, tc * hh, S5_STATE).transpose(0, 2, 1)
        return jnp.concatenate([jnp.real(w), -jnp.imag(w)], axis=1)

    pt = jnp.concatenate([state_out(0, 1 + jnp.arange(tc)), state_out(1, tc - jnp.arange(tc))], axis=1)

    a = pw[:, tc]
    ar, ai = jnp.real(a), jnp.imag(a)
    a1 = jnp.concatenate([ar, ar], axis=-1).reshape(2, -1)
    a2 = jnp.concatenate([-ai, ai], axis=-1).reshape(2, -1)
    a3 = jnp.concatenate([ai, -ai], axis=-1).reshape(2, -1)
    ac = jnp.stack([a1, a2, a3], axis=1)
    return mt.astype(BF16), qt.astype(BF16), pt.astype(BF16), ac.reshape(2, 3, 1, -1)


def _s5_kernel(uc_ref, ul_ref, mt_ref, qt_ref, pt_ref, ac_ref, yc_ref, yl_ref, ef, esf, eb, esb):
    nq = uc_ref.shape[0]
    nc_ctx, nc_lat = uc_ref.shape[1], ul_ref.shape[1]
    n = nc_ctx + nc_lat
    w = 2 * S5_STATE
    for gi in range(nq):
        lanes = slice(gi * w, (gi + 1) * w)
        rc = jnp.dot(uc_ref[gi], qt_ref[gi], preferred_element_type=F32)
        rl = jnp.dot(ul_ref[gi], qt_ref[gi], preferred_element_type=F32)
        ef[0:nc_ctx, lanes] = rc[:, 0:w]
        esf[0:nc_ctx, lanes] = rc[:, w:2 * w]
        ef[nc_ctx:n, lanes] = rl[:, 0:w]
        esf[nc_ctx:n, lanes] = rl[:, w:2 * w]
        eb[0:nc_lat, lanes] = rl[:, 2 * w:3 * w]
        esb[0:nc_lat, lanes] = rl[:, 3 * w:4 * w]
        eb[nc_lat:n, lanes] = rc[:, 2 * w:3 * w]
        esb[nc_lat:n, lanes] = rc[:, 3 * w:4 * w]

    a1f, a2f, a3f = ac_ref[0, 0], ac_ref[0, 1], ac_ref[0, 2]
    a1b, a2b, a3b = ac_ref[1, 0], ac_ref[1, 1], ac_ref[1, 2]

    def body(i, carry):
        hf, hsf, hb, hsb = carry
        rf = pl.ds(i, 1)
        rb = pl.ds(n - 1 - i, 1)
        e_f, es_f = ef[rf, :], esf[rf, :]
        e_b, es_b = eb[rb, :], esb[rb, :]
        ef[rf, :] = hf
        eb[rb, :] = hb
        return (a1f * hf + a2f * hsf + e_f, a1f * hsf + a3f * hf + es_f,
                a1b * hb + a2b * hsb + e_b, a1b * hsb + a3b * hb + es_b)

    zero = jnp.zeros((1, nq * w), F32)
    lax.fori_loop(0, n, body, (zero, zero, zero, zero))

    for gi in range(nq):
        lanes = slice(gi * w, (gi + 1) * w)
        hc = jnp.concatenate([ef[0:nc_ctx, lanes], eb[nc_lat:n, lanes]], axis=1).astype(BF16)
        hl = jnp.concatenate([ef[nc_ctx:n, lanes], eb[0:nc_lat, lanes]], axis=1).astype(BF16)
        yc_ref[gi] = (jnp.dot(uc_ref[gi], mt_ref[gi], preferred_element_type=F32)
                      + jnp.dot(hc, pt_ref[gi], preferred_element_type=F32))
        yl_ref[gi] = (jnp.dot(ul_ref[gi], mt_ref[gi], preferred_element_type=F32)
                      + jnp.dot(hl, pt_ref[gi], preferred_element_type=F32))


def _s5_scan(proj, mats, bsz, ctx_len, seq):
    mt, qt, pt, ac = mats
    g_n, tc, hh = S5_GROUPS, S5_T, S5_GROUP_CH
    n_ctx_rows = bsz * ctx_len
    u = proj[:, COL_S5:COL_S5 + C_WIDTH].astype(BF16)

    def to_rows(part):
        r = part.shape[0]
        return part.reshape(r // tc, tc, g_n, hh).transpose(2, 0, 1, 3).reshape(g_n, r // tc, tc * hh)

    uc, ul = to_rows(u[:n_ctx_rows]), to_rows(u[n_ctx_rows:])
    nc_ctx, nc_lat = ctx_len // tc, seq // tc
    nq = S5_QUARTER
    wq = nq * 2 * S5_STATE
    n = nc_ctx + nc_lat
    yc, yl = pl.pallas_call(
        _s5_kernel,
        grid=(bsz, g_n // nq),
        in_specs=[
            pl.BlockSpec((nq, nc_ctx, tc * hh), lambda b, qi: (qi, b, 0)),
            pl.BlockSpec((nq, nc_lat, tc * hh), lambda b, qi: (qi, b, 0)),
            pl.BlockSpec((nq, tc * hh, tc * hh), lambda b, qi: (qi, 0, 0)),
            pl.BlockSpec((nq, tc * hh, 8 * S5_STATE), lambda b, qi: (qi, 0, 0)),
            pl.BlockSpec((nq, 4 * S5_STATE, tc * hh), lambda b, qi: (qi, 0, 0)),
            pl.BlockSpec((2, 3, 1, wq), lambda b, qi: (0, 0, 0, qi)),
        ],
        out_specs=[
            pl.BlockSpec((nq, nc_ctx, tc * hh), lambda b, qi: (qi, b, 0)),
            pl.BlockSpec((nq, nc_lat, tc * hh), lambda b, qi: (qi, b, 0)),
        ],
        out_shape=[
            jax.ShapeDtypeStruct((g_n, bsz * nc_ctx, tc * hh), F32),
            jax.ShapeDtypeStruct((g_n, bsz * nc_lat, tc * hh), F32),
        ],
        scratch_shapes=[pltpu.VMEM((n, wq), F32) for _ in range(4)],
        compiler_params=pltpu.CompilerParams(
            dimension_semantics=("arbitrary", "arbitrary"), vmem_limit_bytes=VMEM_LIMIT),
    )(uc, ul, mt, qt, pt, ac)

    def from_rows(y):
        r = y.shape[1]
        return y.reshape(g_n, r, tc, hh).transpose(1, 2, 0, 3).reshape(r * tc, g_n * hh)

    return jnp.concatenate([from_rows(yc), from_rows(yl)], axis=0)


def _outproj_kernel(alpha, a_ref, of_ref, ob_ref, g_ref, ng_ref, y_ref, u_ref, x_ref, mod_ref, d_ref, gw_ref,
                    gb_ref, wo_ref, lng_ref, lnb_ref, rwh_ref, rwl_ref, rb_ref, x1_ref, h2_ref, lg_ref):
    m = mod_ref[0]
    heads = []
    for h in range(B_HEADS):
        hv = slice(h * B_DV, (h + 1) * B_DV)
        o = of_ref[:, hv] + ob_ref[:, hv]
        heads.append(o * lax.rsqrt(jnp.mean(o * o, axis=-1, keepdims=True) + LN_EPS))
    gate = g_ref[...]
    gla = (jnp.concatenate(heads, axis=1) * ng_ref[...] * (gate * jax.nn.sigmoid(gate))).astype(BF16)
    y = jax.nn.gelu(y_ref[...] + d_ref[...] * u_ref[...])
    s5 = y * jax.nn.sigmoid(jnp.dot(y.astype(BF16), gw_ref[...], preferred_element_type=F32) + gb_ref[...])
    mix = (jnp.dot(a_ref[...], wo_ref[0:A_WIDTH, :], preferred_element_type=F32)
           + jnp.dot(gla, wo_ref[A_WIDTH:A_WIDTH + B_WIDTH, :], preferred_element_type=F32)
           + jnp.dot(s5.astype(BF16), wo_ref[A_WIDTH + B_WIDTH:, :], preferred_element_type=F32))
    x1 = _layer_norm(alpha * x_ref[...] + m[2:3] * mix, lng_ref[...], lnb_ref[...])
    x1_ref[...] = x1
    h2 = x1 * (1.0 + m[4:5]) + m[3:4]
    _store_row_tiles(h2_ref, (), h2)
    h_hi = h2.astype(BF16)
    h_lo = (h2 - h_hi.astype(F32)).astype(BF16)
    lg_ref[...] = (jnp.dot(h_hi, rwh_ref[...], preferred_element_type=F32)
                   + jnp.dot(h_lo, rwh_ref[...], preferred_element_type=F32)
                   + jnp.dot(h_hi, rwl_ref[...], preferred_element_type=F32) + rb_ref[...])


def _out_projection(alpha, a_out, o_fwd, o_bwd, norm_g, y_s5, proj, x, mod_l, s5_d, glu_w, glu_b, w_out_b, ln_g,
                    ln_b, router_w, router_b, n_ctx_rows, seq):
    t = x.shape[0]
    tm = ROW_TILE
    rw = jnp.zeros((D_MODEL, 128), F32).at[:, :N_EXPERTS].set(router_w)
    rw_hi = rw.astype(BF16)
    rw_lo = (rw - rw_hi.astype(F32)).astype(BF16)
    rb = jnp.zeros((1, 128), F32).at[0, :N_EXPERTS].set(router_b)
    row = lambda i: (i, 0)
    fixed = lambda i: (0, 0)
    return pl.pallas_call(
        functools.partial(_outproj_kernel, alpha),
        grid=(t // tm,),
        in_specs=[
            pl.BlockSpec((tm, A_WIDTH), row),
            pl.BlockSpec((tm, B_WIDTH), row),
            pl.BlockSpec((tm, B_WIDTH), row),
            pl.BlockSpec((tm, B_WIDTH), lambda i: (i, 3)),
            pl.BlockSpec((1, B_WIDTH), fixed),
            pl.BlockSpec((tm, C_WIDTH), row),
            pl.BlockSpec((tm, C_WIDTH), lambda i: (i, COL_S5 // C_WIDTH)),
            pl.BlockSpec((tm, D_MODEL), row),
            pl.BlockSpec((1, 6, D_MODEL), lambda i: (_segment(i * tm, n_ctx_rows, seq), 0, 0)),
            pl.BlockSpec((1, C_WIDTH), fixed),
            pl.BlockSpec((C_WIDTH, C_WIDTH), fixed),
            pl.BlockSpec((1, C_WIDTH), fixed),
            pl.BlockSpec((D_MODEL, D_MODEL), fixed),
            pl.BlockSpec((1, D_MODEL), fixed),
            pl.BlockSpec((1, D_MODEL), fixed),
            pl.BlockSpec((D_MODEL, 128), fixed),
            pl.BlockSpec((D_MODEL, 128), fixed),
            pl.BlockSpec((1, 128), fixed),
        ],
        out_specs=[
            pl.BlockSpec((tm, D_MODEL), row),
            pl.BlockSpec((tm * ROW_SUB, LANES), row),
            pl.BlockSpec((tm, 128), row),
        ],
        out_shape=[
            jax.ShapeDtypeStruct((t, D_MODEL), F32),
            jax.ShapeDtypeStruct((t * ROW_SUB, LANES), F32),
            jax.ShapeDtypeStruct((t, 128), F32),
        ],
        compiler_params=pltpu.CompilerParams(
            dimension_semantics=("arbitrary",), vmem_limit_bytes=VMEM_LIMIT),
    )(a_out, o_fwd, o_bwd, proj, norm_g.reshape(1, -1), y_s5, proj, x, mod_l, s5_d.reshape(1, -1), glu_w.astype(BF16), glu_b.reshape(1, -1),
      w_out_b, ln_g.reshape(1, -1), ln_b.reshape(1, -1), rw_hi, rw_lo, rb)


def _route_kernel(lg_ref, gate_ref, pos_ref, tbase_ref, tcnt_ref, cnt_ref, base, before, below):
    tm = lg_ref.shape[0]

    @pl.when(pl.program_id(0) == 0)
    def _():
        base[...] = jnp.zeros_like(base)
        r = lax.broadcasted_iota(jnp.int32, (tm, tm), 0)
        c = lax.broadcasted_iota(jnp.int32, (tm, tm), 1)
        before[...] = jnp.where(r < c, 1.0, 0.0).astype(BF16)
        r = lax.broadcasted_iota(jnp.int32, (N_EXPERTS, N_EXPERTS), 0)
        c = lax.broadcasted_iota(jnp.int32, (N_EXPERTS, N_EXPERTS), 1)
        below[...] = jnp.where(c < r, 1.0, 0.0)

    logit = jnp.transpose(lg_ref[...])[:N_EXPERTS]
    eid = lax.broadcasted_iota(jnp.int32, (N_EXPERTS, tm), 0)
    vals, hots = [], []
    work = logit
    for kk in range(TOP_K):
        m = jnp.max(work, axis=0, keepdims=True)
        ix = jnp.min(jnp.where(work == m, eid, N_EXPERTS), axis=0, keepdims=True)
        hot = eid == ix
        vals.append(m)
        hots.append(hot)
        work = jnp.where(hot, -jnp.inf, work)
    ex = [jnp.exp(v - vals[0]) for v in vals]
    den = ex[0] + ex[1] + ex[2] + ex[3]
    member = jnp.zeros((N_EXPERTS, tm), F32)
    for kk in range(TOP_K):
        gate_ref[kk:kk + 1, :] = ex[kk] / den
        member = member + jnp.where(hots[kk], 1.0, 0.0)
    tile_cnt = jnp.broadcast_to(jnp.sum(member, axis=1, keepdims=True), (N_EXPERTS, LANES))
    group_off = jnp.dot(below[...], tile_cnt, precision=HIGHEST, preferred_element_type=F32)[:, 0:1]
    in_group = jnp.dot(member.astype(BF16), before[...], preferred_element_type=F32)
    for kk in range(TOP_K):
        pos_ref[kk:kk + 1, :] = jnp.sum(jnp.where(hots[kk], group_off + in_group, 0.0), axis=0,
                                        keepdims=True).astype(jnp.int32)
    tbase_ref[...] = jnp.broadcast_to(base[...], tbase_ref.shape)
    tcnt_ref[...] = tile_cnt
    total = base[...] + tile_cnt[:, 0:1]
    base[...] = total
    cnt_ref[...] = jnp.broadcast_to(total, cnt_ref.shape)


def _routing(logits, n_blocks):
    t = logits.shape[0]
    tm = ROW_TILE
    nt = t // tm
    gates, pos, tbase, tcnt, cnt = pl.pallas_call(
        _route_kernel,
        grid=(nt,),
        in_specs=[pl.BlockSpec((tm, LANES), lambda i: (i, 0))],
        out_specs=[
            pl.BlockSpec((TOP_K, tm), lambda i: (0, i)),
            pl.BlockSpec((TOP_K, tm), lambda i: (0, i)),
            pl.BlockSpec((N_EXPERTS, LANES), lambda i: (i, 0)),
            pl.BlockSpec((N_EXPERTS, LANES), lambda i: (i, 0)),
            pl.BlockSpec((N_EXPERTS, LANES), lambda i: (0, 0)),
        ],
        out_shape=[
            jax.ShapeDtypeStruct((TOP_K, t), F32),
            jax.ShapeDtypeStruct((TOP_K, t), jnp.int32),
            jax.ShapeDtypeStruct((nt * N_EXPERTS, LANES), F32),
            jax.ShapeDtypeStruct((nt * N_EXPERTS, LANES), F32),
            jax.ShapeDtypeStruct((N_EXPERTS, LANES), F32),
        ],
        scratch_shapes=[pltpu.VMEM((N_EXPERTS, 1), F32), pltpu.VMEM((tm, tm), BF16),
                        pltpu.VMEM((N_EXPERTS, N_EXPERTS), F32)],
        compiler_params=pltpu.CompilerParams(
            dimension_semantics=("arbitrary",), vmem_limit_bytes=VMEM_LIMIT),
    )(logits)
    counts = cnt[:, 0].astype(jnp.int32)
    padded = (counts + MOE_BLOCK - 1) // MOE_BLOCK * MOE_BLOCK
    padded_end = jnp.cumsum(padded)
    padded_start = (padded_end - padded).astype(jnp.int32)
    first_slot = jnp.arange(n_blocks, dtype=jnp.int32) * MOE_BLOCK
    block_expert = jnp.minimum(jnp.sum((padded_end[None, :] <= first_slot[:, None]).astype(jnp.int32), axis=1),
                               N_EXPERTS - 1).astype(jnp.int32)
    n_valid = (padded_end[-1] // MOE_BLOCK).astype(jnp.int32).reshape(1)
    pad_lo = (padded_start + counts).astype(jnp.int32)
    route = dict(
        gates=gates.T.reshape(nt, 1, tm * TOP_K),
        pos=pos.T.reshape(nt, 1, tm * TOP_K),
        tile_base=tbase[:, 0].astype(jnp.int32).reshape(nt, 1, N_EXPERTS),
        tile_cnt=tcnt[:, 0].astype(jnp.int32).reshape(nt, 1, N_EXPERTS),
        start=padded_start, pad_lo=pad_lo, pad_hi=padded_end.astype(jnp.int32))
    return route, block_expert, n_valid


def _expert_runs(tile_cnt_ref, tile_base_ref, start_ref, copy):
    off = 0
    for e in range(N_EXPERTS):
        n = tile_cnt_ref[0, 0, e]
        slot0 = start_ref[e] + tile_base_ref[0, 0, e]
        for bit in range(ROW_TILE.bit_length() - 1, -1, -1):
            size = 1 << bit
            done = (n >> (bit + 1)) << (bit + 1)

            @pl.when(((n >> bit) & 1) == 1)
            def _(off=off, done=done, slot0=slot0, size=size):
                copy(off + done, slot0 + done, size)
        off = off + n


def _rows(ref, lead, row0, n):
    return ref.at[lead + (pl.ds(pl.multiple_of(row0 * ROW_SUB, ROW_SUB), n * ROW_SUB), slice(None))]


def _dispatch_kernel(start_ref, lo_ref, hi_ref, pos_ref, tbase_ref, tcnt_ref, h_ref, o_ref, stage, zbuf, sem, zsem):
    i = pl.program_id(0)
    nt = pl.num_programs(0)
    tm = ROW_TILE
    slot = i % 2
    blk_rows = MOE_BLOCK * ROW_SUB
    n_blocks = o_ref.shape[0] // blk_rows

    def wait_stage(s):
        pltpu.make_async_copy(stage.at[s], stage.at[s], sem.at[s]).wait()

    @pl.when(i >= 2)
    def _():
        wait_stage(slot)

    def place(r, carry):
        row = h_ref[pl.ds(pl.multiple_of(r * ROW_SUB, ROW_SUB), ROW_SUB), :]
        for kk in range(TOP_K):
            p = pos_ref[0, 0, r * TOP_K + kk]
            stage[slot, pl.ds(pl.multiple_of(p * ROW_SUB, ROW_SUB), ROW_SUB), :] = row
        return carry
    lax.fori_loop(0, tm, place, 0, unroll=4)

    def copy(stage_row, slot_row, n):
        pltpu.make_async_copy(_rows(stage, (slot,), stage_row, n), _rows(o_ref, (), slot_row, n),
                              sem.at[slot]).start()
    _expert_runs(tcnt_ref, tbase_ref, start_ref, copy)

    @pl.when(i == nt - 1)
    def _():
        wait_stage(slot)

        @pl.when(i >= 1)
        def _():
            wait_stage(1 - slot)
        zbuf[...] = jnp.zeros_like(zbuf)
        zrow = zbuf.at[pl.ds(0, ROW_SUB), :]
        for e in range(N_EXPERTS):
            def fill(s, carry):
                pltpu.make_async_copy(zrow, _row_tile(o_ref, (), s), zsem).start()
                return carry
            lax.fori_loop(lo_ref[e], hi_ref[e], fill, 0)
        for e in range(N_EXPERTS):
            def drain(s, carry):
                pltpu.make_async_copy(zrow, zrow, zsem).wait()
                return carry
            lax.fori_loop(lo_ref[e], hi_ref[e], drain, 0)
        used = hi_ref[N_EXPERTS - 1] // MOE_BLOCK

        def fill_block(j, carry):
            rows = pl.ds(pl.multiple_of(j * blk_rows, blk_rows), blk_rows)
            pltpu.make_async_copy(zbuf, o_ref.at[rows, :], zsem).start()
            return carry
        lax.fori_loop(used, n_blocks, fill_block, 0)

        def drain_block(j, carry):
            pltpu.make_async_copy(zbuf, zbuf, zsem).wait()
            return carry
        lax.fori_loop(used, n_blocks, drain_block, 0)


def _moe_dispatch(h2t, route, n_blocks):
    nt = route['pos'].shape[0]
    tm = ROW_TILE
    smem = lambda width: pl.BlockSpec((1, 1, width), lambda i, *_: (i, 0, 0), memory_space=pltpu.SMEM)
    grid_spec = pltpu.PrefetchScalarGridSpec(
        num_scalar_prefetch=3,
        grid=(nt,),
        in_specs=[
            smem(tm * TOP_K), smem(N_EXPERTS), smem(N_EXPERTS),
            pl.BlockSpec((tm * ROW_SUB, LANES), lambda i, *_: (i, 0)),
        ],
        out_specs=pl.BlockSpec(memory_space=pl.ANY),
        scratch_shapes=[
            pltpu.VMEM((2, tm * TOP_K * ROW_SUB, LANES), F32),
            pltpu.VMEM((MOE_BLOCK * ROW_SUB, LANES), F32),
            pltpu.SemaphoreType.DMA((2,)),
            pltpu.SemaphoreType.DMA(()),
        ],
    )
    return pl.pallas_call(
        _dispatch_kernel,
        grid_spec=grid_spec,
        out_shape=jax.ShapeDtypeStruct((n_blocks * MOE_BLOCK * ROW_SUB, LANES), F32),
        compiler_params=pltpu.CompilerParams(
            dimension_semantics=("arbitrary",), vmem_limit_bytes=VMEM_LIMIT),
    )(route['start'], route['pad_lo'], route['pad_hi'], route['pos'], route['tile_base'], route['tile_cnt'], h2t)


def _moe_kernel(be_ref, nv_ref, x_ref, wu_ref, bu_ref, wd_ref, bd_ref, o_ref, wu_b, wd_b):
    i = pl.program_id(0)
    n_valid = nv_ref[0]

    @pl.when(i < n_valid)
    def _():
        first = jnp.logical_or(i == 0, be_ref[i] != be_ref[jnp.maximum(i - 1, 0)])

        @pl.when(first)
        def _():
            rows = 64

            def cast(r, carry):
                rs = pl.ds(pl.multiple_of(r * rows, rows), rows)
                wu_b[rs, :] = wu_ref[0, 0, rs, :].astype(BF16)
                wd_b[rs, :] = wd_ref[0, 0, rs, :].astype(BF16)
                return carry
            lax.fori_loop(0, D_MODEL // rows, cast, 0)

        x = _load_row_tiles(x_ref, (), MOE_BLOCK).astype(BF16)
        acc = jnp.zeros((MOE_BLOCK, D_MODEL), F32) + bd_ref[0, 0]
        cw = 512
        for jc in range(D_EXPERT // cw):
            cg = slice(jc * cw, (jc + 1) * cw)
            cl = slice(D_EXPERT + jc * cw, D_EXPERT + (jc + 1) * cw)
            ug = jnp.dot(x, wu_b[:, cg], preferred_element_type=F32) + bu_ref[0, 0, :, cg]
            ul = jnp.dot(x, wu_b[:, cl], preferred_element_type=F32) + bu_ref[0, 0, :, cl]
            xg = jnp.minimum(ug, SWIGLU_LIMIT)
            xl = jnp.clip(ul, -SWIGLU_LIMIT, SWIGLU_LIMIT)
            act = xg * jax.nn.sigmoid(SWIGLU_ALPHA * xg) * (xl + 1.0)
            acc = acc + jnp.dot(act.astype(BF16), wd_b[cg, :], preferred_element_type=F32)
        _store_row_tiles(o_ref, (), acc)

    @pl.when(i >= n_valid)
    def _():
        o_ref[...] = jnp.zeros_like(o_ref)


def _moe_experts(layer, xs, block_expert, n_valid, w_up, b_up, w_down, b_down):
    n_blocks = block_expert.shape[0]
    depth = w_up.shape[0]
    grid_spec = pltpu.PrefetchScalarGridSpec(
        num_scalar_prefetch=2,
        grid=(n_blocks,),
        in_specs=[
            pl.BlockSpec((MOE_BLOCK * ROW_SUB, LANES),
                         lambda i, be, nv: (jnp.minimum(i, jnp.maximum(nv[0] - 1, 0)), 0)),
            pl.BlockSpec((1, 1, D_MODEL, 2 * D_EXPERT), lambda i, be, nv: (layer, be[i], 0, 0)),
            pl.BlockSpec((1, 1, 1, 2 * D_EXPERT), lambda i, be, nv: (layer, be[i], 0, 0)),
            pl.BlockSpec((1, 1, D_EXPERT, D_MODEL), lambda i, be, nv: (layer, be[i], 0, 0)),
            pl.BlockSpec((1, 1, 1, D_MODEL), lambda i, be, nv: (layer, be[i], 0, 0)),
        ],
        out_specs=pl.BlockSpec((MOE_BLOCK * ROW_SUB, LANES), lambda i, be, nv: (i, 0)),
        scratch_shapes=[
            pltpu.VMEM((D_MODEL, 2 * D_EXPERT), BF16),
            pltpu.VMEM((D_EXPERT, D_MODEL), BF16),
        ],
    )
    return pl.pallas_call(
        _moe_kernel,
        grid_spec=grid_spec,
        out_shape=jax.ShapeDtypeStruct((n_blocks * MOE_BLOCK * ROW_SUB, LANES), F32),
        compiler_params=pltpu.CompilerParams(
            dimension_semantics=("arbitrary",), vmem_limit_bytes=VMEM_LIMIT),
    )(block_expert, n_valid, xs, w_up, b_up.reshape(depth, N_EXPERTS, 1, -1), w_down,
      b_down.reshape(depth, N_EXPERTS, 1, -1))


def _combine_kernel(alpha, start_ref, pos_ref, gate_ref, tbase_ref, tcnt_ref, tbase_n_ref, tcnt_n_ref, y_ref, x_ref,
                    mod_ref, lng_ref, lnb_ref, o_ref, stage, frow, sem):
    i = pl.program_id(0)
    nt = pl.num_programs(0)
    tm = ROW_TILE
    slot = i % 2

    def fetch(cnt_ref, base_ref, s):
        def copy(stage_row, slot_row, n):
            pltpu.make_async_copy(_rows(y_ref, (), slot_row, n), _rows(stage, (s,), stage_row, n),
                                  sem.at[s]).start()
        _expert_runs(cnt_ref, base_ref, start_ref, copy)

    @pl.when(i == 0)
    def _():
        fetch(tcnt_ref, tbase_ref, 0)

    @pl.when(i + 1 < nt)
    def _():
        fetch(tcnt_n_ref, tbase_n_ref, 1 - slot)

    pltpu.make_async_copy(stage.at[slot], stage.at[slot], sem.at[slot]).wait()

    def mix(r, carry):
        acc = None
        for kk in range(TOP_K):
            p = pos_ref[0, 0, r * TOP_K + kk]
            term = gate_ref[0, 0, r * TOP_K + kk] * stage[slot, pl.ds(pl.multiple_of(p * ROW_SUB, ROW_SUB), ROW_SUB), :]
            acc = term if acc is None else acc + term
        frow[pl.ds(pl.multiple_of(r * ROW_SUB, ROW_SUB), ROW_SUB), :] = acc
        return carry
    lax.fori_loop(0, tm, mix, 0, unroll=4)

    m = mod_ref[0]
    f = _load_row_tiles(frow, (), tm)
    o_ref[...] = _layer_norm(alpha * x_ref[...] + m[5:6] * f, lng_ref[...], lnb_ref[...])


def _moe_combine(alpha, route, ys, x1, mod_l, ln_g, ln_b, n_ctx_rows, seq):
    t = x1.shape[0]
    tm = ROW_TILE
    nt = t // tm
    cur = lambda width: pl.BlockSpec((1, 1, width), lambda i, *_: (i, 0, 0), memory_space=pltpu.SMEM)
    nxt = lambda width: pl.BlockSpec((1, 1, width), lambda i, *_: (jnp.minimum(i + 1, nt - 1), 0, 0),
                                     memory_space=pltpu.SMEM)
    grid_spec = pltpu.PrefetchScalarGridSpec(
        num_scalar_prefetch=1,
        grid=(nt,),
        in_specs=[
            cur(tm * TOP_K), cur(tm * TOP_K), cur(N_EXPERTS), cur(N_EXPERTS), nxt(N_EXPERTS), nxt(N_EXPERTS),
            pl.BlockSpec(memory_space=pl.ANY),
            pl.BlockSpec((tm, D_MODEL), lambda i, *_: (i, 0)),
            pl.BlockSpec((1, 6, D_MODEL), lambda i, *_: (_segment(i * tm, n_ctx_rows, seq), 0, 0)),
            pl.BlockSpec((1, D_MODEL), lambda i, *_: (0, 0)),
            pl.BlockSpec((1, D_MODEL), lambda i, *_: (0, 0)),
        ],
        out_specs=pl.BlockSpec((tm, D_MODEL), lambda i, *_: (i, 0)),
        scratch_shapes=[
            pltpu.VMEM((2, tm * TOP_K * ROW_SUB, LANES), F32),
            pltpu.VMEM((tm * ROW_SUB, LANES), F32),
            pltpu.SemaphoreType.DMA((2,)),
        ],
    )
    return pl.pallas_call(
        functools.partial(_combine_kernel, alpha),
        grid_spec=grid_spec,
        out_shape=jax.ShapeDtypeStruct((t, D_MODEL), F32),
        compiler_params=pltpu.CompilerParams(
            dimension_semantics=("arbitrary",), vmem_limit_bytes=VMEM_LIMIT),
    )(route['start'], route['pos'], route['gates'], route['tile_base'], route['tile_cnt'], route['tile_base'],
      route['tile_cnt'], ys, x1, mod_l, ln_g.reshape(1, -1), ln_b.reshape(1, -1))


def kernel(x, c, ctx, c_ctx, w_mod, b_mod, w_in, sgu_ln_g, sgu_ln_b, sgu_w, sgu_b, gla_gate_up, gla_gate_b,
           gla_norm_g, s5_lam_re, s5_lam_im, s5_log_dt, s5_b_re, s5_b_im, s5_c_re, s5_c_im, s5_d, s5_glu_w,
           s5_glu_b, w_out, ln_g, ln_b, router_w, router_b, w_up, b_up, w_down, b_down):
    bsz, seq, d = x.shape
    ctx_len = ctx.shape[1]
    depth = w_in.shape[0]
    alpha = float((2 * depth) ** 0.25)
    n_ctx_rows = bsz * ctx_len
    t = n_ctx_rows + bsz * seq
    assert d == D_MODEL and bsz + 1 <= 8
    assert n_ctx_rows % ROW_TILE == 0 and seq % ROW_TILE == 0
    assert ctx_len % SEQ_TILE == 0 and seq % SEQ_TILE == 0

    xa = jnp.concatenate([ctx.reshape(n_ctx_rows, d), x.reshape(bsz * seq, d)], axis=0)
    cvec = jnp.zeros((8, d), F32).at[0].set(c_ctx).at[1:1 + bsz].set(c)
    mod = _modulation(cvec, w_mod, b_mod).reshape(depth, 8, 6, d)

    o = np.cumsum((0, A_WIDTH, A_WIDTH, B_QK, B_QK, B_WIDTH, B_WIDTH, 2 * GATE_RANK, C_WIDTH))
    w_in_r = jnp.concatenate(
        [w_in[:, :, o[0]:o[6]], w_in[:, :, o[7]:o[8]], w_in[:, :, o[6]:o[7]],
         jnp.zeros((depth, d, N_IN_PAD - int(o[8])), w_in.dtype)], axis=-1).astype(BF16)
    w_out_b = w_out.astype(BF16)

    n_assign = t * TOP_K
    n_blocks = -(-(n_assign + N_EXPERTS * (MOE_BLOCK - 1)) // MOE_BLOCK)

    for l in range(depth):
        mod_l = mod[l]
        proj = _in_projection(xa, mod_l, w_in_r[l], n_ctx_rows, seq)
        a_out = _spatial_gate(proj, sgu_ln_g[l], sgu_ln_b[l], sgu_w[l], sgu_b[l])
        o_fwd, o_bwd = _gla_sweep(proj, gla_gate_up[l], gla_gate_b[l], bsz, ctx_len, seq)
        mats = _s5_matrices(s5_lam_re[l], s5_lam_im[l], s5_log_dt[l], s5_b_re[l], s5_b_im[l],
                            s5_c_re[l], s5_c_im[l])
        y_s5 = _s5_scan(proj, mats, bsz, ctx_len, seq)
        x1, h2, logits = _out_projection(alpha, a_out, o_fwd, o_bwd, gla_norm_g[l], y_s5, proj, xa, mod_l, s5_d[l], s5_glu_w[l],
                                         s5_glu_b[l], w_out_b[l], ln_g[l, 0], ln_b[l, 0], router_w[l],
                                         router_b[l], n_ctx_rows, seq)
        route, block_expert, n_valid = _routing(logits, n_blocks)
        xs = _moe_dispatch(h2, route, n_blocks)
        ys = _moe_experts(l, xs, block_expert, n_valid, w_up, b_up, w_down, b_down)
        xa = _moe_combine(alpha, route, ys, x1, mod_l, ln_g[l, 1], ln_b[l, 1], n_ctx_rows, seq)
    return xa[n_ctx_rows:].reshape(bsz, seq, d)
```

```python
import functools

import numpy as np
import jax
import jax.numpy as jnp
from jax import lax
from jax.experimental import pallas as pl
from jax.experimental.pallas import tpu as pltpu

F32 = jnp.float32
BF16 = jnp.bfloat16
HIGHEST = lax.Precision.HIGHEST

D_MODEL = 1024
CHUNK = 128
A_HEADS = 4
A_HEAD_DIM = 64
A_WIDTH = 256
B_HEADS = 4
B_DK = 64
B_DV = 128
B_QK = 256
B_WIDTH = 512
GATE_RANK = 16
GATE_TAU = 16.0
GLA_CHUNK = 64
S5_GROUPS = 16
S5_GROUP_CH = 16
S5_STATE = 64
C_WIDTH = 256
N_EXPERTS = 32
TOP_K = 4
D_EXPERT = 1024
SWIGLU_LIMIT = 7.0
SWIGLU_ALPHA = 1.702
LN_EPS = 1e-5

N_IN_PAD = 2432
COL_GL = 2304
COL_S5 = 2048

ROW_TILE = 512
SEQ_TILE = 256
S5_T = 16
S5_QUARTER = 4
MOE_BLOCK = 512
VMEM_LIMIT = 56 * 1024 * 1024


def _layer_norm(x, g, b):
    mu = jnp.mean(x, axis=-1, keepdims=True)
    xc = x - mu
    var = jnp.mean(xc * xc, axis=-1, keepdims=True)
    return xc * lax.rsqrt(var + LN_EPS) * g + b


LANES = 128
ROW_SUB = D_MODEL // LANES


def _store_row_tiles(ref, lead, val):
    n = val.shape[0]
    for j in range(ROW_SUB):
        ref[lead + (pl.ds(j, n, stride=ROW_SUB), slice(None))] = val[:, j * LANES:(j + 1) * LANES]


def _load_row_tiles(ref, lead, n):
    return jnp.concatenate(
        [ref[lead + (pl.ds(j, n, stride=ROW_SUB), slice(None))] for j in range(ROW_SUB)], axis=1)


def _row_tile(ref, lead, r):
    return ref.at[lead + (pl.ds(pl.multiple_of(r * ROW_SUB, ROW_SUB), ROW_SUB), slice(None))]


def _segment(row0, n_ctx_rows, seq):
    return jnp.where(row0 < n_ctx_rows, 0, 1 + (row0 - n_ctx_rows) // seq)


def _mod_kernel(c_ref, w_ref, b_ref, o_ref):
    c = c_ref[...]
    s = c * jax.nn.sigmoid(c)
    o_ref[0] = jnp.dot(s, w_ref[0], precision=HIGHEST, preferred_element_type=F32) + b_ref[0]


def _modulation(cvec, w_mod, b_mod):
    depth = w_mod.shape[0]
    n6 = w_mod.shape[2]
    tn = 1024
    return pl.pallas_call(
        _mod_kernel,
        grid=(depth, n6 // tn),
        in_specs=[
            pl.BlockSpec((8, D_MODEL), lambda l, j: (0, 0)),
            pl.BlockSpec((1, D_MODEL, tn), lambda l, j: (l, 0, j)),
            pl.BlockSpec((1, 1, tn), lambda l, j: (l, 0, j)),
        ],
        out_specs=pl.BlockSpec((1, 8, tn), lambda l, j: (l, 0, j)),
        out_shape=jax.ShapeDtypeStruct((depth, 8, n6), F32),
        compiler_params=pltpu.CompilerParams(
            dimension_semantics=("arbitrary", "arbitrary"), vmem_limit_bytes=VMEM_LIMIT),
    )(cvec, w_mod, b_mod.reshape(depth, 1, n6))


def _inproj_kernel(x_ref, mod_ref, w_ref, o_ref):
    m = mod_ref[0]
    h = x_ref[...] * (1.0 + m[1:2]) + m[0:1]
    o_ref[...] = jnp.dot(h.astype(BF16), w_ref[...], preferred_element_type=F32).astype(o_ref.dtype)


def _in_projection(x, mod_l, w_in_b, n_ctx_rows, seq):
    t = x.shape[0]
    tm = ROW_TILE
    return pl.pallas_call(
        _inproj_kernel,
        grid=(t // tm,),
        in_specs=[
            pl.BlockSpec((tm, D_MODEL), lambda i: (i, 0)),
            pl.BlockSpec((1, 6, D_MODEL), lambda i: (_segment(i * tm, n_ctx_rows, seq), 0, 0)),
            pl.BlockSpec((D_MODEL, N_IN_PAD), lambda i: (0, 0)),
        ],
        out_specs=pl.BlockSpec((tm, N_IN_PAD), lambda i: (i, 0)),
        out_shape=jax.ShapeDtypeStruct((t, N_IN_PAD), BF16),
        compiler_params=pltpu.CompilerParams(
            dimension_semantics=("arbitrary",), vmem_limit_bytes=VMEM_LIMIT),
    )(x, mod_l, w_in_b)


def _sgu_kernel(uv_ref, g_ref, b_ref, w_ref, bias_ref, o_ref):
    tm = uv_ref.shape[0]
    u = jax.nn.gelu(uv_ref[:, :A_WIDTH].astype(F32))
    v = _layer_norm(jax.nn.gelu(uv_ref[:, A_WIDTH:].astype(F32)), g_ref[...], b_ref[...]).astype(BF16)
    head = lax.broadcasted_iota(jnp.int32, (1, A_WIDTH), 1) // A_HEAD_DIM
    for c in range(tm // CHUNK):
        rows = slice(c * CHUNK, (c + 1) * CHUNK)
        vc = v[rows]
        acc = bias_ref[...]
        for h in range(A_HEADS):
            r = jnp.dot(w_ref[h], vc, preferred_element_type=F32)
            acc = acc + jnp.where(head == h, r, 0.0)
        o_ref[rows, :] = (u[rows] * acc).astype(BF16)


def _spatial_gate(proj, ln_g, ln_b, w_s, b_s):
    t = proj.shape[0]
    tm = ROW_TILE
    bias = jnp.repeat(b_s.T, A_HEAD_DIM, axis=1)
    return pl.pallas_call(
        _sgu_kernel,
        grid=(t // tm,),
        in_specs=[
            pl.BlockSpec((tm, 2 * A_WIDTH), lambda i: (i, 0)),
            pl.BlockSpec((1, A_WIDTH), lambda i: (0, 0)),
            pl.BlockSpec((1, A_WIDTH), lambda i: (0, 0)),
            pl.BlockSpec((A_HEADS, CHUNK, CHUNK), lambda i: (0, 0, 0)),
            pl.BlockSpec((CHUNK, A_WIDTH), lambda i: (0, 0)),
        ],
        out_specs=pl.BlockSpec((tm, A_WIDTH), lambda i: (i, 0)),
        out_shape=jax.ShapeDtypeStruct((t, A_WIDTH), BF16),
        compiler_params=pltpu.CompilerParams(
            dimension_semantics=("arbitrary",), vmem_limit_bytes=VMEM_LIMIT),
    )(proj, ln_g.reshape(1, -1), ln_b.reshape(1, -1), w_s.astype(BF16), bias)


_NT = (((1,), (1,)), ((), ()))
_TN = (((0,), (0,)), ((), ()))


def _gla_direction(backward, q_ref, k_ref, v_ref, gl_ref, gup_ref, gb_ref, o_ref, s_ref):
    c_len = GLA_CHUNK
    n_rows = q_ref.shape[0]
    n_chunks = n_rows // c_len

    row = lax.broadcasted_iota(jnp.int32, (n_rows, n_rows), 0)
    col = lax.broadcasted_iota(jnp.int32, (n_rows, n_rows), 1)
    same = (row // c_len) == (col // c_len)
    if backward:
        tri = jnp.where(same & (col >= row), 1.0, 0.0).astype(BF16)
        keep = same & (col > row)
        i_last, i_mid = 0, c_len - 1 - c_len // 2
        lo = GATE_RANK
    else:
        tri = jnp.where(same & (col <= row), 1.0, 0.0).astype(BF16)
        keep = same & (col <= row)
        i_last, i_mid = c_len - 1, c_len // 2
        lo = 0

    z = jnp.dot(gl_ref[:, lo:lo + GATE_RANK].astype(F32), gup_ref[...], precision=HIGHEST,
                preferred_element_type=F32) + gb_ref[...]
    la = jax.nn.log_sigmoid(z) / GATE_TAU
    l1 = la.astype(BF16)
    r1 = la - l1.astype(F32)
    l2 = r1.astype(BF16)
    l3 = (r1 - l2.astype(F32)).astype(BF16)
    b = (jnp.dot(tri, l1, preferred_element_type=F32) + jnp.dot(tri, l2, preferred_element_type=F32)
         + jnp.dot(tri, l3, preferred_element_type=F32))

    def per_chunk(index):
        return jnp.concatenate(
            [jnp.broadcast_to(b[c * c_len + index:c * c_len + index + 1], (c_len, B_QK))
             for c in range(n_chunks)], axis=0)

    b_last = per_chunk(i_last)
    b_mid = per_chunk(i_mid)
    q = q_ref[...].astype(F32) * (B_DK ** -0.5)
    k = k_ref[...].astype(F32)
    q_mid = (q * jnp.exp(b - b_mid)).astype(BF16)
    k_mid = (k * jnp.exp(b_mid - b)).astype(BF16)
    q_in = (q * jnp.exp(b)).astype(BF16)
    k_out = (k * jnp.exp(b_last - b)).astype(BF16)
    order = range(n_chunks - 1, -1, -1) if backward else range(n_chunks)
    v_all = v_ref[...].astype(BF16)
    intra = []
    for h in range(B_HEADS):
        hk = slice(h * B_DK, (h + 1) * B_DK)
        hv = slice(h * B_DV, (h + 1) * B_DV)
        sc = lax.dot_general(q_mid[:, hk], k_mid[:, hk], _NT, preferred_element_type=F32)
        sc = jnp.where(keep, sc, 0.0).astype(BF16)
        intra.append(jnp.dot(sc, v_all[:, hv], preferred_element_type=F32))
    o_intra = jnp.concatenate(intra, axis=1)
    own = (lax.broadcasted_iota(jnp.int32, (B_WIDTH, B_QK), 0) // B_DV
           == lax.broadcasted_iota(jnp.int32, (B_WIDTH, B_QK), 1) // B_DK)
    state = s_ref[...]
    for c in order:
        rows = slice(c * c_len, (c + 1) * c_len)
        o_ref[rows, :] = o_intra[rows] + lax.dot_general(q_in[rows], state.astype(BF16), _NT,
                                                         preferred_element_type=F32)
        decay = jnp.exp(b[c * c_len + i_last:c * c_len + i_last + 1])
        update = lax.dot_general(v_all[rows], k_out[rows], _TN, preferred_element_type=F32)
        state = jnp.where(own, state * decay + update, 0.0)
    s_ref[...] = state


def _gla_kernel(qf, kf, vf, glf, qb, kb, vb, glb, gup_ref, gb_ref, of_ref, ob_ref, s_ref):
    @pl.when(pl.program_id(1) == 0)
    def _():
        s_ref[...] = jnp.zeros_like(s_ref)

    _gla_direction(False, qf, kf, vf, glf, gup_ref.at[0], gb_ref.at[0], of_ref, s_ref.at[0])
    _gla_direction(True, qb, kb, vb, glb, gup_ref.at[1], gb_ref.at[1], ob_ref, s_ref.at[1])


def _gla_block(backward, bsz, nctx_blk, nlat_blk, b, j):
    if backward:
        ctx_i = b * nctx_blk + (nctx_blk - 1 - j)
        lat_i = bsz * nctx_blk + b * nlat_blk + (nlat_blk - 1 - (j - nctx_blk))
    else:
        ctx_i = b * nctx_blk + j
        lat_i = bsz * nctx_blk + b * nlat_blk + (j - nctx_blk)
    return jnp.where(j < nctx_blk, ctx_i, lat_i)


def _gla_sweep(proj, gate_up, gate_b, bsz, ctx_len, seq):
    t = proj.shape[0]
    r = SEQ_TILE
    nctx_blk, nlat_blk = ctx_len // r, seq // r
    in_specs = []
    for backward in (False, True):
        blk = functools.partial(_gla_block, backward, bsz, nctx_blk, nlat_blk)
        in_specs += [
            pl.BlockSpec((r, B_QK), lambda b, j, blk=blk: (blk(b, j), 2)),
            pl.BlockSpec((r, B_QK), lambda b, j, blk=blk: (blk(b, j), 3)),
            pl.BlockSpec((r, B_WIDTH), lambda b, j, blk=blk: (blk(b, j), 2)),
            pl.BlockSpec((r, 128), lambda b, j, blk=blk: (blk(b, j), COL_GL // 128)),
        ]
    in_specs += [
        pl.BlockSpec((2, GATE_RANK, B_QK), lambda b, j: (0, 0, 0)),
        pl.BlockSpec((2, 1, B_QK), lambda b, j: (0, 0, 0)),
    ]
    fwd = functools.partial(_gla_block, False, bsz, nctx_blk, nlat_blk)
    bwd = functools.partial(_gla_block, True, bsz, nctx_blk, nlat_blk)
    return pl.pallas_call(
        _gla_kernel,
        grid=(bsz, nctx_blk + nlat_blk),
        in_specs=in_specs,
        out_specs=[
            pl.BlockSpec((r, B_WIDTH), lambda b, j: (fwd(b, j), 0)),
            pl.BlockSpec((r, B_WIDTH), lambda b, j: (bwd(b, j), 0)),
        ],
        out_shape=[jax.ShapeDtypeStruct((t, B_WIDTH), F32), jax.ShapeDtypeStruct((t, B_WIDTH), F32)],
        scratch_shapes=[pltpu.VMEM((2, B_WIDTH, B_QK), F32)],
        compiler_params=pltpu.CompilerParams(
            dimension_semantics=("arbitrary", "arbitrary"), vmem_limit_bytes=VMEM_LIMIT),
    )(*([proj] * 8), gate_up, gate_b.reshape(2, 1, -1))


def _s5_matrices(lam_re, lam_im, log_dt, b_re, b_im, c_re, c_im):
    tc = S5_T
    lam = lax.complex(lam_re.astype(F32), lam_im.astype(F32))
    dt = jnp.exp(log_dt.astype(F32))
    bm = lax.complex(b_re.astype(F32), b_im.astype(F32))
    cm = lax.complex(c_re.astype(F32), c_im.astype(F32))
    ldt = lam * dt[..., None]
    lam_bar = jnp.exp(ldt)
    b_bar = ((lam_bar - 1.0) / lam)[..., None] * bm
    steps = jnp.arange(tc + 1, dtype=F32)
    pw = jnp.exp(ldt[:, None] * steps[None, :, None, None])
    kern = jnp.real(jnp.einsum('dgip,dtgp,dgpj->dgtij', cm, pw[:, :tc], b_bar))
    s_i = jnp.arange(tc)[:, None]
    t_i = jnp.arange(tc)[None, :]
    g_n = lam.shape[1]
    hh = S5_GROUP_CH

    def toeplitz(kd, lag, ok):
        m = kd[:, jnp.clip(lag, 0, tc - 1)]
        m = jnp.where(ok[None, :, :, None, None], m, 0.0)
        return m.transpose(0, 1, 4, 2, 3).reshape(g_n, tc * hh, tc * hh)

    mt = toeplitz(kern[0], t_i - s_i, t_i >= s_i) + toeplitz(kern[1], s_i - t_i, s_i >= t_i)

    def state_in(d, powers):
        w = pw[d][powers][:, :, :, None] * b_bar[d][None]
        w = w.transpose(1, 0, 3, 2).reshape(g_n, tc * hh, S5_STATE)
        re, im = jnp.real(w), jnp.imag(w)
        return jnp.concatenate([re, im, im, re], axis=-1)

    qt = jnp.concatenate([state_in(0, tc - 1 - jnp.arange(tc)), state_in(1, jnp.arange(tc))], axis=-1)

    def state_out(d, powers):
        w = cm[d][:, None] * pw[d][powers].transpose(1, 0, 2)[:, :, None, :]
        w = w.reshape(g_n, tc * hh, S5_STATE).transpose(0, 2, 1)
        return jnp.concatenate([jnp.real(w), -jnp.imag(w)], axis=1)

    pt = jnp.concatenate([state_out(0, 1 + jnp.arange(tc)), state_out(1, tc - jnp.arange(tc))], axis=1)

    a = pw[:, tc]
    ar, ai = jnp.real(a), jnp.imag(a)
    a1 = jnp.concatenate([ar, ar], axis=-1).reshape(2, -1)
    a2 = jnp.concatenate([-ai, ai], axis=-1).reshape(2, -1)
    a3 = jnp.concatenate([ai, -ai], axis=-1).reshape(2, -1)
    ac = jnp.stack([a1, a2, a3], axis=1)
    return mt.astype(BF16), qt.astype(BF16), pt.astype(BF16), ac.reshape(2, 3, 1, -1)


def _s5_kernel(uc_ref, ul_ref, mt_ref, qt_ref, pt_ref, ac_ref, yc_ref, yl_ref, ef, esf, eb, esb):
    nq = uc_ref.shape[0]
    nc_ctx, nc_lat = uc_ref.shape[1], ul_ref.shape[1]
    n = nc_ctx + nc_lat
    w = 2 * S5_STATE
    for gi in range(nq):
        lanes = slice(gi * w, (gi + 1) * w)
        rc = jnp.dot(uc_ref[gi], qt_ref[gi], preferred_element_type=F32)
        rl = jnp.dot(ul_ref[gi], qt_ref[gi], preferred_element_type=F32)
        ef[0:nc_ctx, lanes] = rc[:, 0:w]
        esf[0:nc_ctx, lanes] = rc[:, w:2 * w]
        ef[nc_ctx:n, lanes] = rl[:, 0:w]
        esf[nc_ctx:n, lanes] = rl[:, w:2 * w]
        eb[0:nc_lat, lanes] = rl[:, 2 * w:3 * w]
        esb[0:nc_lat, lanes] = rl[:, 3 * w:4 * w]
        eb[nc_lat:n, lanes] = rc[:, 2 * w:3 * w]
        esb[nc_lat:n, lanes] = rc[:, 3 * w:4 * w]

    a1f, a2f, a3f = ac_ref[0, 0], ac_ref[0, 1], ac_ref[0, 2]
    a1b, a2b, a3b = ac_ref[1, 0], ac_ref[1, 1], ac_ref[1, 2]

    def body(i, carry):
        hf, hsf, hb, hsb = carry
        rf = pl.ds(i, 1)
        rb = pl.ds(n - 1 - i, 1)
        e_f, es_f = ef[rf, :], esf[rf, :]
        e_b, es_b = eb[rb, :], esb[rb, :]
        ef[rf, :] = hf
        eb[rb, :] = hb
        return (a1f * hf + a2f * hsf + e_f, a1f * hsf + a3f * hf + es_f,
                a1b * hb + a2b * hsb + e_b, a1b * hsb + a3b * hb + es_b)

    zero = jnp.zeros((1, nq * w), F32)
    lax.fori_loop(0, n, body, (zero, zero, zero, zero))

    for gi in range(nq):
        lanes = slice(gi * w, (gi + 1) * w)
        hc = jnp.concatenate([ef[0:nc_ctx, lanes], eb[nc_lat:n, lanes]], axis=1).astype(BF16)
        hl = jnp.concatenate([ef[nc_ctx:n, lanes], eb[0:nc_lat, lanes]], axis=1).astype(BF16)
        yc_ref[gi] = (jnp.dot(uc_ref[gi], mt_ref[gi], preferred_element_type=F32)
                      + jnp.dot(hc, pt_ref[gi], preferred_element_type=F32))
        yl_ref[gi] = (jnp.dot(ul_ref[gi], mt_ref[gi], preferred_element_type=F32)
                      + jnp.dot(hl, pt_ref[gi], preferred_element_type=F32))


def _s5_scan(proj, mats, bsz, ctx_len, seq):
    mt, qt, pt, ac = mats
    g_n, tc, hh = S5_GROUPS, S5_T, S5_GROUP_CH
    n_ctx_rows = bsz * ctx_len
    u = proj[:, COL_S5:COL_S5 + C_WIDTH]

    def to_rows(part):
        r = part.shape[0]
        return part.reshape(r // tc, tc, g_n, hh).transpose(2, 0, 1, 3).reshape(g_n, r // tc, tc * hh)

    uc, ul = to_rows(u[:n_ctx_rows]), to_rows(u[n_ctx_rows:])
    nc_ctx, nc_lat = ctx_len // tc, seq // tc
    nq = S5_QUARTER
    wq = nq * 2 * S5_STATE
    n = nc_ctx + nc_lat
    yc, yl = pl.pallas_call(
        _s5_kernel,
        grid=(bsz, g_n // nq),
        in_specs=[
            pl.BlockSpec((nq, nc_ctx, tc * hh), lambda b, qi: (qi, b, 0)),
            pl.BlockSpec((nq, nc_lat, tc * hh), lambda b, qi: (qi, b, 0)),
            pl.BlockSpec((nq, tc * hh, tc * hh), lambda b, qi: (qi, 0, 0)),
            pl.BlockSpec((nq, tc * hh, 8 * S5_STATE), lambda b, qi: (qi, 0, 0)),
            pl.BlockSpec((nq, 4 * S5_STATE, tc * hh), lambda b, qi: (qi, 0, 0)),
            pl.BlockSpec((2, 3, 1, wq), lambda b, qi: (0, 0, 0, qi)),
        ],
        out_specs=[
            pl.BlockSpec((nq, nc_ctx, tc * hh), lambda b, qi: (qi, b, 0)),
            pl.BlockSpec((nq, nc_lat, tc * hh), lambda b, qi: (qi, b, 0)),
        ],
        out_shape=[
            jax.ShapeDtypeStruct((g_n, bsz * nc_ctx, tc * hh), F32),
            jax.ShapeDtypeStruct((g_n, bsz * nc_lat, tc * hh), F32),
        ],
        scratch_shapes=[pltpu.VMEM((n, wq), F32) for _ in range(4)],
        compiler_params=pltpu.CompilerParams(
            dimension_semantics=("arbitrary", "arbitrary"), vmem_limit_bytes=VMEM_LIMIT),
    )(uc, ul, mt, qt, pt, ac)

    def from_rows(y):
        r = y.shape[1]
        return y.reshape(g_n, r, tc, hh).transpose(1, 2, 0, 3).reshape(r * tc, g_n * hh)

    return jnp.concatenate([from_rows(yc), from_rows(yl)], axis=0)


def _outproj_kernel(alpha, a_ref, of_ref, ob_ref, g_ref, ng_ref, y_ref, u_ref, x_ref, mod_ref, d_ref, gw_ref,
                    gb_ref, wo_ref, lng_ref, lnb_ref, rwh_ref, rwl_ref, rb_ref, x1_ref, h2_ref, lg_ref):
    m = mod_ref[0]
    heads = []
    for h in range(B_HEADS):
        hv = slice(h * B_DV, (h + 1) * B_DV)
        o = of_ref[:, hv] + ob_ref[:, hv]
        heads.append(o * lax.rsqrt(jnp.mean(o * o, axis=-1, keepdims=True) + LN_EPS))
    gate = g_ref[...].astype(F32)
    gla = (jnp.concatenate(heads, axis=1) * ng_ref[...] * (gate * jax.nn.sigmoid(gate))).astype(BF16)
    y = jax.nn.gelu(y_ref[...] + d_ref[...] * u_ref[...].astype(F32))
    s5 = y * jax.nn.sigmoid(jnp.dot(y.astype(BF16), gw_ref[...], preferred_element_type=F32) + gb_ref[...])
    mix = (jnp.dot(a_ref[...], wo_ref[0:A_WIDTH, :], preferred_element_type=F32)
           + jnp.dot(gla, wo_ref[A_WIDTH:A_WIDTH + B_WIDTH, :], preferred_element_type=F32)
           + jnp.dot(s5.astype(BF16), wo_ref[A_WIDTH + B_WIDTH:, :], preferred_element_type=F32))
    x1 = _layer_norm(alpha * x_ref[...] + m[2:3] * mix, lng_ref[...], lnb_ref[...])
    x1_ref[...] = x1
    h2 = x1 * (1.0 + m[4:5]) + m[3:4]
    _store_row_tiles(h2_ref, (), h2)
    h_hi = h2.astype(BF16)
    h_lo = (h2 - h_hi.astype(F32)).astype(BF16)
    lg_ref[...] = (jnp.dot(h_hi, rwh_ref[...], preferred_element_type=F32)
                   + jnp.dot(h_lo, rwh_ref[...], preferred_element_type=F32)
                   + jnp.dot(h_hi, rwl_ref[...], preferred_element_type=F32) + rb_ref[...])


def _out_projection(alpha, a_out, o_fwd, o_bwd, norm_g, y_s5, proj, x, mod_l, s5_d, glu_w, glu_b, w_out_b, ln_g,
                    ln_b, router_w, router_b, n_ctx_rows, seq):
    t = x.shape[0]
    tm = ROW_TILE
    rw = jnp.zeros((D_MODEL, 128), F32).at[:, :N_EXPERTS].set(router_w)
    rw_hi = rw.astype(BF16)
    rw_lo = (rw - rw_hi.astype(F32)).astype(BF16)
    rb = jnp.zeros((1, 128), F32).at[0, :N_EXPERTS].set(router_b)
    row = lambda i: (i, 0)
    fixed = lambda i: (0, 0)
    return pl.pallas_call(
        functools.partial(_outproj_kernel, alpha),
        grid=(t // tm,),
        in_specs=[
            pl.BlockSpec((tm, A_WIDTH), row),
            pl.BlockSpec((tm, B_WIDTH), row),
            pl.BlockSpec((tm, B_WIDTH), row),
            pl.BlockSpec((tm, B_WIDTH), lambda i: (i, 3)),
            pl.BlockSpec((1, B_WIDTH), fixed),
            pl.BlockSpec((tm, C_WIDTH), row),
            pl.BlockSpec((tm, C_WIDTH), lambda i: (i, COL_S5 // C_WIDTH)),
            pl.BlockSpec((tm, D_MODEL), row),
            pl.BlockSpec((1, 6, D_MODEL), lambda i: (_segment(i * tm, n_ctx_rows, seq), 0, 0)),
            pl.BlockSpec((1, C_WIDTH), fixed),
            pl.BlockSpec((C_WIDTH, C_WIDTH), fixed),
            pl.BlockSpec((1, C_WIDTH), fixed),
            pl.BlockSpec((D_MODEL, D_MODEL), fixed),
            pl.BlockSpec((1, D_MODEL), fixed),
            pl.BlockSpec((1, D_MODEL), fixed),
            pl.BlockSpec((D_MODEL, 128), fixed),
            pl.BlockSpec((D_MODEL, 128), fixed),
            pl.BlockSpec((1, 128), fixed),
        ],
        out_specs=[
            pl.BlockSpec((tm, D_MODEL), row),
            pl.BlockSpec((tm * ROW_SUB, LANES), row),
            pl.BlockSpec((tm, 128), row),
        ],
        out_shape=[
            jax.ShapeDtypeStruct((t, D_MODEL), F32),
            jax.ShapeDtypeStruct((t * ROW_SUB, LANES), F32),
            jax.ShapeDtypeStruct((t, 128), F32),
        ],
        compiler_params=pltpu.CompilerParams(
            dimension_semantics=("arbitrary",), vmem_limit_bytes=VMEM_LIMIT),
    )(a_out, o_fwd, o_bwd, proj, norm_g.reshape(1, -1), y_s5, proj, x, mod_l, s5_d.reshape(1, -1), glu_w.astype(BF16), glu_b.reshape(1, -1),
      w_out_b, ln_g.reshape(1, -1), ln_b.reshape(1, -1), rw_hi, rw_lo, rb)


def _route_kernel(lg_ref, gate_ref, pos_ref, tbase_ref, tcnt_ref, cnt_ref, base, before, below):
    tm = lg_ref.shape[0]

    @pl.when(pl.program_id(0) == 0)
    def _():
        base[...] = jnp.zeros_like(base)
        r = lax.broadcasted_iota(jnp.int32, (tm, tm), 0)
        c = lax.broadcasted_iota(jnp.int32, (tm, tm), 1)
        before[...] = jnp.where(r < c, 1.0, 0.0).astype(BF16)
        r = lax.broadcasted_iota(jnp.int32, (N_EXPERTS, N_EXPERTS), 0)
        c = lax.broadcasted_iota(jnp.int32, (N_EXPERTS, N_EXPERTS), 1)
        below[...] = jnp.where(c < r, 1.0, 0.0)

    logit = jnp.transpose(lg_ref[...])[:N_EXPERTS]
    eid = lax.broadcasted_iota(jnp.int32, (N_EXPERTS, tm), 0)
    vals, hots = [], []
    work = logit
    for kk in range(TOP_K):
        m = jnp.max(work, axis=0, keepdims=True)
        ix = jnp.min(jnp.where(work == m, eid, N_EXPERTS), axis=0, keepdims=True)
        hot = eid == ix
        vals.append(m)
        hots.append(hot)
        work = jnp.where(hot, -jnp.inf, work)
    ex = [jnp.exp(v - vals[0]) for v in vals]
    den = ex[0] + ex[1] + ex[2] + ex[3]
    member = jnp.zeros((N_EXPERTS, tm), F32)
    for kk in range(TOP_K):
        gate_ref[kk:kk + 1, :] = ex[kk] / den
        member = member + jnp.where(hots[kk], 1.0, 0.0)
    tile_cnt = jnp.broadcast_to(jnp.sum(member, axis=1, keepdims=True), (N_EXPERTS, LANES))
    group_off = jnp.dot(below[...], tile_cnt, precision=HIGHEST, preferred_element_type=F32)[:, 0:1]
    in_group = jnp.dot(member.astype(BF16), before[...], preferred_element_type=F32)
    for kk in range(TOP_K):
        pos_ref[kk:kk + 1, :] = jnp.sum(jnp.where(hots[kk], group_off + in_group, 0.0), axis=0,
                                        keepdims=True).astype(jnp.int32)
    tbase_ref[...] = jnp.broadcast_to(base[...], tbase_ref.shape)
    tcnt_ref[...] = tile_cnt
    total = base[...] + tile_cnt[:, 0:1]
    base[...] = total
    cnt_ref[...] = jnp.broadcast_to(total, cnt_ref.shape)


def _routing(logits, n_blocks):
    t = logits.shape[0]
    tm = ROW_TILE
    nt = t // tm
    gates, pos, tbase, tcnt, cnt = pl.pallas_call(
        _route_kernel,
        grid=(nt,),
        in_specs=[pl.BlockSpec((tm, LANES), lambda i: (i, 0))],
        out_specs=[
            pl.BlockSpec((TOP_K, tm), lambda i: (0, i)),
            pl.BlockSpec((TOP_K, tm), lambda i: (0, i)),
            pl.BlockSpec((N_EXPERTS, LANES), lambda i: (i, 0)),
            pl.BlockSpec((N_EXPERTS, LANES), lambda i: (i, 0)),
            pl.BlockSpec((N_EXPERTS, LANES), lambda i: (0, 0)),
        ],
        out_shape=[
            jax.ShapeDtypeStruct((TOP_K, t), F32),
            jax.ShapeDtypeStruct((TOP_K, t), jnp.int32),
            jax.ShapeDtypeStruct((nt * N_EXPERTS, LANES), F32),
            jax.ShapeDtypeStruct((nt * N_EXPERTS, LANES), F32),
            jax.ShapeDtypeStruct((N_EXPERTS, LANES), F32),
        ],
        scratch_shapes=[pltpu.VMEM((N_EXPERTS, 1), F32), pltpu.VMEM((tm, tm), BF16),
                        pltpu.VMEM((N_EXPERTS, N_EXPERTS), F32)],
        compiler_params=pltpu.CompilerParams(
            dimension_semantics=("arbitrary",), vmem_limit_bytes=VMEM_LIMIT),
    )(logits)
    counts = cnt[:, 0].astype(jnp.int32)
    padded = (counts + MOE_BLOCK - 1) // MOE_BLOCK * MOE_BLOCK
    padded_end = jnp.cumsum(padded)
    padded_start = (padded_end - padded).astype(jnp.int32)
    first_slot = jnp.arange(n_blocks, dtype=jnp.int32) * MOE_BLOCK
    block_expert = jnp.minimum(jnp.sum((padded_end[None, :] <= first_slot[:, None]).astype(jnp.int32), axis=1),
                               N_EXPERTS - 1).astype(jnp.int32)
    n_valid = (padded_end[-1] // MOE_BLOCK).astype(jnp.int32).reshape(1)
    pad_lo = (padded_start + counts).astype(jnp.int32)
    route = dict(
        gates=gates.T.reshape(nt, 1, tm * TOP_K),
        pos=pos.T.reshape(nt, 1, tm * TOP_K),
        tile_base=tbase[:, 0].astype(jnp.int32).reshape(nt, 1, N_EXPERTS),
        tile_cnt=tcnt[:, 0].astype(jnp.int32).reshape(nt, 1, N_EXPERTS),
        start=padded_start, pad_lo=pad_lo, pad_hi=padded_end.astype(jnp.int32))
    return route, block_expert, n_valid


def _expert_runs(tile_cnt_ref, tile_base_ref, start_ref, copy):
    off = 0
    for e in range(N_EXPERTS):
        n = tile_cnt_ref[0, 0, e]
        slot0 = start_ref[e] + tile_base_ref[0, 0, e]
        for bit in range(ROW_TILE.bit_length() - 1, -1, -1):
            size = 1 << bit
            done = (n >> (bit + 1)) << (bit + 1)

            @pl.when(((n >> bit) & 1) == 1)
            def _(off=off, done=done, slot0=slot0, size=size):
                copy(off + done, slot0 + done, size)
        off = off + n


def _rows(ref, lead, row0, n):
    return ref.at[lead + (pl.ds(pl.multiple_of(row0 * ROW_SUB, ROW_SUB), n * ROW_SUB), slice(None))]


def _dispatch_kernel(start_ref, lo_ref, hi_ref, pos_ref, tbase_ref, tcnt_ref, h_ref, o_ref, stage, zbuf, sem, zsem):
    i = pl.program_id(0)
    nt = pl.num_programs(0)
    tm = ROW_TILE
    slot = i % 2
    blk_rows = MOE_BLOCK * ROW_SUB
    n_blocks = o_ref.shape[0] // blk_rows

    def wait_stage(s):
        pltpu.make_async_copy(stage.at[s], stage.at[s], sem.at[s]).wait()

    @pl.when(i >= 2)
    def _():
        wait_stage(slot)

    def place(r, carry):
        row = h_ref[pl.ds(pl.multiple_of(r * ROW_SUB, ROW_SUB), ROW_SUB), :]
        for kk in range(TOP_K):
            p = pos_ref[0, 0, r * TOP_K + kk]
            stage[slot, pl.ds(pl.multiple_of(p * ROW_SUB, ROW_SUB), ROW_SUB), :] = row
        return carry
    lax.fori_loop(0, tm, place, 0, unroll=4)

    def copy(stage_row, slot_row, n):
        pltpu.make_async_copy(_rows(stage, (slot,), stage_row, n), _rows(o_ref, (), slot_row, n),
                              sem.at[slot]).start()
    _expert_runs(tcnt_ref, tbase_ref, start_ref, copy)

    @pl.when(i == nt - 1)
    def _():
        wait_stage(slot)

        @pl.when(i >= 1)
        def _():
            wait_stage(1 - slot)
        zbuf[...] = jnp.zeros_like(zbuf)
        zrow = zbuf.at[pl.ds(0, ROW_SUB), :]
        for e in range(N_EXPERTS):
            def fill(s, carry):
                pltpu.make_async_copy(zrow, _row_tile(o_ref, (), s), zsem).start()
                return carry
            lax.fori_loop(lo_ref[e], hi_ref[e], fill, 0)
        for e in range(N_EXPERTS):
            def drain(s, carry):
                pltpu.make_async_copy(zrow, zrow, zsem).wait()
                return carry
            lax.fori_loop(lo_ref[e], hi_ref[e], drain, 0)
        used = hi_ref[N_EXPERTS - 1] // MOE_BLOCK

        def fill_block(j, carry):
            rows = pl.ds(pl.multiple_of(j * blk_rows, blk_rows), blk_rows)
            pltpu.make_async_copy(zbuf, o_ref.at[rows, :], zsem).start()
            return carry
        lax.fori_loop(used, n_blocks, fill_block, 0)

        def drain_block(j, carry):
            pltpu.make_async_copy(zbuf, zbuf, zsem).wait()
            return carry
        lax.fori_loop(used, n_blocks, drain_block, 0)


def _moe_dispatch(h2t, route, n_blocks):
    nt = route['pos'].shape[0]
    tm = ROW_TILE
    smem = lambda width: pl.BlockSpec((1, 1, width), lambda i, *_: (i, 0, 0), memory_space=pltpu.SMEM)
    grid_spec = pltpu.PrefetchScalarGridSpec(
        num_scalar_prefetch=3,
        grid=(nt,),
        in_specs=[
            smem(tm * TOP_K), smem(N_EXPERTS), smem(N_EXPERTS),
            pl.BlockSpec((tm * ROW_SUB, LANES), lambda i, *_: (i, 0)),
        ],
        out_specs=pl.BlockSpec(memory_space=pl.ANY),
        scratch_shapes=[
            pltpu.VMEM((2, tm * TOP_K * ROW_SUB, LANES), F32),
            pltpu.VMEM((MOE_BLOCK * ROW_SUB, LANES), F32),
            pltpu.SemaphoreType.DMA((2,)),
            pltpu.SemaphoreType.DMA(()),
        ],
    )
    return pl.pallas_call(
        _dispatch_kernel,
        grid_spec=grid_spec,
        out_shape=jax.ShapeDtypeStruct((n_blocks * MOE_BLOCK * ROW_SUB, LANES), F32),
        compiler_params=pltpu.CompilerParams(
            dimension_semantics=("arbitrary",), vmem_limit_bytes=VMEM_LIMIT),
    )(route['start'], route['pad_lo'], route['pad_hi'], route['pos'], route['tile_base'], route['tile_cnt'], h2t)


def _moe_kernel(be_ref, nv_ref, x_ref, wu_ref, bu_ref, wd_ref, bd_ref, o_ref, wu_b, wd_b):
    i = pl.program_id(0)
    n_valid = nv_ref[0]

    @pl.when(i < n_valid)
    def _():
        first = jnp.logical_or(i == 0, be_ref[i] != be_ref[jnp.maximum(i - 1, 0)])

        @pl.when(first)
        def _():
            rows = 64

            def cast(r, carry):
                rs = pl.ds(pl.multiple_of(r * rows, rows), rows)
                wu_b[rs, :] = wu_ref[0, 0, rs, :].astype(BF16)
                wd_b[rs, :] = wd_ref[0, 0, rs, :].astype(BF16)
                return carry
            lax.fori_loop(0, D_MODEL // rows, cast, 0)

        x = _load_row_tiles(x_ref, (), MOE_BLOCK).astype(BF16)
        acc = jnp.zeros((MOE_BLOCK, D_MODEL), F32) + bd_ref[0, 0]
        cw = 512
        for jc in range(D_EXPERT // cw):
            cg = slice(jc * cw, (jc + 1) * cw)
            cl = slice(D_EXPERT + jc * cw, D_EXPERT + (jc + 1) * cw)
            ug = jnp.dot(x, wu_b[:, cg], preferred_element_type=F32) + bu_ref[0, 0, :, cg]
            ul = jnp.dot(x, wu_b[:, cl], preferred_element_type=F32) + bu_ref[0, 0, :, cl]
            xg = jnp.minimum(ug, SWIGLU_LIMIT)
            xl = jnp.clip(ul, -SWIGLU_LIMIT, SWIGLU_LIMIT)
            act = xg * jax.nn.sigmoid(SWIGLU_ALPHA * xg) * (xl + 1.0)
            acc = acc + jnp.dot(act.astype(BF16), wd_b[cg, :], preferred_element_type=F32)
        _store_row_tiles(o_ref, (), acc)

    @pl.when(i >= n_valid)
    def _():
        o_ref[...] = jnp.zeros_like(o_ref)


def _moe_experts(layer, xs, block_expert, n_valid, w_up, b_up, w_down, b_down):
    n_blocks = block_expert.shape[0]
    depth = w_up.shape[0]
    grid_spec = pltpu.PrefetchScalarGridSpec(
        num_scalar_prefetch=2,
        grid=(n_blocks,),
        in_specs=[
            pl.BlockSpec((MOE_BLOCK * ROW_SUB, LANES),
                         lambda i, be, nv: (jnp.minimum(i, jnp.maximum(nv[0] - 1, 0)), 0)),
            pl.BlockSpec((1, 1, D_MODEL, 2 * D_EXPERT), lambda i, be, nv: (layer, be[i], 0, 0)),
            pl.BlockSpec((1, 1, 1, 2 * D_EXPERT), lambda i, be, nv: (layer, be[i], 0, 0)),
            pl.BlockSpec((1, 1, D_EXPERT, D_MODEL), lambda i, be, nv: (layer, be[i], 0, 0)),
            pl.BlockSpec((1, 1, 1, D_MODEL), lambda i, be, nv: (layer, be[i], 0, 0)),
        ],
        out_specs=pl.BlockSpec((MOE_BLOCK * ROW_SUB, LANES), lambda i, be, nv: (i, 0)),
        scratch_shapes=[
            pltpu.VMEM((D_MODEL, 2 * D_EXPERT), BF16),
            pltpu.VMEM((D_EXPERT, D_MODEL), BF16),
        ],
    )
    return pl.pallas_call(
        _moe_kernel,
        grid_spec=grid_spec,
        out_shape=jax.ShapeDtypeStruct((n_blocks * MOE_BLOCK * ROW_SUB, LANES), F32),
        compiler_params=pltpu.CompilerParams(
            dimension_semantics=("arbitrary",), vmem_limit_bytes=VMEM_LIMIT),
    )(block_expert, n_valid, xs, w_up, b_up.reshape(depth, N_EXPERTS, 1, -1), w_down,
      b_down.reshape(depth, N_EXPERTS, 1, -1))


def _combine_kernel(alpha, start_ref, pos_ref, gate_ref, tbase_ref, tcnt_ref, tbase_n_ref, tcnt_n_ref, y_ref, x_ref,
                    mod_ref, lng_ref, lnb_ref, o_ref, stage, frow, sem):
    i = pl.program_id(0)
    nt = pl.num_programs(0)
    tm = ROW_TILE
    slot = i % 2

    def fetch(cnt_ref, base_ref, s):
        def copy(stage_row, slot_row, n):
            pltpu.make_async_copy(_rows(y_ref, (), slot_row, n), _rows(stage, (s,), stage_row, n),
                                  sem.at[s]).start()
        _expert_runs(cnt_ref, base_ref, start_ref, copy)

    @pl.when(i == 0)
    def _():
        fetch(tcnt_ref, tbase_ref, 0)

    @pl.when(i + 1 < nt)
    def _():
        fetch(tcnt_n_ref, tbase_n_ref, 1 - slot)

    pltpu.make_async_copy(stage.at[slot], stage.at[slot], sem.at[slot]).wait()

    def mix(r, carry):
        acc = None
        for kk in range(TOP_K):
            p = pos_ref[0, 0, r * TOP_K + kk]
            term = gate_ref[0, 0, r * TOP_K + kk] * stage[slot, pl.ds(pl.multiple_of(p * ROW_SUB, ROW_SUB), ROW_SUB), :]
            acc = term if acc is None else acc + term
        frow[pl.ds(pl.multiple_of(r * ROW_SUB, ROW_SUB), ROW_SUB), :] = acc
        return carry
    lax.fori_loop(0, tm, mix, 0, unroll=4)

    m = mod_ref[0]
    f = _load_row_tiles(frow, (), tm)
    o_ref[...] = _layer_norm(alpha * x_ref[...] + m[5:6] * f, lng_ref[...], lnb_ref[...])


def _moe_combine(alpha, route, ys, x1, mod_l, ln_g, ln_b, n_ctx_rows, seq, drop_ctx):
    t = x1.shape[0]
    tm = ROW_TILE
    nt = t // tm
    skip = n_ctx_rows // tm if drop_ctx else 0
    cur = lambda width: pl.BlockSpec((1, 1, width), lambda i, *_: (i, 0, 0), memory_space=pltpu.SMEM)
    nxt = lambda width: pl.BlockSpec((1, 1, width), lambda i, *_: (jnp.minimum(i + 1, nt - 1), 0, 0),
                                     memory_space=pltpu.SMEM)
    grid_spec = pltpu.PrefetchScalarGridSpec(
        num_scalar_prefetch=1,
        grid=(nt,),
        in_specs=[
            cur(tm * TOP_K), cur(tm * TOP_K), cur(N_EXPERTS), cur(N_EXPERTS), nxt(N_EXPERTS), nxt(N_EXPERTS),
            pl.BlockSpec(memory_space=pl.ANY),
            pl.BlockSpec((tm, D_MODEL), lambda i, *_: (i, 0)),
            pl.BlockSpec((1, 6, D_MODEL), lambda i, *_: (_segment(i * tm, n_ctx_rows, seq), 0, 0)),
            pl.BlockSpec((1, D_MODEL), lambda i, *_: (0, 0)),
            pl.BlockSpec((1, D_MODEL), lambda i, *_: (0, 0)),
        ],
        out_specs=pl.BlockSpec((tm, D_MODEL), lambda i, *_: (jnp.maximum(i - skip, 0), 0)),
        scratch_shapes=[
            pltpu.VMEM((2, tm * TOP_K * ROW_SUB, LANES), F32),
            pltpu.VMEM((tm * ROW_SUB, LANES), F32),
            pltpu.SemaphoreType.DMA((2,)),
        ],
    )
    return pl.pallas_call(
        functools.partial(_combine_kernel, alpha),
        grid_spec=grid_spec,
        out_shape=jax.ShapeDtypeStruct((t - skip * tm, D_MODEL), F32),
        compiler_params=pltpu.CompilerParams(
            dimension_semantics=("arbitrary",), vmem_limit_bytes=VMEM_LIMIT),
    )(route['start'], route['pos'], route['gates'], route['tile_base'], route['tile_cnt'], route['tile_base'],
      route['tile_cnt'], ys, x1, mod_l, ln_g.reshape(1, -1), ln_b.reshape(1, -1))


def kernel(x, c, ctx, c_ctx, w_mod, b_mod, w_in, sgu_ln_g, sgu_ln_b, sgu_w, sgu_b, gla_gate_up, gla_gate_b,
           gla_norm_g, s5_lam_re, s5_lam_im, s5_log_dt, s5_b_re, s5_b_im, s5_c_re, s5_c_im, s5_d, s5_glu_w,
           s5_glu_b, w_out, ln_g, ln_b, router_w, router_b, w_up, b_up, w_down, b_down):
    bsz, seq, d = x.shape
    ctx_len = ctx.shape[1]
    depth = w_in.shape[0]
    alpha = float((2 * depth) ** 0.25)
    n_ctx_rows = bsz * ctx_len
    t = n_ctx_rows + bsz * seq
    assert d == D_MODEL and bsz + 1 <= 8
    assert n_ctx_rows % ROW_TILE == 0 and seq % ROW_TILE == 0
    assert ctx_len % SEQ_TILE == 0 and seq % SEQ_TILE == 0

    xa = jnp.concatenate([ctx.reshape(n_ctx_rows, d), x.reshape(bsz * seq, d)], axis=0)
    cvec = jnp.zeros((8, d), F32).at[0].set(c_ctx).at[1:1 + bsz].set(c)
    mod = _modulation(cvec, w_mod, b_mod).reshape(depth, 8, 6, d)

    o = np.cumsum((0, A_WIDTH, A_WIDTH, B_QK, B_QK, B_WIDTH, B_WIDTH, 2 * GATE_RANK, C_WIDTH))
    w_in_r = jnp.concatenate(
        [w_in[:, :, o[0]:o[6]], w_in[:, :, o[7]:o[8]], w_in[:, :, o[6]:o[7]],
         jnp.zeros((depth, d, N_IN_PAD - int(o[8])), w_in.dtype)], axis=-1).astype(BF16)
    w_out_b = w_out.astype(BF16)

    n_assign = t * TOP_K
    n_blocks = -(-(n_assign + N_EXPERTS * (MOE_BLOCK - 1)) // MOE_BLOCK)

    for l in range(depth):
        mod_l = mod[l]
        proj = _in_projection(xa, mod_l, w_in_r[l], n_ctx_rows, seq)
        a_out = _spatial_gate(proj, sgu_ln_g[l], sgu_ln_b[l], sgu_w[l], sgu_b[l])
        o_fwd, o_bwd = _gla_sweep(proj, gla_gate_up[l], gla_gate_b[l], bsz, ctx_len, seq)
        mats = _s5_matrices(s5_lam_re[l], s5_lam_im[l], s5_log_dt[l], s5_b_re[l], s5_b_im[l],
                            s5_c_re[l], s5_c_im[l])
        y_s5 = _s5_scan(proj, mats, bsz, ctx_len, seq)
        x1, h2, logits = _out_projection(alpha, a_out, o_fwd, o_bwd, gla_norm_g[l], y_s5, proj, xa, mod_l, s5_d[l], s5_glu_w[l],
                                         s5_glu_b[l], w_out_b[l], ln_g[l, 0], ln_b[l, 0], router_w[l],
                                         router_b[l], n_ctx_rows, seq)
        route, block_expert, n_valid = _routing(logits, n_blocks)
        xs = _moe_dispatch(h2, route, n_blocks)
        ys = _moe_experts(l, xs, block_expert, n_valid, w_up, b_up, w_down, b_down)
        xa = _moe_combine(alpha, route, ys, x1, mod_l, ln_g[l, 1], ln_b[l, 1], n_ctx_rows, seq,
                          drop_ctx=(l == depth - 1))
    return xa.reshape(bsz, seq, d)
```

```python
import functools

import numpy as np
import jax
import jax.numpy as jnp
from jax import lax
from jax.experimental import pallas as pl
from jax.experimental.pallas import tpu as pltpu

F32 = jnp.float32
BF16 = jnp.bfloat16
HIGHEST = lax.Precision.HIGHEST

D_MODEL = 1024
CHUNK = 128
A_HEADS = 4
A_HEAD_DIM = 64
A_WIDTH = 256
B_HEADS = 4
B_DK = 64
B_DV = 128
B_QK = 256
B_WIDTH = 512
GATE_RANK = 16
GATE_TAU = 16.0
GLA_CHUNK = 64
S5_GROUPS = 16
S5_GROUP_CH = 16
S5_STATE = 64
C_WIDTH = 256
N_EXPERTS = 32
TOP_K = 4
D_EXPERT = 1024
SWIGLU_LIMIT = 7.0
SWIGLU_ALPHA = 1.702
LN_EPS = 1e-5

N_IN_PAD = 2432
COL_GL = 2304
COL_S5 = 2048

ROW_TILE = 512
SEQ_TILE = 256
S5_T = 16
S5_QUARTER = 4
MOE_BLOCK = 512
VMEM_LIMIT = 56 * 1024 * 1024


def _layer_norm(x, g, b):
    mu = jnp.mean(x, axis=-1, keepdims=True)
    xc = x - mu
    var = jnp.mean(xc * xc, axis=-1, keepdims=True)
    return xc * lax.rsqrt(var + LN_EPS) * g + b


LANES = 128
ROW_SUB = D_MODEL // LANES


def _store_row_tiles(ref, lead, val):
    n = val.shape[0]
    for j in range(ROW_SUB):
        ref[lead + (pl.ds(j, n, stride=ROW_SUB), slice(None))] = val[:, j * LANES:(j + 1) * LANES]


def _load_row_tiles(ref, lead, n):
    return jnp.concatenate(
        [ref[lead + (pl.ds(j, n, stride=ROW_SUB), slice(None))] for j in range(ROW_SUB)], axis=1)


def _row_tile(ref, lead, r):
    return ref.at[lead + (pl.ds(pl.multiple_of(r * ROW_SUB, ROW_SUB), ROW_SUB), slice(None))]


def _segment(row0, n_ctx_rows, seq):
    return jnp.where(row0 < n_ctx_rows, 0, 1 + (row0 - n_ctx_rows) // seq)


def _mod_kernel(c_ref, w_ref, b_ref, o_ref):
    c = c_ref[...]
    s = c * jax.nn.sigmoid(c)
    o_ref[0] = jnp.dot(s, w_ref[0], precision=HIGHEST, preferred_element_type=F32) + b_ref[0]


def _modulation(cvec, w_mod, b_mod):
    depth = w_mod.shape[0]
    n6 = w_mod.shape[2]
    tn = 1024
    return pl.pallas_call(
        _mod_kernel,
        grid=(depth, n6 // tn),
        in_specs=[
            pl.BlockSpec((8, D_MODEL), lambda l, j: (0, 0)),
            pl.BlockSpec((1, D_MODEL, tn), lambda l, j: (l, 0, j)),
            pl.BlockSpec((1, 1, tn), lambda l, j: (l, 0, j)),
        ],
        out_specs=pl.BlockSpec((1, 8, tn), lambda l, j: (l, 0, j)),
        out_shape=jax.ShapeDtypeStruct((depth, 8, n6), F32),
        compiler_params=pltpu.CompilerParams(
            dimension_semantics=("arbitrary", "arbitrary"), vmem_limit_bytes=VMEM_LIMIT),
    )(cvec, w_mod, b_mod.reshape(depth, 1, n6))


def _inproj_kernel(x_ref, mod_ref, w_ref, o_ref, u5_ref, s5_scr):
    m = mod_ref[0]
    h = x_ref[...] * (1.0 + m[1:2]) + m[0:1]
    res = jnp.dot(h.astype(BF16), w_ref[...], preferred_element_type=F32)
    o_ref[...] = res.astype(o_ref.dtype)
    n_row = x_ref.shape[0] // S5_T
    per_tile = LANES // S5_GROUP_CH
    for half in range(C_WIDTH // LANES):
        s5_scr[half] = res[:, COL_S5 + half * LANES:COL_S5 + (half + 1) * LANES]
    at_step = [[s5_scr[half, pl.ds(step, n_row, stride=S5_T), :] for half in range(C_WIDTH // LANES)]
               for step in range(S5_T)]
    for g in range(S5_GROUPS):
        lanes = slice((g % per_tile) * S5_GROUP_CH, (g % per_tile + 1) * S5_GROUP_CH)
        u5_ref[g] = jnp.concatenate(
            [at_step[step][g // per_tile][:, lanes] for step in range(S5_T)], axis=1).astype(BF16)


def _in_projection(x, mod_l, w_in_b, n_ctx_rows, seq):
    t = x.shape[0]
    tm = ROW_TILE
    return pl.pallas_call(
        _inproj_kernel,
        grid=(t // tm,),
        in_specs=[
            pl.BlockSpec((tm, D_MODEL), lambda i: (i, 0)),
            pl.BlockSpec((1, 6, D_MODEL), lambda i: (_segment(i * tm, n_ctx_rows, seq), 0, 0)),
            pl.BlockSpec((D_MODEL, N_IN_PAD), lambda i: (0, 0)),
        ],
        out_specs=[
            pl.BlockSpec((tm, N_IN_PAD), lambda i: (i, 0)),
            pl.BlockSpec((S5_GROUPS, tm // S5_T, S5_T * S5_GROUP_CH), lambda i: (0, i, 0)),
        ],
        out_shape=[
            jax.ShapeDtypeStruct((t, N_IN_PAD), BF16),
            jax.ShapeDtypeStruct((S5_GROUPS, t // S5_T, S5_T * S5_GROUP_CH), BF16),
        ],
        scratch_shapes=[pltpu.VMEM((C_WIDTH // LANES, tm, LANES), F32)],
        compiler_params=pltpu.CompilerParams(
            dimension_semantics=("arbitrary",), vmem_limit_bytes=VMEM_LIMIT),
    )(x, mod_l, w_in_b)


def _sgu_kernel(uv_ref, g_ref, b_ref, w_ref, bias_ref, o_ref):
    tm = uv_ref.shape[0]
    u = jax.nn.gelu(uv_ref[:, :A_WIDTH].astype(F32))
    v = _layer_norm(jax.nn.gelu(uv_ref[:, A_WIDTH:].astype(F32)), g_ref[...], b_ref[...]).astype(BF16)
    head = lax.broadcasted_iota(jnp.int32, (1, A_WIDTH), 1) // A_HEAD_DIM
    for c in range(tm // CHUNK):
        rows = slice(c * CHUNK, (c + 1) * CHUNK)
        vc = v[rows]
        acc = bias_ref[...]
        for h in range(A_HEADS):
            r = jnp.dot(w_ref[h], vc, preferred_element_type=F32)
            acc = acc + jnp.where(head == h, r, 0.0)
        o_ref[rows, :] = (u[rows] * acc).astype(BF16)


def _spatial_gate(proj, ln_g, ln_b, w_s, b_s):
    t = proj.shape[0]
    tm = ROW_TILE
    bias = jnp.repeat(b_s.T, A_HEAD_DIM, axis=1)
    return pl.pallas_call(
        _sgu_kernel,
        grid=(t // tm,),
        in_specs=[
            pl.BlockSpec((tm, 2 * A_WIDTH), lambda i: (i, 0)),
            pl.BlockSpec((1, A_WIDTH), lambda i: (0, 0)),
            pl.BlockSpec((1, A_WIDTH), lambda i: (0, 0)),
            pl.BlockSpec((A_HEADS, CHUNK, CHUNK), lambda i: (0, 0, 0)),
            pl.BlockSpec((CHUNK, A_WIDTH), lambda i: (0, 0)),
        ],
        out_specs=pl.BlockSpec((tm, A_WIDTH), lambda i: (i, 0)),
        out_shape=jax.ShapeDtypeStruct((t, A_WIDTH), BF16),
        compiler_params=pltpu.CompilerParams(
            dimension_semantics=("arbitrary",), vmem_limit_bytes=VMEM_LIMIT),
    )(proj, ln_g.reshape(1, -1), ln_b.reshape(1, -1), w_s.astype(BF16), bias)


_NT = (((1,), (1,)), ((), ()))
_TN = (((0,), (0,)), ((), ()))


def _gla_direction(backward, q_ref, k_ref, v_ref, gl_ref, gup_ref, gb_ref, o_ref, s_ref):
    c_len = GLA_CHUNK
    n_rows = q_ref.shape[0]
    n_chunks = n_rows // c_len

    row = lax.broadcasted_iota(jnp.int32, (n_rows, n_rows), 0)
    col = lax.broadcasted_iota(jnp.int32, (n_rows, n_rows), 1)
    same = (row // c_len) == (col // c_len)
    if backward:
        tri = jnp.where(same & (col >= row), 1.0, 0.0).astype(BF16)
        keep = same & (col > row)
        i_last, i_mid = 0, c_len - 1 - c_len // 2
        lo = GATE_RANK
    else:
        tri = jnp.where(same & (col <= row), 1.0, 0.0).astype(BF16)
        keep = same & (col <= row)
        i_last, i_mid = c_len - 1, c_len // 2
        lo = 0

    gl = gl_ref[:, lo:lo + GATE_RANK]
    z = (jnp.dot(gl, gup_ref[0], preferred_element_type=F32) + jnp.dot(gl, gup_ref[1], preferred_element_type=F32)
         + gb_ref[...])
    la = jax.nn.log_sigmoid(z) / GATE_TAU
    l1 = la.astype(BF16)
    r1 = la - l1.astype(F32)
    l2 = r1.astype(BF16)
    l3 = (r1 - l2.astype(F32)).astype(BF16)
    b = (jnp.dot(tri, l1, preferred_element_type=F32) + jnp.dot(tri, l2, preferred_element_type=F32)
         + jnp.dot(tri, l3, preferred_element_type=F32))

    def per_chunk(index):
        return jnp.concatenate(
            [jnp.broadcast_to(b[c * c_len + index:c * c_len + index + 1], (c_len, B_QK))
             for c in range(n_chunks)], axis=0)

    b_last = per_chunk(i_last)
    b_mid = per_chunk(i_mid)
    q = q_ref[...].astype(F32) * (B_DK ** -0.5)
    k = k_ref[...].astype(F32)
    q_mid = (q * jnp.exp(b - b_mid)).astype(BF16)
    k_mid = (k * jnp.exp(b_mid - b)).astype(BF16)
    q_in = (q * jnp.exp(b)).astype(BF16)
    k_out = (k * jnp.exp(b_last - b)).astype(BF16)
    order = range(n_chunks - 1, -1, -1) if backward else range(n_chunks)
    v_all = v_ref[...].astype(BF16)
    intra = []
    for h in range(B_HEADS):
        hk = slice(h * B_DK, (h + 1) * B_DK)
        hv = slice(h * B_DV, (h + 1) * B_DV)
        sc = lax.dot_general(q_mid[:, hk], k_mid[:, hk], _NT, preferred_element_type=F32)
        sc = jnp.where(keep, sc, 0.0).astype(BF16)
        intra.append(jnp.dot(sc, v_all[:, hv], preferred_element_type=F32))
    o_intra = jnp.concatenate(intra, axis=1)
    own = (lax.broadcasted_iota(jnp.int32, (B_WIDTH, B_QK), 0) // B_DV
           == lax.broadcasted_iota(jnp.int32, (B_WIDTH, B_QK), 1) // B_DK)
    state = s_ref[...]
    for c in order:
        rows = slice(c * c_len, (c + 1) * c_len)
        o_ref[rows, :] = o_intra[rows] + lax.dot_general(q_in[rows], state.astype(BF16), _NT,
                                                         preferred_element_type=F32)
        decay = jnp.exp(b[c * c_len + i_last:c * c_len + i_last + 1])
        update = lax.dot_general(v_all[rows], k_out[rows], _TN, preferred_element_type=F32)
        state = jnp.where(own, state * decay + update, 0.0)
    s_ref[...] = state


def _gla_kernel(qf, kf, vf, glf, qb, kb, vb, glb, gup_ref, gb_ref, of_ref, ob_ref, s_ref):
    @pl.when(pl.program_id(1) == 0)
    def _():
        s_ref[...] = jnp.zeros_like(s_ref)

    _gla_direction(False, qf, kf, vf, glf, gup_ref.at[0], gb_ref.at[0], of_ref, s_ref.at[0])
    _gla_direction(True, qb, kb, vb, glb, gup_ref.at[1], gb_ref.at[1], ob_ref, s_ref.at[1])


def _gla_block(backward, bsz, nctx_blk, nlat_blk, b, j):
    if backward:
        ctx_i = b * nctx_blk + (nctx_blk - 1 - j)
        lat_i = bsz * nctx_blk + b * nlat_blk + (nlat_blk - 1 - (j - nctx_blk))
    else:
        ctx_i = b * nctx_blk + j
        lat_i = bsz * nctx_blk + b * nlat_blk + (j - nctx_blk)
    return jnp.where(j < nctx_blk, ctx_i, lat_i)


def _gla_sweep(proj, gate_up, gate_b, bsz, ctx_len, seq):
    t = proj.shape[0]
    r = SEQ_TILE
    nctx_blk, nlat_blk = ctx_len // r, seq // r
    up_hi = gate_up.astype(BF16)
    in_specs = []
    for backward in (False, True):
        blk = functools.partial(_gla_block, backward, bsz, nctx_blk, nlat_blk)
        in_specs += [
            pl.BlockSpec((r, B_QK), lambda b, j, blk=blk: (blk(b, j), 2)),
            pl.BlockSpec((r, B_QK), lambda b, j, blk=blk: (blk(b, j), 3)),
            pl.BlockSpec((r, B_WIDTH), lambda b, j, blk=blk: (blk(b, j), 2)),
            pl.BlockSpec((r, 128), lambda b, j, blk=blk: (blk(b, j), COL_GL // 128)),
        ]
    in_specs += [
        pl.BlockSpec((2, 2, GATE_RANK, B_QK), lambda b, j: (0, 0, 0, 0)),
        pl.BlockSpec((2, 1, B_QK), lambda b, j: (0, 0, 0)),
    ]
    fwd = functools.partial(_gla_block, False, bsz, nctx_blk, nlat_blk)
    bwd = functools.partial(_gla_block, True, bsz, nctx_blk, nlat_blk)
    return pl.pallas_call(
        _gla_kernel,
        grid=(bsz, nctx_blk + nlat_blk),
        in_specs=in_specs,
        out_specs=[
            pl.BlockSpec((r, B_WIDTH), lambda b, j: (fwd(b, j), 0)),
            pl.BlockSpec((r, B_WIDTH), lambda b, j: (bwd(b, j), 0)),
        ],
        out_shape=[jax.ShapeDtypeStruct((t, B_WIDTH), F32), jax.ShapeDtypeStruct((t, B_WIDTH), F32)],
        scratch_shapes=[pltpu.VMEM((2, B_WIDTH, B_QK), F32)],
        compiler_params=pltpu.CompilerParams(
            dimension_semantics=("arbitrary", "arbitrary"), vmem_limit_bytes=VMEM_LIMIT),
    )(*([proj] * 8), jnp.stack([up_hi, (gate_up - up_hi.astype(F32)).astype(BF16)], axis=1),
      gate_b.reshape(2, 1, -1))


def _s5_matrices(lam_re, lam_im, log_dt, b_re, b_im, c_re, c_im):
    tc = S5_T
    lam = lax.complex(lam_re.astype(F32), lam_im.astype(F32))
    dt = jnp.exp(log_dt.astype(F32))
    bm = lax.complex(b_re.astype(F32), b_im.astype(F32))
    cm = lax.complex(c_re.astype(F32), c_im.astype(F32))
    ldt = lam * dt[..., None]
    lam_bar = jnp.exp(ldt)
    b_bar = ((lam_bar - 1.0) / lam)[..., None] * bm
    steps = jnp.arange(tc + 1, dtype=F32)
    pw = jnp.exp(ldt[:, None] * steps[None, :, None, None])
    kern = jnp.real(jnp.einsum('dgip,dtgp,dgpj->dgtij', cm, pw[:, :tc], b_bar))
    s_i = jnp.arange(tc)[:, None]
    t_i = jnp.arange(tc)[None, :]
    g_n = lam.shape[1]
    hh = S5_GROUP_CH

    def toeplitz(kd, lag, ok):
        m = kd[:, jnp.clip(lag, 0, tc - 1)]
        m = jnp.where(ok[None, :, :, None, None], m, 0.0)
        return m.transpose(0, 1, 4, 2, 3).reshape(g_n, tc * hh, tc * hh)

    mt = toeplitz(kern[0], t_i - s_i, t_i >= s_i) + toeplitz(kern[1], s_i - t_i, s_i >= t_i)

    def state_in(d, powers):
        w = pw[d][powers][:, :, :, None] * b_bar[d][None]
        w = w.transpose(1, 0, 3, 2).reshape(g_n, tc * hh, S5_STATE)
        re, im = jnp.real(w), jnp.imag(w)
        return jnp.concatenate([re, im, im, re], axis=-1)

    qt = jnp.concatenate([state_in(0, tc - 1 - jnp.arange(tc)), state_in(1, jnp.arange(tc))], axis=-1)

    def state_out(d, powers):
        w = cm[d][:, None] * pw[d][powers].transpose(1, 0, 2)[:, :, None, :]
        w = w.reshape(g_n, tc * hh, S5_STATE).transpose(0, 2, 1)
        return jnp.concatenate([jnp.real(w), -jnp.imag(w)], axis=1)

    pt = jnp.concatenate([state_out(0, 1 + jnp.arange(tc)), state_out(1, tc - jnp.arange(tc))], axis=1)

    a = pw[:, tc]
    ar, ai = jnp.real(a), jnp.imag(a)
    a1 = jnp.concatenate([ar, ar], axis=-1).reshape(2, -1)
    a2 = jnp.concatenate([-ai, ai], axis=-1).reshape(2, -1)
    a3 = jnp.concatenate([ai, -ai], axis=-1).reshape(2, -1)
    ac = jnp.stack([a1, a2, a3], axis=1)
    return mt.astype(BF16), qt.astype(BF16), pt.astype(BF16), ac.reshape(2, 3, 1, -1)


def _s5_kernel(uc_ref, ul_ref, mt_ref, qt_ref, pt_ref, ac_ref, yc_ref, yl_ref, ef, esf, eb, esb):
    nq = uc_ref.shape[0]
    nc_ctx, nc_lat = uc_ref.shape[1], ul_ref.shape[1]
    n = nc_ctx + nc_lat
    w = 2 * S5_STATE
    for gi in range(nq):
        lanes = slice(gi * w, (gi + 1) * w)
        rc = jnp.dot(uc_ref[gi], qt_ref[gi], preferred_element_type=F32)
        rl = jnp.dot(ul_ref[gi], qt_ref[gi], preferred_element_type=F32)
        ef[0:nc_ctx, lanes] = rc[:, 0:w]
        esf[0:nc_ctx, lanes] = rc[:, w:2 * w]
        ef[nc_ctx:n, lanes] = rl[:, 0:w]
        esf[nc_ctx:n, lanes] = rl[:, w:2 * w]
        eb[0:nc_lat, lanes] = rl[:, 2 * w:3 * w]
        esb[0:nc_lat, lanes] = rl[:, 3 * w:4 * w]
        eb[nc_lat:n, lanes] = rc[:, 2 * w:3 * w]
        esb[nc_lat:n, lanes] = rc[:, 3 * w:4 * w]

    a1f, a2f, a3f = ac_ref[0, 0], ac_ref[0, 1], ac_ref[0, 2]
    a1b, a2b, a3b = ac_ref[1, 0], ac_ref[1, 1], ac_ref[1, 2]

    def body(i, carry):
        hf, hsf, hb, hsb = carry
        rf = pl.ds(i, 1)
        rb = pl.ds(n - 1 - i, 1)
        e_f, es_f = ef[rf, :], esf[rf, :]
        e_b, es_b = eb[rb, :], esb[rb, :]
        ef[rf, :] = hf
        eb[rb, :] = hb
        return (a1f * hf + a2f * hsf + e_f, a1f * hsf + a3f * hf + es_f,
                a1b * hb + a2b * hsb + e_b, a1b * hsb + a3b * hb + es_b)

    zero = jnp.zeros((1, nq * w), F32)
    lax.fori_loop(0, n, body, (zero, zero, zero, zero))

    for gi in range(nq):
        lanes = slice(gi * w, (gi + 1) * w)
        hc = jnp.concatenate([ef[0:nc_ctx, lanes], eb[nc_lat:n, lanes]], axis=1).astype(BF16)
        hl = jnp.concatenate([ef[nc_ctx:n, lanes], eb[0:nc_lat, lanes]], axis=1).astype(BF16)
        yc_ref[gi] = (jnp.dot(uc_ref[gi], mt_ref[gi], preferred_element_type=F32)
                      + jnp.dot(hc, pt_ref[gi], preferred_element_type=F32))
        yl_ref[gi] = (jnp.dot(ul_ref[gi], mt_ref[gi], preferred_element_type=F32)
                      + jnp.dot(hl, pt_ref[gi], preferred_element_type=F32))


def _s5_scan(u5, mats, bsz, ctx_len, seq):
    mt, qt, pt, ac = mats
    g_n, tc, hh = S5_GROUPS, S5_T, S5_GROUP_CH
    n_ctx_rows = bsz * ctx_len
    uc, ul = u5[:, :n_ctx_rows // tc], u5[:, n_ctx_rows // tc:]
    nc_ctx, nc_lat = ctx_len // tc, seq // tc
    nq = S5_QUARTER
    wq = nq * 2 * S5_STATE
    n = nc_ctx + nc_lat
    yc, yl = pl.pallas_call(
        _s5_kernel,
        grid=(bsz, g_n // nq),
        in_specs=[
            pl.BlockSpec((nq, nc_ctx, tc * hh), lambda b, qi: (qi, b, 0)),
            pl.BlockSpec((nq, nc_lat, tc * hh), lambda b, qi: (qi, b, 0)),
            pl.BlockSpec((nq, tc * hh, tc * hh), lambda b, qi: (qi, 0, 0)),
            pl.BlockSpec((nq, tc * hh, 8 * S5_STATE), lambda b, qi: (qi, 0, 0)),
            pl.BlockSpec((nq, 4 * S5_STATE, tc * hh), lambda b, qi: (qi, 0, 0)),
            pl.BlockSpec((2, 3, 1, wq), lambda b, qi: (0, 0, 0, qi)),
        ],
        out_specs=[
            pl.BlockSpec((nq, nc_ctx, tc * hh), lambda b, qi: (qi, b, 0)),
            pl.BlockSpec((nq, nc_lat, tc * hh), lambda b, qi: (qi, b, 0)),
        ],
        out_shape=[
            jax.ShapeDtypeStruct((g_n, bsz * nc_ctx, tc * hh), F32),
            jax.ShapeDtypeStruct((g_n, bsz * nc_lat, tc * hh), F32),
        ],
        scratch_shapes=[pltpu.VMEM((n, wq), F32) for _ in range(4)],
        compiler_params=pltpu.CompilerParams(
            dimension_semantics=("arbitrary", "arbitrary"), vmem_limit_bytes=VMEM_LIMIT),
    )(uc, ul, mt, qt, pt, ac)

    def from_rows(y):
        r = y.shape[1]
        return y.reshape(g_n, r, tc, hh).transpose(1, 2, 0, 3).reshape(r * tc, g_n * hh)

    return jnp.concatenate([from_rows(yc), from_rows(yl)], axis=0)


def _outproj_kernel(alpha, a_ref, of_ref, ob_ref, g_ref, ng_ref, y_ref, u_ref, x_ref, mod_ref, d_ref, gw_ref,
                    gb_ref, wo_ref, lng_ref, lnb_ref, rwh_ref, rwl_ref, rb_ref, x1_ref, h2_ref, lg_ref):
    m = mod_ref[0]
    heads = []
    for h in range(B_HEADS):
        hv = slice(h * B_DV, (h + 1) * B_DV)
        o = of_ref[:, hv] + ob_ref[:, hv]
        heads.append(o * lax.rsqrt(jnp.mean(o * o, axis=-1, keepdims=True) + LN_EPS))
    gate = g_ref[...].astype(F32)
    gla = (jnp.concatenate(heads, axis=1) * ng_ref[...] * (gate * jax.nn.sigmoid(gate))).astype(BF16)
    y = jax.nn.gelu(y_ref[...] + d_ref[...] * u_ref[...].astype(F32))
    s5 = y * jax.nn.sigmoid(jnp.dot(y.astype(BF16), gw_ref[...], preferred_element_type=F32) + gb_ref[...])
    mix = (jnp.dot(a_ref[...], wo_ref[0:A_WIDTH, :], preferred_element_type=F32)
           + jnp.dot(gla, wo_ref[A_WIDTH:A_WIDTH + B_WIDTH, :], preferred_element_type=F32)
           + jnp.dot(s5.astype(BF16), wo_ref[A_WIDTH + B_WIDTH:, :], preferred_element_type=F32))
    x1 = _layer_norm(alpha * x_ref[...] + m[2:3] * mix, lng_ref[...], lnb_ref[...])
    x1_ref[...] = x1
    h2 = x1 * (1.0 + m[4:5]) + m[3:4]
    _store_row_tiles(h2_ref, (), h2)
    h_hi = h2.astype(BF16)
    h_lo = (h2 - h_hi.astype(F32)).astype(BF16)
    lg_ref[...] = (jnp.dot(h_hi, rwh_ref[...], preferred_element_type=F32)
                   + jnp.dot(h_lo, rwh_ref[...], preferred_element_type=F32)
                   + jnp.dot(h_hi, rwl_ref[...], preferred_element_type=F32) + rb_ref[...])


def _out_projection(alpha, a_out, o_fwd, o_bwd, norm_g, y_s5, proj, x, mod_l, s5_d, glu_w, glu_b, w_out_b, ln_g,
                    ln_b, router_w, router_b, n_ctx_rows, seq):
    t = x.shape[0]
    tm = ROW_TILE
    rw = jnp.zeros((D_MODEL, 128), F32).at[:, :N_EXPERTS].set(router_w)
    rw_hi = rw.astype(BF16)
    rw_lo = (rw - rw_hi.astype(F32)).astype(BF16)
    rb = jnp.zeros((1, 128), F32).at[0, :N_EXPERTS].set(router_b)
    row = lambda i: (i, 0)
    fixed = lambda i: (0, 0)
    return pl.pallas_call(
        functools.partial(_outproj_kernel, alpha),
        grid=(t // tm,),
        in_specs=[
            pl.BlockSpec((tm, A_WIDTH), row),
            pl.BlockSpec((tm, B_WIDTH), row),
            pl.BlockSpec((tm, B_WIDTH), row),
            pl.BlockSpec((tm, B_WIDTH), lambda i: (i, 3)),
            pl.BlockSpec((1, B_WIDTH), fixed),
            pl.BlockSpec((tm, C_WIDTH), row),
            pl.BlockSpec((tm, C_WIDTH), lambda i: (i, COL_S5 // C_WIDTH)),
            pl.BlockSpec((tm, D_MODEL), row),
            pl.BlockSpec((1, 6, D_MODEL), lambda i: (_segment(i * tm, n_ctx_rows, seq), 0, 0)),
            pl.BlockSpec((1, C_WIDTH), fixed),
            pl.BlockSpec((C_WIDTH, C_WIDTH), fixed),
            pl.BlockSpec((1, C_WIDTH), fixed),
            pl.BlockSpec((D_MODEL, D_MODEL), fixed),
            pl.BlockSpec((1, D_MODEL), fixed),
            pl.BlockSpec((1, D_MODEL), fixed),
            pl.BlockSpec((D_MODEL, 128), fixed),
            pl.BlockSpec((D_MODEL, 128), fixed),
            pl.BlockSpec((1, 128), fixed),
        ],
        out_specs=[
            pl.BlockSpec((tm, D_MODEL), row),
            pl.BlockSpec((tm * ROW_SUB, LANES), row),
            pl.BlockSpec((tm, 128), row),
        ],
        out_shape=[
            jax.ShapeDtypeStruct((t, D_MODEL), F32),
            jax.ShapeDtypeStruct((t * ROW_SUB, LANES), F32),
            jax.ShapeDtypeStruct((t, 128), F32),
        ],
        compiler_params=pltpu.CompilerParams(
            dimension_semantics=("arbitrary",), vmem_limit_bytes=VMEM_LIMIT),
    )(a_out, o_fwd, o_bwd, proj, norm_g.reshape(1, -1), y_s5, proj, x, mod_l, s5_d.reshape(1, -1), glu_w.astype(BF16), glu_b.reshape(1, -1),
      w_out_b, ln_g.reshape(1, -1), ln_b.reshape(1, -1), rw_hi, rw_lo, rb)


def _route_kernel(lg_ref, gate_ref, pos_ref, tbase_ref, tcnt_ref, cnt_ref, base, before, below):
    tm = lg_ref.shape[0]

    @pl.when(pl.program_id(0) == 0)
    def _():
        base[...] = jnp.zeros_like(base)
        r = lax.broadcasted_iota(jnp.int32, (tm, tm), 0)
        c = lax.broadcasted_iota(jnp.int32, (tm, tm), 1)
        before[...] = jnp.where(r < c, 1.0, 0.0).astype(BF16)
        r = lax.broadcasted_iota(jnp.int32, (N_EXPERTS, N_EXPERTS), 0)
        c = lax.broadcasted_iota(jnp.int32, (N_EXPERTS, N_EXPERTS), 1)
        below[...] = jnp.where(c < r, 1.0, 0.0)

    logit = jnp.transpose(lg_ref[...])[:N_EXPERTS]
    eid = lax.broadcasted_iota(jnp.int32, (N_EXPERTS, tm), 0)
    vals, hots = [], []
    work = logit
    for kk in range(TOP_K):
        m = jnp.max(work, axis=0, keepdims=True)
        ix = jnp.min(jnp.where(work == m, eid, N_EXPERTS), axis=0, keepdims=True)
        hot = eid == ix
        vals.append(m)
        hots.append(hot)
        work = jnp.where(hot, -jnp.inf, work)
    ex = [jnp.exp(v - vals[0]) for v in vals]
    den = ex[0] + ex[1] + ex[2] + ex[3]
    member = jnp.zeros((N_EXPERTS, tm), F32)
    for kk in range(TOP_K):
        gate_ref[kk:kk + 1, :] = ex[kk] / den
        member = member + jnp.where(hots[kk], 1.0, 0.0)
    tile_cnt = jnp.broadcast_to(jnp.sum(member, axis=1, keepdims=True), (N_EXPERTS, LANES))
    group_off = jnp.dot(below[...], tile_cnt, precision=HIGHEST, preferred_element_type=F32)[:, 0:1]
    in_group = jnp.dot(member.astype(BF16), before[...], preferred_element_type=F32)
    for kk in range(TOP_K):
        pos_ref[kk:kk + 1, :] = jnp.sum(jnp.where(hots[kk], group_off + in_group, 0.0), axis=0,
                                        keepdims=True).astype(jnp.int32)
    tbase_ref[...] = jnp.broadcast_to(base[...], tbase_ref.shape)
    tcnt_ref[...] = tile_cnt
    total = base[...] + tile_cnt[:, 0:1]
    base[...] = total
    cnt_ref[...] = jnp.broadcast_to(total, cnt_ref.shape)


def _routing(logits, n_blocks):
    t = logits.shape[0]
    tm = ROW_TILE
    nt = t // tm
    gates, pos, tbase, tcnt, cnt = pl.pallas_call(
        _route_kernel,
        grid=(nt,),
        in_specs=[pl.BlockSpec((tm, LANES), lambda i: (i, 0))],
        out_specs=[
            pl.BlockSpec((TOP_K, tm), lambda i: (0, i)),
            pl.BlockSpec((TOP_K, tm), lambda i: (0, i)),
            pl.BlockSpec((N_EXPERTS, LANES), lambda i: (i, 0)),
            pl.BlockSpec((N_EXPERTS, LANES), lambda i: (i, 0)),
            pl.BlockSpec((N_EXPERTS, LANES), lambda i: (0, 0)),
        ],
        out_shape=[
            jax.ShapeDtypeStruct((TOP_K, t), F32),
            jax.ShapeDtypeStruct((TOP_K, t), jnp.int32),
            jax.ShapeDtypeStruct((nt * N_EXPERTS, LANES), F32),
            jax.ShapeDtypeStruct((nt * N_EXPERTS, LANES), F32),
            jax.ShapeDtypeStruct((N_EXPERTS, LANES), F32),
        ],
        scratch_shapes=[pltpu.VMEM((N_EXPERTS, 1), F32), pltpu.VMEM((tm, tm), BF16),
                        pltpu.VMEM((N_EXPERTS, N_EXPERTS), F32)],
        compiler_params=pltpu.CompilerParams(
            dimension_semantics=("arbitrary",), vmem_limit_bytes=VMEM_LIMIT),
    )(logits)
    counts = cnt[:, 0].astype(jnp.int32)
    padded = (counts + MOE_BLOCK - 1) // MOE_BLOCK * MOE_BLOCK
    padded_end = jnp.cumsum(padded)
    padded_start = (padded_end - padded).astype(jnp.int32)
    first_slot = jnp.arange(n_blocks, dtype=jnp.int32) * MOE_BLOCK
    block_expert = jnp.minimum(jnp.sum((padded_end[None, :] <= first_slot[:, None]).astype(jnp.int32), axis=1),
                               N_EXPERTS - 1).astype(jnp.int32)
    n_valid = (padded_end[-1] // MOE_BLOCK).astype(jnp.int32).reshape(1)
    pad_lo = (padded_start + counts).astype(jnp.int32)
    route = dict(
        gates=gates.T.reshape(nt, 1, tm * TOP_K),
        pos=pos.T.reshape(nt, 1, tm * TOP_K),
        tile_base=tbase[:, 0].astype(jnp.int32).reshape(nt, 1, N_EXPERTS),
        tile_cnt=tcnt[:, 0].astype(jnp.int32).reshape(nt, 1, N_EXPERTS),
        start=padded_start, pad_lo=pad_lo, pad_hi=padded_end.astype(jnp.int32))
    return route, block_expert, n_valid


def _expert_runs(tile_cnt_ref, tile_base_ref, start_ref, copy):
    off = 0
    for e in range(N_EXPERTS):
        n = tile_cnt_ref[0, 0, e]
        slot0 = start_ref[e] + tile_base_ref[0, 0, e]
        for bit in range(ROW_TILE.bit_length() - 1, -1, -1):
            size = 1 << bit
            done = (n >> (bit + 1)) << (bit + 1)

            @pl.when(((n >> bit) & 1) == 1)
            def _(off=off, done=done, slot0=slot0, size=size):
                copy(off + done, slot0 + done, size)
        off = off + n


def _rows(ref, lead, row0, n):
    return ref.at[lead + (pl.ds(pl.multiple_of(row0 * ROW_SUB, ROW_SUB), n * ROW_SUB), slice(None))]


def _dispatch_kernel(start_ref, lo_ref, hi_ref, pos_ref, tbase_ref, tcnt_ref, h_ref, o_ref, stage, zbuf, sem, zsem):
    i = pl.program_id(0)
    nt = pl.num_programs(0)
    tm = ROW_TILE
    slot = i % 2
    blk_rows = MOE_BLOCK * ROW_SUB
    n_blocks = o_ref.shape[0] // blk_rows

    def wait_stage(s):
        pltpu.make_async_copy(stage.at[s], stage.at[s], sem.at[s]).wait()

    @pl.when(i >= 2)
    def _():
        wait_stage(slot)

    def place(r, carry):
        row = h_ref[pl.ds(pl.multiple_of(r * ROW_SUB, ROW_SUB), ROW_SUB), :]
        for kk in range(TOP_K):
            p = pos_ref[0, 0, r * TOP_K + kk]
            stage[slot, pl.ds(pl.multiple_of(p * ROW_SUB, ROW_SUB), ROW_SUB), :] = row
        return carry
    lax.fori_loop(0, tm, place, 0, unroll=8)

    def copy(stage_row, slot_row, n):
        pltpu.make_async_copy(_rows(stage, (slot,), stage_row, n), _rows(o_ref, (), slot_row, n),
                              sem.at[slot]).start()
    _expert_runs(tcnt_ref, tbase_ref, start_ref, copy)

    @pl.when(i == nt - 1)
    def _():
        wait_stage(slot)

        @pl.when(i >= 1)
        def _():
            wait_stage(1 - slot)
        zbuf[...] = jnp.zeros_like(zbuf)
        zrow = zbuf.at[pl.ds(0, ROW_SUB), :]
        for e in range(N_EXPERTS):
            def fill(s, carry):
                pltpu.make_async_copy(zrow, _row_tile(o_ref, (), s), zsem).start()
                return carry
            lax.fori_loop(lo_ref[e], hi_ref[e], fill, 0)
        for e in range(N_EXPERTS):
            def drain(s, carry):
                pltpu.make_async_copy(zrow, zrow, zsem).wait()
                return carry
            lax.fori_loop(lo_ref[e], hi_ref[e], drain, 0)
        used = hi_ref[N_EXPERTS - 1] // MOE_BLOCK

        def fill_block(j, carry):
            rows = pl.ds(pl.multiple_of(j * blk_rows, blk_rows), blk_rows)
            pltpu.make_async_copy(zbuf, o_ref.at[rows, :], zsem).start()
            return carry
        lax.fori_loop(used, n_blocks, fill_block, 0)

        def drain_block(j, carry):
            pltpu.make_async_copy(zbuf, zbuf, zsem).wait()
            return carry
        lax.fori_loop(used, n_blocks, drain_block, 0)


def _moe_dispatch(h2t, route, n_blocks):
    nt = route['pos'].shape[0]
    tm = ROW_TILE
    smem = lambda width: pl.BlockSpec((1, 1, width), lambda i, *_: (i, 0, 0), memory_space=pltpu.SMEM)
    grid_spec = pltpu.PrefetchScalarGridSpec(
        num_scalar_prefetch=3,
        grid=(nt,),
        in_specs=[
            smem(tm * TOP_K), smem(N_EXPERTS), smem(N_EXPERTS),
            pl.BlockSpec((tm * ROW_SUB, LANES), lambda i, *_: (i, 0)),
        ],
        out_specs=pl.BlockSpec(memory_space=pl.ANY),
        scratch_shapes=[
            pltpu.VMEM((2, tm * TOP_K * ROW_SUB, LANES), F32),
            pltpu.VMEM((MOE_BLOCK * ROW_SUB, LANES), F32),
            pltpu.SemaphoreType.DMA((2,)),
            pltpu.SemaphoreType.DMA(()),
        ],
    )
    return pl.pallas_call(
        _dispatch_kernel,
        grid_spec=grid_spec,
        out_shape=jax.ShapeDtypeStruct((n_blocks * MOE_BLOCK * ROW_SUB, LANES), F32),
        compiler_params=pltpu.CompilerParams(
            dimension_semantics=("arbitrary",), vmem_limit_bytes=VMEM_LIMIT),
    )(route['start'], route['pad_lo'], route['pad_hi'], route['pos'], route['tile_base'], route['tile_cnt'], h2t)


def _moe_kernel(be_ref, nv_ref, x_ref, wu_ref, bu_ref, wd_ref, bd_ref, o_ref, wu_b, wd_b):
    i = pl.program_id(0)
    n_valid = nv_ref[0]

    @pl.when(i < n_valid)
    def _():
        first = jnp.logical_or(i == 0, be_ref[i] != be_ref[jnp.maximum(i - 1, 0)])

        @pl.when(first)
        def _():
            rows = 64

            def cast(r, carry):
                rs = pl.ds(pl.multiple_of(r * rows, rows), rows)
                wu_b[rs, :] = wu_ref[0, 0, rs, :].astype(BF16)
                wd_b[rs, :] = wd_ref[0, 0, rs, :].astype(BF16)
                return carry
            lax.fori_loop(0, D_MODEL // rows, cast, 0)

        x = _load_row_tiles(x_ref, (), MOE_BLOCK).astype(BF16)
        acc = jnp.zeros((MOE_BLOCK, D_MODEL), F32) + bd_ref[0, 0]
        cw = 512
        for jc in range(D_EXPERT // cw):
            cg = slice(jc * cw, (jc + 1) * cw)
            cl = slice(D_EXPERT + jc * cw, D_EXPERT + (jc + 1) * cw)
            ug = jnp.dot(x, wu_b[:, cg], preferred_element_type=F32) + bu_ref[0, 0, :, cg]
            ul = jnp.dot(x, wu_b[:, cl], preferred_element_type=F32) + bu_ref[0, 0, :, cl]
            xg = jnp.minimum(ug, SWIGLU_LIMIT)
            xl = jnp.clip(ul, -SWIGLU_LIMIT, SWIGLU_LIMIT)
            act = xg * jax.nn.sigmoid(SWIGLU_ALPHA * xg) * (xl + 1.0)
            acc = acc + jnp.dot(act.astype(BF16), wd_b[cg, :], preferred_element_type=F32)
        _store_row_tiles(o_ref, (), acc)

    @pl.when(i >= n_valid)
    def _():
        o_ref[...] = jnp.zeros_like(o_ref)


def _moe_experts(layer, xs, block_expert, n_valid, w_up, b_up, w_down, b_down):
    n_blocks = block_expert.shape[0]
    depth = w_up.shape[0]
    grid_spec = pltpu.PrefetchScalarGridSpec(
        num_scalar_prefetch=2,
        grid=(n_blocks,),
        in_specs=[
            pl.BlockSpec((MOE_BLOCK * ROW_SUB, LANES),
                         lambda i, be, nv: (jnp.minimum(i, jnp.maximum(nv[0] - 1, 0)), 0)),
            pl.BlockSpec((1, 1, D_MODEL, 2 * D_EXPERT), lambda i, be, nv: (layer, be[i], 0, 0)),
            pl.BlockSpec((1, 1, 1, 2 * D_EXPERT), lambda i, be, nv: (layer, be[i], 0, 0)),
            pl.BlockSpec((1, 1, D_EXPERT, D_MODEL), lambda i, be, nv: (layer, be[i], 0, 0)),
            pl.BlockSpec((1, 1, 1, D_MODEL), lambda i, be, nv: (layer, be[i], 0, 0)),
        ],
        out_specs=pl.BlockSpec((MOE_BLOCK * ROW_SUB, LANES), lambda i, be, nv: (i, 0)),
        scratch_shapes=[
            pltpu.VMEM((D_MODEL, 2 * D_EXPERT), BF16),
            pltpu.VMEM((D_EXPERT, D_MODEL), BF16),
        ],
    )
    return pl.pallas_call(
        _moe_kernel,
        grid_spec=grid_spec,
        out_shape=jax.ShapeDtypeStruct((n_blocks * MOE_BLOCK * ROW_SUB, LANES), F32),
        compiler_params=pltpu.CompilerParams(
            dimension_semantics=("arbitrary",), vmem_limit_bytes=VMEM_LIMIT),
    )(block_expert, n_valid, xs, w_up, b_up.reshape(depth, N_EXPERTS, 1, -1), w_down,
      b_down.reshape(depth, N_EXPERTS, 1, -1))


def _combine_kernel(alpha, start_ref, pos_ref, gate_ref, tbase_ref, tcnt_ref, tbase_n_ref, tcnt_n_ref, y_ref, x_ref,
                    mod_ref, lng_ref, lnb_ref, o_ref, stage, frow, sem):
    i = pl.program_id(0)
    nt = pl.num_programs(0)
    tm = ROW_TILE
    slot = i % 2

    def fetch(cnt_ref, base_ref, s):
        def copy(stage_row, slot_row, n):
            pltpu.make_async_copy(_rows(y_ref, (), slot_row, n), _rows(stage, (s,), stage_row, n),
                                  sem.at[s]).start()
        _expert_runs(cnt_ref, base_ref, start_ref, copy)

    @pl.when(i == 0)
    def _():
        fetch(tcnt_ref, tbase_ref, 0)

    @pl.when(i + 1 < nt)
    def _():
        fetch(tcnt_n_ref, tbase_n_ref, 1 - slot)

    pltpu.make_async_copy(stage.at[slot], stage.at[slot], sem.at[slot]).wait()

    def mix(r, carry):
        acc = None
        for kk in range(TOP_K):
            p = pos_ref[0, 0, r * TOP_K + kk]
            term = gate_ref[0, 0, r * TOP_K + kk] * stage[slot, pl.ds(pl.multiple_of(p * ROW_SUB, ROW_SUB), ROW_SUB), :]
            acc = term if acc is None else acc + term
        frow[pl.ds(pl.multiple_of(r * ROW_SUB, ROW_SUB), ROW_SUB), :] = acc
        return carry
    lax.fori_loop(0, tm, mix, 0, unroll=8)

    m = mod_ref[0]
    f = _load_row_tiles(frow, (), tm)
    o_ref[...] = _layer_norm(alpha * x_ref[...] + m[5:6] * f, lng_ref[...], lnb_ref[...])


def _moe_combine(alpha, route, ys, x1, mod_l, ln_g, ln_b, n_ctx_rows, seq, drop_ctx):
    t = x1.shape[0]
    tm = ROW_TILE
    nt = t // tm
    skip = n_ctx_rows // tm if drop_ctx else 0
    cur = lambda width: pl.BlockSpec((1, 1, width), lambda i, *_: (i, 0, 0), memory_space=pltpu.SMEM)
    nxt = lambda width: pl.BlockSpec((1, 1, width), lambda i, *_: (jnp.minimum(i + 1, nt - 1), 0, 0),
                                     memory_space=pltpu.SMEM)
    grid_spec = pltpu.PrefetchScalarGridSpec(
        num_scalar_prefetch=1,
        grid=(nt,),
        in_specs=[
            cur(tm * TOP_K), cur(tm * TOP_K), cur(N_EXPERTS), cur(N_EXPERTS), nxt(N_EXPERTS), nxt(N_EXPERTS),
            pl.BlockSpec(memory_space=pl.ANY),
            pl.BlockSpec((tm, D_MODEL), lambda i, *_: (i, 0)),
            pl.BlockSpec((1, 6, D_MODEL), lambda i, *_: (_segment(i * tm, n_ctx_rows, seq), 0, 0)),
            pl.BlockSpec((1, D_MODEL), lambda i, *_: (0, 0)),
            pl.BlockSpec((1, D_MODEL), lambda i, *_: (0, 0)),
        ],
        out_specs=pl.BlockSpec((tm, D_MODEL), lambda i, *_: (jnp.maximum(i - skip, 0), 0)),
        scratch_shapes=[
            pltpu.VMEM((2, tm * TOP_K * ROW_SUB, LANES), F32),
            pltpu.VMEM((tm * ROW_SUB, LANES), F32),
            pltpu.SemaphoreType.DMA((2,)),
        ],
    )
    return pl.pallas_call(
        functools.partial(_combine_kernel, alpha),
        grid_spec=grid_spec,
        out_shape=jax.ShapeDtypeStruct((t - skip * tm, D_MODEL), F32),
        compiler_params=pltpu.CompilerParams(
            dimension_semantics=("arbitrary",), vmem_limit_bytes=VMEM_LIMIT),
    )(route['start'], route['pos'], route['gates'], route['tile_base'], route['tile_cnt'], route['tile_base'],
      route['tile_cnt'], ys, x1, mod_l, ln_g.reshape(1, -1), ln_b.reshape(1, -1))


def kernel(x, c, ctx, c_ctx, w_mod, b_mod, w_in, sgu_ln_g, sgu_ln_b, sgu_w, sgu_b, gla_gate_up, gla_gate_b,
           gla_norm_g, s5_lam_re, s5_lam_im, s5_log_dt, s5_b_re, s5_b_im, s5_c_re, s5_c_im, s5_d, s5_glu_w,
           s5_glu_b, w_out, ln_g, ln_b, router_w, router_b, w_up, b_up, w_down, b_down):
    bsz, seq, d = x.shape
    ctx_len = ctx.shape[1]
    depth = w_in.shape[0]
    alpha = float((2 * depth) ** 0.25)
    n_ctx_rows = bsz * ctx_len
    t = n_ctx_rows + bsz * seq
    assert d == D_MODEL and bsz + 1 <= 8
    assert n_ctx_rows % ROW_TILE == 0 and seq % ROW_TILE == 0
    assert ctx_len % SEQ_TILE == 0 and seq % SEQ_TILE == 0

    xa = jnp.concatenate([ctx.reshape(n_ctx_rows, d), x.reshape(bsz * seq, d)], axis=0)
    cvec = jnp.zeros((8, d), F32).at[0].set(c_ctx).at[1:1 + bsz].set(c)
    mod = _modulation(cvec, w_mod, b_mod).reshape(depth, 8, 6, d)

    o = np.cumsum((0, A_WIDTH, A_WIDTH, B_QK, B_QK, B_WIDTH, B_WIDTH, 2 * GATE_RANK, C_WIDTH))
    w_in_r = jnp.concatenate(
        [w_in[:, :, o[0]:o[6]], w_in[:, :, o[7]:o[8]], w_in[:, :, o[6]:o[7]],
         jnp.zeros((depth, d, N_IN_PAD - int(o[8])), w_in.dtype)], axis=-1).astype(BF16)
    w_out_b = w_out.astype(BF16)

    n_assign = t * TOP_K
    n_blocks = -(-(n_assign + N_EXPERTS * (MOE_BLOCK - 1)) // MOE_BLOCK)

    for l in range(depth):
        mod_l = mod[l]
        proj, u5 = _in_projection(xa, mod_l, w_in_r[l], n_ctx_rows, seq)
        a_out = _spatial_gate(proj, sgu_ln_g[l], sgu_ln_b[l], sgu_w[l], sgu_b[l])
        o_fwd, o_bwd = _gla_sweep(proj, gla_gate_up[l], gla_gate_b[l], bsz, ctx_len, seq)
        mats = _s5_matrices(s5_lam_re[l], s5_lam_im[l], s5_log_dt[l], s5_b_re[l], s5_b_im[l],
                            s5_c_re[l], s5_c_im[l])
        y_s5 = _s5_scan(u5, mats, bsz, ctx_len, seq)
        x1, h2, logits = _out_projection(alpha, a_out, o_fwd, o_bwd, gla_norm_g[l], y_s5, proj, xa, mod_l, s5_d[l], s5_glu_w[l],
                                         s5_glu_b[l], w_out_b[l], ln_g[l, 0], ln_b[l, 0], router_w[l],
                                         router_b[l], n_ctx_rows, seq)
        route, block_expert, n_valid = _routing(logits, n_blocks)
        xs = _moe_dispatch(h2, route, n_blocks)
        ys = _moe_experts(l, xs, block_expert, n_valid, w_up, b_up, w_down, b_down)
        xa = _moe_combine(alpha, route, ys, x1, mod_l, ln_g[l, 1], ln_b[l, 1], n_ctx_rows, seq,
                          drop_ctx=(l == depth - 1))
    return xa.reshape(bsz, seq, d)
```

```python
import functools

import numpy as np
import jax
import jax.numpy as jnp
from jax import lax
from jax.experimental import pallas as pl
from jax.experimental.pallas import tpu as pltpu

F32 = jnp.float32
BF16 = jnp.bfloat16
HIGHEST = lax.Precision.HIGHEST

D_MODEL = 1024
CHUNK = 128
A_HEADS = 4
A_HEAD_DIM = 64
A_WIDTH = 256
B_HEADS = 4
B_DK = 64
B_DV = 128
B_QK = 256
B_WIDTH = 512
GATE_RANK = 16
GATE_TAU = 16.0
GLA_CHUNK = 64
S5_GROUPS = 16
S5_GROUP_CH = 16
S5_STATE = 64
C_WIDTH = 256
N_EXPERTS = 32
TOP_K = 4
D_EXPERT = 1024
SWIGLU_LIMIT = 7.0
SWIGLU_ALPHA = 1.702
LN_EPS = 1e-5

N_IN_PAD = 2432
COL_GL = 2304
COL_S5 = 2048

ROW_TILE = 512
SEQ_TILE = 256
S5_T = 16
S5_QUARTER = 4
MOE_BLOCK = 512
VMEM_LIMIT = 56 * 1024 * 1024


def _layer_norm(x, g, b):
    mu = jnp.mean(x, axis=-1, keepdims=True)
    xc = x - mu
    var = jnp.mean(xc * xc, axis=-1, keepdims=True)
    return xc * lax.rsqrt(var + LN_EPS) * g + b


LANES = 128
ROW_SUB = D_MODEL // LANES


def _store_row_tiles(ref, lead, val):
    n = val.shape[0]
    for j in range(ROW_SUB):
        ref[lead + (pl.ds(j, n, stride=ROW_SUB), slice(None))] = val[:, j * LANES:(j + 1) * LANES]


def _load_row_tiles(ref, lead, n):
    return jnp.concatenate(
        [ref[lead + (pl.ds(j, n, stride=ROW_SUB), slice(None))] for j in range(ROW_SUB)], axis=1)


def _row_tile(ref, lead, r):
    return ref.at[lead + (pl.ds(pl.multiple_of(r * ROW_SUB, ROW_SUB), ROW_SUB), slice(None))]


def _segment(row0, n_ctx_rows, seq):
    return jnp.where(row0 < n_ctx_rows, 0, 1 + (row0 - n_ctx_rows) // seq)


def _mod_kernel(c_ref, w_ref, b_ref, o_ref):
    c = c_ref[...]
    s = c * jax.nn.sigmoid(c)
    o_ref[0] = jnp.dot(s, w_ref[0], precision=HIGHEST, preferred_element_type=F32) + b_ref[0]


def _modulation(cvec, w_mod, b_mod):
    depth = w_mod.shape[0]
    n6 = w_mod.shape[2]
    tn = 1024
    return pl.pallas_call(
        _mod_kernel,
        grid=(depth, n6 // tn),
        in_specs=[
            pl.BlockSpec((8, D_MODEL), lambda l, j: (0, 0)),
            pl.BlockSpec((1, D_MODEL, tn), lambda l, j: (l, 0, j)),
            pl.BlockSpec((1, 1, tn), lambda l, j: (l, 0, j)),
        ],
        out_specs=pl.BlockSpec((1, 8, tn), lambda l, j: (l, 0, j)),
        out_shape=jax.ShapeDtypeStruct((depth, 8, n6), F32),
        compiler_params=pltpu.CompilerParams(
            dimension_semantics=("arbitrary", "arbitrary"), vmem_limit_bytes=VMEM_LIMIT),
    )(cvec, w_mod, b_mod.reshape(depth, 1, n6))


def _inproj_kernel(x_ref, mod_ref, w_ref, o_ref, u5_ref, s5_scr):
    m = mod_ref[0]
    h = x_ref[...] * (1.0 + m[1:2]) + m[0:1]
    res = jnp.dot(h.astype(BF16), w_ref[...], preferred_element_type=F32)
    o_ref[...] = res.astype(o_ref.dtype)
    n_row = x_ref.shape[0] // S5_T
    per_tile = LANES // S5_GROUP_CH
    for half in range(C_WIDTH // LANES):
        s5_scr[half] = res[:, COL_S5 + half * LANES:COL_S5 + (half + 1) * LANES]
    at_step = [[s5_scr[half, pl.ds(step, n_row, stride=S5_T), :] for half in range(C_WIDTH // LANES)]
               for step in range(S5_T)]
    for g in range(S5_GROUPS):
        lanes = slice((g % per_tile) * S5_GROUP_CH, (g % per_tile + 1) * S5_GROUP_CH)
        u5_ref[g] = jnp.concatenate(
            [at_step[step][g // per_tile][:, lanes] for step in range(S5_T)], axis=1).astype(BF16)


def _in_projection(x, mod_l, w_in_b, n_ctx_rows, seq):
    t = x.shape[0]
    tm = ROW_TILE
    return pl.pallas_call(
        _inproj_kernel,
        grid=(t // tm,),
        in_specs=[
            pl.BlockSpec((tm, D_MODEL), lambda i: (i, 0)),
            pl.BlockSpec((1, 6, D_MODEL), lambda i: (_segment(i * tm, n_ctx_rows, seq), 0, 0)),
            pl.BlockSpec((D_MODEL, N_IN_PAD), lambda i: (0, 0)),
        ],
        out_specs=[
            pl.BlockSpec((tm, N_IN_PAD), lambda i: (i, 0)),
            pl.BlockSpec((S5_GROUPS, tm // S5_T, S5_T * S5_GROUP_CH), lambda i: (0, i, 0)),
        ],
        out_shape=[
            jax.ShapeDtypeStruct((t, N_IN_PAD), BF16),
            jax.ShapeDtypeStruct((S5_GROUPS, t // S5_T, S5_T * S5_GROUP_CH), BF16),
        ],
        scratch_shapes=[pltpu.VMEM((C_WIDTH // LANES, tm, LANES), F32)],
        compiler_params=pltpu.CompilerParams(
            dimension_semantics=("arbitrary",), vmem_limit_bytes=VMEM_LIMIT),
    )(x, mod_l, w_in_b)


def _sgu_kernel(uv_ref, g_ref, b_ref, w_ref, bias_ref, o_ref):
    tm = uv_ref.shape[0]
    u = jax.nn.gelu(uv_ref[:, :A_WIDTH].astype(F32))
    v = _layer_norm(jax.nn.gelu(uv_ref[:, A_WIDTH:].astype(F32)), g_ref[...], b_ref[...]).astype(BF16)
    head = lax.broadcasted_iota(jnp.int32, (1, A_WIDTH), 1) // A_HEAD_DIM
    for c in range(tm // CHUNK):
        rows = slice(c * CHUNK, (c + 1) * CHUNK)
        vc = v[rows]
        acc = bias_ref[...]
        for h in range(A_HEADS):
            r = jnp.dot(w_ref[h], vc, preferred_element_type=F32)
            acc = acc + jnp.where(head == h, r, 0.0)
        o_ref[rows, :] = (u[rows] * acc).astype(BF16)


def _spatial_gate(proj, ln_g, ln_b, w_s, b_s):
    t = proj.shape[0]
    tm = ROW_TILE
    bias = jnp.repeat(b_s.T, A_HEAD_DIM, axis=1)
    return pl.pallas_call(
        _sgu_kernel,
        grid=(t // tm,),
        in_specs=[
            pl.BlockSpec((tm, 2 * A_WIDTH), lambda i: (i, 0)),
            pl.BlockSpec((1, A_WIDTH), lambda i: (0, 0)),
            pl.BlockSpec((1, A_WIDTH), lambda i: (0, 0)),
            pl.BlockSpec((A_HEADS, CHUNK, CHUNK), lambda i: (0, 0, 0)),
            pl.BlockSpec((CHUNK, A_WIDTH), lambda i: (0, 0)),
        ],
        out_specs=pl.BlockSpec((tm, A_WIDTH), lambda i: (i, 0)),
        out_shape=jax.ShapeDtypeStruct((t, A_WIDTH), BF16),
        compiler_params=pltpu.CompilerParams(
            dimension_semantics=("arbitrary",), vmem_limit_bytes=VMEM_LIMIT),
    )(proj, ln_g.reshape(1, -1), ln_b.reshape(1, -1), w_s.astype(BF16), bias)


_NT = (((1,), (1,)), ((), ()))
_TN = (((0,), (0,)), ((), ()))


def _gla_direction(backward, q_ref, k_ref, v_ref, gl_ref, gup_ref, gb_ref, o_ref, s_ref):
    c_len = GLA_CHUNK
    n_rows = q_ref.shape[0]
    n_chunks = n_rows // c_len

    row = lax.broadcasted_iota(jnp.int32, (n_rows, n_rows), 0)
    col = lax.broadcasted_iota(jnp.int32, (n_rows, n_rows), 1)
    same = (row // c_len) == (col // c_len)
    if backward:
        tri = jnp.where(same & (col >= row), 1.0, 0.0).astype(BF16)
        keep = same & (col > row)
        i_last, i_mid = 0, c_len - 1 - c_len // 2
        lo = GATE_RANK
    else:
        tri = jnp.where(same & (col <= row), 1.0, 0.0).astype(BF16)
        keep = same & (col <= row)
        i_last, i_mid = c_len - 1, c_len // 2
        lo = 0

    gl = gl_ref[:, lo:lo + GATE_RANK]
    z = (jnp.dot(gl, gup_ref[0], preferred_element_type=F32) + jnp.dot(gl, gup_ref[1], preferred_element_type=F32)
         + gb_ref[...])
    la = jax.nn.log_sigmoid(z) / GATE_TAU
    l1 = la.astype(BF16)
    r1 = la - l1.astype(F32)
    l2 = r1.astype(BF16)
    l3 = (r1 - l2.astype(F32)).astype(BF16)
    b = (jnp.dot(tri, l1, preferred_element_type=F32) + jnp.dot(tri, l2, preferred_element_type=F32)
         + jnp.dot(tri, l3, preferred_element_type=F32))

    def per_chunk(index):
        return jnp.concatenate(
            [jnp.broadcast_to(b[c * c_len + index:c * c_len + index + 1], (c_len, B_QK))
             for c in range(n_chunks)], axis=0)

    b_last = per_chunk(i_last)
    b_mid = per_chunk(i_mid)
    q = q_ref[...].astype(F32) * (B_DK ** -0.5)
    k = k_ref[...].astype(F32)
    q_mid = (q * jnp.exp(b - b_mid)).astype(BF16)
    k_mid = (k * jnp.exp(b_mid - b)).astype(BF16)
    q_in = (q * jnp.exp(b)).astype(BF16)
    k_out = (k * jnp.exp(b_last - b)).astype(BF16)
    order = range(n_chunks - 1, -1, -1) if backward else range(n_chunks)
    v_all = v_ref[...].astype(BF16)
    intra = []
    for h in range(B_HEADS):
        hk = slice(h * B_DK, (h + 1) * B_DK)
        hv = slice(h * B_DV, (h + 1) * B_DV)
        sc = lax.dot_general(q_mid[:, hk], k_mid[:, hk], _NT, preferred_element_type=F32)
        sc = jnp.where(keep, sc, 0.0).astype(BF16)
        intra.append(jnp.dot(sc, v_all[:, hv], preferred_element_type=F32))
    o_intra = jnp.concatenate(intra, axis=1)
    own = (lax.broadcasted_iota(jnp.int32, (B_WIDTH, B_QK), 0) // B_DV
           == lax.broadcasted_iota(jnp.int32, (B_WIDTH, B_QK), 1) // B_DK)
    state = s_ref[...]
    for c in order:
        rows = slice(c * c_len, (c + 1) * c_len)
        o_ref[rows, :] = o_intra[rows] + lax.dot_general(q_in[rows], state.astype(BF16), _NT,
                                                         preferred_element_type=F32)
        decay = jnp.exp(b[c * c_len + i_last:c * c_len + i_last + 1])
        update = lax.dot_general(v_all[rows], k_out[rows], _TN, preferred_element_type=F32)
        state = jnp.where(own, state * decay + update, 0.0)
    s_ref[...] = state


def _gla_kernel(qf, kf, vf, glf, qb, kb, vb, glb, gup_ref, gb_ref, of_ref, ob_ref, s_ref):
    @pl.when(pl.program_id(1) == 0)
    def _():
        s_ref[...] = jnp.zeros_like(s_ref)

    _gla_direction(False, qf, kf, vf, glf, gup_ref.at[0], gb_ref.at[0], of_ref, s_ref.at[0])
    _gla_direction(True, qb, kb, vb, glb, gup_ref.at[1], gb_ref.at[1], ob_ref, s_ref.at[1])


def _gla_block(backward, bsz, nctx_blk, nlat_blk, b, j):
    if backward:
        ctx_i = b * nctx_blk + (nctx_blk - 1 - j)
        lat_i = bsz * nctx_blk + b * nlat_blk + (nlat_blk - 1 - (j - nctx_blk))
    else:
        ctx_i = b * nctx_blk + j
        lat_i = bsz * nctx_blk + b * nlat_blk + (j - nctx_blk)
    return jnp.where(j < nctx_blk, ctx_i, lat_i)


def _gla_sweep(proj, gate_up, gate_b, bsz, ctx_len, seq):
    t = proj.shape[0]
    r = SEQ_TILE
    nctx_blk, nlat_blk = ctx_len // r, seq // r
    up_hi = gate_up.astype(BF16)
    in_specs = []
    for backward in (False, True):
        blk = functools.partial(_gla_block, backward, bsz, nctx_blk, nlat_blk)
        in_specs += [
            pl.BlockSpec((r, B_QK), lambda b, j, blk=blk: (blk(b, j), 2)),
            pl.BlockSpec((r, B_QK), lambda b, j, blk=blk: (blk(b, j), 3)),
            pl.BlockSpec((r, B_WIDTH), lambda b, j, blk=blk: (blk(b, j), 2)),
            pl.BlockSpec((r, 128), lambda b, j, blk=blk: (blk(b, j), COL_GL // 128)),
        ]
    in_specs += [
        pl.BlockSpec((2, 2, GATE_RANK, B_QK), lambda b, j: (0, 0, 0, 0)),
        pl.BlockSpec((2, 1, B_QK), lambda b, j: (0, 0, 0)),
    ]
    fwd = functools.partial(_gla_block, False, bsz, nctx_blk, nlat_blk)
    bwd = functools.partial(_gla_block, True, bsz, nctx_blk, nlat_blk)
    return pl.pallas_call(
        _gla_kernel,
        grid=(bsz, nctx_blk + nlat_blk),
        in_specs=in_specs,
        out_specs=[
            pl.BlockSpec((r, B_WIDTH), lambda b, j: (fwd(b, j), 0)),
            pl.BlockSpec((r, B_WIDTH), lambda b, j: (bwd(b, j), 0)),
        ],
        out_shape=[jax.ShapeDtypeStruct((t, B_WIDTH), F32), jax.ShapeDtypeStruct((t, B_WIDTH), F32)],
        scratch_shapes=[pltpu.VMEM((2, B_WIDTH, B_QK), F32)],
        compiler_params=pltpu.CompilerParams(
            dimension_semantics=("arbitrary", "arbitrary"), vmem_limit_bytes=VMEM_LIMIT),
    )(*([proj] * 8), jnp.stack([up_hi, (gate_up - up_hi.astype(F32)).astype(BF16)], axis=1),
      gate_b.reshape(2, 1, -1))


def _s5_matrices(lam_re, lam_im, log_dt, b_re, b_im, c_re, c_im):
    tc = S5_T
    lam = lax.complex(lam_re.astype(F32), lam_im.astype(F32))
    dt = jnp.exp(log_dt.astype(F32))
    bm = lax.complex(b_re.astype(F32), b_im.astype(F32))
    cm = lax.complex(c_re.astype(F32), c_im.astype(F32))
    ldt = lam * dt[..., None]
    lam_bar = jnp.exp(ldt)
    b_bar = ((lam_bar - 1.0) / lam)[..., None] * bm
    steps = jnp.arange(tc + 1, dtype=F32)
    pw = jnp.exp(ldt[:, None] * steps[None, :, None, None])
    kern = jnp.real(jnp.einsum('dgip,dtgp,dgpj->dgtij', cm, pw[:, :tc], b_bar))
    s_i = jnp.arange(tc)[:, None]
    t_i = jnp.arange(tc)[None, :]
    g_n = lam.shape[1]
    hh = S5_GROUP_CH

    def toeplitz(kd, lag, ok):
        m = kd[:, jnp.clip(lag, 0, tc - 1)]
        m = jnp.where(ok[None, :, :, None, None], m, 0.0)
        return m.transpose(0, 1, 4, 2, 3).reshape(g_n, tc * hh, tc * hh)

    mt = toeplitz(kern[0], t_i - s_i, t_i >= s_i) + toeplitz(kern[1], s_i - t_i, s_i >= t_i)

    def state_in(d, powers):
        w = pw[d][powers][:, :, :, None] * b_bar[d][None]
        w = w.transpose(1, 0, 3, 2).reshape(g_n, tc * hh, S5_STATE)
        re, im = jnp.real(w), jnp.imag(w)
        return jnp.concatenate([re, im, im, re], axis=-1)

    qt = jnp.concatenate([state_in(0, tc - 1 - jnp.arange(tc)), state_in(1, jnp.arange(tc))], axis=-1)

    def state_out(d, powers):
        w = cm[d][:, None] * pw[d][powers].transpose(1, 0, 2)[:, :, None, :]
        w = w.reshape(g_n, tc * hh, S5_STATE).transpose(0, 2, 1)
        return jnp.concatenate([jnp.real(w), -jnp.imag(w)], axis=1)

    pt = jnp.concatenate([state_out(0, 1 + jnp.arange(tc)), state_out(1, tc - jnp.arange(tc))], axis=1)

    a = pw[:, tc]
    ar, ai = jnp.real(a), jnp.imag(a)
    a1 = jnp.concatenate([ar, ar], axis=-1).reshape(2, -1)
    a2 = jnp.concatenate([-ai, ai], axis=-1).reshape(2, -1)
    a3 = jnp.concatenate([ai, -ai], axis=-1).reshape(2, -1)
    ac = jnp.stack([a1, a2, a3], axis=1)
    return mt.astype(BF16), qt.astype(BF16), pt.astype(BF16), ac.reshape(2, 3, 1, -1)


def _s5_kernel(uc_ref, ul_ref, mt_ref, qt_ref, pt_ref, ac_ref, yc_ref, yl_ref, ef, esf, eb, esb):
    nq = uc_ref.shape[0]
    nc_ctx, nc_lat = uc_ref.shape[1], ul_ref.shape[1]
    n = nc_ctx + nc_lat
    w = 2 * S5_STATE
    for gi in range(nq):
        lanes = slice(gi * w, (gi + 1) * w)
        rc = jnp.dot(uc_ref[gi], qt_ref[gi], preferred_element_type=F32)
        rl = jnp.dot(ul_ref[gi], qt_ref[gi], preferred_element_type=F32)
        ef[0:nc_ctx, lanes] = rc[:, 0:w]
        esf[0:nc_ctx, lanes] = rc[:, w:2 * w]
        ef[nc_ctx:n, lanes] = rl[:, 0:w]
        esf[nc_ctx:n, lanes] = rl[:, w:2 * w]
        eb[0:nc_lat, lanes] = rl[:, 2 * w:3 * w]
        esb[0:nc_lat, lanes] = rl[:, 3 * w:4 * w]
        eb[nc_lat:n, lanes] = rc[:, 2 * w:3 * w]
        esb[nc_lat:n, lanes] = rc[:, 3 * w:4 * w]

    a1f, a2f, a3f = ac_ref[0, 0], ac_ref[0, 1], ac_ref[0, 2]
    a1b, a2b, a3b = ac_ref[1, 0], ac_ref[1, 1], ac_ref[1, 2]

    def body(i, carry):
        hf, hsf, hb, hsb = carry
        rf = pl.ds(i, 1)
        rb = pl.ds(n - 1 - i, 1)
        e_f, es_f = ef[rf, :], esf[rf, :]
        e_b, es_b = eb[rb, :], esb[rb, :]
        ef[rf, :] = hf
        eb[rb, :] = hb
        return (a1f * hf + a2f * hsf + e_f, a1f * hsf + a3f * hf + es_f,
                a1b * hb + a2b * hsb + e_b, a1b * hsb + a3b * hb + es_b)

    zero = jnp.zeros((1, nq * w), F32)
    lax.fori_loop(0, n, body, (zero, zero, zero, zero))

    for gi in range(nq):
        lanes = slice(gi * w, (gi + 1) * w)
        hc = jnp.concatenate([ef[0:nc_ctx, lanes], eb[nc_lat:n, lanes]], axis=1).astype(BF16)
        hl = jnp.concatenate([ef[nc_ctx:n, lanes], eb[0:nc_lat, lanes]], axis=1).astype(BF16)
        yc_ref[gi] = (jnp.dot(uc_ref[gi], mt_ref[gi], preferred_element_type=F32)
                      + jnp.dot(hc, pt_ref[gi], preferred_element_type=F32))
        yl_ref[gi] = (jnp.dot(ul_ref[gi], mt_ref[gi], preferred_element_type=F32)
                      + jnp.dot(hl, pt_ref[gi], preferred_element_type=F32))


def _s5_scan(u5, mats, bsz, ctx_len, seq):
    mt, qt, pt, ac = mats
    g_n, tc, hh = S5_GROUPS, S5_T, S5_GROUP_CH
    n_ctx_rows = bsz * ctx_len
    uc, ul = u5[:, :n_ctx_rows // tc], u5[:, n_ctx_rows // tc:]
    nc_ctx, nc_lat = ctx_len // tc, seq // tc
    nq = S5_QUARTER
    wq = nq * 2 * S5_STATE
    n = nc_ctx + nc_lat
    yc, yl = pl.pallas_call(
        _s5_kernel,
        grid=(bsz, g_n // nq),
        in_specs=[
            pl.BlockSpec((nq, nc_ctx, tc * hh), lambda b, qi: (qi, b, 0)),
            pl.BlockSpec((nq, nc_lat, tc * hh), lambda b, qi: (qi, b, 0)),
            pl.BlockSpec((nq, tc * hh, tc * hh), lambda b, qi: (qi, 0, 0)),
            pl.BlockSpec((nq, tc * hh, 8 * S5_STATE), lambda b, qi: (qi, 0, 0)),
            pl.BlockSpec((nq, 4 * S5_STATE, tc * hh), lambda b, qi: (qi, 0, 0)),
            pl.BlockSpec((2, 3, 1, wq), lambda b, qi: (0, 0, 0, qi)),
        ],
        out_specs=[
            pl.BlockSpec((nq, nc_ctx, tc * hh), lambda b, qi: (qi, b, 0)),
            pl.BlockSpec((nq, nc_lat, tc * hh), lambda b, qi: (qi, b, 0)),
        ],
        out_shape=[
            jax.ShapeDtypeStruct((g_n, bsz * nc_ctx, tc * hh), F32),
            jax.ShapeDtypeStruct((g_n, bsz * nc_lat, tc * hh), F32),
        ],
        scratch_shapes=[pltpu.VMEM((n, wq), F32) for _ in range(4)],
        compiler_params=pltpu.CompilerParams(
            dimension_semantics=("arbitrary", "arbitrary"), vmem_limit_bytes=VMEM_LIMIT),
    )(uc, ul, mt, qt, pt, ac)

    return jnp.concatenate([yc, yl], axis=1)


def _outproj_kernel(alpha, a_ref, of_ref, ob_ref, g_ref, ng_ref, y_ref, u_ref, x_ref, mod_ref, d_ref, gw_ref,
                    gb_ref, wo_ref, lng_ref, lnb_ref, rwh_ref, rwl_ref, rb_ref, x1_ref, h2_ref, lg_ref, y_scr):
    m = mod_ref[0]
    n_row = y_ref.shape[1]
    for step in range(S5_T):
        lanes = slice(step * S5_GROUP_CH, (step + 1) * S5_GROUP_CH)
        row = jnp.concatenate([y_ref[g][:, lanes] for g in range(S5_GROUPS)], axis=1)
        for half in range(C_WIDTH // LANES):
            y_scr[half, pl.ds(step, n_row, stride=S5_T), :] = row[:, half * LANES:(half + 1) * LANES]
    y_nat = jnp.concatenate([y_scr[half] for half in range(C_WIDTH // LANES)], axis=1)
    heads = []
    for h in range(B_HEADS):
        hv = slice(h * B_DV, (h + 1) * B_DV)
        o = of_ref[:, hv] + ob_ref[:, hv]
        heads.append(o * lax.rsqrt(jnp.mean(o * o, axis=-1, keepdims=True) + LN_EPS))
    gate = g_ref[...].astype(F32)
    gla = (jnp.concatenate(heads, axis=1) * ng_ref[...] * (gate * jax.nn.sigmoid(gate))).astype(BF16)
    y = jax.nn.gelu(y_nat + d_ref[...] * u_ref[...].astype(F32))
    s5 = y * jax.nn.sigmoid(jnp.dot(y.astype(BF16), gw_ref[...], preferred_element_type=F32) + gb_ref[...])
    mix = (jnp.dot(a_ref[...], wo_ref[0:A_WIDTH, :], preferred_element_type=F32)
           + jnp.dot(gla, wo_ref[A_WIDTH:A_WIDTH + B_WIDTH, :], preferred_element_type=F32)
           + jnp.dot(s5.astype(BF16), wo_ref[A_WIDTH + B_WIDTH:, :], preferred_element_type=F32))
    x1 = _layer_norm(alpha * x_ref[...] + m[2:3] * mix, lng_ref[...], lnb_ref[...])
    x1_ref[...] = x1
    h2 = x1 * (1.0 + m[4:5]) + m[3:4]
    _store_row_tiles(h2_ref, (), h2)
    h_hi = h2.astype(BF16)
    h_lo = (h2 - h_hi.astype(F32)).astype(BF16)
    lg_ref[...] = (jnp.dot(h_hi, rwh_ref[...], preferred_element_type=F32)
                   + jnp.dot(h_lo, rwh_ref[...], preferred_element_type=F32)
                   + jnp.dot(h_hi, rwl_ref[...], preferred_element_type=F32) + rb_ref[...])


def _out_projection(alpha, a_out, o_fwd, o_bwd, norm_g, y_s5, proj, x, mod_l, s5_d, glu_w, glu_b, w_out_b, ln_g,
                    ln_b, router_w, router_b, n_ctx_rows, seq):
    t = x.shape[0]
    tm = ROW_TILE
    rw = jnp.zeros((D_MODEL, 128), F32).at[:, :N_EXPERTS].set(router_w)
    rw_hi = rw.astype(BF16)
    rw_lo = (rw - rw_hi.astype(F32)).astype(BF16)
    rb = jnp.zeros((1, 128), F32).at[0, :N_EXPERTS].set(router_b)
    row = lambda i: (i, 0)
    fixed = lambda i: (0, 0)
    return pl.pallas_call(
        functools.partial(_outproj_kernel, alpha),
        grid=(t // tm,),
        in_specs=[
            pl.BlockSpec((tm, A_WIDTH), row),
            pl.BlockSpec((tm, B_WIDTH), row),
            pl.BlockSpec((tm, B_WIDTH), row),
            pl.BlockSpec((tm, B_WIDTH), lambda i: (i, 3)),
            pl.BlockSpec((1, B_WIDTH), fixed),
            pl.BlockSpec((S5_GROUPS, tm // S5_T, S5_T * S5_GROUP_CH), lambda i: (0, i, 0)),
            pl.BlockSpec((tm, C_WIDTH), lambda i: (i, COL_S5 // C_WIDTH)),
            pl.BlockSpec((tm, D_MODEL), row),
            pl.BlockSpec((1, 6, D_MODEL), lambda i: (_segment(i * tm, n_ctx_rows, seq), 0, 0)),
            pl.BlockSpec((1, C_WIDTH), fixed),
            pl.BlockSpec((C_WIDTH, C_WIDTH), fixed),
            pl.BlockSpec((1, C_WIDTH), fixed),
            pl.BlockSpec((D_MODEL, D_MODEL), fixed),
            pl.BlockSpec((1, D_MODEL), fixed),
            pl.BlockSpec((1, D_MODEL), fixed),
            pl.BlockSpec((D_MODEL, 128), fixed),
            pl.BlockSpec((D_MODEL, 128), fixed),
            pl.BlockSpec((1, 128), fixed),
        ],
        out_specs=[
            pl.BlockSpec((tm, D_MODEL), row),
            pl.BlockSpec((tm * ROW_SUB, LANES), row),
            pl.BlockSpec((tm, 128), row),
        ],
        out_shape=[
            jax.ShapeDtypeStruct((t, D_MODEL), F32),
            jax.ShapeDtypeStruct((t * ROW_SUB, LANES), F32),
            jax.ShapeDtypeStruct((t, 128), F32),
        ],
        scratch_shapes=[pltpu.VMEM((C_WIDTH // LANES, tm, LANES), F32)],
        compiler_params=pltpu.CompilerParams(
            dimension_semantics=("arbitrary",), vmem_limit_bytes=VMEM_LIMIT),
    )(a_out, o_fwd, o_bwd, proj, norm_g.reshape(1, -1), y_s5, proj, x, mod_l, s5_d.reshape(1, -1), glu_w.astype(BF16), glu_b.reshape(1, -1),
      w_out_b, ln_g.reshape(1, -1), ln_b.reshape(1, -1), rw_hi, rw_lo, rb)


def _route_kernel(lg_ref, gate_ref, pos_ref, tbase_ref, tcnt_ref, cnt_ref, base, before, below):
    tm = lg_ref.shape[0]

    @pl.when(pl.program_id(0) == 0)
    def _():
        base[...] = jnp.zeros_like(base)
        r = lax.broadcasted_iota(jnp.int32, (tm, tm), 0)
        c = lax.broadcasted_iota(jnp.int32, (tm, tm), 1)
        before[...] = jnp.where(r < c, 1.0, 0.0).astype(BF16)
        r = lax.broadcasted_iota(jnp.int32, (N_EXPERTS, N_EXPERTS), 0)
        c = lax.broadcasted_iota(jnp.int32, (N_EXPERTS, N_EXPERTS), 1)
        below[...] = jnp.where(c < r, 1.0, 0.0)

    logit = jnp.transpose(lg_ref[...])[:N_EXPERTS]
    eid = lax.broadcasted_iota(jnp.int32, (N_EXPERTS, tm), 0)
    vals, hots = [], []
    work = logit
    for kk in range(TOP_K):
        m = jnp.max(work, axis=0, keepdims=True)
        ix = jnp.min(jnp.where(work == m, eid, N_EXPERTS), axis=0, keepdims=True)
        hot = eid == ix
        vals.append(m)
        hots.append(hot)
        work = jnp.where(hot, -jnp.inf, work)
    ex = [jnp.exp(v - vals[0]) for v in vals]
    den = ex[0] + ex[1] + ex[2] + ex[3]
    member = jnp.zeros((N_EXPERTS, tm), F32)
    for kk in range(TOP_K):
        gate_ref[kk:kk + 1, :] = ex[kk] / den
        member = member + jnp.where(hots[kk], 1.0, 0.0)
    tile_cnt = jnp.broadcast_to(jnp.sum(member, axis=1, keepdims=True), (N_EXPERTS, LANES))
    group_off = jnp.dot(below[...], tile_cnt, precision=HIGHEST, preferred_element_type=F32)[:, 0:1]
    in_group = jnp.dot(member.astype(BF16), before[...], preferred_element_type=F32)
    for kk in range(TOP_K):
        pos_ref[kk:kk + 1, :] = jnp.sum(jnp.where(hots[kk], group_off + in_group, 0.0), axis=0,
                                        keepdims=True).astype(jnp.int32)
    tbase_ref[...] = jnp.broadcast_to(base[...], tbase_ref.shape)
    tcnt_ref[...] = tile_cnt
    total = base[...] + tile_cnt[:, 0:1]
    base[...] = total
    cnt_ref[...] = jnp.broadcast_to(total, cnt_ref.shape)


def _routing(logits, n_blocks):
    t = logits.shape[0]
    tm = ROW_TILE
    nt = t // tm
    gates, pos, tbase, tcnt, cnt = pl.pallas_call(
        _route_kernel,
        grid=(nt,),
        in_specs=[pl.BlockSpec((tm, LANES), lambda i: (i, 0))],
        out_specs=[
            pl.BlockSpec((TOP_K, tm), lambda i: (0, i)),
            pl.BlockSpec((TOP_K, tm), lambda i: (0, i)),
            pl.BlockSpec((N_EXPERTS, LANES), lambda i: (i, 0)),
            pl.BlockSpec((N_EXPERTS, LANES), lambda i: (i, 0)),
            pl.BlockSpec((N_EXPERTS, LANES), lambda i: (0, 0)),
        ],
        out_shape=[
            jax.ShapeDtypeStruct((TOP_K, t), F32),
            jax.ShapeDtypeStruct((TOP_K, t), jnp.int32),
            jax.ShapeDtypeStruct((nt * N_EXPERTS, LANES), F32),
            jax.ShapeDtypeStruct((nt * N_EXPERTS, LANES), F32),
            jax.ShapeDtypeStruct((N_EXPERTS, LANES), F32),
        ],
        scratch_shapes=[pltpu.VMEM((N_EXPERTS, 1), F32), pltpu.VMEM((tm, tm), BF16),
                        pltpu.VMEM((N_EXPERTS, N_EXPERTS), F32)],
        compiler_params=pltpu.CompilerParams(
            dimension_semantics=("arbitrary",), vmem_limit_bytes=VMEM_LIMIT),
    )(logits)
    counts = cnt[:, 0].astype(jnp.int32)
    padded = (counts + MOE_BLOCK - 1) // MOE_BLOCK * MOE_BLOCK
    padded_end = jnp.cumsum(padded)
    padded_start = (padded_end - padded).astype(jnp.int32)
    first_slot = jnp.arange(n_blocks, dtype=jnp.int32) * MOE_BLOCK
    block_expert = jnp.minimum(jnp.sum((padded_end[None, :] <= first_slot[:, None]).astype(jnp.int32), axis=1),
                               N_EXPERTS - 1).astype(jnp.int32)
    n_valid = (padded_end[-1] // MOE_BLOCK).astype(jnp.int32).reshape(1)
    pad_lo = (padded_start + counts).astype(jnp.int32)
    route = dict(
        gates=gates.T.reshape(nt, 1, tm * TOP_K),
        pos=pos.T.reshape(nt, 1, tm * TOP_K),
        tile_base=tbase[:, 0].astype(jnp.int32).reshape(nt, 1, N_EXPERTS),
        tile_cnt=tcnt[:, 0].astype(jnp.int32).reshape(nt, 1, N_EXPERTS),
        start=padded_start, pad_lo=pad_lo, pad_hi=padded_end.astype(jnp.int32))
    return route, block_expert, n_valid


def _expert_runs(tile_cnt_ref, tile_base_ref, start_ref, copy):
    off = 0
    for e in range(N_EXPERTS):
        n = tile_cnt_ref[0, 0, e]
        slot0 = start_ref[e] + tile_base_ref[0, 0, e]
        for bit in range(ROW_TILE.bit_length() - 1, -1, -1):
            size = 1 << bit
            done = (n >> (bit + 1)) << (bit + 1)

            @pl.when(((n >> bit) & 1) == 1)
            def _(off=off, done=done, slot0=slot0, size=size):
                copy(off + done, slot0 + done, size)
        off = off + n


def _rows(ref, lead, row0, n):
    return ref.at[lead + (pl.ds(pl.multiple_of(row0 * ROW_SUB, ROW_SUB), n * ROW_SUB), slice(None))]


def _dispatch_kernel(start_ref, lo_ref, hi_ref, pos_ref, tbase_ref, tcnt_ref, h_ref, o_ref, stage, zbuf, sem, zsem):
    i = pl.program_id(0)
    nt = pl.num_programs(0)
    tm = ROW_TILE
    slot = i % 2
    blk_rows = MOE_BLOCK * ROW_SUB
    n_blocks = o_ref.shape[0] // blk_rows

    def wait_stage(s):
        pltpu.make_async_copy(stage.at[s], stage.at[s], sem.at[s]).wait()

    @pl.when(i >= 2)
    def _():
        wait_stage(slot)

    def place(r, carry):
        row = h_ref[pl.ds(pl.multiple_of(r * ROW_SUB, ROW_SUB), ROW_SUB), :]
        for kk in range(TOP_K):
            p = pos_ref[0, 0, r * TOP_K + kk]
            stage[slot, pl.ds(pl.multiple_of(p * ROW_SUB, ROW_SUB), ROW_SUB), :] = row
        return carry
    lax.fori_loop(0, tm, place, 0, unroll=8)

    def copy(stage_row, slot_row, n):
        pltpu.make_async_copy(_rows(stage, (slot,), stage_row, n), _rows(o_ref, (), slot_row, n),
                              sem.at[slot]).start()
    _expert_runs(tcnt_ref, tbase_ref, start_ref, copy)

    @pl.when(i == nt - 1)
    def _():
        wait_stage(slot)

        @pl.when(i >= 1)
        def _():
            wait_stage(1 - slot)
        zbuf[...] = jnp.zeros_like(zbuf)
        zrow = zbuf.at[pl.ds(0, ROW_SUB), :]
        for e in range(N_EXPERTS):
            def fill(s, carry):
                pltpu.make_async_copy(zrow, _row_tile(o_ref, (), s), zsem).start()
                return carry
            lax.fori_loop(lo_ref[e], hi_ref[e], fill, 0)
        for e in range(N_EXPERTS):
            def drain(s, carry):
                pltpu.make_async_copy(zrow, zrow, zsem).wait()
                return carry
            lax.fori_loop(lo_ref[e], hi_ref[e], drain, 0)
        used = hi_ref[N_EXPERTS - 1] // MOE_BLOCK

        def fill_block(j, carry):
            rows = pl.ds(pl.multiple_of(j * blk_rows, blk_rows), blk_rows)
            pltpu.make_async_copy(zbuf, o_ref.at[rows, :], zsem).start()
            return carry
        lax.fori_loop(used, n_blocks, fill_block, 0)

        def drain_block(j, carry):
            pltpu.make_async_copy(zbuf, zbuf, zsem).wait()
            return carry
        lax.fori_loop(used, n_blocks, drain_block, 0)


def _moe_dispatch(h2t, route, n_blocks):
    nt = route['pos'].shape[0]
    tm = ROW_TILE
    smem = lambda width: pl.BlockSpec((1, 1, width), lambda i, *_: (i, 0, 0), memory_space=pltpu.SMEM)
    grid_spec = pltpu.PrefetchScalarGridSpec(
        num_scalar_prefetch=3,
        grid=(nt,),
        in_specs=[
            smem(tm * TOP_K), smem(N_EXPERTS), smem(N_EXPERTS),
            pl.BlockSpec((tm * ROW_SUB, LANES), lambda i, *_: (i, 0)),
        ],
        out_specs=pl.BlockSpec(memory_space=pl.ANY),
        scratch_shapes=[
            pltpu.VMEM((2, tm * TOP_K * ROW_SUB, LANES), F32),
            pltpu.VMEM((MOE_BLOCK * ROW_SUB, LANES), F32),
            pltpu.SemaphoreType.DMA((2,)),
            pltpu.SemaphoreType.DMA(()),
        ],
    )
    return pl.pallas_call(
        _dispatch_kernel,
        grid_spec=grid_spec,
        out_shape=jax.ShapeDtypeStruct((n_blocks * MOE_BLOCK * ROW_SUB, LANES), F32),
        compiler_params=pltpu.CompilerParams(
            dimension_semantics=("arbitrary",), vmem_limit_bytes=VMEM_LIMIT),
    )(route['start'], route['pad_lo'], route['pad_hi'], route['pos'], route['tile_base'], route['tile_cnt'], h2t)


def _moe_kernel(be_ref, nv_ref, x_ref, wu_ref, bu_ref, wd_ref, bd_ref, o_ref, wu_b, wd_b):
    i = pl.program_id(0)
    n_valid = nv_ref[0]

    @pl.when(i < n_valid)
    def _():
        first = jnp.logical_or(i == 0, be_ref[i] != be_ref[jnp.maximum(i - 1, 0)])

        @pl.when(first)
        def _():
            rows = 64

            def cast(r, carry):
                rs = pl.ds(pl.multiple_of(r * rows, rows), rows)
                wu_b[rs, :] = wu_ref[0, 0, rs, :].astype(BF16)
                wd_b[rs, :] = wd_ref[0, 0, rs, :].astype(BF16)
                return carry
            lax.fori_loop(0, D_MODEL // rows, cast, 0)

        x = _load_row_tiles(x_ref, (), MOE_BLOCK).astype(BF16)
        acc = jnp.zeros((MOE_BLOCK, D_MODEL), F32) + bd_ref[0, 0]
        cw = 512
        for jc in range(D_EXPERT // cw):
            cg = slice(jc * cw, (jc + 1) * cw)
            cl = slice(D_EXPERT + jc * cw, D_EXPERT + (jc + 1) * cw)
            ug = jnp.dot(x, wu_b[:, cg], preferred_element_type=F32) + bu_ref[0, 0, :, cg]
            ul = jnp.dot(x, wu_b[:, cl], preferred_element_type=F32) + bu_ref[0, 0, :, cl]
            xg = jnp.minimum(ug, SWIGLU_LIMIT)
            xl = jnp.clip(ul, -SWIGLU_LIMIT, SWIGLU_LIMIT)
            act = xg * jax.nn.sigmoid(SWIGLU_ALPHA * xg) * (xl + 1.0)
            acc = acc + jnp.dot(act.astype(BF16), wd_b[cg, :], preferred_element_type=F32)
        _store_row_tiles(o_ref, (), acc)

    @pl.when(i >= n_valid)
    def _():
        o_ref[...] = jnp.zeros_like(o_ref)


def _moe_experts(layer, xs, block_expert, n_valid, w_up, b_up, w_down, b_down):
    n_blocks = block_expert.shape[0]
    depth = w_up.shape[0]
    grid_spec = pltpu.PrefetchScalarGridSpec(
        num_scalar_prefetch=2,
        grid=(n_blocks,),
        in_specs=[
            pl.BlockSpec((MOE_BLOCK * ROW_SUB, LANES),
                         lambda i, be, nv: (jnp.minimum(i, jnp.maximum(nv[0] - 1, 0)), 0)),
            pl.BlockSpec((1, 1, D_MODEL, 2 * D_EXPERT), lambda i, be, nv: (layer, be[i], 0, 0)),
            pl.BlockSpec((1, 1, 1, 2 * D_EXPERT), lambda i, be, nv: (layer, be[i], 0, 0)),
            pl.BlockSpec((1, 1, D_EXPERT, D_MODEL), lambda i, be, nv: (layer, be[i], 0, 0)),
            pl.BlockSpec((1, 1, 1, D_MODEL), lambda i, be, nv: (layer, be[i], 0, 0)),
        ],
        out_specs=pl.BlockSpec((MOE_BLOCK * ROW_SUB, LANES), lambda i, be, nv: (i, 0)),
        scratch_shapes=[
            pltpu.VMEM((D_MODEL, 2 * D_EXPERT), BF16),
            pltpu.VMEM((D_EXPERT, D_MODEL), BF16),
        ],
    )
    return pl.pallas_call(
        _moe_kernel,
        grid_spec=grid_spec,
        out_shape=jax.ShapeDtypeStruct((n_blocks * MOE_BLOCK * ROW_SUB, LANES), F32),
        compiler_params=pltpu.CompilerParams(
            dimension_semantics=("arbitrary",), vmem_limit_bytes=VMEM_LIMIT),
    )(block_expert, n_valid, xs, w_up, b_up.reshape(depth, N_EXPERTS, 1, -1), w_down,
      b_down.reshape(depth, N_EXPERTS, 1, -1))


def _combine_kernel(alpha, start_ref, pos_ref, gate_ref, tbase_ref, tcnt_ref, tbase_n_ref, tcnt_n_ref, y_ref, x_ref,
                    mod_ref, lng_ref, lnb_ref, o_ref, stage, frow, sem):
    i = pl.program_id(0)
    nt = pl.num_programs(0)
    tm = ROW_TILE
    slot = i % 2

    def fetch(cnt_ref, base_ref, s):
        def copy(stage_row, slot_row, n):
            pltpu.make_async_copy(_rows(y_ref, (), slot_row, n), _rows(stage, (s,), stage_row, n),
                                  sem.at[s]).start()
        _expert_runs(cnt_ref, base_ref, start_ref, copy)

    @pl.when(i == 0)
    def _():
        fetch(tcnt_ref, tbase_ref, 0)

    @pl.when(i + 1 < nt)
    def _():
        fetch(tcnt_n_ref, tbase_n_ref, 1 - slot)

    pltpu.make_async_copy(stage.at[slot], stage.at[slot], sem.at[slot]).wait()

    def mix(r, carry):
        acc = None
        for kk in range(TOP_K):
            p = pos_ref[0, 0, r * TOP_K + kk]
            term = gate_ref[0, 0, r * TOP_K + kk] * stage[slot, pl.ds(pl.multiple_of(p * ROW_SUB, ROW_SUB), ROW_SUB), :]
            acc = term if acc is None else acc + term
        frow[pl.ds(pl.multiple_of(r * ROW_SUB, ROW_SUB), ROW_SUB), :] = acc
        return carry
    lax.fori_loop(0, tm, mix, 0, unroll=8)

    m = mod_ref[0]
    f = _load_row_tiles(frow, (), tm)
    o_ref[...] = _layer_norm(alpha * x_ref[...] + m[5:6] * f, lng_ref[...], lnb_ref[...])


def _moe_combine(alpha, route, ys, x1, mod_l, ln_g, ln_b, n_ctx_rows, seq, drop_ctx):
    t = x1.shape[0]
    tm = ROW_TILE
    nt = t // tm
    skip = n_ctx_rows // tm if drop_ctx else 0
    cur = lambda width: pl.BlockSpec((1, 1, width), lambda i, *_: (i, 0, 0), memory_space=pltpu.SMEM)
    nxt = lambda width: pl.BlockSpec((1, 1, width), lambda i, *_: (jnp.minimum(i + 1, nt - 1), 0, 0),
                                     memory_space=pltpu.SMEM)
    grid_spec = pltpu.PrefetchScalarGridSpec(
        num_scalar_prefetch=1,
        grid=(nt,),
        in_specs=[
            cur(tm * TOP_K), cur(tm * TOP_K), cur(N_EXPERTS), cur(N_EXPERTS), nxt(N_EXPERTS), nxt(N_EXPERTS),
            pl.BlockSpec(memory_space=pl.ANY),
            pl.BlockSpec((tm, D_MODEL), lambda i, *_: (i, 0)),
            pl.BlockSpec((1, 6, D_MODEL), lambda i, *_: (_segment(i * tm, n_ctx_rows, seq), 0, 0)),
            pl.BlockSpec((1, D_MODEL), lambda i, *_: (0, 0)),
            pl.BlockSpec((1, D_MODEL), lambda i, *_: (0, 0)),
        ],
        out_specs=pl.BlockSpec((tm, D_MODEL), lambda i, *_: (jnp.maximum(i - skip, 0), 0)),
        scratch_shapes=[
            pltpu.VMEM((2, tm * TOP_K * ROW_SUB, LANES), F32),
            pltpu.VMEM((tm * ROW_SUB, LANES), F32),
            pltpu.SemaphoreType.DMA((2,)),
        ],
    )
    return pl.pallas_call(
        functools.partial(_combine_kernel, alpha),
        grid_spec=grid_spec,
        out_shape=jax.ShapeDtypeStruct((t - skip * tm, D_MODEL), F32),
        compiler_params=pltpu.CompilerParams(
            dimension_semantics=("arbitrary",), vmem_limit_bytes=VMEM_LIMIT),
    )(route['start'], route['pos'], route['gates'], route['tile_base'], route['tile_cnt'], route['tile_base'],
      route['tile_cnt'], ys, x1, mod_l, ln_g.reshape(1, -1), ln_b.reshape(1, -1))


def kernel(x, c, ctx, c_ctx, w_mod, b_mod, w_in, sgu_ln_g, sgu_ln_b, sgu_w, sgu_b, gla_gate_up, gla_gate_b,
           gla_norm_g, s5_lam_re, s5_lam_im, s5_log_dt, s5_b_re, s5_b_im, s5_c_re, s5_c_im, s5_d, s5_glu_w,
           s5_glu_b, w_out, ln_g, ln_b, router_w, router_b, w_up, b_up, w_down, b_down):
    bsz, seq, d = x.shape
    ctx_len = ctx.shape[1]
    depth = w_in.shape[0]
    alpha = float((2 * depth) ** 0.25)
    n_ctx_rows = bsz * ctx_len
    t = n_ctx_rows + bsz * seq
    assert d == D_MODEL and bsz + 1 <= 8
    assert n_ctx_rows % ROW_TILE == 0 and seq % ROW_TILE == 0
    assert ctx_len % SEQ_TILE == 0 and seq % SEQ_TILE == 0

    xa = jnp.concatenate([ctx.reshape(n_ctx_rows, d), x.reshape(bsz * seq, d)], axis=0)
    cvec = jnp.zeros((8, d), F32).at[0].set(c_ctx).at[1:1 + bsz].set(c)
    mod = _modulation(cvec, w_mod, b_mod).reshape(depth, 8, 6, d)

    o = np.cumsum((0, A_WIDTH, A_WIDTH, B_QK, B_QK, B_WIDTH, B_WIDTH, 2 * GATE_RANK, C_WIDTH))
    w_in_r = jnp.concatenate(
        [w_in[:, :, o[0]:o[6]], w_in[:, :, o[7]:o[8]], w_in[:, :, o[6]:o[7]],
         jnp.zeros((depth, d, N_IN_PAD - int(o[8])), w_in.dtype)], axis=-1).astype(BF16)
    w_out_b = w_out.astype(BF16)

    n_assign = t * TOP_K
    n_blocks = -(-(n_assign + N_EXPERTS * (MOE_BLOCK - 1)) // MOE_BLOCK)

    for l in range(depth):
        mod_l = mod[l]
        proj, u5 = _in_projection(xa, mod_l, w_in_r[l], n_ctx_rows, seq)
        a_out = _spatial_gate(proj, sgu_ln_g[l], sgu_ln_b[l], sgu_w[l], sgu_b[l])
        o_fwd, o_bwd = _gla_sweep(proj, gla_gate_up[l], gla_gate_b[l], bsz, ctx_len, seq)
        mats = _s5_matrices(s5_lam_re[l], s5_lam_im[l], s5_log_dt[l], s5_b_re[l], s5_b_im[l],
                            s5_c_re[l], s5_c_im[l])
        y_s5 = _s5_scan(u5, mats, bsz, ctx_len, seq)
        x1, h2, logits = _out_projection(alpha, a_out, o_fwd, o_bwd, gla_norm_g[l], y_s5, proj, xa, mod_l, s5_d[l], s5_glu_w[l],
                                         s5_glu_b[l], w_out_b[l], ln_g[l, 0], ln_b[l, 0], router_w[l],
                                         router_b[l], n_ctx_rows, seq)
        route, block_expert, n_valid = _routing(logits, n_blocks)
        xs = _moe_dispatch(h2, route, n_blocks)
        ys = _moe_experts(l, xs, block_expert, n_valid, w_up, b_up, w_down, b_down)
        xa = _moe_combine(alpha, route, ys, x1, mod_l, ln_g[l, 1], ln_b[l, 1], n_ctx_rows, seq,
                          drop_ctx=(l == depth - 1))
    return xa.reshape(bsz, seq, d)
```

```python
import functools

import numpy as np
import jax
import jax.numpy as jnp
from jax import lax
from jax.experimental import pallas as pl
from jax.experimental.pallas import tpu as pltpu

F32 = jnp.float32
BF16 = jnp.bfloat16
HIGHEST = lax.Precision.HIGHEST

D_MODEL = 1024
CHUNK = 128
A_HEADS = 4
A_HEAD_DIM = 64
A_WIDTH = 256
B_HEADS = 4
B_DK = 64
B_DV = 128
B_QK = 256
B_WIDTH = 512
GATE_RANK = 16
GATE_TAU = 16.0
GLA_CHUNK = 64
S5_GROUPS = 16
S5_GROUP_CH = 16
S5_STATE = 64
C_WIDTH = 256
N_EXPERTS = 32
TOP_K = 4
D_EXPERT = 1024
SWIGLU_LIMIT = 7.0
SWIGLU_ALPHA = 1.702
LN_EPS = 1e-5

N_IN_PAD = 2432
COL_GL = 2304
COL_S5 = 2048

ROW_TILE = 512
SEQ_TILE = 256
S5_T = 16
S5_QUARTER = 4
MOE_BLOCK = 512
VMEM_LIMIT = 56 * 1024 * 1024


def _layer_norm(x, g, b):
    mu = jnp.mean(x, axis=-1, keepdims=True)
    xc = x - mu
    var = jnp.mean(xc * xc, axis=-1, keepdims=True)
    return xc * lax.rsqrt(var + LN_EPS) * g + b


LANES = 128
ROW_SUB = D_MODEL // LANES


def _store_row_tiles(ref, lead, val):
    n = val.shape[0]
    for j in range(ROW_SUB):
        ref[lead + (pl.ds(j, n, stride=ROW_SUB), slice(None))] = val[:, j * LANES:(j + 1) * LANES]


def _load_row_tiles(ref, lead, n):
    return jnp.concatenate(
        [ref[lead + (pl.ds(j, n, stride=ROW_SUB), slice(None))] for j in range(ROW_SUB)], axis=1)


def _row_tile(ref, lead, r):
    return ref.at[lead + (pl.ds(pl.multiple_of(r * ROW_SUB, ROW_SUB), ROW_SUB), slice(None))]


def _segment(row0, n_ctx_rows, seq):
    return jnp.where(row0 < n_ctx_rows, 0, 1 + (row0 - n_ctx_rows) // seq)


def _mod_kernel(c_ref, w_ref, b_ref, o_ref):
    c = c_ref[...]
    s = c * jax.nn.sigmoid(c)
    o_ref[0] = jnp.dot(s, w_ref[0], precision=HIGHEST, preferred_element_type=F32) + b_ref[0]


def _modulation(cvec, w_mod, b_mod):
    depth = w_mod.shape[0]
    n6 = w_mod.shape[2]
    tn = 1024
    return pl.pallas_call(
        _mod_kernel,
        grid=(depth, n6 // tn),
        in_specs=[
            pl.BlockSpec((8, D_MODEL), lambda l, j: (0, 0)),
            pl.BlockSpec((1, D_MODEL, tn), lambda l, j: (l, 0, j)),
            pl.BlockSpec((1, 1, tn), lambda l, j: (l, 0, j)),
        ],
        out_specs=pl.BlockSpec((1, 8, tn), lambda l, j: (l, 0, j)),
        out_shape=jax.ShapeDtypeStruct((depth, 8, n6), F32),
        compiler_params=pltpu.CompilerParams(
            dimension_semantics=("arbitrary", "arbitrary"), vmem_limit_bytes=VMEM_LIMIT),
    )(cvec, w_mod, b_mod.reshape(depth, 1, n6))


def _inproj_kernel(x_ref, mod_ref, w_ref, o_ref, u5_ref, s5_scr):
    m = mod_ref[0]
    h = x_ref[...] * (1.0 + m[1:2]) + m[0:1]
    res = jnp.dot(h.astype(BF16), w_ref[...], preferred_element_type=F32)
    o_ref[...] = res.astype(o_ref.dtype)
    n_row = x_ref.shape[0] // S5_T
    per_tile = LANES // S5_GROUP_CH
    for half in range(C_WIDTH // LANES):
        s5_scr[half] = res[:, COL_S5 + half * LANES:COL_S5 + (half + 1) * LANES]
    at_step = [[s5_scr[half, pl.ds(step, n_row, stride=S5_T), :] for half in range(C_WIDTH // LANES)]
               for step in range(S5_T)]
    for g in range(S5_GROUPS):
        lanes = slice((g % per_tile) * S5_GROUP_CH, (g % per_tile + 1) * S5_GROUP_CH)
        u5_ref[g] = jnp.concatenate(
            [at_step[step][g // per_tile][:, lanes] for step in range(S5_T)], axis=1).astype(BF16)


def _in_projection(x, mod_l, w_in_b, n_ctx_rows, seq):
    t = x.shape[0]
    tm = ROW_TILE
    return pl.pallas_call(
        _inproj_kernel,
        grid=(t // tm,),
        in_specs=[
            pl.BlockSpec((tm, D_MODEL), lambda i: (i, 0)),
            pl.BlockSpec((1, 6, D_MODEL), lambda i: (_segment(i * tm, n_ctx_rows, seq), 0, 0)),
            pl.BlockSpec((D_MODEL, N_IN_PAD), lambda i: (0, 0)),
        ],
        out_specs=[
            pl.BlockSpec((tm, N_IN_PAD), lambda i: (i, 0)),
            pl.BlockSpec((S5_GROUPS, tm // S5_T, S5_T * S5_GROUP_CH), lambda i: (0, i, 0)),
        ],
        out_shape=[
            jax.ShapeDtypeStruct((t, N_IN_PAD), BF16),
            jax.ShapeDtypeStruct((S5_GROUPS, t // S5_T, S5_T * S5_GROUP_CH), BF16),
        ],
        scratch_shapes=[pltpu.VMEM((C_WIDTH // LANES, tm, LANES), F32)],
        compiler_params=pltpu.CompilerParams(
            dimension_semantics=("arbitrary",), vmem_limit_bytes=VMEM_LIMIT),
    )(x, mod_l, w_in_b)


def _sgu_kernel(uv_ref, g_ref, b_ref, w_ref, bias_ref, o_ref):
    tm = uv_ref.shape[0]
    u = jax.nn.gelu(uv_ref[:, :A_WIDTH].astype(F32))
    v = _layer_norm(jax.nn.gelu(uv_ref[:, A_WIDTH:].astype(F32)), g_ref[...], b_ref[...]).astype(BF16)
    head = lax.broadcasted_iota(jnp.int32, (1, A_WIDTH), 1) // A_HEAD_DIM
    for c in range(tm // CHUNK):
        rows = slice(c * CHUNK, (c + 1) * CHUNK)
        vc = v[rows]
        acc = bias_ref[...]
        for h in range(A_HEADS):
            r = jnp.dot(w_ref[h], vc, preferred_element_type=F32)
            acc = acc + jnp.where(head == h, r, 0.0)
        o_ref[rows, :] = (u[rows] * acc).astype(BF16)


def _spatial_gate(proj, ln_g, ln_b, w_s, b_s):
    t = proj.shape[0]
    tm = ROW_TILE
    bias = jnp.repeat(b_s.T, A_HEAD_DIM, axis=1)
    return pl.pallas_call(
        _sgu_kernel,
        grid=(t // tm,),
        in_specs=[
            pl.BlockSpec((tm, 2 * A_WIDTH), lambda i: (i, 0)),
            pl.BlockSpec((1, A_WIDTH), lambda i: (0, 0)),
            pl.BlockSpec((1, A_WIDTH), lambda i: (0, 0)),
            pl.BlockSpec((A_HEADS, CHUNK, CHUNK), lambda i: (0, 0, 0)),
            pl.BlockSpec((CHUNK, A_WIDTH), lambda i: (0, 0)),
        ],
        out_specs=pl.BlockSpec((tm, A_WIDTH), lambda i: (i, 0)),
        out_shape=jax.ShapeDtypeStruct((t, A_WIDTH), BF16),
        compiler_params=pltpu.CompilerParams(
            dimension_semantics=("arbitrary",), vmem_limit_bytes=VMEM_LIMIT),
    )(proj, ln_g.reshape(1, -1), ln_b.reshape(1, -1), w_s.astype(BF16), bias)


_NT = (((1,), (1,)), ((), ()))
_TN = (((0,), (0,)), ((), ()))


def _gla_direction(backward, q_ref, k_ref, v_ref, gl_ref, gup_ref, gb_ref, o_ref, s_ref):
    c_len = GLA_CHUNK
    n_rows = q_ref.shape[0]
    n_chunks = n_rows // c_len

    row = lax.broadcasted_iota(jnp.int32, (n_rows, n_rows), 0)
    col = lax.broadcasted_iota(jnp.int32, (n_rows, n_rows), 1)
    same = (row // c_len) == (col // c_len)
    if backward:
        tri = jnp.where(same & (col >= row), 1.0, 0.0).astype(BF16)
        keep = same & (col > row)
        i_last, i_mid = 0, c_len - 1 - c_len // 2
        lo = GATE_RANK
    else:
        tri = jnp.where(same & (col <= row), 1.0, 0.0).astype(BF16)
        keep = same & (col <= row)
        i_last, i_mid = c_len - 1, c_len // 2
        lo = 0

    gl = gl_ref[:, lo:lo + GATE_RANK]
    z = (jnp.dot(gl, gup_ref[0], preferred_element_type=F32) + jnp.dot(gl, gup_ref[1], preferred_element_type=F32)
         + gb_ref[...])
    la = jax.nn.log_sigmoid(z) / GATE_TAU
    l1 = la.astype(BF16)
    r1 = la - l1.astype(F32)
    l2 = r1.astype(BF16)
    l3 = (r1 - l2.astype(F32)).astype(BF16)
    b = (jnp.dot(tri, l1, preferred_element_type=F32) + jnp.dot(tri, l2, preferred_element_type=F32)
         + jnp.dot(tri, l3, preferred_element_type=F32))

    def per_chunk(index):
        return jnp.concatenate(
            [jnp.broadcast_to(b[c * c_len + index:c * c_len + index + 1], (c_len, B_QK))
             for c in range(n_chunks)], axis=0)

    b_last = per_chunk(i_last)
    b_mid = per_chunk(i_mid)
    q = q_ref[...].astype(F32) * (B_DK ** -0.5)
    k = k_ref[...].astype(F32)
    q_mid = (q * jnp.exp(b - b_mid)).astype(BF16)
    k_mid = (k * jnp.exp(b_mid - b)).astype(BF16)
    q_in = (q * jnp.exp(b)).astype(BF16)
    k_out = (k * jnp.exp(b_last - b)).astype(BF16)
    order = range(n_chunks - 1, -1, -1) if backward else range(n_chunks)
    v_all = v_ref[...].astype(BF16)
    intra = []
    for h in range(B_HEADS):
        hk = slice(h * B_DK, (h + 1) * B_DK)
        hv = slice(h * B_DV, (h + 1) * B_DV)
        sc = lax.dot_general(q_mid[:, hk], k_mid[:, hk], _NT, preferred_element_type=F32)
        sc = jnp.where(keep, sc, 0.0).astype(BF16)
        intra.append(jnp.dot(sc, v_all[:, hv], preferred_element_type=F32))
    o_intra = jnp.concatenate(intra, axis=1)
    own = (lax.broadcasted_iota(jnp.int32, (B_WIDTH, B_QK), 0) // B_DV
           == lax.broadcasted_iota(jnp.int32, (B_WIDTH, B_QK), 1) // B_DK)
    state = s_ref[...]
    for c in order:
        rows = slice(c * c_len, (c + 1) * c_len)
        o_ref[rows, :] = o_intra[rows] + lax.dot_general(q_in[rows], state.astype(BF16), _NT,
                                                         preferred_element_type=F32)
        decay = jnp.exp(b[c * c_len + i_last:c * c_len + i_last + 1])
        update = lax.dot_general(v_all[rows], k_out[rows], _TN, preferred_element_type=F32)
        state = jnp.where(own, state * decay + update, 0.0)
    s_ref[...] = state


def _gla_kernel(qf, kf, vf, glf, qb, kb, vb, glb, gup_ref, gb_ref, of_ref, ob_ref, s_ref):
    @pl.when(pl.program_id(1) == 0)
    def _():
        s_ref[...] = jnp.zeros_like(s_ref)

    _gla_direction(False, qf, kf, vf, glf, gup_ref.at[0], gb_ref.at[0], of_ref, s_ref.at[0])
    _gla_direction(True, qb, kb, vb, glb, gup_ref.at[1], gb_ref.at[1], ob_ref, s_ref.at[1])


def _gla_block(backward, bsz, nctx_blk, nlat_blk, b, j):
    if backward:
        ctx_i = b * nctx_blk + (nctx_blk - 1 - j)
        lat_i = bsz * nctx_blk + b * nlat_blk + (nlat_blk - 1 - (j - nctx_blk))
    else:
        ctx_i = b * nctx_blk + j
        lat_i = bsz * nctx_blk + b * nlat_blk + (j - nctx_blk)
    return jnp.where(j < nctx_blk, ctx_i, lat_i)


def _gla_sweep(proj, gate_up, gate_b, bsz, ctx_len, seq):
    t = proj.shape[0]
    r = SEQ_TILE
    nctx_blk, nlat_blk = ctx_len // r, seq // r
    up_hi = gate_up.astype(BF16)
    in_specs = []
    for backward in (False, True):
        blk = functools.partial(_gla_block, backward, bsz, nctx_blk, nlat_blk)
        in_specs += [
            pl.BlockSpec((r, B_QK), lambda b, j, blk=blk: (blk(b, j), 2)),
            pl.BlockSpec((r, B_QK), lambda b, j, blk=blk: (blk(b, j), 3)),
            pl.BlockSpec((r, B_WIDTH), lambda b, j, blk=blk: (blk(b, j), 2)),
            pl.BlockSpec((r, 128), lambda b, j, blk=blk: (blk(b, j), COL_GL // 128)),
        ]
    in_specs += [
        pl.BlockSpec((2, 2, GATE_RANK, B_QK), lambda b, j: (0, 0, 0, 0)),
        pl.BlockSpec((2, 1, B_QK), lambda b, j: (0, 0, 0)),
    ]
    fwd = functools.partial(_gla_block, False, bsz, nctx_blk, nlat_blk)
    bwd = functools.partial(_gla_block, True, bsz, nctx_blk, nlat_blk)
    return pl.pallas_call(
        _gla_kernel,
        grid=(bsz, nctx_blk + nlat_blk),
        in_specs=in_specs,
        out_specs=[
            pl.BlockSpec((r, B_WIDTH), lambda b, j: (fwd(b, j), 0)),
            pl.BlockSpec((r, B_WIDTH), lambda b, j: (bwd(b, j), 0)),
        ],
        out_shape=[jax.ShapeDtypeStruct((t, B_WIDTH), F32), jax.ShapeDtypeStruct((t, B_WIDTH), F32)],
        scratch_shapes=[pltpu.VMEM((2, B_WIDTH, B_QK), F32)],
        compiler_params=pltpu.CompilerParams(
            dimension_semantics=("arbitrary", "arbitrary"), vmem_limit_bytes=VMEM_LIMIT),
    )(*([proj] * 8), jnp.stack([up_hi, (gate_up - up_hi.astype(F32)).astype(BF16)], axis=1),
      gate_b.reshape(2, 1, -1))


def _s5_matrices(lam_re, lam_im, log_dt, b_re, b_im, c_re, c_im):
    tc, hh, width = S5_T, S5_GROUP_CH, S5_T * S5_GROUP_CH
    lam = lax.complex(lam_re.astype(F32), lam_im.astype(F32))
    dt = jnp.exp(log_dt.astype(F32))
    bm = lax.complex(b_re.astype(F32), b_im.astype(F32))
    cm = lax.complex(c_re.astype(F32), c_im.astype(F32))
    ldt = lam * dt[..., None]
    lam_bar = jnp.exp(ldt)
    b_bar = ((lam_bar - 1.0) / lam)[..., None] * bm
    steps = jnp.arange(tc + 1, dtype=F32)
    pw = jnp.exp(ldt[:, :, None] * steps[None, None, :, None, None])
    n_l, g_n = lam.shape[0], lam.shape[2]

    kern = jnp.real(jnp.einsum('ldgip,ldtgp,ldgpj->ldgtji', cm, pw[:, :, :tc], b_bar))
    kern = jnp.tile(kern, (1, 1, 1, 1, 1, tc))
    col_t = jnp.arange(width) // hh
    s_i = jnp.arange(tc)[:, None]
    tau = jnp.arange(tc)[:, None, None]
    lag_f = (col_t[None, None, :] - s_i[None] == tau).astype(F32)
    lag_b = (s_i[None] - col_t[None, None, :] == tau).astype(F32)
    lag = jnp.stack([lag_f, lag_b])
    mt = jnp.sum(lag[None, :, None, :, :, None, :] * kern[:, :, :, :, None, :, :], axis=(1, 3))
    mt = mt.reshape(n_l, g_n, width, width)

    b_t = jnp.swapaxes(b_bar, -1, -2)
    pw_in = jnp.stack([pw[:, 0, :tc][:, ::-1], pw[:, 1, :tc]], axis=1)
    w = jnp.swapaxes(pw_in, 2, 3)[:, :, :, :, None, :] * b_t[:, :, :, None, :, :]
    w = w.reshape(n_l, 2, g_n, width, S5_STATE)
    re, im = jnp.real(w), jnp.imag(w)
    qt = jnp.concatenate([re[:, 0], im[:, 0], im[:, 0], re[:, 0], re[:, 1], im[:, 1], im[:, 1], re[:, 1]], axis=-1)

    pw_out = jnp.stack([pw[:, 0, 1:], pw[:, 1, 1:][:, ::-1]], axis=1)
    pw_out = jnp.repeat(jnp.transpose(pw_out, (0, 1, 3, 4, 2)), hh, axis=-1)
    c_t = jnp.tile(jnp.swapaxes(cm, -1, -2), (1, 1, 1, 1, tc))
    w = c_t * pw_out
    pt = jnp.concatenate([jnp.real(w[:, 0]), -jnp.imag(w[:, 0]), jnp.real(w[:, 1]), -jnp.imag(w[:, 1])], axis=2)

    a = pw[:, :, tc]
    ar, ai = jnp.real(a), jnp.imag(a)
    a1 = jnp.concatenate([ar, ar], axis=-1).reshape(n_l, 2, -1)
    a2 = jnp.concatenate([-ai, ai], axis=-1).reshape(n_l, 2, -1)
    a3 = jnp.concatenate([ai, -ai], axis=-1).reshape(n_l, 2, -1)
    ac = jnp.stack([a1, a2, a3], axis=2)
    return mt.astype(BF16), qt.astype(BF16), pt.astype(BF16), ac.reshape(n_l, 2, 3, 1, -1)


def _s5_kernel(uc_ref, ul_ref, mt_ref, qt_ref, pt_ref, ac_ref, yc_ref, yl_ref, ef, esf, eb, esb):
    nq = uc_ref.shape[0]
    nc_ctx, nc_lat = uc_ref.shape[1], ul_ref.shape[1]
    n = nc_ctx + nc_lat
    w = 2 * S5_STATE
    for gi in range(nq):
        lanes = slice(gi * w, (gi + 1) * w)
        rc = jnp.dot(uc_ref[gi], qt_ref[0, gi], preferred_element_type=F32)
        rl = jnp.dot(ul_ref[gi], qt_ref[0, gi], preferred_element_type=F32)
        ef[0:nc_ctx, lanes] = rc[:, 0:w]
        esf[0:nc_ctx, lanes] = rc[:, w:2 * w]
        ef[nc_ctx:n, lanes] = rl[:, 0:w]
        esf[nc_ctx:n, lanes] = rl[:, w:2 * w]
        eb[0:nc_lat, lanes] = rl[:, 2 * w:3 * w]
        esb[0:nc_lat, lanes] = rl[:, 3 * w:4 * w]
        eb[nc_lat:n, lanes] = rc[:, 2 * w:3 * w]
        esb[nc_lat:n, lanes] = rc[:, 3 * w:4 * w]

    a1f, a2f, a3f = ac_ref[0, 0, 0], ac_ref[0, 0, 1], ac_ref[0, 0, 2]
    a1b, a2b, a3b = ac_ref[0, 1, 0], ac_ref[0, 1, 1], ac_ref[0, 1, 2]

    def body(i, carry):
        hf, hsf, hb, hsb = carry
        rf = pl.ds(i, 1)
        rb = pl.ds(n - 1 - i, 1)
        e_f, es_f = ef[rf, :], esf[rf, :]
        e_b, es_b = eb[rb, :], esb[rb, :]
        ef[rf, :] = hf
        eb[rb, :] = hb
        return (a1f * hf + a2f * hsf + e_f, a1f * hsf + a3f * hf + es_f,
                a1b * hb + a2b * hsb + e_b, a1b * hsb + a3b * hb + es_b)

    zero = jnp.zeros((1, nq * w), F32)
    lax.fori_loop(0, n, body, (zero, zero, zero, zero))

    for gi in range(nq):
        lanes = slice(gi * w, (gi + 1) * w)
        hc = jnp.concatenate([ef[0:nc_ctx, lanes], eb[nc_lat:n, lanes]], axis=1).astype(BF16)
        hl = jnp.concatenate([ef[nc_ctx:n, lanes], eb[0:nc_lat, lanes]], axis=1).astype(BF16)
        yc_ref[gi] = (jnp.dot(uc_ref[gi], mt_ref[0, gi], preferred_element_type=F32)
                      + jnp.dot(hc, pt_ref[0, gi], preferred_element_type=F32))
        yl_ref[gi] = (jnp.dot(ul_ref[gi], mt_ref[0, gi], preferred_element_type=F32)
                      + jnp.dot(hl, pt_ref[0, gi], preferred_element_type=F32))


def _s5_scan(u5, mats, layer, bsz, ctx_len, seq):
    mt, qt, pt, ac = mats
    g_n, tc, hh = S5_GROUPS, S5_T, S5_GROUP_CH
    n_ctx_rows = bsz * ctx_len
    uc, ul = u5[:, :n_ctx_rows // tc], u5[:, n_ctx_rows // tc:]
    nc_ctx, nc_lat = ctx_len // tc, seq // tc
    nq = S5_QUARTER
    wq = nq * 2 * S5_STATE
    n = nc_ctx + nc_lat
    yc, yl = pl.pallas_call(
        _s5_kernel,
        grid=(bsz, g_n // nq),
        in_specs=[
            pl.BlockSpec((nq, nc_ctx, tc * hh), lambda b, qi: (qi, b, 0)),
            pl.BlockSpec((nq, nc_lat, tc * hh), lambda b, qi: (qi, b, 0)),
            pl.BlockSpec((1, nq, tc * hh, tc * hh), lambda b, qi: (layer, qi, 0, 0)),
            pl.BlockSpec((1, nq, tc * hh, 8 * S5_STATE), lambda b, qi: (layer, qi, 0, 0)),
            pl.BlockSpec((1, nq, 4 * S5_STATE, tc * hh), lambda b, qi: (layer, qi, 0, 0)),
            pl.BlockSpec((1, 2, 3, 1, wq), lambda b, qi: (layer, 0, 0, 0, qi)),
        ],
        out_specs=[
            pl.BlockSpec((nq, nc_ctx, tc * hh), lambda b, qi: (qi, b, 0)),
            pl.BlockSpec((nq, nc_lat, tc * hh), lambda b, qi: (qi, b, 0)),
        ],
        out_shape=[
            jax.ShapeDtypeStruct((g_n, bsz * nc_ctx, tc * hh), F32),
            jax.ShapeDtypeStruct((g_n, bsz * nc_lat, tc * hh), F32),
        ],
        scratch_shapes=[pltpu.VMEM((n, wq), F32) for _ in range(4)],
        compiler_params=pltpu.CompilerParams(
            dimension_semantics=("arbitrary", "arbitrary"), vmem_limit_bytes=VMEM_LIMIT),
    )(uc, ul, mt, qt, pt, ac)

    return jnp.concatenate([yc, yl], axis=1)


def _outproj_kernel(alpha, a_ref, of_ref, ob_ref, g_ref, ng_ref, y_ref, u_ref, x_ref, mod_ref, d_ref, gw_ref,
                    gb_ref, wo_ref, lng_ref, lnb_ref, rwh_ref, rwl_ref, rb_ref, x1_ref, h2_ref, lg_ref, y_scr):
    m = mod_ref[0]
    n_row = y_ref.shape[1]
    for step in range(S5_T):
        lanes = slice(step * S5_GROUP_CH, (step + 1) * S5_GROUP_CH)
        row = jnp.concatenate([y_ref[g][:, lanes] for g in range(S5_GROUPS)], axis=1)
        for half in range(C_WIDTH // LANES):
            y_scr[half, pl.ds(step, n_row, stride=S5_T), :] = row[:, half * LANES:(half + 1) * LANES]
    y_nat = jnp.concatenate([y_scr[half] for half in range(C_WIDTH // LANES)], axis=1)
    heads = []
    for h in range(B_HEADS):
        hv = slice(h * B_DV, (h + 1) * B_DV)
        o = of_ref[:, hv] + ob_ref[:, hv]
        heads.append(o * lax.rsqrt(jnp.mean(o * o, axis=-1, keepdims=True) + LN_EPS))
    gate = g_ref[...].astype(F32)
    gla = (jnp.concatenate(heads, axis=1) * ng_ref[...] * (gate * jax.nn.sigmoid(gate))).astype(BF16)
    y = jax.nn.gelu(y_nat + d_ref[...] * u_ref[...].astype(F32))
    s5 = y * jax.nn.sigmoid(jnp.dot(y.astype(BF16), gw_ref[...], preferred_element_type=F32) + gb_ref[...])
    mix = (jnp.dot(a_ref[...], wo_ref[0:A_WIDTH, :], preferred_element_type=F32)
           + jnp.dot(gla, wo_ref[A_WIDTH:A_WIDTH + B_WIDTH, :], preferred_element_type=F32)
           + jnp.dot(s5.astype(BF16), wo_ref[A_WIDTH + B_WIDTH:, :], preferred_element_type=F32))
    x1 = _layer_norm(alpha * x_ref[...] + m[2:3] * mix, lng_ref[...], lnb_ref[...])
    x1_ref[...] = x1
    h2 = x1 * (1.0 + m[4:5]) + m[3:4]
    _store_row_tiles(h2_ref, (), h2)
    h_hi = h2.astype(BF16)
    h_lo = (h2 - h_hi.astype(F32)).astype(BF16)
    lg_ref[...] = (jnp.dot(h_hi, rwh_ref[...], preferred_element_type=F32)
                   + jnp.dot(h_lo, rwh_ref[...], preferred_element_type=F32)
                   + jnp.dot(h_hi, rwl_ref[...], preferred_element_type=F32) + rb_ref[...])


def _out_projection(alpha, a_out, o_fwd, o_bwd, norm_g, y_s5, proj, x, mod_l, s5_d, glu_w, glu_b, w_out_b, ln_g,
                    ln_b, rw_hi, rw_lo, rb, n_ctx_rows, seq):
    t = x.shape[0]
    tm = ROW_TILE
    row = lambda i: (i, 0)
    fixed = lambda i: (0, 0)
    return pl.pallas_call(
        functools.partial(_outproj_kernel, alpha),
        grid=(t // tm,),
        in_specs=[
            pl.BlockSpec((tm, A_WIDTH), row),
            pl.BlockSpec((tm, B_WIDTH), row),
            pl.BlockSpec((tm, B_WIDTH), row),
            pl.BlockSpec((tm, B_WIDTH), lambda i: (i, 3)),
            pl.BlockSpec((1, B_WIDTH), fixed),
            pl.BlockSpec((S5_GROUPS, tm // S5_T, S5_T * S5_GROUP_CH), lambda i: (0, i, 0)),
            pl.BlockSpec((tm, C_WIDTH), lambda i: (i, COL_S5 // C_WIDTH)),
            pl.BlockSpec((tm, D_MODEL), row),
            pl.BlockSpec((1, 6, D_MODEL), lambda i: (_segment(i * tm, n_ctx_rows, seq), 0, 0)),
            pl.BlockSpec((1, C_WIDTH), fixed),
            pl.BlockSpec((C_WIDTH, C_WIDTH), fixed),
            pl.BlockSpec((1, C_WIDTH), fixed),
            pl.BlockSpec((D_MODEL, D_MODEL), fixed),
            pl.BlockSpec((1, D_MODEL), fixed),
            pl.BlockSpec((1, D_MODEL), fixed),
            pl.BlockSpec((D_MODEL, 128), fixed),
            pl.BlockSpec((D_MODEL, 128), fixed),
            pl.BlockSpec((1, 128), fixed),
        ],
        out_specs=[
            pl.BlockSpec((tm, D_MODEL), row),
            pl.BlockSpec((tm * ROW_SUB, LANES), row),
            pl.BlockSpec((tm, 128), row),
        ],
        out_shape=[
            jax.ShapeDtypeStruct((t, D_MODEL), F32),
            jax.ShapeDtypeStruct((t * ROW_SUB, LANES), F32),
            jax.ShapeDtypeStruct((t, 128), F32),
        ],
        scratch_shapes=[pltpu.VMEM((C_WIDTH // LANES, tm, LANES), F32)],
        compiler_params=pltpu.CompilerParams(
            dimension_semantics=("arbitrary",), vmem_limit_bytes=VMEM_LIMIT),
    )(a_out, o_fwd, o_bwd, proj, norm_g.reshape(1, -1), y_s5, proj, x, mod_l, s5_d.reshape(1, -1), glu_w.astype(BF16), glu_b.reshape(1, -1),
      w_out_b, ln_g.reshape(1, -1), ln_b.reshape(1, -1), rw_hi, rw_lo, rb)


def _route_kernel(lg_ref, gate_ref, pos_ref, tbase_ref, tcnt_ref, cnt_ref, base, before, below):
    tm = lg_ref.shape[0]

    @pl.when(pl.program_id(0) == 0)
    def _():
        base[...] = jnp.zeros_like(base)
        r = lax.broadcasted_iota(jnp.int32, (tm, tm), 0)
        c = lax.broadcasted_iota(jnp.int32, (tm, tm), 1)
        before[...] = jnp.where(r < c, 1.0, 0.0).astype(BF16)
        r = lax.broadcasted_iota(jnp.int32, (N_EXPERTS, N_EXPERTS), 0)
        c = lax.broadcasted_iota(jnp.int32, (N_EXPERTS, N_EXPERTS), 1)
        below[...] = jnp.where(c < r, 1.0, 0.0)

    logit = jnp.transpose(lg_ref[...])[:N_EXPERTS]
    eid = lax.broadcasted_iota(jnp.int32, (N_EXPERTS, tm), 0)
    vals, hots = [], []
    work = logit
    for kk in range(TOP_K):
        m = jnp.max(work, axis=0, keepdims=True)
        ix = jnp.min(jnp.where(work == m, eid, N_EXPERTS), axis=0, keepdims=True)
        hot = eid == ix
        vals.append(m)
        hots.append(hot)
        work = jnp.where(hot, -jnp.inf, work)
    ex = [jnp.exp(v - vals[0]) for v in vals]
    den = ex[0] + ex[1] + ex[2] + ex[3]
    member = jnp.zeros((N_EXPERTS, tm), F32)
    for kk in range(TOP_K):
        gate_ref[kk:kk + 1, :] = ex[kk] / den
        member = member + jnp.where(hots[kk], 1.0, 0.0)
    tile_cnt = jnp.broadcast_to(jnp.sum(member, axis=1, keepdims=True), (N_EXPERTS, LANES))
    group_off = jnp.dot(below[...], tile_cnt, precision=HIGHEST, preferred_element_type=F32)[:, 0:1]
    in_group = jnp.dot(member.astype(BF16), before[...], preferred_element_type=F32)
    for kk in range(TOP_K):
        pos_ref[kk:kk + 1, :] = jnp.sum(jnp.where(hots[kk], group_off + in_group, 0.0), axis=0,
                                        keepdims=True).astype(jnp.int32)
    tbase_ref[...] = jnp.broadcast_to(base[...], tbase_ref.shape)
    tcnt_ref[...] = tile_cnt
    total = base[...] + tile_cnt[:, 0:1]
    base[...] = total
    cnt_ref[...] = jnp.broadcast_to(total, cnt_ref.shape)


def _routing(logits, n_blocks):
    t = logits.shape[0]
    tm = ROW_TILE
    nt = t // tm
    gates, pos, tbase, tcnt, cnt = pl.pallas_call(
        _route_kernel,
        grid=(nt,),
        in_specs=[pl.BlockSpec((tm, LANES), lambda i: (i, 0))],
        out_specs=[
            pl.BlockSpec((TOP_K, tm), lambda i: (0, i)),
            pl.BlockSpec((TOP_K, tm), lambda i: (0, i)),
            pl.BlockSpec((N_EXPERTS, LANES), lambda i: (i, 0)),
            pl.BlockSpec((N_EXPERTS, LANES), lambda i: (i, 0)),
            pl.BlockSpec((N_EXPERTS, LANES), lambda i: (0, 0)),
        ],
        out_shape=[
            jax.ShapeDtypeStruct((TOP_K, t), F32),
            jax.ShapeDtypeStruct((TOP_K, t), jnp.int32),
            jax.ShapeDtypeStruct((nt * N_EXPERTS, LANES), F32),
            jax.ShapeDtypeStruct((nt * N_EXPERTS, LANES), F32),
            jax.ShapeDtypeStruct((N_EXPERTS, LANES), F32),
        ],
        scratch_shapes=[pltpu.VMEM((N_EXPERTS, 1), F32), pltpu.VMEM((tm, tm), BF16),
                        pltpu.VMEM((N_EXPERTS, N_EXPERTS), F32)],
        compiler_params=pltpu.CompilerParams(
            dimension_semantics=("arbitrary",), vmem_limit_bytes=VMEM_LIMIT),
    )(logits)
    counts = cnt[:, 0].astype(jnp.int32)
    padded = (counts + MOE_BLOCK - 1) // MOE_BLOCK * MOE_BLOCK
    padded_end = jnp.cumsum(padded)
    padded_start = (padded_end - padded).astype(jnp.int32)
    first_slot = jnp.arange(n_blocks, dtype=jnp.int32) * MOE_BLOCK
    block_expert = jnp.minimum(jnp.sum((padded_end[None, :] <= first_slot[:, None]).astype(jnp.int32), axis=1),
                               N_EXPERTS - 1).astype(jnp.int32)
    n_valid = (padded_end[-1] // MOE_BLOCK).astype(jnp.int32).reshape(1)
    pad_lo = (padded_start + counts).astype(jnp.int32)
    route = dict(
        gates=gates.T.reshape(nt, 1, tm * TOP_K),
        pos=pos.T.reshape(nt, 1, tm * TOP_K),
        tile_base=tbase[:, 0].astype(jnp.int32).reshape(nt, 1, N_EXPERTS),
        tile_cnt=tcnt[:, 0].astype(jnp.int32).reshape(nt, 1, N_EXPERTS),
        start=padded_start, pad_lo=pad_lo, pad_hi=padded_end.astype(jnp.int32))
    return route, block_expert, n_valid


def _expert_runs(tile_cnt_ref, tile_base_ref, start_ref, copy):
    off = 0
    for e in range(N_EXPERTS):
        n = tile_cnt_ref[0, 0, e]
        slot0 = start_ref[e] + tile_base_ref[0, 0, e]
        for bit in range(ROW_TILE.bit_length() - 1, -1, -1):
            size = 1 << bit
            done = (n >> (bit + 1)) << (bit + 1)

            @pl.when(((n >> bit) & 1) == 1)
            def _(off=off, done=done, slot0=slot0, size=size):
                copy(off + done, slot0 + done, size)
        off = off + n


def _rows(ref, lead, row0, n):
    return ref.at[lead + (pl.ds(pl.multiple_of(row0 * ROW_SUB, ROW_SUB), n * ROW_SUB), slice(None))]


def _dispatch_kernel(start_ref, lo_ref, hi_ref, pos_ref, tbase_ref, tcnt_ref, h_ref, o_ref, stage, zbuf, sem, zsem):
    i = pl.program_id(0)
    nt = pl.num_programs(0)
    tm = ROW_TILE
    slot = i % 2
    blk_rows = MOE_BLOCK * ROW_SUB
    n_blocks = o_ref.shape[0] // blk_rows

    def wait_stage(s):
        pltpu.make_async_copy(stage.at[s], stage.at[s], sem.at[s]).wait()

    @pl.when(i >= 2)
    def _():
        wait_stage(slot)

    def place(r, carry):
        row = h_ref[pl.ds(pl.multiple_of(r * ROW_SUB, ROW_SUB), ROW_SUB), :]
        for kk in range(TOP_K):
            p = pos_ref[0, 0, r * TOP_K + kk]
            stage[slot, pl.ds(pl.multiple_of(p * ROW_SUB, ROW_SUB), ROW_SUB), :] = row
        return carry
    lax.fori_loop(0, tm, place, 0, unroll=8)

    def copy(stage_row, slot_row, n):
        pltpu.make_async_copy(_rows(stage, (slot,), stage_row, n), _rows(o_ref, (), slot_row, n),
                              sem.at[slot]).start()
    _expert_runs(tcnt_ref, tbase_ref, start_ref, copy)

    @pl.when(i == nt - 1)
    def _():
        wait_stage(slot)

        @pl.when(i >= 1)
        def _():
            wait_stage(1 - slot)
        zbuf[...] = jnp.zeros_like(zbuf)
        zrow = zbuf.at[pl.ds(0, ROW_SUB), :]
        for e in range(N_EXPERTS):
            def fill(s, carry):
                pltpu.make_async_copy(zrow, _row_tile(o_ref, (), s), zsem).start()
                return carry
            lax.fori_loop(lo_ref[e], hi_ref[e], fill, 0)
        for e in range(N_EXPERTS):
            def drain(s, carry):
                pltpu.make_async_copy(zrow, zrow, zsem).wait()
                return carry
            lax.fori_loop(lo_ref[e], hi_ref[e], drain, 0)
        used = hi_ref[N_EXPERTS - 1] // MOE_BLOCK

        def fill_block(j, carry):
            rows = pl.ds(pl.multiple_of(j * blk_rows, blk_rows), blk_rows)
            pltpu.make_async_copy(zbuf, o_ref.at[rows, :], zsem).start()
            return carry
        lax.fori_loop(used, n_blocks, fill_block, 0)

        def drain_block(j, carry):
            pltpu.make_async_copy(zbuf, zbuf, zsem).wait()
            return carry
        lax.fori_loop(used, n_blocks, drain_block, 0)


def _moe_dispatch(h2t, route, n_blocks):
    nt = route['pos'].shape[0]
    tm = ROW_TILE
    smem = lambda width: pl.BlockSpec((1, 1, width), lambda i, *_: (i, 0, 0), memory_space=pltpu.SMEM)
    grid_spec = pltpu.PrefetchScalarGridSpec(
        num_scalar_prefetch=3,
        grid=(nt,),
        in_specs=[
            smem(tm * TOP_K), smem(N_EXPERTS), smem(N_EXPERTS),
            pl.BlockSpec((tm * ROW_SUB, LANES), lambda i, *_: (i, 0)),
        ],
        out_specs=pl.BlockSpec(memory_space=pl.ANY),
        scratch_shapes=[
            pltpu.VMEM((2, tm * TOP_K * ROW_SUB, LANES), F32),
            pltpu.VMEM((MOE_BLOCK * ROW_SUB, LANES), F32),
            pltpu.SemaphoreType.DMA((2,)),
            pltpu.SemaphoreType.DMA(()),
        ],
    )
    return pl.pallas_call(
        _dispatch_kernel,
        grid_spec=grid_spec,
        out_shape=jax.ShapeDtypeStruct((n_blocks * MOE_BLOCK * ROW_SUB, LANES), F32),
        compiler_params=pltpu.CompilerParams(
            dimension_semantics=("arbitrary",), vmem_limit_bytes=VMEM_LIMIT),
    )(route['start'], route['pad_lo'], route['pad_hi'], route['pos'], route['tile_base'], route['tile_cnt'], h2t)


def _moe_kernel(be_ref, nv_ref, x_ref, wu_ref, bu_ref, wd_ref, bd_ref, o_ref, wu_b, wd_b):
    i = pl.program_id(0)
    n_valid = nv_ref[0]

    @pl.when(i < n_valid)
    def _():
        first = jnp.logical_or(i == 0, be_ref[i] != be_ref[jnp.maximum(i - 1, 0)])

        @pl.when(first)
        def _():
            rows = 64

            def cast(r, carry):
                rs = pl.ds(pl.multiple_of(r * rows, rows), rows)
                wu_b[rs, :] = wu_ref[0, 0, rs, :].astype(BF16)
                wd_b[rs, :] = wd_ref[0, 0, rs, :].astype(BF16)
                return carry
            lax.fori_loop(0, D_MODEL // rows, cast, 0)

        x = _load_row_tiles(x_ref, (), MOE_BLOCK).astype(BF16)
        acc = jnp.zeros((MOE_BLOCK, D_MODEL), F32) + bd_ref[0, 0]
        cw = 512
        for jc in range(D_EXPERT // cw):
            cg = slice(jc * cw, (jc + 1) * cw)
            cl = slice(D_EXPERT + jc * cw, D_EXPERT + (jc + 1) * cw)
            ug = jnp.dot(x, wu_b[:, cg], preferred_element_type=F32) + bu_ref[0, 0, :, cg]
            ul = jnp.dot(x, wu_b[:, cl], preferred_element_type=F32) + bu_ref[0, 0, :, cl]
            xg = jnp.minimum(ug, SWIGLU_LIMIT)
            xl = jnp.clip(ul, -SWIGLU_LIMIT, SWIGLU_LIMIT)
            act = xg * jax.nn.sigmoid(SWIGLU_ALPHA * xg) * (xl + 1.0)
            acc = acc + jnp.dot(act.astype(BF16), wd_b[cg, :], preferred_element_type=F32)
        _store_row_tiles(o_ref, (), acc)

    @pl.when(i >= n_valid)
    def _():
        o_ref[...] = jnp.zeros_like(o_ref)


def _moe_experts(layer, xs, block_expert, n_valid, w_up, b_up, w_down, b_down):
    n_blocks = block_expert.shape[0]
    depth = w_up.shape[0]
    grid_spec = pltpu.PrefetchScalarGridSpec(
        num_scalar_prefetch=2,
        grid=(n_blocks,),
        in_specs=[
            pl.BlockSpec((MOE_BLOCK * ROW_SUB, LANES),
                         lambda i, be, nv: (jnp.minimum(i, jnp.maximum(nv[0] - 1, 0)), 0)),
            pl.BlockSpec((1, 1, D_MODEL, 2 * D_EXPERT), lambda i, be, nv: (layer, be[i], 0, 0)),
            pl.BlockSpec((1, 1, 1, 2 * D_EXPERT), lambda i, be, nv: (layer, be[i], 0, 0)),
            pl.BlockSpec((1, 1, D_EXPERT, D_MODEL), lambda i, be, nv: (layer, be[i], 0, 0)),
            pl.BlockSpec((1, 1, 1, D_MODEL), lambda i, be, nv: (layer, be[i], 0, 0)),
        ],
        out_specs=pl.BlockSpec((MOE_BLOCK * ROW_SUB, LANES), lambda i, be, nv: (i, 0)),
        scratch_shapes=[
            pltpu.VMEM((D_MODEL, 2 * D_EXPERT), BF16),
            pltpu.VMEM((D_EXPERT, D_MODEL), BF16),
        ],
    )
    return pl.pallas_call(
        _moe_kernel,
        grid_spec=grid_spec,
        out_shape=jax.ShapeDtypeStruct((n_blocks * MOE_BLOCK * ROW_SUB, LANES), F32),
        compiler_params=pltpu.CompilerParams(
            dimension_semantics=("arbitrary",), vmem_limit_bytes=VMEM_LIMIT),
    )(block_expert, n_valid, xs, w_up, b_up.reshape(depth, N_EXPERTS, 1, -1), w_down,
      b_down.reshape(depth, N_EXPERTS, 1, -1))


def _combine_kernel(alpha, start_ref, pos_ref, gate_ref, tbase_ref, tcnt_ref, tbase_n_ref, tcnt_n_ref, y_ref, x_ref,
                    mod_ref, lng_ref, lnb_ref, o_ref, stage, frow, sem):
    i = pl.program_id(0)
    nt = pl.num_programs(0)
    tm = ROW_TILE
    slot = i % 2

    def fetch(cnt_ref, base_ref, s):
        def copy(stage_row, slot_row, n):
            pltpu.make_async_copy(_rows(y_ref, (), slot_row, n), _rows(stage, (s,), stage_row, n),
                                  sem.at[s]).start()
        _expert_runs(cnt_ref, base_ref, start_ref, copy)

    @pl.when(i == 0)
    def _():
        fetch(tcnt_ref, tbase_ref, 0)

    @pl.when(i + 1 < nt)
    def _():
        fetch(tcnt_n_ref, tbase_n_ref, 1 - slot)

    pltpu.make_async_copy(stage.at[slot], stage.at[slot], sem.at[slot]).wait()

    def mix(r, carry):
        acc = None
        for kk in range(TOP_K):
            p = pos_ref[0, 0, r * TOP_K + kk]
            term = gate_ref[0, 0, r * TOP_K + kk] * stage[slot, pl.ds(pl.multiple_of(p * ROW_SUB, ROW_SUB), ROW_SUB), :]
            acc = term if acc is None else acc + term
        frow[pl.ds(pl.multiple_of(r * ROW_SUB, ROW_SUB), ROW_SUB), :] = acc
        return carry
    lax.fori_loop(0, tm, mix, 0, unroll=8)

    m = mod_ref[0]
    f = _load_row_tiles(frow, (), tm)
    o_ref[...] = _layer_norm(alpha * x_ref[...] + m[5:6] * f, lng_ref[...], lnb_ref[...])


def _moe_combine(alpha, route, ys, x1, mod_l, ln_g, ln_b, n_ctx_rows, seq, drop_ctx):
    t = x1.shape[0]
    tm = ROW_TILE
    nt = t // tm
    skip = n_ctx_rows // tm if drop_ctx else 0
    cur = lambda width: pl.BlockSpec((1, 1, width), lambda i, *_: (i, 0, 0), memory_space=pltpu.SMEM)
    nxt = lambda width: pl.BlockSpec((1, 1, width), lambda i, *_: (jnp.minimum(i + 1, nt - 1), 0, 0),
                                     memory_space=pltpu.SMEM)
    grid_spec = pltpu.PrefetchScalarGridSpec(
        num_scalar_prefetch=1,
        grid=(nt,),
        in_specs=[
            cur(tm * TOP_K), cur(tm * TOP_K), cur(N_EXPERTS), cur(N_EXPERTS), nxt(N_EXPERTS), nxt(N_EXPERTS),
            pl.BlockSpec(memory_space=pl.ANY),
            pl.BlockSpec((tm, D_MODEL), lambda i, *_: (i, 0)),
            pl.BlockSpec((1, 6, D_MODEL), lambda i, *_: (_segment(i * tm, n_ctx_rows, seq), 0, 0)),
            pl.BlockSpec((1, D_MODEL), lambda i, *_: (0, 0)),
            pl.BlockSpec((1, D_MODEL), lambda i, *_: (0, 0)),
        ],
        out_specs=pl.BlockSpec((tm, D_MODEL), lambda i, *_: (jnp.maximum(i - skip, 0), 0)),
        scratch_shapes=[
            pltpu.VMEM((2, tm * TOP_K * ROW_SUB, LANES), F32),
            pltpu.VMEM((tm * ROW_SUB, LANES), F32),
            pltpu.SemaphoreType.DMA((2,)),
        ],
    )
    return pl.pallas_call(
        functools.partial(_combine_kernel, alpha),
        grid_spec=grid_spec,
        out_shape=jax.ShapeDtypeStruct((t - skip * tm, D_MODEL), F32),
        compiler_params=pltpu.CompilerParams(
            dimension_semantics=("arbitrary",), vmem_limit_bytes=VMEM_LIMIT),
    )(route['start'], route['pos'], route['gates'], route['tile_base'], route['tile_cnt'], route['tile_base'],
      route['tile_cnt'], ys, x1, mod_l, ln_g.reshape(1, -1), ln_b.reshape(1, -1))


def kernel(x, c, ctx, c_ctx, w_mod, b_mod, w_in, sgu_ln_g, sgu_ln_b, sgu_w, sgu_b, gla_gate_up, gla_gate_b,
           gla_norm_g, s5_lam_re, s5_lam_im, s5_log_dt, s5_b_re, s5_b_im, s5_c_re, s5_c_im, s5_d, s5_glu_w,
           s5_glu_b, w_out, ln_g, ln_b, router_w, router_b, w_up, b_up, w_down, b_down):
    bsz, seq, d = x.shape
    ctx_len = ctx.shape[1]
    depth = w_in.shape[0]
    alpha = float((2 * depth) ** 0.25)
    n_ctx_rows = bsz * ctx_len
    t = n_ctx_rows + bsz * seq
    assert d == D_MODEL and bsz + 1 <= 8
    assert n_ctx_rows % ROW_TILE == 0 and seq % ROW_TILE == 0
    assert ctx_len % SEQ_TILE == 0 and seq % SEQ_TILE == 0

    xa = jnp.concatenate([ctx.reshape(n_ctx_rows, d), x.reshape(bsz * seq, d)], axis=0)
    cvec = jnp.zeros((8, d), F32).at[0].set(c_ctx).at[1:1 + bsz].set(c)
    mod = _modulation(cvec, w_mod, b_mod).reshape(depth, 8, 6, d)

    o = np.cumsum((0, A_WIDTH, A_WIDTH, B_QK, B_QK, B_WIDTH, B_WIDTH, 2 * GATE_RANK, C_WIDTH))
    w_in_r = jnp.concatenate(
        [w_in[:, :, o[0]:o[6]], w_in[:, :, o[7]:o[8]], w_in[:, :, o[6]:o[7]],
         jnp.zeros((depth, d, N_IN_PAD - int(o[8])), w_in.dtype)], axis=-1).astype(BF16)
    w_out_b = w_out.astype(BF16)

    mats = _s5_matrices(s5_lam_re, s5_lam_im, s5_log_dt, s5_b_re, s5_b_im, s5_c_re, s5_c_im)
    rw = jnp.zeros((depth, D_MODEL, LANES), F32).at[:, :, :N_EXPERTS].set(router_w)
    rw_hi = rw.astype(BF16)
    rw_lo = (rw - rw_hi.astype(F32)).astype(BF16)
    rb = jnp.zeros((depth, 1, LANES), F32).at[:, 0, :N_EXPERTS].set(router_b)

    n_assign = t * TOP_K
    n_blocks = -(-(n_assign + N_EXPERTS * (MOE_BLOCK - 1)) // MOE_BLOCK)

    for l in range(depth):
        mod_l = mod[l]
        proj, u5 = _in_projection(xa, mod_l, w_in_r[l], n_ctx_rows, seq)
        a_out = _spatial_gate(proj, sgu_ln_g[l], sgu_ln_b[l], sgu_w[l], sgu_b[l])
        o_fwd, o_bwd = _gla_sweep(proj, gla_gate_up[l], gla_gate_b[l], bsz, ctx_len, seq)
        y_s5 = _s5_scan(u5, mats, l, bsz, ctx_len, seq)
        x1, h2, logits = _out_projection(alpha, a_out, o_fwd, o_bwd, gla_norm_g[l], y_s5, proj, xa, mod_l, s5_d[l],
                                         s5_glu_w[l], s5_glu_b[l], w_out_b[l], ln_g[l, 0], ln_b[l, 0], rw_hi[l],
                                         rw_lo[l], rb[l], n_ctx_rows, seq)
        route, block_expert, n_valid = _routing(logits, n_blocks)
        xs = _moe_dispatch(h2, route, n_blocks)
        ys = _moe_experts(l, xs, block_expert, n_valid, w_up, b_up, w_down, b_down)
        xa = _moe_combine(alpha, route, ys, x1, mod_l, ln_g[l, 1], ln_b[l, 1], n_ctx_rows, seq,
                          drop_ctx=(l == depth - 1))
    return xa.reshape(bsz, seq, d)
```

```python
import functools

import numpy as np
import jax
import jax.numpy as jnp
from jax import lax
from jax.experimental import pallas as pl
from jax.experimental.pallas import tpu as pltpu

F32 = jnp.float32
BF16 = jnp.bfloat16
HIGHEST = lax.Precision.HIGHEST

D_MODEL = 1024
CHUNK = 128
A_HEADS = 4
A_HEAD_DIM = 64
A_WIDTH = 256
B_HEADS = 4
B_DK = 64
B_DV = 128
B_QK = 256
B_WIDTH = 512
GATE_RANK = 16
GATE_TAU = 16.0
GLA_CHUNK = 64
S5_GROUPS = 16
S5_GROUP_CH = 16
S5_STATE = 64
C_WIDTH = 256
N_EXPERTS = 32
TOP_K = 4
D_EXPERT = 1024
SWIGLU_LIMIT = 7.0
SWIGLU_ALPHA = 1.702
LN_EPS = 1e-5

N_IN_PAD = 2432
COL_GL = 2304
COL_S5 = 2048

ROW_TILE = 512
SEQ_TILE = 256
S5_T = 16
S5_QUARTER = 4
MOE_BLOCK = 512
VMEM_LIMIT = 56 * 1024 * 1024


def _layer_norm(x, g, b):
    mu = jnp.mean(x, axis=-1, keepdims=True)
    xc = x - mu
    var = jnp.mean(xc * xc, axis=-1, keepdims=True)
    return xc * lax.rsqrt(var + LN_EPS) * g + b


LANES = 128
ROW_SUB = D_MODEL // LANES


def _store_row_tiles(ref, lead, val):
    n = val.shape[0]
    for j in range(ROW_SUB):
        ref[lead + (pl.ds(j, n, stride=ROW_SUB), slice(None))] = val[:, j * LANES:(j + 1) * LANES]


def _load_row_tiles(ref, lead, n):
    return jnp.concatenate(
        [ref[lead + (pl.ds(j, n, stride=ROW_SUB), slice(None))] for j in range(ROW_SUB)], axis=1)


def _row_tile(ref, lead, r):
    return ref.at[lead + (pl.ds(pl.multiple_of(r * ROW_SUB, ROW_SUB), ROW_SUB), slice(None))]


def _segment(row0, n_ctx_rows, seq):
    return jnp.where(row0 < n_ctx_rows, 0, 1 + (row0 - n_ctx_rows) // seq)


def _mod_kernel(c_ref, w_ref, b_ref, o_ref):
    c = c_ref[...]
    s = c * jax.nn.sigmoid(c)
    o_ref[0] = jnp.dot(s, w_ref[0], precision=HIGHEST, preferred_element_type=F32) + b_ref[0]


def _modulation(cvec, w_mod, b_mod):
    depth = w_mod.shape[0]
    n6 = w_mod.shape[2]
    tn = 1024
    return pl.pallas_call(
        _mod_kernel,
        grid=(depth, n6 // tn),
        in_specs=[
            pl.BlockSpec((8, D_MODEL), lambda l, j: (0, 0)),
            pl.BlockSpec((1, D_MODEL, tn), lambda l, j: (l, 0, j)),
            pl.BlockSpec((1, 1, tn), lambda l, j: (l, 0, j)),
        ],
        out_specs=pl.BlockSpec((1, 8, tn), lambda l, j: (l, 0, j)),
        out_shape=jax.ShapeDtypeStruct((depth, 8, n6), F32),
        compiler_params=pltpu.CompilerParams(
            dimension_semantics=("arbitrary", "arbitrary"), vmem_limit_bytes=VMEM_LIMIT),
    )(cvec, w_mod, b_mod.reshape(depth, 1, n6))


def _inproj_kernel(x_ref, mod_ref, w_ref, o_ref, u5_ref, s5_scr):
    m = mod_ref[0]
    h = x_ref[...] * (1.0 + m[1:2]) + m[0:1]
    res = jnp.dot(h.astype(BF16), w_ref[...], preferred_element_type=F32)
    o_ref[...] = res.astype(o_ref.dtype)
    n_row = x_ref.shape[0] // S5_T
    per_tile = LANES // S5_GROUP_CH
    for half in range(C_WIDTH // LANES):
        s5_scr[half] = res[:, COL_S5 + half * LANES:COL_S5 + (half + 1) * LANES]
    at_step = [[s5_scr[half, pl.ds(step, n_row, stride=S5_T), :] for half in range(C_WIDTH // LANES)]
               for step in range(S5_T)]
    for g in range(S5_GROUPS):
        lanes = slice((g % per_tile) * S5_GROUP_CH, (g % per_tile + 1) * S5_GROUP_CH)
        u5_ref[g] = jnp.concatenate(
            [at_step[step][g // per_tile][:, lanes] for step in range(S5_T)], axis=1).astype(BF16)


def _in_projection(x, mod_l, w_in_b, n_ctx_rows, seq):
    t = x.shape[0]
    tm = ROW_TILE
    return pl.pallas_call(
        _inproj_kernel,
        grid=(t // tm,),
        in_specs=[
            pl.BlockSpec((tm, D_MODEL), lambda i: (i, 0)),
            pl.BlockSpec((1, 6, D_MODEL), lambda i: (_segment(i * tm, n_ctx_rows, seq), 0, 0)),
            pl.BlockSpec((D_MODEL, N_IN_PAD), lambda i: (0, 0)),
        ],
        out_specs=[
            pl.BlockSpec((tm, N_IN_PAD), lambda i: (i, 0)),
            pl.BlockSpec((S5_GROUPS, tm // S5_T, S5_T * S5_GROUP_CH), lambda i: (0, i, 0)),
        ],
        out_shape=[
            jax.ShapeDtypeStruct((t, N_IN_PAD), BF16),
            jax.ShapeDtypeStruct((S5_GROUPS, t // S5_T, S5_T * S5_GROUP_CH), BF16),
        ],
        scratch_shapes=[pltpu.VMEM((C_WIDTH // LANES, tm, LANES), F32)],
        compiler_params=pltpu.CompilerParams(
            dimension_semantics=("arbitrary",), vmem_limit_bytes=VMEM_LIMIT),
    )(x, mod_l, w_in_b)


def _sgu_kernel(uv_ref, g_ref, b_ref, w_ref, bias_ref, o_ref):
    tm = uv_ref.shape[0]
    u = jax.nn.gelu(uv_ref[:, :A_WIDTH].astype(F32))
    v = _layer_norm(jax.nn.gelu(uv_ref[:, A_WIDTH:].astype(F32)), g_ref[...], b_ref[...]).astype(BF16)
    head = lax.broadcasted_iota(jnp.int32, (1, A_WIDTH), 1) // A_HEAD_DIM
    for c in range(tm // CHUNK):
        rows = slice(c * CHUNK, (c + 1) * CHUNK)
        vc = v[rows]
        acc = bias_ref[...]
        for h in range(A_HEADS):
            r = jnp.dot(w_ref[h], vc, preferred_element_type=F32)
            acc = acc + jnp.where(head == h, r, 0.0)
        o_ref[rows, :] = (u[rows] * acc).astype(BF16)


def _spatial_gate(proj, ln_g, ln_b, w_s, b_s):
    t = proj.shape[0]
    tm = ROW_TILE
    bias = jnp.repeat(b_s.T, A_HEAD_DIM, axis=1)
    return pl.pallas_call(
        _sgu_kernel,
        grid=(t // tm,),
        in_specs=[
            pl.BlockSpec((tm, 2 * A_WIDTH), lambda i: (i, 0)),
            pl.BlockSpec((1, A_WIDTH), lambda i: (0, 0)),
            pl.BlockSpec((1, A_WIDTH), lambda i: (0, 0)),
            pl.BlockSpec((A_HEADS, CHUNK, CHUNK), lambda i: (0, 0, 0)),
            pl.BlockSpec((CHUNK, A_WIDTH), lambda i: (0, 0)),
        ],
        out_specs=pl.BlockSpec((tm, A_WIDTH), lambda i: (i, 0)),
        out_shape=jax.ShapeDtypeStruct((t, A_WIDTH), BF16),
        compiler_params=pltpu.CompilerParams(
            dimension_semantics=("arbitrary",), vmem_limit_bytes=VMEM_LIMIT),
    )(proj, ln_g.reshape(1, -1), ln_b.reshape(1, -1), w_s.astype(BF16), bias)


_NT = (((1,), (1,)), ((), ()))
_TN = (((0,), (0,)), ((), ()))


def _gla_direction(backward, q_ref, k_ref, v_ref, gl_ref, gup_ref, gb_ref, o_ref, s_ref):
    c_len = GLA_CHUNK
    n_rows = q_ref.shape[0]
    n_chunks = n_rows // c_len

    row = lax.broadcasted_iota(jnp.int32, (n_rows, n_rows), 0)
    col = lax.broadcasted_iota(jnp.int32, (n_rows, n_rows), 1)
    same = (row // c_len) == (col // c_len)
    if backward:
        tri = jnp.where(same & (col >= row), 1.0, 0.0).astype(BF16)
        keep = same & (col > row)
        i_last, i_mid = 0, c_len - 1 - c_len // 2
        lo = GATE_RANK
    else:
        tri = jnp.where(same & (col <= row), 1.0, 0.0).astype(BF16)
        keep = same & (col <= row)
        i_last, i_mid = c_len - 1, c_len // 2
        lo = 0

    gl = gl_ref[:, lo:lo + GATE_RANK]
    z = (jnp.dot(gl, gup_ref[0], preferred_element_type=F32) + jnp.dot(gl, gup_ref[1], preferred_element_type=F32)
         + gb_ref[...])
    la = jax.nn.log_sigmoid(z) / GATE_TAU
    l1 = la.astype(BF16)
    r1 = la - l1.astype(F32)
    l2 = r1.astype(BF16)
    l3 = (r1 - l2.astype(F32)).astype(BF16)
    b = (jnp.dot(tri, l1, preferred_element_type=F32) + jnp.dot(tri, l2, preferred_element_type=F32)
         + jnp.dot(tri, l3, preferred_element_type=F32))

    def per_chunk(index):
        return jnp.concatenate(
            [jnp.broadcast_to(b[c * c_len + index:c * c_len + index + 1], (c_len, B_QK))
             for c in range(n_chunks)], axis=0)

    b_last = per_chunk(i_last)
    b_mid = per_chunk(i_mid)
    q = q_ref[...].astype(F32) * (B_DK ** -0.5)
    k = k_ref[...].astype(F32)
    q_mid = (q * jnp.exp(b - b_mid)).astype(BF16)
    k_mid = (k * jnp.exp(b_mid - b)).astype(BF16)
    q_in = (q * jnp.exp(b)).astype(BF16)
    k_out = (k * jnp.exp(b_last - b)).astype(BF16)
    order = range(n_chunks - 1, -1, -1) if backward else range(n_chunks)
    v_all = v_ref[...].astype(BF16)
    intra = []
    for h in range(B_HEADS):
        hk = slice(h * B_DK, (h + 1) * B_DK)
        hv = slice(h * B_DV, (h + 1) * B_DV)
        sc = lax.dot_general(q_mid[:, hk], k_mid[:, hk], _NT, preferred_element_type=F32)
        sc = jnp.where(keep, sc, 0.0).astype(BF16)
        intra.append(jnp.dot(sc, v_all[:, hv], preferred_element_type=F32))
    o_intra = jnp.concatenate(intra, axis=1)
    own = (lax.broadcasted_iota(jnp.int32, (B_WIDTH, B_QK), 0) // B_DV
           == lax.broadcasted_iota(jnp.int32, (B_WIDTH, B_QK), 1) // B_DK)
    state = s_ref[...]
    for c in order:
        rows = slice(c * c_len, (c + 1) * c_len)
        o_ref[rows, :] = o_intra[rows] + lax.dot_general(q_in[rows], state.astype(BF16), _NT,
                                                         preferred_element_type=F32)
        decay = jnp.exp(b[c * c_len + i_last:c * c_len + i_last + 1])
        update = lax.dot_general(v_all[rows], k_out[rows], _TN, preferred_element_type=F32)
        state = jnp.where(own, state * decay + update, 0.0)
    s_ref[...] = state


def _gla_kernel(qf, kf, vf, glf, qb, kb, vb, glb, gup_ref, gb_ref, of_ref, ob_ref, s_ref):
    @pl.when(pl.program_id(1) == 0)
    def _():
        s_ref[...] = jnp.zeros_like(s_ref)

    _gla_direction(False, qf, kf, vf, glf, gup_ref.at[0], gb_ref.at[0], of_ref, s_ref.at[0])
    _gla_direction(True, qb, kb, vb, glb, gup_ref.at[1], gb_ref.at[1], ob_ref, s_ref.at[1])


def _gla_block(backward, bsz, nctx_blk, nlat_blk, b, j):
    if backward:
        ctx_i = b * nctx_blk + (nctx_blk - 1 - j)
        lat_i = bsz * nctx_blk + b * nlat_blk + (nlat_blk - 1 - (j - nctx_blk))
    else:
        ctx_i = b * nctx_blk + j
        lat_i = bsz * nctx_blk + b * nlat_blk + (j - nctx_blk)
    return jnp.where(j < nctx_blk, ctx_i, lat_i)


def _gla_sweep(proj, gate_up, gate_b, bsz, ctx_len, seq):
    t = proj.shape[0]
    r = SEQ_TILE
    nctx_blk, nlat_blk = ctx_len // r, seq // r
    up_hi = gate_up.astype(BF16)
    in_specs = []
    for backward in (False, True):
        blk = functools.partial(_gla_block, backward, bsz, nctx_blk, nlat_blk)
        in_specs += [
            pl.BlockSpec((r, B_QK), lambda b, j, blk=blk: (blk(b, j), 2)),
            pl.BlockSpec((r, B_QK), lambda b, j, blk=blk: (blk(b, j), 3)),
            pl.BlockSpec((r, B_WIDTH), lambda b, j, blk=blk: (blk(b, j), 2)),
            pl.BlockSpec((r, 128), lambda b, j, blk=blk: (blk(b, j), COL_GL // 128)),
        ]
    in_specs += [
        pl.BlockSpec((2, 2, GATE_RANK, B_QK), lambda b, j: (0, 0, 0, 0)),
        pl.BlockSpec((2, 1, B_QK), lambda b, j: (0, 0, 0)),
    ]
    fwd = functools.partial(_gla_block, False, bsz, nctx_blk, nlat_blk)
    bwd = functools.partial(_gla_block, True, bsz, nctx_blk, nlat_blk)
    return pl.pallas_call(
        _gla_kernel,
        grid=(bsz, nctx_blk + nlat_blk),
        in_specs=in_specs,
        out_specs=[
            pl.BlockSpec((r, B_WIDTH), lambda b, j: (fwd(b, j), 0)),
            pl.BlockSpec((r, B_WIDTH), lambda b, j: (bwd(b, j), 0)),
        ],
        out_shape=[jax.ShapeDtypeStruct((t, B_WIDTH), F32), jax.ShapeDtypeStruct((t, B_WIDTH), F32)],
        scratch_shapes=[pltpu.VMEM((2, B_WIDTH, B_QK), F32)],
        compiler_params=pltpu.CompilerParams(
            dimension_semantics=("arbitrary", "arbitrary"), vmem_limit_bytes=VMEM_LIMIT),
    )(*([proj] * 8), jnp.stack([up_hi, (gate_up - up_hi.astype(F32)).astype(BF16)], axis=1),
      gate_b.reshape(2, 1, -1))


def _s5_matrices(lam_re, lam_im, log_dt, b_re, b_im, c_re, c_im):
    tc, hh, width = S5_T, S5_GROUP_CH, S5_T * S5_GROUP_CH
    lam = lax.complex(lam_re.astype(F32), lam_im.astype(F32))
    dt = jnp.exp(log_dt.astype(F32))
    bm = lax.complex(b_re.astype(F32), b_im.astype(F32))
    cm = lax.complex(c_re.astype(F32), c_im.astype(F32))
    ldt = lam * dt[..., None]
    lam_bar = jnp.exp(ldt)
    b_bar = ((lam_bar - 1.0) / lam)[..., None] * bm
    steps = jnp.arange(tc + 1, dtype=F32)
    pw = jnp.exp(ldt[:, :, None] * steps[None, None, :, None, None])
    n_l, g_n = lam.shape[0], lam.shape[2]

    kern = jnp.real(jnp.einsum('ldgip,ldtgp,ldgpj->ldgjti', cm, pw[:, :, :tc], b_bar))
    k_f = kern[:, 0].reshape(n_l, g_n, hh, width)
    k_b = kern[:, 1, :, :, ::-1].reshape(n_l, g_n, hh, width)
    zero = jnp.zeros_like(k_f)
    wide_f = jnp.concatenate([zero, k_f], axis=-1)
    wide_b = jnp.concatenate([k_b, zero], axis=-1)
    mt = jnp.stack([wide_f[..., width - s * hh:2 * width - s * hh]
                    + wide_b[..., (tc - 1 - s) * hh:(tc - 1 - s) * hh + width] for s in range(tc)], axis=2)
    mt = mt.reshape(n_l, g_n, width, width)

    b_t = jnp.swapaxes(b_bar, -1, -2)
    pw_in = jnp.stack([pw[:, 0, :tc][:, ::-1], pw[:, 1, :tc]], axis=1)
    w = jnp.swapaxes(pw_in, 2, 3)[:, :, :, :, None, :] * b_t[:, :, :, None, :, :]
    w = w.reshape(n_l, 2, g_n, width, S5_STATE)
    re, im = jnp.real(w), jnp.imag(w)
    qt = jnp.concatenate([re[:, 0], im[:, 0], im[:, 0], re[:, 0], re[:, 1], im[:, 1], im[:, 1], re[:, 1]], axis=-1)

    pw_out = jnp.stack([pw[:, 0, 1:], pw[:, 1, 1:][:, ::-1]], axis=1)
    pw_out = jnp.repeat(jnp.transpose(pw_out, (0, 1, 3, 4, 2)), hh, axis=-1)
    c_t = jnp.tile(jnp.swapaxes(cm, -1, -2), (1, 1, 1, 1, tc))
    w = c_t * pw_out
    pt = jnp.concatenate([jnp.real(w[:, 0]), -jnp.imag(w[:, 0]), jnp.real(w[:, 1]), -jnp.imag(w[:, 1])], axis=2)

    a = pw[:, :, tc]
    ar, ai = jnp.real(a), jnp.imag(a)
    a1 = jnp.concatenate([ar, ar], axis=-1).reshape(n_l, 2, -1)
    a2 = jnp.concatenate([-ai, ai], axis=-1).reshape(n_l, 2, -1)
    a3 = jnp.concatenate([ai, -ai], axis=-1).reshape(n_l, 2, -1)
    ac = jnp.stack([a1, a2, a3], axis=2)
    return mt.astype(BF16), qt.astype(BF16), pt.astype(BF16), ac.reshape(n_l, 2, 3, 1, -1)


def _s5_kernel(uc_ref, ul_ref, mt_ref, qt_ref, pt_ref, ac_ref, yc_ref, yl_ref, ef, esf, eb, esb):
    nq = uc_ref.shape[0]
    nc_ctx, nc_lat = uc_ref.shape[1], ul_ref.shape[1]
    n = nc_ctx + nc_lat
    w = 2 * S5_STATE
    for gi in range(nq):
        lanes = slice(gi * w, (gi + 1) * w)
        rc = jnp.dot(uc_ref[gi], qt_ref[0, gi], preferred_element_type=F32)
        rl = jnp.dot(ul_ref[gi], qt_ref[0, gi], preferred_element_type=F32)
        ef[0:nc_ctx, lanes] = rc[:, 0:w]
        esf[0:nc_ctx, lanes] = rc[:, w:2 * w]
        ef[nc_ctx:n, lanes] = rl[:, 0:w]
        esf[nc_ctx:n, lanes] = rl[:, w:2 * w]
        eb[0:nc_lat, lanes] = rl[:, 2 * w:3 * w]
        esb[0:nc_lat, lanes] = rl[:, 3 * w:4 * w]
        eb[nc_lat:n, lanes] = rc[:, 2 * w:3 * w]
        esb[nc_lat:n, lanes] = rc[:, 3 * w:4 * w]

    a1f, a2f, a3f = ac_ref[0, 0, 0], ac_ref[0, 0, 1], ac_ref[0, 0, 2]
    a1b, a2b, a3b = ac_ref[0, 1, 0], ac_ref[0, 1, 1], ac_ref[0, 1, 2]

    def body(i, carry):
        hf, hsf, hb, hsb = carry
        rf = pl.ds(i, 1)
        rb = pl.ds(n - 1 - i, 1)
        e_f, es_f = ef[rf, :], esf[rf, :]
        e_b, es_b = eb[rb, :], esb[rb, :]
        ef[rf, :] = hf
        eb[rb, :] = hb
        return (a1f * hf + a2f * hsf + e_f, a1f * hsf + a3f * hf + es_f,
                a1b * hb + a2b * hsb + e_b, a1b * hsb + a3b * hb + es_b)

    zero = jnp.zeros((1, nq * w), F32)
    lax.fori_loop(0, n, body, (zero, zero, zero, zero))

    for gi in range(nq):
        lanes = slice(gi * w, (gi + 1) * w)
        hc = jnp.concatenate([ef[0:nc_ctx, lanes], eb[nc_lat:n, lanes]], axis=1).astype(BF16)
        hl = jnp.concatenate([ef[nc_ctx:n, lanes], eb[0:nc_lat, lanes]], axis=1).astype(BF16)
        yc_ref[gi] = (jnp.dot(uc_ref[gi], mt_ref[0, gi], preferred_element_type=F32)
                      + jnp.dot(hc, pt_ref[0, gi], preferred_element_type=F32))
        yl_ref[gi] = (jnp.dot(ul_ref[gi], mt_ref[0, gi], preferred_element_type=F32)
                      + jnp.dot(hl, pt_ref[0, gi], preferred_element_type=F32))


def _s5_scan(u5, mats, layer, bsz, ctx_len, seq):
    mt, qt, pt, ac = mats
    g_n, tc, hh = S5_GROUPS, S5_T, S5_GROUP_CH
    n_ctx_rows = bsz * ctx_len
    uc, ul = u5[:, :n_ctx_rows // tc], u5[:, n_ctx_rows // tc:]
    nc_ctx, nc_lat = ctx_len // tc, seq // tc
    nq = S5_QUARTER
    wq = nq * 2 * S5_STATE
    n = nc_ctx + nc_lat
    yc, yl = pl.pallas_call(
        _s5_kernel,
        grid=(bsz, g_n // nq),
        in_specs=[
            pl.BlockSpec((nq, nc_ctx, tc * hh), lambda b, qi: (qi, b, 0)),
            pl.BlockSpec((nq, nc_lat, tc * hh), lambda b, qi: (qi, b, 0)),
            pl.BlockSpec((1, nq, tc * hh, tc * hh), lambda b, qi: (layer, qi, 0, 0)),
            pl.BlockSpec((1, nq, tc * hh, 8 * S5_STATE), lambda b, qi: (layer, qi, 0, 0)),
            pl.BlockSpec((1, nq, 4 * S5_STATE, tc * hh), lambda b, qi: (layer, qi, 0, 0)),
            pl.BlockSpec((1, 2, 3, 1, wq), lambda b, qi: (layer, 0, 0, 0, qi)),
        ],
        out_specs=[
            pl.BlockSpec((nq, nc_ctx, tc * hh), lambda b, qi: (qi, b, 0)),
            pl.BlockSpec((nq, nc_lat, tc * hh), lambda b, qi: (qi, b, 0)),
        ],
        out_shape=[
            jax.ShapeDtypeStruct((g_n, bsz * nc_ctx, tc * hh), F32),
            jax.ShapeDtypeStruct((g_n, bsz * nc_lat, tc * hh), F32),
        ],
        scratch_shapes=[pltpu.VMEM((n, wq), F32) for _ in range(4)],
        compiler_params=pltpu.CompilerParams(
            dimension_semantics=("arbitrary", "arbitrary"), vmem_limit_bytes=VMEM_LIMIT),
    )(uc, ul, mt, qt, pt, ac)

    return jnp.concatenate([yc, yl], axis=1)


def _outproj_kernel(alpha, a_ref, of_ref, ob_ref, g_ref, ng_ref, y_ref, u_ref, x_ref, mod_ref, d_ref, gw_ref,
                    gb_ref, wo_ref, lng_ref, lnb_ref, rwh_ref, rwl_ref, rb_ref, x1_ref, h2_ref, lg_ref, y_scr):
    m = mod_ref[0]
    n_row = y_ref.shape[1]
    for step in range(S5_T):
        lanes = slice(step * S5_GROUP_CH, (step + 1) * S5_GROUP_CH)
        row = jnp.concatenate([y_ref[g][:, lanes] for g in range(S5_GROUPS)], axis=1)
        for half in range(C_WIDTH // LANES):
            y_scr[half, pl.ds(step, n_row, stride=S5_T), :] = row[:, half * LANES:(half + 1) * LANES]
    y_nat = jnp.concatenate([y_scr[half] for half in range(C_WIDTH // LANES)], axis=1)
    heads = []
    for h in range(B_HEADS):
        hv = slice(h * B_DV, (h + 1) * B_DV)
        o = of_ref[:, hv] + ob_ref[:, hv]
        heads.append(o * lax.rsqrt(jnp.mean(o * o, axis=-1, keepdims=True) + LN_EPS))
    gate = g_ref[...].astype(F32)
    gla = (jnp.concatenate(heads, axis=1) * ng_ref[...] * (gate * jax.nn.sigmoid(gate))).astype(BF16)
    y = jax.nn.gelu(y_nat + d_ref[...] * u_ref[...].astype(F32))
    s5 = y * jax.nn.sigmoid(jnp.dot(y.astype(BF16), gw_ref[...], preferred_element_type=F32) + gb_ref[...])
    mix = (jnp.dot(a_ref[...], wo_ref[0:A_WIDTH, :], preferred_element_type=F32)
           + jnp.dot(gla, wo_ref[A_WIDTH:A_WIDTH + B_WIDTH, :], preferred_element_type=F32)
           + jnp.dot(s5.astype(BF16), wo_ref[A_WIDTH + B_WIDTH:, :], preferred_element_type=F32))
    x1 = _layer_norm(alpha * x_ref[...] + m[2:3] * mix, lng_ref[...], lnb_ref[...])
    x1_ref[...] = x1
    h2 = x1 * (1.0 + m[4:5]) + m[3:4]
    _store_row_tiles(h2_ref, (), h2)
    h_hi = h2.astype(BF16)
    h_lo = (h2 - h_hi.astype(F32)).astype(BF16)
    lg_ref[...] = (jnp.dot(h_hi, rwh_ref[...], preferred_element_type=F32)
                   + jnp.dot(h_lo, rwh_ref[...], preferred_element_type=F32)
                   + jnp.dot(h_hi, rwl_ref[...], preferred_element_type=F32) + rb_ref[...])


def _out_projection(alpha, a_out, o_fwd, o_bwd, norm_g, y_s5, proj, x, mod_l, s5_d, glu_w, glu_b, w_out_b, ln_g,
                    ln_b, rw_hi, rw_lo, rb, n_ctx_rows, seq):
    t = x.shape[0]
    tm = ROW_TILE
    row = lambda i: (i, 0)
    fixed = lambda i: (0, 0)
    return pl.pallas_call(
        functools.partial(_outproj_kernel, alpha),
        grid=(t // tm,),
        in_specs=[
            pl.BlockSpec((tm, A_WIDTH), row),
            pl.BlockSpec((tm, B_WIDTH), row),
            pl.BlockSpec((tm, B_WIDTH), row),
            pl.BlockSpec((tm, B_WIDTH), lambda i: (i, 3)),
            pl.BlockSpec((1, B_WIDTH), fixed),
            pl.BlockSpec((S5_GROUPS, tm // S5_T, S5_T * S5_GROUP_CH), lambda i: (0, i, 0)),
            pl.BlockSpec((tm, C_WIDTH), lambda i: (i, COL_S5 // C_WIDTH)),
            pl.BlockSpec((tm, D_MODEL), row),
            pl.BlockSpec((1, 6, D_MODEL), lambda i: (_segment(i * tm, n_ctx_rows, seq), 0, 0)),
            pl.BlockSpec((1, C_WIDTH), fixed),
            pl.BlockSpec((C_WIDTH, C_WIDTH), fixed),
            pl.BlockSpec((1, C_WIDTH), fixed),
            pl.BlockSpec((D_MODEL, D_MODEL), fixed),
            pl.BlockSpec((1, D_MODEL), fixed),
            pl.BlockSpec((1, D_MODEL), fixed),
            pl.BlockSpec((D_MODEL, 128), fixed),
            pl.BlockSpec((D_MODEL, 128), fixed),
            pl.BlockSpec((1, 128), fixed),
        ],
        out_specs=[
            pl.BlockSpec((tm, D_MODEL), row),
            pl.BlockSpec((tm * ROW_SUB, LANES), row),
            pl.BlockSpec((tm, 128), row),
        ],
        out_shape=[
            jax.ShapeDtypeStruct((t, D_MODEL), F32),
            jax.ShapeDtypeStruct((t * ROW_SUB, LANES), F32),
            jax.ShapeDtypeStruct((t, 128), F32),
        ],
        scratch_shapes=[pltpu.VMEM((C_WIDTH // LANES, tm, LANES), F32)],
        compiler_params=pltpu.CompilerParams(
            dimension_semantics=("arbitrary",), vmem_limit_bytes=VMEM_LIMIT),
    )(a_out, o_fwd, o_bwd, proj, norm_g.reshape(1, -1), y_s5, proj, x, mod_l, s5_d.reshape(1, -1), glu_w.astype(BF16), glu_b.reshape(1, -1),
      w_out_b, ln_g.reshape(1, -1), ln_b.reshape(1, -1), rw_hi, rw_lo, rb)


def _route_kernel(lg_ref, gate_ref, pos_ref, tbase_ref, tcnt_ref, cnt_ref, base, before, below):
    tm = lg_ref.shape[0]

    @pl.when(pl.program_id(0) == 0)
    def _():
        base[...] = jnp.zeros_like(base)
        r = lax.broadcasted_iota(jnp.int32, (tm, tm), 0)
        c = lax.broadcasted_iota(jnp.int32, (tm, tm), 1)
        before[...] = jnp.where(r < c, 1.0, 0.0).astype(BF16)
        r = lax.broadcasted_iota(jnp.int32, (N_EXPERTS, N_EXPERTS), 0)
        c = lax.broadcasted_iota(jnp.int32, (N_EXPERTS, N_EXPERTS), 1)
        below[...] = jnp.where(c < r, 1.0, 0.0)

    logit = jnp.transpose(lg_ref[...])[:N_EXPERTS]
    eid = lax.broadcasted_iota(jnp.int32, (N_EXPERTS, tm), 0)
    vals, hots = [], []
    work = logit
    for kk in range(TOP_K):
        m = jnp.max(work, axis=0, keepdims=True)
        ix = jnp.min(jnp.where(work == m, eid, N_EXPERTS), axis=0, keepdims=True)
        hot = eid == ix
        vals.append(m)
        hots.append(hot)
        work = jnp.where(hot, -jnp.inf, work)
    ex = [jnp.exp(v - vals[0]) for v in vals]
    den = ex[0] + ex[1] + ex[2] + ex[3]
    member = jnp.zeros((N_EXPERTS, tm), F32)
    for kk in range(TOP_K):
        gate_ref[kk:kk + 1, :] = ex[kk] / den
        member = member + jnp.where(hots[kk], 1.0, 0.0)
    tile_cnt = jnp.broadcast_to(jnp.sum(member, axis=1, keepdims=True), (N_EXPERTS, LANES))
    group_off = jnp.dot(below[...], tile_cnt, precision=HIGHEST, preferred_element_type=F32)[:, 0:1]
    in_group = jnp.dot(member.astype(BF16), before[...], preferred_element_type=F32)
    for kk in range(TOP_K):
        pos_ref[kk:kk + 1, :] = jnp.sum(jnp.where(hots[kk], group_off + in_group, 0.0), axis=0,
                                        keepdims=True).astype(jnp.int32)
    tbase_ref[...] = jnp.broadcast_to(base[...], tbase_ref.shape)
    tcnt_ref[...] = tile_cnt
    total = base[...] + tile_cnt[:, 0:1]
    base[...] = total
    cnt_ref[...] = jnp.broadcast_to(total, cnt_ref.shape)


def _routing(logits, n_blocks):
    t = logits.shape[0]
    tm = ROW_TILE
    nt = t // tm
    gates, pos, tbase, tcnt, cnt = pl.pallas_call(
        _route_kernel,
        grid=(nt,),
        in_specs=[pl.BlockSpec((tm, LANES), lambda i: (i, 0))],
        out_specs=[
            pl.BlockSpec((TOP_K, tm), lambda i: (0, i)),
            pl.BlockSpec((TOP_K, tm), lambda i: (0, i)),
            pl.BlockSpec((N_EXPERTS, LANES), lambda i: (i, 0)),
            pl.BlockSpec((N_EXPERTS, LANES), lambda i: (i, 0)),
            pl.BlockSpec((N_EXPERTS, LANES), lambda i: (0, 0)),
        ],
        out_shape=[
            jax.ShapeDtypeStruct((TOP_K, t), F32),
            jax.ShapeDtypeStruct((TOP_K, t), jnp.int32),
            jax.ShapeDtypeStruct((nt * N_EXPERTS, LANES), F32),
            jax.ShapeDtypeStruct((nt * N_EXPERTS, LANES), F32),
            jax.ShapeDtypeStruct((N_EXPERTS, LANES), F32),
        ],
        scratch_shapes=[pltpu.VMEM((N_EXPERTS, 1), F32), pltpu.VMEM((tm, tm), BF16),
                        pltpu.VMEM((N_EXPERTS, N_EXPERTS), F32)],
        compiler_params=pltpu.CompilerParams(
            dimension_semantics=("arbitrary",), vmem_limit_bytes=VMEM_LIMIT),
    )(logits)
    counts = cnt[:, 0].astype(jnp.int32)
    padded = (counts + MOE_BLOCK - 1) // MOE_BLOCK * MOE_BLOCK
    padded_end = jnp.cumsum(padded)
    padded_start = (padded_end - padded).astype(jnp.int32)
    first_slot = jnp.arange(n_blocks, dtype=jnp.int32) * MOE_BLOCK
    block_expert = jnp.minimum(jnp.sum((padded_end[None, :] <= first_slot[:, None]).astype(jnp.int32), axis=1),
                               N_EXPERTS - 1).astype(jnp.int32)
    n_valid = (padded_end[-1] // MOE_BLOCK).astype(jnp.int32).reshape(1)
    pad_lo = (padded_start + counts).astype(jnp.int32)
    route = dict(
        gates=gates.T.reshape(nt, 1, tm * TOP_K),
        pos=pos.T.reshape(nt, 1, tm * TOP_K),
        tile_base=tbase[:, 0].astype(jnp.int32).reshape(nt, 1, N_EXPERTS),
        tile_cnt=tcnt[:, 0].astype(jnp.int32).reshape(nt, 1, N_EXPERTS),
        start=padded_start, pad_lo=pad_lo, pad_hi=padded_end.astype(jnp.int32))
    return route, block_expert, n_valid


def _expert_runs(tile_cnt_ref, tile_base_ref, start_ref, copy):
    off = 0
    for e in range(N_EXPERTS):
        n = tile_cnt_ref[0, 0, e]
        slot0 = start_ref[e] + tile_base_ref[0, 0, e]
        for bit in range(ROW_TILE.bit_length() - 1, -1, -1):
            size = 1 << bit
            done = (n >> (bit + 1)) << (bit + 1)

            @pl.when(((n >> bit) & 1) == 1)
            def _(off=off, done=done, slot0=slot0, size=size):
                copy(off + done, slot0 + done, size)
        off = off + n


def _rows(ref, lead, row0, n):
    return ref.at[lead + (pl.ds(pl.multiple_of(row0 * ROW_SUB, ROW_SUB), n * ROW_SUB), slice(None))]


def _dispatch_kernel(start_ref, lo_ref, hi_ref, pos_ref, tbase_ref, tcnt_ref, h_ref, o_ref, stage, zbuf, sem, zsem):
    i = pl.program_id(0)
    nt = pl.num_programs(0)
    tm = ROW_TILE
    slot = i % 2
    blk_rows = MOE_BLOCK * ROW_SUB
    n_blocks = o_ref.shape[0] // blk_rows

    def wait_stage(s):
        pltpu.make_async_copy(stage.at[s], stage.at[s], sem.at[s]).wait()

    @pl.when(i >= 2)
    def _():
        wait_stage(slot)

    def place(r, carry):
        row = h_ref[pl.ds(pl.multiple_of(r * ROW_SUB, ROW_SUB), ROW_SUB), :]
        for kk in range(TOP_K):
            p = pos_ref[0, 0, r * TOP_K + kk]
            stage[slot, pl.ds(pl.multiple_of(p * ROW_SUB, ROW_SUB), ROW_SUB), :] = row
        return carry
    lax.fori_loop(0, tm, place, 0, unroll=8)

    def copy(stage_row, slot_row, n):
        pltpu.make_async_copy(_rows(stage, (slot,), stage_row, n), _rows(o_ref, (), slot_row, n),
                              sem.at[slot]).start()
    _expert_runs(tcnt_ref, tbase_ref, start_ref, copy)

    @pl.when(i == nt - 1)
    def _():
        wait_stage(slot)

        @pl.when(i >= 1)
        def _():
            wait_stage(1 - slot)
        zbuf[...] = jnp.zeros_like(zbuf)
        zrow = zbuf.at[pl.ds(0, ROW_SUB), :]
        for e in range(N_EXPERTS):
            def fill(s, carry):
                pltpu.make_async_copy(zrow, _row_tile(o_ref, (), s), zsem).start()
                return carry
            lax.fori_loop(lo_ref[e], hi_ref[e], fill, 0)
        for e in range(N_EXPERTS):
            def drain(s, carry):
                pltpu.make_async_copy(zrow, zrow, zsem).wait()
                return carry
            lax.fori_loop(lo_ref[e], hi_ref[e], drain, 0)
        used = hi_ref[N_EXPERTS - 1] // MOE_BLOCK

        def fill_block(j, carry):
            rows = pl.ds(pl.multiple_of(j * blk_rows, blk_rows), blk_rows)
            pltpu.make_async_copy(zbuf, o_ref.at[rows, :], zsem).start()
            return carry
        lax.fori_loop(used, n_blocks, fill_block, 0)

        def drain_block(j, carry):
            pltpu.make_async_copy(zbuf, zbuf, zsem).wait()
            return carry
        lax.fori_loop(used, n_blocks, drain_block, 0)


def _moe_dispatch(h2t, route, n_blocks):
    nt = route['pos'].shape[0]
    tm = ROW_TILE
    smem = lambda width: pl.BlockSpec((1, 1, width), lambda i, *_: (i, 0, 0), memory_space=pltpu.SMEM)
    grid_spec = pltpu.PrefetchScalarGridSpec(
        num_scalar_prefetch=3,
        grid=(nt,),
        in_specs=[
            smem(tm * TOP_K), smem(N_EXPERTS), smem(N_EXPERTS),
            pl.BlockSpec((tm * ROW_SUB, LANES), lambda i, *_: (i, 0)),
        ],
        out_specs=pl.BlockSpec(memory_space=pl.ANY),
        scratch_shapes=[
            pltpu.VMEM((2, tm * TOP_K * ROW_SUB, LANES), F32),
            pltpu.VMEM((MOE_BLOCK * ROW_SUB, LANES), F32),
            pltpu.SemaphoreType.DMA((2,)),
            pltpu.SemaphoreType.DMA(()),
        ],
    )
    return pl.pallas_call(
        _dispatch_kernel,
        grid_spec=grid_spec,
        out_shape=jax.ShapeDtypeStruct((n_blocks * MOE_BLOCK * ROW_SUB, LANES), F32),
        compiler_params=pltpu.CompilerParams(
            dimension_semantics=("arbitrary",), vmem_limit_bytes=VMEM_LIMIT),
    )(route['start'], route['pad_lo'], route['pad_hi'], route['pos'], route['tile_base'], route['tile_cnt'], h2t)


def _moe_kernel(be_ref, nv_ref, x_ref, wu_ref, bu_ref, wd_ref, bd_ref, o_ref, wu_b, wd_b):
    i = pl.program_id(0)
    n_valid = nv_ref[0]

    @pl.when(i < n_valid)
    def _():
        first = jnp.logical_or(i == 0, be_ref[i] != be_ref[jnp.maximum(i - 1, 0)])

        @pl.when(first)
        def _():
            rows = 64

            def cast(r, carry):
                rs = pl.ds(pl.multiple_of(r * rows, rows), rows)
                wu_b[rs, :] = wu_ref[0, 0, rs, :].astype(BF16)
                wd_b[rs, :] = wd_ref[0, 0, rs, :].astype(BF16)
                return carry
            lax.fori_loop(0, D_MODEL // rows, cast, 0)

        x = _load_row_tiles(x_ref, (), MOE_BLOCK).astype(BF16)
        acc = jnp.zeros((MOE_BLOCK, D_MODEL), F32) + bd_ref[0, 0]
        cw = 512
        for jc in range(D_EXPERT // cw):
            cg = slice(jc * cw, (jc + 1) * cw)
            cl = slice(D_EXPERT + jc * cw, D_EXPERT + (jc + 1) * cw)
            ug = jnp.dot(x, wu_b[:, cg], preferred_element_type=F32) + bu_ref[0, 0, :, cg]
            ul = jnp.dot(x, wu_b[:, cl], preferred_element_type=F32) + bu_ref[0, 0, :, cl]
            xg = jnp.minimum(ug, SWIGLU_LIMIT)
            xl = jnp.clip(ul, -SWIGLU_LIMIT, SWIGLU_LIMIT)
            act = xg * jax.nn.sigmoid(SWIGLU_ALPHA * xg) * (xl + 1.0)
            acc = acc + jnp.dot(act.astype(BF16), wd_b[cg, :], preferred_element_type=F32)
        _store_row_tiles(o_ref, (), acc)

    @pl.when(i >= n_valid)
    def _():
        o_ref[...] = jnp.zeros_like(o_ref)


def _moe_experts(layer, xs, block_expert, n_valid, w_up, b_up, w_down, b_down):
    n_blocks = block_expert.shape[0]
    depth = w_up.shape[0]
    grid_spec = pltpu.PrefetchScalarGridSpec(
        num_scalar_prefetch=2,
        grid=(n_blocks,),
        in_specs=[
            pl.BlockSpec((MOE_BLOCK * ROW_SUB, LANES),
                         lambda i, be, nv: (jnp.minimum(i, jnp.maximum(nv[0] - 1, 0)), 0)),
            pl.BlockSpec((1, 1, D_MODEL, 2 * D_EXPERT), lambda i, be, nv: (layer, be[i], 0, 0)),
            pl.BlockSpec((1, 1, 1, 2 * D_EXPERT), lambda i, be, nv: (layer, be[i], 0, 0)),
            pl.BlockSpec((1, 1, D_EXPERT, D_MODEL), lambda i, be, nv: (layer, be[i], 0, 0)),
            pl.BlockSpec((1, 1, 1, D_MODEL), lambda i, be, nv: (layer, be[i], 0, 0)),
        ],
        out_specs=pl.BlockSpec((MOE_BLOCK * ROW_SUB, LANES), lambda i, be, nv: (i, 0)),
        scratch_shapes=[
            pltpu.VMEM((D_MODEL, 2 * D_EXPERT), BF16),
            pltpu.VMEM((D_EXPERT, D_MODEL), BF16),
        ],
    )
    return pl.pallas_call(
        _moe_kernel,
        grid_spec=grid_spec,
        out_shape=jax.ShapeDtypeStruct((n_blocks * MOE_BLOCK * ROW_SUB, LANES), F32),
        compiler_params=pltpu.CompilerParams(
            dimension_semantics=("arbitrary",), vmem_limit_bytes=VMEM_LIMIT),
    )(block_expert, n_valid, xs, w_up, b_up.reshape(depth, N_EXPERTS, 1, -1), w_down,
      b_down.reshape(depth, N_EXPERTS, 1, -1))


def _combine_kernel(alpha, start_ref, pos_ref, gate_ref, tbase_ref, tcnt_ref, tbase_n_ref, tcnt_n_ref, y_ref, x_ref,
                    mod_ref, lng_ref, lnb_ref, o_ref, stage, frow, sem):
    i = pl.program_id(0)
    nt = pl.num_programs(0)
    tm = ROW_TILE
    slot = i % 2

    def fetch(cnt_ref, base_ref, s):
        def copy(stage_row, slot_row, n):
            pltpu.make_async_copy(_rows(y_ref, (), slot_row, n), _rows(stage, (s,), stage_row, n),
                                  sem.at[s]).start()
        _expert_runs(cnt_ref, base_ref, start_ref, copy)

    @pl.when(i == 0)
    def _():
        fetch(tcnt_ref, tbase_ref, 0)

    @pl.when(i + 1 < nt)
    def _():
        fetch(tcnt_n_ref, tbase_n_ref, 1 - slot)

    pltpu.make_async_copy(stage.at[slot], stage.at[slot], sem.at[slot]).wait()

    def mix(r, carry):
        acc = None
        for kk in range(TOP_K):
            p = pos_ref[0, 0, r * TOP_K + kk]
            term = gate_ref[0, 0, r * TOP_K + kk] * stage[slot, pl.ds(pl.multiple_of(p * ROW_SUB, ROW_SUB), ROW_SUB), :]
            acc = term if acc is None else acc + term
        frow[pl.ds(pl.multiple_of(r * ROW_SUB, ROW_SUB), ROW_SUB), :] = acc
        return carry
    lax.fori_loop(0, tm, mix, 0, unroll=8)

    m = mod_ref[0]
    f = _load_row_tiles(frow, (), tm)
    o_ref[...] = _layer_norm(alpha * x_ref[...] + m[5:6] * f, lng_ref[...], lnb_ref[...])


def _moe_combine(alpha, route, ys, x1, mod_l, ln_g, ln_b, n_ctx_rows, seq, drop_ctx):
    t = x1.shape[0]
    tm = ROW_TILE
    nt = t // tm
    skip = n_ctx_rows // tm if drop_ctx else 0
    cur = lambda width: pl.BlockSpec((1, 1, width), lambda i, *_: (i, 0, 0), memory_space=pltpu.SMEM)
    nxt = lambda width: pl.BlockSpec((1, 1, width), lambda i, *_: (jnp.minimum(i + 1, nt - 1), 0, 0),
                                     memory_space=pltpu.SMEM)
    grid_spec = pltpu.PrefetchScalarGridSpec(
        num_scalar_prefetch=1,
        grid=(nt,),
        in_specs=[
            cur(tm * TOP_K), cur(tm * TOP_K), cur(N_EXPERTS), cur(N_EXPERTS), nxt(N_EXPERTS), nxt(N_EXPERTS),
            pl.BlockSpec(memory_space=pl.ANY),
            pl.BlockSpec((tm, D_MODEL), lambda i, *_: (i, 0)),
            pl.BlockSpec((1, 6, D_MODEL), lambda i, *_: (_segment(i * tm, n_ctx_rows, seq), 0, 0)),
            pl.BlockSpec((1, D_MODEL), lambda i, *_: (0, 0)),
            pl.BlockSpec((1, D_MODEL), lambda i, *_: (0, 0)),
        ],
        out_specs=pl.BlockSpec((tm, D_MODEL), lambda i, *_: (jnp.maximum(i - skip, 0), 0)),
        scratch_shapes=[
            pltpu.VMEM((2, tm * TOP_K * ROW_SUB, LANES), F32),
            pltpu.VMEM((tm * ROW_SUB, LANES), F32),
            pltpu.SemaphoreType.DMA((2,)),
        ],
    )
    return pl.pallas_call(
        functools.partial(_combine_kernel, alpha),
        grid_spec=grid_spec,
        out_shape=jax.ShapeDtypeStruct((t - skip * tm, D_MODEL), F32),
        compiler_params=pltpu.CompilerParams(
            dimension_semantics=("arbitrary",), vmem_limit_bytes=VMEM_LIMIT),
    )(route['start'], route['pos'], route['gates'], route['tile_base'], route['tile_cnt'], route['tile_base'],
      route['tile_cnt'], ys, x1, mod_l, ln_g.reshape(1, -1), ln_b.reshape(1, -1))


def kernel(x, c, ctx, c_ctx, w_mod, b_mod, w_in, sgu_ln_g, sgu_ln_b, sgu_w, sgu_b, gla_gate_up, gla_gate_b,
           gla_norm_g, s5_lam_re, s5_lam_im, s5_log_dt, s5_b_re, s5_b_im, s5_c_re, s5_c_im, s5_d, s5_glu_w,
           s5_glu_b, w_out, ln_g, ln_b, router_w, router_b, w_up, b_up, w_down, b_down):
    bsz, seq, d = x.shape
    ctx_len = ctx.shape[1]
    depth = w_in.shape[0]
    alpha = float((2 * depth) ** 0.25)
    n_ctx_rows = bsz * ctx_len
    t = n_ctx_rows + bsz * seq
    assert d == D_MODEL and bsz + 1 <= 8
    assert n_ctx_rows % ROW_TILE == 0 and seq % ROW_TILE == 0
    assert ctx_len % SEQ_TILE == 0 and seq % SEQ_TILE == 0

    xa = jnp.concatenate([ctx.reshape(n_ctx_rows, d), x.reshape(bsz * seq, d)], axis=0)
    cvec = jnp.zeros((8, d), F32).at[0].set(c_ctx).at[1:1 + bsz].set(c)
    mod = _modulation(cvec, w_mod, b_mod).reshape(depth, 8, 6, d)

    o = np.cumsum((0, A_WIDTH, A_WIDTH, B_QK, B_QK, B_WIDTH, B_WIDTH, 2 * GATE_RANK, C_WIDTH))
    w_in_r = jnp.concatenate(
        [w_in[:, :, o[0]:o[6]], w_in[:, :, o[7]:o[8]], w_in[:, :, o[6]:o[7]],
         jnp.zeros((depth, d, N_IN_PAD - int(o[8])), w_in.dtype)], axis=-1).astype(BF16)
    w_out_b = w_out.astype(BF16)

    mats = _s5_matrices(s5_lam_re, s5_lam_im, s5_log_dt, s5_b_re, s5_b_im, s5_c_re, s5_c_im)
    rw = jnp.zeros((depth, D_MODEL, LANES), F32).at[:, :, :N_EXPERTS].set(router_w)
    rw_hi = rw.astype(BF16)
    rw_lo = (rw - rw_hi.astype(F32)).astype(BF16)
    rb = jnp.zeros((depth, 1, LANES), F32).at[:, 0, :N_EXPERTS].set(router_b)

    n_assign = t * TOP_K
    n_blocks = -(-(n_assign + N_EXPERTS * (MOE_BLOCK - 1)) // MOE_BLOCK)

    for l in range(depth):
        mod_l = mod[l]
        proj, u5 = _in_projection(xa, mod_l, w_in_r[l], n_ctx_rows, seq)
        a_out = _spatial_gate(proj, sgu_ln_g[l], sgu_ln_b[l], sgu_w[l], sgu_b[l])
        o_fwd, o_bwd = _gla_sweep(proj, gla_gate_up[l], gla_gate_b[l], bsz, ctx_len, seq)
        y_s5 = _s5_scan(u5, mats, l, bsz, ctx_len, seq)
        x1, h2, logits = _out_projection(alpha, a_out, o_fwd, o_bwd, gla_norm_g[l], y_s5, proj, xa, mod_l, s5_d[l],
                                         s5_glu_w[l], s5_glu_b[l], w_out_b[l], ln_g[l, 0], ln_b[l, 0], rw_hi[l],
                                         rw_lo[l], rb[l], n_ctx_rows, seq)
        route, block_expert, n_valid = _routing(logits, n_blocks)
        xs = _moe_dispatch(h2, route, n_blocks)
        ys = _moe_experts(l, xs, block_expert, n_valid, w_up, b_up, w_down, b_down)
        xa = _moe_combine(alpha, route, ys, x1, mod_l, ln_g[l, 1], ln_b[l, 1], n_ctx_rows, seq,
                          drop_ctx=(l == depth - 1))
    return xa.reshape(bsz, seq, d)
```

```python
import functools

import numpy as np
import jax
import jax.numpy as jnp
from jax import lax
from jax.experimental import pallas as pl
from jax.experimental.pallas import tpu as pltpu

F32 = jnp.float32
BF16 = jnp.bfloat16
HIGHEST = lax.Precision.HIGHEST

D_MODEL = 1024
CHUNK = 128
A_HEADS = 4
A_HEAD_DIM = 64
A_WIDTH = 256
B_HEADS = 4
B_DK = 64
B_DV = 128
B_QK = 256
B_WIDTH = 512
GATE_RANK = 16
GATE_TAU = 16.0
GLA_CHUNK = 64
S5_GROUPS = 16
S5_GROUP_CH = 16
S5_STATE = 64
C_WIDTH = 256
N_EXPERTS = 32
TOP_K = 4
D_EXPERT = 1024
SWIGLU_LIMIT = 7.0
SWIGLU_ALPHA = 1.702
LN_EPS = 1e-5

N_IN_PAD = 2432
COL_GL = 2304
COL_S5 = 2048

ROW_TILE = 512
SEQ_TILE = 256
S5_T = 16
S5_QUARTER = 4
MOE_BLOCK = 512
VMEM_LIMIT = 56 * 1024 * 1024


def _layer_norm(x, g, b):
    mu = jnp.mean(x, axis=-1, keepdims=True)
    xc = x - mu
    var = jnp.mean(xc * xc, axis=-1, keepdims=True)
    return xc * lax.rsqrt(var + LN_EPS) * g + b


LANES = 128
ROW_SUB = D_MODEL // LANES


def _store_row_tiles(ref, lead, val):
    n = val.shape[0]
    for j in range(ROW_SUB):
        ref[lead + (pl.ds(j, n, stride=ROW_SUB), slice(None))] = val[:, j * LANES:(j + 1) * LANES]


def _load_row_tiles(ref, lead, n):
    return jnp.concatenate(
        [ref[lead + (pl.ds(j, n, stride=ROW_SUB), slice(None))] for j in range(ROW_SUB)], axis=1)


def _row_tile(ref, lead, r):
    return ref.at[lead + (pl.ds(pl.multiple_of(r * ROW_SUB, ROW_SUB), ROW_SUB), slice(None))]


def _segment(row0, n_ctx_rows, seq):
    return jnp.where(row0 < n_ctx_rows, 0, 1 + (row0 - n_ctx_rows) // seq)


def _mod_kernel(c_ref, w_ref, b_ref, o_ref):
    c = c_ref[...]
    s = c * jax.nn.sigmoid(c)
    o_ref[0] = jnp.dot(s, w_ref[0], precision=HIGHEST, preferred_element_type=F32) + b_ref[0]


def _modulation(cvec, w_mod, b_mod):
    depth = w_mod.shape[0]
    n6 = w_mod.shape[2]
    tn = 1024
    return pl.pallas_call(
        _mod_kernel,
        grid=(depth, n6 // tn),
        in_specs=[
            pl.BlockSpec((8, D_MODEL), lambda l, j: (0, 0)),
            pl.BlockSpec((1, D_MODEL, tn), lambda l, j: (l, 0, j)),
            pl.BlockSpec((1, 1, tn), lambda l, j: (l, 0, j)),
        ],
        out_specs=pl.BlockSpec((1, 8, tn), lambda l, j: (l, 0, j)),
        out_shape=jax.ShapeDtypeStruct((depth, 8, n6), F32),
        compiler_params=pltpu.CompilerParams(
            dimension_semantics=("arbitrary", "arbitrary"), vmem_limit_bytes=VMEM_LIMIT),
    )(cvec, w_mod, b_mod.reshape(depth, 1, n6))


def _inproj_kernel(x_ref, mod_ref, w_ref, o_ref, u5_ref, s5_scr):
    m = mod_ref[0]
    h = x_ref[...] * (1.0 + m[1:2]) + m[0:1]
    res = jnp.dot(h.astype(BF16), w_ref[...], preferred_element_type=F32)
    o_ref[...] = res.astype(o_ref.dtype)
    n_row = x_ref.shape[0] // S5_T
    per_tile = LANES // S5_GROUP_CH
    for half in range(C_WIDTH // LANES):
        s5_scr[half] = res[:, COL_S5 + half * LANES:COL_S5 + (half + 1) * LANES]
    at_step = [[s5_scr[half, pl.ds(step, n_row, stride=S5_T), :] for half in range(C_WIDTH // LANES)]
               for step in range(S5_T)]
    for g in range(S5_GROUPS):
        lanes = slice((g % per_tile) * S5_GROUP_CH, (g % per_tile + 1) * S5_GROUP_CH)
        u5_ref[g] = jnp.concatenate(
            [at_step[step][g // per_tile][:, lanes] for step in range(S5_T)], axis=1).astype(BF16)


def _in_projection(x, mod_l, w_in_b, n_ctx_rows, seq):
    t = x.shape[0]
    tm = ROW_TILE
    return pl.pallas_call(
        _inproj_kernel,
        grid=(t // tm,),
        in_specs=[
            pl.BlockSpec((tm, D_MODEL), lambda i: (i, 0)),
            pl.BlockSpec((1, 6, D_MODEL), lambda i: (_segment(i * tm, n_ctx_rows, seq), 0, 0)),
            pl.BlockSpec((D_MODEL, N_IN_PAD), lambda i: (0, 0)),
        ],
        out_specs=[
            pl.BlockSpec((tm, N_IN_PAD), lambda i: (i, 0)),
            pl.BlockSpec((S5_GROUPS, tm // S5_T, S5_T * S5_GROUP_CH), lambda i: (0, i, 0)),
        ],
        out_shape=[
            jax.ShapeDtypeStruct((t, N_IN_PAD), BF16),
            jax.ShapeDtypeStruct((S5_GROUPS, t // S5_T, S5_T * S5_GROUP_CH), BF16),
        ],
        scratch_shapes=[pltpu.VMEM((C_WIDTH // LANES, tm, LANES), F32)],
        compiler_params=pltpu.CompilerParams(
            dimension_semantics=("arbitrary",), vmem_limit_bytes=VMEM_LIMIT),
    )(x, mod_l, w_in_b)


def _sgu_kernel(uv_ref, g_ref, b_ref, w_ref, bias_ref, o_ref):
    tm = uv_ref.shape[0]
    u = jax.nn.gelu(uv_ref[:, :A_WIDTH].astype(F32))
    v = _layer_norm(jax.nn.gelu(uv_ref[:, A_WIDTH:].astype(F32)), g_ref[...], b_ref[...]).astype(BF16)
    head = lax.broadcasted_iota(jnp.int32, (1, A_WIDTH), 1) // A_HEAD_DIM
    for c in range(tm // CHUNK):
        rows = slice(c * CHUNK, (c + 1) * CHUNK)
        vc = v[rows]
        acc = bias_ref[...]
        for h in range(A_HEADS):
            r = jnp.dot(w_ref[h], vc, preferred_element_type=F32)
            acc = acc + jnp.where(head == h, r, 0.0)
        o_ref[rows, :] = (u[rows] * acc).astype(BF16)


def _spatial_gate(proj, ln_g, ln_b, w_s, b_s):
    t = proj.shape[0]
    tm = ROW_TILE
    bias = jnp.repeat(b_s.T, A_HEAD_DIM, axis=1)
    return pl.pallas_call(
        _sgu_kernel,
        grid=(t // tm,),
        in_specs=[
            pl.BlockSpec((tm, 2 * A_WIDTH), lambda i: (i, 0)),
            pl.BlockSpec((1, A_WIDTH), lambda i: (0, 0)),
            pl.BlockSpec((1, A_WIDTH), lambda i: (0, 0)),
            pl.BlockSpec((A_HEADS, CHUNK, CHUNK), lambda i: (0, 0, 0)),
            pl.BlockSpec((CHUNK, A_WIDTH), lambda i: (0, 0)),
        ],
        out_specs=pl.BlockSpec((tm, A_WIDTH), lambda i: (i, 0)),
        out_shape=jax.ShapeDtypeStruct((t, A_WIDTH), BF16),
        compiler_params=pltpu.CompilerParams(
            dimension_semantics=("arbitrary",), vmem_limit_bytes=VMEM_LIMIT),
    )(proj, ln_g.reshape(1, -1), ln_b.reshape(1, -1), w_s.astype(BF16), bias)


_NT = (((1,), (1,)), ((), ()))
_TN = (((0,), (0,)), ((), ()))


def _gla_direction(backward, q_ref, k_ref, v_ref, gl_ref, gup_ref, gb_ref, o_ref, s_ref):
    c_len = GLA_CHUNK
    n_rows = q_ref.shape[0]
    n_chunks = n_rows // c_len

    row = lax.broadcasted_iota(jnp.int32, (n_rows, n_rows), 0)
    col = lax.broadcasted_iota(jnp.int32, (n_rows, n_rows), 1)
    same = (row // c_len) == (col // c_len)
    if backward:
        tri = jnp.where(same & (col >= row), 1.0, 0.0).astype(BF16)
        keep = same & (col > row)
        i_last, i_mid = 0, c_len - 1 - c_len // 2
        lo = GATE_RANK
    else:
        tri = jnp.where(same & (col <= row), 1.0, 0.0).astype(BF16)
        keep = same & (col <= row)
        i_last, i_mid = c_len - 1, c_len // 2
        lo = 0

    gl = gl_ref[:, lo:lo + GATE_RANK]
    z = (jnp.dot(gl, gup_ref[0], preferred_element_type=F32) + jnp.dot(gl, gup_ref[1], preferred_element_type=F32)
         + gb_ref[...])
    la = jax.nn.log_sigmoid(z) / GATE_TAU
    l1 = la.astype(BF16)
    r1 = la - l1.astype(F32)
    l2 = r1.astype(BF16)
    l3 = (r1 - l2.astype(F32)).astype(BF16)
    b = (jnp.dot(tri, l1, preferred_element_type=F32) + jnp.dot(tri, l2, preferred_element_type=F32)
         + jnp.dot(tri, l3, preferred_element_type=F32))

    def per_chunk(index):
        return jnp.concatenate(
            [jnp.broadcast_to(b[c * c_len + index:c * c_len + index + 1], (c_len, B_QK))
             for c in range(n_chunks)], axis=0)

    b_last = per_chunk(i_last)
    b_mid = per_chunk(i_mid)
    q = q_ref[...].astype(F32) * (B_DK ** -0.5)
    k = k_ref[...].astype(F32)
    q_mid = (q * jnp.exp(b - b_mid)).astype(BF16)
    k_mid = (k * jnp.exp(b_mid - b)).astype(BF16)
    q_in = (q * jnp.exp(b)).astype(BF16)
    k_out = (k * jnp.exp(b_last - b)).astype(BF16)
    order = range(n_chunks - 1, -1, -1) if backward else range(n_chunks)
    v_all = v_ref[...].astype(BF16)
    intra = []
    for h in range(B_HEADS):
        hk = slice(h * B_DK, (h + 1) * B_DK)
        hv = slice(h * B_DV, (h + 1) * B_DV)
        sc = lax.dot_general(q_mid[:, hk], k_mid[:, hk], _NT, preferred_element_type=F32)
        sc = jnp.where(keep, sc, 0.0).astype(BF16)
        intra.append(jnp.dot(sc, v_all[:, hv], preferred_element_type=F32))
    o_intra = jnp.concatenate(intra, axis=1)
    own = (lax.broadcasted_iota(jnp.int32, (B_WIDTH, B_QK), 0) // B_DV
           == lax.broadcasted_iota(jnp.int32, (B_WIDTH, B_QK), 1) // B_DK)
    state = s_ref[...]
    for c in order:
        rows = slice(c * c_len, (c + 1) * c_len)
        o_ref[rows, :] = o_intra[rows] + lax.dot_general(q_in[rows], state.astype(BF16), _NT,
                                                         preferred_element_type=F32)
        decay = jnp.exp(b[c * c_len + i_last:c * c_len + i_last + 1])
        update = lax.dot_general(v_all[rows], k_out[rows], _TN, preferred_element_type=F32)
        state = jnp.where(own, state * decay + update, 0.0)
    s_ref[...] = state


def _gla_kernel(qf, kf, vf, glf, qb, kb, vb, glb, gup_ref, gb_ref, of_ref, ob_ref, s_ref):
    @pl.when(pl.program_id(1) == 0)
    def _():
        s_ref[...] = jnp.zeros_like(s_ref)

    _gla_direction(False, qf, kf, vf, glf, gup_ref.at[0], gb_ref.at[0], of_ref, s_ref.at[0])
    _gla_direction(True, qb, kb, vb, glb, gup_ref.at[1], gb_ref.at[1], ob_ref, s_ref.at[1])


def _gla_block(backward, bsz, nctx_blk, nlat_blk, b, j):
    if backward:
        ctx_i = b * nctx_blk + (nctx_blk - 1 - j)
        lat_i = bsz * nctx_blk + b * nlat_blk + (nlat_blk - 1 - (j - nctx_blk))
    else:
        ctx_i = b * nctx_blk + j
        lat_i = bsz * nctx_blk + b * nlat_blk + (j - nctx_blk)
    return jnp.where(j < nctx_blk, ctx_i, lat_i)


def _gla_sweep(proj, gate_up, gate_b, bsz, ctx_len, seq):
    t = proj.shape[0]
    r = SEQ_TILE
    nctx_blk, nlat_blk = ctx_len // r, seq // r
    up_hi = gate_up.astype(BF16)
    in_specs = []
    for backward in (False, True):
        blk = functools.partial(_gla_block, backward, bsz, nctx_blk, nlat_blk)
        in_specs += [
            pl.BlockSpec((r, B_QK), lambda b, j, blk=blk: (blk(b, j), 2)),
            pl.BlockSpec((r, B_QK), lambda b, j, blk=blk: (blk(b, j), 3)),
            pl.BlockSpec((r, B_WIDTH), lambda b, j, blk=blk: (blk(b, j), 2)),
            pl.BlockSpec((r, 128), lambda b, j, blk=blk: (blk(b, j), COL_GL // 128)),
        ]
    in_specs += [
        pl.BlockSpec((2, 2, GATE_RANK, B_QK), lambda b, j: (0, 0, 0, 0)),
        pl.BlockSpec((2, 1, B_QK), lambda b, j: (0, 0, 0)),
    ]
    fwd = functools.partial(_gla_block, False, bsz, nctx_blk, nlat_blk)
    bwd = functools.partial(_gla_block, True, bsz, nctx_blk, nlat_blk)
    return pl.pallas_call(
        _gla_kernel,
        grid=(bsz, nctx_blk + nlat_blk),
        in_specs=in_specs,
        out_specs=[
            pl.BlockSpec((r, B_WIDTH), lambda b, j: (fwd(b, j), 0)),
            pl.BlockSpec((r, B_WIDTH), lambda b, j: (bwd(b, j), 0)),
        ],
        out_shape=[jax.ShapeDtypeStruct((t, B_WIDTH), F32), jax.ShapeDtypeStruct((t, B_WIDTH), F32)],
        scratch_shapes=[pltpu.VMEM((2, B_WIDTH, B_QK), F32)],
        compiler_params=pltpu.CompilerParams(
            dimension_semantics=("arbitrary", "arbitrary"), vmem_limit_bytes=VMEM_LIMIT),
    )(*([proj] * 8), jnp.stack([up_hi, (gate_up - up_hi.astype(F32)).astype(BF16)], axis=1),
      gate_b.reshape(2, 1, -1))


def _s5_matrices(lam_re, lam_im, log_dt, b_re, b_im, c_re, c_im):
    tc, hh, width = S5_T, S5_GROUP_CH, S5_T * S5_GROUP_CH
    lam = lax.complex(lam_re.astype(F32), lam_im.astype(F32))
    dt = jnp.exp(log_dt.astype(F32))
    bm = lax.complex(b_re.astype(F32), b_im.astype(F32))
    cm = lax.complex(c_re.astype(F32), c_im.astype(F32))
    ldt = lam * dt[..., None]
    lam_bar = jnp.exp(ldt)
    b_bar = ((lam_bar - 1.0) / lam)[..., None] * bm
    steps = jnp.arange(tc + 1, dtype=F32)
    pw = jnp.exp(ldt[:, :, None] * steps[None, None, :, None, None])
    n_l, g_n = lam.shape[0], lam.shape[2]

    kern = jnp.real(jnp.einsum('ldgip,ldtgp,ldgpj->ldgjti', cm, pw[:, :, :tc], b_bar))
    k_f = kern[:, 0].reshape(n_l, g_n, hh, width)
    k_b = kern[:, 1, :, :, ::-1].reshape(n_l, g_n, hh, width)
    zero = jnp.zeros_like(k_f)
    wide_f = jnp.concatenate([zero, k_f], axis=-1)
    wide_b = jnp.concatenate([k_b, zero], axis=-1)
    mt = jnp.stack([wide_f[..., width - s * hh:2 * width - s * hh]
                    + wide_b[..., (tc - 1 - s) * hh:(tc - 1 - s) * hh + width] for s in range(tc)], axis=2)
    mt = mt.reshape(n_l, g_n, width, width)

    b_t = jnp.swapaxes(b_bar, -1, -2)
    pw_in = jnp.stack([pw[:, 0, :tc][:, ::-1], pw[:, 1, :tc]], axis=1)
    w = jnp.swapaxes(pw_in, 2, 3)[:, :, :, :, None, :] * b_t[:, :, :, None, :, :]
    w = w.reshape(n_l, 2, g_n, width, S5_STATE)
    re, im = jnp.real(w), jnp.imag(w)
    qt = jnp.concatenate([re[:, 0], im[:, 0], im[:, 0], re[:, 0], re[:, 1], im[:, 1], im[:, 1], re[:, 1]], axis=-1)

    pw_out = jnp.stack([pw[:, 0, 1:], pw[:, 1, 1:][:, ::-1]], axis=1)
    pw_out = jnp.repeat(jnp.transpose(pw_out, (0, 1, 3, 4, 2)), hh, axis=-1)
    c_t = jnp.tile(jnp.swapaxes(cm, -1, -2), (1, 1, 1, 1, tc))
    w = c_t * pw_out
    pt = jnp.concatenate([jnp.real(w[:, 0]), -jnp.imag(w[:, 0]), jnp.real(w[:, 1]), -jnp.imag(w[:, 1])], axis=2)

    a = pw[:, :, tc]
    ar, ai = jnp.real(a), jnp.imag(a)
    a1 = jnp.concatenate([ar, ar], axis=-1).reshape(n_l, 2, -1)
    a2 = jnp.concatenate([-ai, ai], axis=-1).reshape(n_l, 2, -1)
    a3 = jnp.concatenate([ai, -ai], axis=-1).reshape(n_l, 2, -1)
    ac = jnp.stack([a1, a2, a3], axis=2)
    return mt.astype(BF16), qt.astype(BF16), pt.astype(BF16), ac.reshape(n_l, 2, 3, 1, -1)


def _s5_kernel(uc_ref, ul_ref, mt_ref, qt_ref, pt_ref, ac_ref, yc_ref, yl_ref, ef, esf, eb, esb):
    nq = uc_ref.shape[0]
    nc_ctx, nc_lat = uc_ref.shape[1], ul_ref.shape[1]
    n = nc_ctx + nc_lat
    w = 2 * S5_STATE
    for gi in range(nq):
        lanes = slice(gi * w, (gi + 1) * w)
        rc = jnp.dot(uc_ref[gi], qt_ref[0, gi], preferred_element_type=F32)
        rl = jnp.dot(ul_ref[gi], qt_ref[0, gi], preferred_element_type=F32)
        ef[0:nc_ctx, lanes] = rc[:, 0:w]
        esf[0:nc_ctx, lanes] = rc[:, w:2 * w]
        ef[nc_ctx:n, lanes] = rl[:, 0:w]
        esf[nc_ctx:n, lanes] = rl[:, w:2 * w]
        eb[0:nc_lat, lanes] = rl[:, 2 * w:3 * w]
        esb[0:nc_lat, lanes] = rl[:, 3 * w:4 * w]
        eb[nc_lat:n, lanes] = rc[:, 2 * w:3 * w]
        esb[nc_lat:n, lanes] = rc[:, 3 * w:4 * w]

    a1f, a2f, a3f = ac_ref[0, 0, 0], ac_ref[0, 0, 1], ac_ref[0, 0, 2]
    a1b, a2b, a3b = ac_ref[0, 1, 0], ac_ref[0, 1, 1], ac_ref[0, 1, 2]

    def body(i, carry):
        hf, hsf, hb, hsb = carry
        rf = pl.ds(i, 1)
        rb = pl.ds(n - 1 - i, 1)
        e_f, es_f = ef[rf, :], esf[rf, :]
        e_b, es_b = eb[rb, :], esb[rb, :]
        ef[rf, :] = hf
        eb[rb, :] = hb
        return (a1f * hf + a2f * hsf + e_f, a1f * hsf + a3f * hf + es_f,
                a1b * hb + a2b * hsb + e_b, a1b * hsb + a3b * hb + es_b)

    zero = jnp.zeros((1, nq * w), F32)
    lax.fori_loop(0, n, body, (zero, zero, zero, zero))

    for gi in range(nq):
        lanes = slice(gi * w, (gi + 1) * w)
        hc = jnp.concatenate([ef[0:nc_ctx, lanes], eb[nc_lat:n, lanes]], axis=1).astype(BF16)
        hl = jnp.concatenate([ef[nc_ctx:n, lanes], eb[0:nc_lat, lanes]], axis=1).astype(BF16)
        yc_ref[gi] = (jnp.dot(uc_ref[gi], mt_ref[0, gi], preferred_element_type=F32)
                      + jnp.dot(hc, pt_ref[0, gi], preferred_element_type=F32))
        yl_ref[gi] = (jnp.dot(ul_ref[gi], mt_ref[0, gi], preferred_element_type=F32)
                      + jnp.dot(hl, pt_ref[0, gi], preferred_element_type=F32))


def _s5_scan(u5, mats, layer, bsz, ctx_len, seq):
    mt, qt, pt, ac = mats
    g_n, tc, hh = S5_GROUPS, S5_T, S5_GROUP_CH
    n_ctx_rows = bsz * ctx_len
    uc, ul = u5[:, :n_ctx_rows // tc], u5[:, n_ctx_rows // tc:]
    nc_ctx, nc_lat = ctx_len // tc, seq // tc
    nq = S5_QUARTER
    wq = nq * 2 * S5_STATE
    n = nc_ctx + nc_lat
    yc, yl = pl.pallas_call(
        _s5_kernel,
        grid=(bsz, g_n // nq),
        in_specs=[
            pl.BlockSpec((nq, nc_ctx, tc * hh), lambda b, qi: (qi, b, 0)),
            pl.BlockSpec((nq, nc_lat, tc * hh), lambda b, qi: (qi, b, 0)),
            pl.BlockSpec((1, nq, tc * hh, tc * hh), lambda b, qi: (layer, qi, 0, 0)),
            pl.BlockSpec((1, nq, tc * hh, 8 * S5_STATE), lambda b, qi: (layer, qi, 0, 0)),
            pl.BlockSpec((1, nq, 4 * S5_STATE, tc * hh), lambda b, qi: (layer, qi, 0, 0)),
            pl.BlockSpec((1, 2, 3, 1, wq), lambda b, qi: (layer, 0, 0, 0, qi)),
        ],
        out_specs=[
            pl.BlockSpec((nq, nc_ctx, tc * hh), lambda b, qi: (qi, b, 0)),
            pl.BlockSpec((nq, nc_lat, tc * hh), lambda b, qi: (qi, b, 0)),
        ],
        out_shape=[
            jax.ShapeDtypeStruct((g_n, bsz * nc_ctx, tc * hh), F32),
            jax.ShapeDtypeStruct((g_n, bsz * nc_lat, tc * hh), F32),
        ],
        scratch_shapes=[pltpu.VMEM((n, wq), F32) for _ in range(4)],
        compiler_params=pltpu.CompilerParams(
            dimension_semantics=("arbitrary", "arbitrary"), vmem_limit_bytes=VMEM_LIMIT),
    )(uc, ul, mt, qt, pt, ac)

    return jnp.concatenate([yc, yl], axis=1)


def _outproj_kernel(alpha, a_ref, of_ref, ob_ref, g_ref, ng_ref, y_ref, u_ref, x_ref, mod_ref, d_ref, gw_ref,
                    gb_ref, wo_ref, lng_ref, lnb_ref, rwh_ref, rwl_ref, rb_ref, x1_ref, h2_ref, lg_ref, y_scr):
    m = mod_ref[0]
    n_row = y_ref.shape[1]
    for step in range(S5_T):
        lanes = slice(step * S5_GROUP_CH, (step + 1) * S5_GROUP_CH)
        row = jnp.concatenate([y_ref[g][:, lanes] for g in range(S5_GROUPS)], axis=1)
        for half in range(C_WIDTH // LANES):
            y_scr[half, pl.ds(step, n_row, stride=S5_T), :] = row[:, half * LANES:(half + 1) * LANES]
    y_nat = jnp.concatenate([y_scr[half] for half in range(C_WIDTH // LANES)], axis=1)
    heads = []
    for h in range(B_HEADS):
        hv = slice(h * B_DV, (h + 1) * B_DV)
        o = of_ref[:, hv] + ob_ref[:, hv]
        heads.append(o * lax.rsqrt(jnp.mean(o * o, axis=-1, keepdims=True) + LN_EPS))
    gate = g_ref[...].astype(F32)
    gla = (jnp.concatenate(heads, axis=1) * ng_ref[...] * (gate * jax.nn.sigmoid(gate))).astype(BF16)
    y = jax.nn.gelu(y_nat + d_ref[...] * u_ref[...].astype(F32))
    s5 = y * jax.nn.sigmoid(jnp.dot(y.astype(BF16), gw_ref[...], preferred_element_type=F32) + gb_ref[...])
    mix = (jnp.dot(a_ref[...], wo_ref[0:A_WIDTH, :], preferred_element_type=F32)
           + jnp.dot(gla, wo_ref[A_WIDTH:A_WIDTH + B_WIDTH, :], preferred_element_type=F32)
           + jnp.dot(s5.astype(BF16), wo_ref[A_WIDTH + B_WIDTH:, :], preferred_element_type=F32))
    x1 = _layer_norm(alpha * x_ref[...] + m[2:3] * mix, lng_ref[...], lnb_ref[...])
    x1_ref[...] = x1
    h2 = x1 * (1.0 + m[4:5]) + m[3:4]
    _store_row_tiles(h2_ref, (), h2)
    h_hi = h2.astype(BF16)
    h_lo = (h2 - h_hi.astype(F32)).astype(BF16)
    lg_ref[...] = (jnp.dot(h_hi, rwh_ref[...], preferred_element_type=F32)
                   + jnp.dot(h_lo, rwh_ref[...], preferred_element_type=F32)
                   + jnp.dot(h_hi, rwl_ref[...], preferred_element_type=F32) + rb_ref[...])


def _out_projection(alpha, a_out, o_fwd, o_bwd, norm_g, y_s5, proj, x, mod_l, s5_d, glu_w, glu_b, w_out_b, ln_g,
                    ln_b, rw_hi, rw_lo, rb, n_ctx_rows, seq):
    t = x.shape[0]
    tm = ROW_TILE
    row = lambda i: (i, 0)
    fixed = lambda i: (0, 0)
    return pl.pallas_call(
        functools.partial(_outproj_kernel, alpha),
        grid=(t // tm,),
        in_specs=[
            pl.BlockSpec((tm, A_WIDTH), row),
            pl.BlockSpec((tm, B_WIDTH), row),
            pl.BlockSpec((tm, B_WIDTH), row),
            pl.BlockSpec((tm, B_WIDTH), lambda i: (i, 3)),
            pl.BlockSpec((1, B_WIDTH), fixed),
            pl.BlockSpec((S5_GROUPS, tm // S5_T, S5_T * S5_GROUP_CH), lambda i: (0, i, 0)),
            pl.BlockSpec((tm, C_WIDTH), lambda i: (i, COL_S5 // C_WIDTH)),
            pl.BlockSpec((tm, D_MODEL), row),
            pl.BlockSpec((1, 6, D_MODEL), lambda i: (_segment(i * tm, n_ctx_rows, seq), 0, 0)),
            pl.BlockSpec((1, C_WIDTH), fixed),
            pl.BlockSpec((C_WIDTH, C_WIDTH), fixed),
            pl.BlockSpec((1, C_WIDTH), fixed),
            pl.BlockSpec((D_MODEL, D_MODEL), fixed),
            pl.BlockSpec((1, D_MODEL), fixed),
            pl.BlockSpec((1, D_MODEL), fixed),
            pl.BlockSpec((D_MODEL, 128), fixed),
            pl.BlockSpec((D_MODEL, 128), fixed),
            pl.BlockSpec((1, 128), fixed),
        ],
        out_specs=[
            pl.BlockSpec((tm, D_MODEL), row),
            pl.BlockSpec((tm * ROW_SUB, LANES), row),
            pl.BlockSpec((tm, 128), row),
        ],
        out_shape=[
            jax.ShapeDtypeStruct((t, D_MODEL), F32),
            jax.ShapeDtypeStruct((t * ROW_SUB, LANES), F32),
            jax.ShapeDtypeStruct((t, 128), F32),
        ],
        scratch_shapes=[pltpu.VMEM((C_WIDTH // LANES, tm, LANES), F32)],
        compiler_params=pltpu.CompilerParams(
            dimension_semantics=("arbitrary",), vmem_limit_bytes=VMEM_LIMIT),
    )(a_out, o_fwd, o_bwd, proj, norm_g.reshape(1, -1), y_s5, proj, x, mod_l, s5_d.reshape(1, -1), glu_w.astype(BF16), glu_b.reshape(1, -1),
      w_out_b, ln_g.reshape(1, -1), ln_b.reshape(1, -1), rw_hi, rw_lo, rb)


def _route_kernel(lg_ref, gate_ref, pos_ref, tbase_ref, tcnt_ref, cnt_ref, base, before, below):
    tm = lg_ref.shape[0]

    @pl.when(pl.program_id(0) == 0)
    def _():
        base[...] = jnp.zeros_like(base)
        r = lax.broadcasted_iota(jnp.int32, (tm, tm), 0)
        c = lax.broadcasted_iota(jnp.int32, (tm, tm), 1)
        before[...] = jnp.where(r < c, 1.0, 0.0).astype(BF16)
        r = lax.broadcasted_iota(jnp.int32, (N_EXPERTS, N_EXPERTS), 0)
        c = lax.broadcasted_iota(jnp.int32, (N_EXPERTS, N_EXPERTS), 1)
        below[...] = jnp.where(c < r, 1.0, 0.0)

    logit = jnp.transpose(lg_ref[...])[:N_EXPERTS]
    eid = lax.broadcasted_iota(jnp.int32, (N_EXPERTS, tm), 0)
    vals, hots = [], []
    work = logit
    for kk in range(TOP_K):
        m = jnp.max(work, axis=0, keepdims=True)
        ix = jnp.min(jnp.where(work == m, eid, N_EXPERTS), axis=0, keepdims=True)
        hot = eid == ix
        vals.append(m)
        hots.append(hot)
        work = jnp.where(hot, -jnp.inf, work)
    ex = [jnp.exp(v - vals[0]) for v in vals]
    den = ex[0] + ex[1] + ex[2] + ex[3]
    member = jnp.zeros((N_EXPERTS, tm), F32)
    for kk in range(TOP_K):
        gate_ref[0, kk:kk + 1, :] = ex[kk] / den
        member = member + jnp.where(hots[kk], 1.0, 0.0)
    tile_cnt = jnp.broadcast_to(jnp.sum(member, axis=1, keepdims=True), (N_EXPERTS, LANES))
    group_off = jnp.dot(below[...], tile_cnt, precision=HIGHEST, preferred_element_type=F32)[:, 0:1]
    in_group = jnp.dot(member.astype(BF16), before[...], preferred_element_type=F32)
    for kk in range(TOP_K):
        pos_ref[0, kk:kk + 1, :] = jnp.sum(jnp.where(hots[kk], group_off + in_group, 0.0), axis=0,
                                        keepdims=True).astype(jnp.int32)
    tbase_ref[...] = jnp.broadcast_to(base[...], tbase_ref.shape)
    tcnt_ref[...] = tile_cnt
    total = base[...] + tile_cnt[:, 0:1]
    base[...] = total
    cnt_ref[...] = jnp.broadcast_to(total, cnt_ref.shape)


def _routing(logits, n_blocks):
    t = logits.shape[0]
    tm = ROW_TILE
    nt = t // tm
    gates, pos, tbase, tcnt, cnt = pl.pallas_call(
        _route_kernel,
        grid=(nt,),
        in_specs=[pl.BlockSpec((tm, LANES), lambda i: (i, 0))],
        out_specs=[
            pl.BlockSpec((1, TOP_K, tm), lambda i: (i, 0, 0)),
            pl.BlockSpec((1, TOP_K, tm), lambda i: (i, 0, 0)),
            pl.BlockSpec((N_EXPERTS, LANES), lambda i: (i, 0)),
            pl.BlockSpec((N_EXPERTS, LANES), lambda i: (i, 0)),
            pl.BlockSpec((N_EXPERTS, LANES), lambda i: (0, 0)),
        ],
        out_shape=[
            jax.ShapeDtypeStruct((nt, TOP_K, tm), F32),
            jax.ShapeDtypeStruct((nt, TOP_K, tm), jnp.int32),
            jax.ShapeDtypeStruct((nt * N_EXPERTS, LANES), F32),
            jax.ShapeDtypeStruct((nt * N_EXPERTS, LANES), F32),
            jax.ShapeDtypeStruct((N_EXPERTS, LANES), F32),
        ],
        scratch_shapes=[pltpu.VMEM((N_EXPERTS, 1), F32), pltpu.VMEM((tm, tm), BF16),
                        pltpu.VMEM((N_EXPERTS, N_EXPERTS), F32)],
        compiler_params=pltpu.CompilerParams(
            dimension_semantics=("arbitrary",), vmem_limit_bytes=VMEM_LIMIT),
    )(logits)
    counts = cnt[:, 0].astype(jnp.int32)
    padded = (counts + MOE_BLOCK - 1) // MOE_BLOCK * MOE_BLOCK
    padded_end = jnp.cumsum(padded)
    padded_start = (padded_end - padded).astype(jnp.int32)
    first_slot = jnp.arange(n_blocks, dtype=jnp.int32) * MOE_BLOCK
    block_expert = jnp.minimum(jnp.sum((padded_end[None, :] <= first_slot[:, None]).astype(jnp.int32), axis=1),
                               N_EXPERTS - 1).astype(jnp.int32)
    n_valid = (padded_end[-1] // MOE_BLOCK).astype(jnp.int32).reshape(1)
    pad_lo = (padded_start + counts).astype(jnp.int32)
    route = dict(
        gates=gates, pos=pos,
        tile_base=tbase[:, 0].astype(jnp.int32).reshape(nt, 1, N_EXPERTS),
        tile_cnt=tcnt[:, 0].astype(jnp.int32).reshape(nt, 1, N_EXPERTS),
        start=padded_start, pad_lo=pad_lo, pad_hi=padded_end.astype(jnp.int32))
    return route, block_expert, n_valid


def _expert_runs(tile_cnt_ref, tile_base_ref, start_ref, copy):
    off = 0
    for e in range(N_EXPERTS):
        n = tile_cnt_ref[0, 0, e]
        slot0 = start_ref[e] + tile_base_ref[0, 0, e]
        for bit in range(ROW_TILE.bit_length() - 1, -1, -1):
            size = 1 << bit
            done = (n >> (bit + 1)) << (bit + 1)

            @pl.when(((n >> bit) & 1) == 1)
            def _(off=off, done=done, slot0=slot0, size=size):
                copy(off + done, slot0 + done, size)
        off = off + n


def _rows(ref, lead, row0, n):
    return ref.at[lead + (pl.ds(pl.multiple_of(row0 * ROW_SUB, ROW_SUB), n * ROW_SUB), slice(None))]


def _dispatch_kernel(start_ref, lo_ref, hi_ref, pos_ref, tbase_ref, tcnt_ref, h_ref, o_ref, stage, zbuf, sem, zsem):
    i = pl.program_id(0)
    nt = pl.num_programs(0)
    tm = ROW_TILE
    slot = i % 2
    blk_rows = MOE_BLOCK * ROW_SUB
    n_blocks = o_ref.shape[0] // blk_rows

    def wait_stage(s):
        pltpu.make_async_copy(stage.at[s], stage.at[s], sem.at[s]).wait()

    @pl.when(i >= 2)
    def _():
        wait_stage(slot)

    def place(r, carry):
        row = h_ref[pl.ds(pl.multiple_of(r * ROW_SUB, ROW_SUB), ROW_SUB), :]
        for kk in range(TOP_K):
            p = pos_ref[0, kk, r]
            stage[slot, pl.ds(pl.multiple_of(p * ROW_SUB, ROW_SUB), ROW_SUB), :] = row
        return carry
    lax.fori_loop(0, tm, place, 0, unroll=8)

    def copy(stage_row, slot_row, n):
        pltpu.make_async_copy(_rows(stage, (slot,), stage_row, n), _rows(o_ref, (), slot_row, n),
                              sem.at[slot]).start()
    _expert_runs(tcnt_ref, tbase_ref, start_ref, copy)

    @pl.when(i == nt - 1)
    def _():
        wait_stage(slot)

        @pl.when(i >= 1)
        def _():
            wait_stage(1 - slot)
        zbuf[...] = jnp.zeros_like(zbuf)
        zrow = zbuf.at[pl.ds(0, ROW_SUB), :]
        for e in range(N_EXPERTS):
            def fill(s, carry):
                pltpu.make_async_copy(zrow, _row_tile(o_ref, (), s), zsem).start()
                return carry
            lax.fori_loop(lo_ref[e], hi_ref[e], fill, 0)
        for e in range(N_EXPERTS):
            def drain(s, carry):
                pltpu.make_async_copy(zrow, zrow, zsem).wait()
                return carry
            lax.fori_loop(lo_ref[e], hi_ref[e], drain, 0)
        used = hi_ref[N_EXPERTS - 1] // MOE_BLOCK

        def fill_block(j, carry):
            rows = pl.ds(pl.multiple_of(j * blk_rows, blk_rows), blk_rows)
            pltpu.make_async_copy(zbuf, o_ref.at[rows, :], zsem).start()
            return carry
        lax.fori_loop(used, n_blocks, fill_block, 0)

        def drain_block(j, carry):
            pltpu.make_async_copy(zbuf, zbuf, zsem).wait()
            return carry
        lax.fori_loop(used, n_blocks, drain_block, 0)


def _moe_dispatch(h2t, route, n_blocks):
    nt = route['pos'].shape[0]
    tm = ROW_TILE
    smem = lambda width: pl.BlockSpec((1, 1, width), lambda i, *_: (i, 0, 0), memory_space=pltpu.SMEM)
    per_token = pl.BlockSpec((1, TOP_K, tm), lambda i, *_: (i, 0, 0), memory_space=pltpu.SMEM)
    grid_spec = pltpu.PrefetchScalarGridSpec(
        num_scalar_prefetch=3,
        grid=(nt,),
        in_specs=[
            per_token, smem(N_EXPERTS), smem(N_EXPERTS),
            pl.BlockSpec((tm * ROW_SUB, LANES), lambda i, *_: (i, 0)),
        ],
        out_specs=pl.BlockSpec(memory_space=pl.ANY),
        scratch_shapes=[
            pltpu.VMEM((2, tm * TOP_K * ROW_SUB, LANES), F32),
            pltpu.VMEM((MOE_BLOCK * ROW_SUB, LANES), F32),
            pltpu.SemaphoreType.DMA((2,)),
            pltpu.SemaphoreType.DMA(()),
        ],
    )
    return pl.pallas_call(
        _dispatch_kernel,
        grid_spec=grid_spec,
        out_shape=jax.ShapeDtypeStruct((n_blocks * MOE_BLOCK * ROW_SUB, LANES), F32),
        compiler_params=pltpu.CompilerParams(
            dimension_semantics=("arbitrary",), vmem_limit_bytes=VMEM_LIMIT),
    )(route['start'], route['pad_lo'], route['pad_hi'], route['pos'], route['tile_base'], route['tile_cnt'], h2t)


def _moe_kernel(be_ref, nv_ref, x_ref, wu_ref, bu_ref, wd_ref, bd_ref, o_ref, wu_b, wd_b):
    i = pl.program_id(0)
    n_valid = nv_ref[0]

    @pl.when(i < n_valid)
    def _():
        first = jnp.logical_or(i == 0, be_ref[i] != be_ref[jnp.maximum(i - 1, 0)])

        @pl.when(first)
        def _():
            rows = 64

            def cast(r, carry):
                rs = pl.ds(pl.multiple_of(r * rows, rows), rows)
                wu_b[rs, :] = wu_ref[0, 0, rs, :].astype(BF16)
                wd_b[rs, :] = wd_ref[0, 0, rs, :].astype(BF16)
                return carry
            lax.fori_loop(0, D_MODEL // rows, cast, 0)

        x = _load_row_tiles(x_ref, (), MOE_BLOCK).astype(BF16)
        acc = jnp.zeros((MOE_BLOCK, D_MODEL), F32) + bd_ref[0, 0]
        cw = 512
        for jc in range(D_EXPERT // cw):
            cg = slice(jc * cw, (jc + 1) * cw)
            cl = slice(D_EXPERT + jc * cw, D_EXPERT + (jc + 1) * cw)
            ug = jnp.dot(x, wu_b[:, cg], preferred_element_type=F32) + bu_ref[0, 0, :, cg]
            ul = jnp.dot(x, wu_b[:, cl], preferred_element_type=F32) + bu_ref[0, 0, :, cl]
            xg = jnp.minimum(ug, SWIGLU_LIMIT)
            xl = jnp.clip(ul, -SWIGLU_LIMIT, SWIGLU_LIMIT)
            act = xg * jax.nn.sigmoid(SWIGLU_ALPHA * xg) * (xl + 1.0)
            acc = acc + jnp.dot(act.astype(BF16), wd_b[cg, :], preferred_element_type=F32)
        _store_row_tiles(o_ref, (), acc)

    @pl.when(i >= n_valid)
    def _():
        o_ref[...] = jnp.zeros_like(o_ref)


def _moe_experts(layer, xs, block_expert, n_valid, w_up, b_up, w_down, b_down):
    n_blocks = block_expert.shape[0]
    depth = w_up.shape[0]
    grid_spec = pltpu.PrefetchScalarGridSpec(
        num_scalar_prefetch=2,
        grid=(n_blocks,),
        in_specs=[
            pl.BlockSpec((MOE_BLOCK * ROW_SUB, LANES),
                         lambda i, be, nv: (jnp.minimum(i, jnp.maximum(nv[0] - 1, 0)), 0)),
            pl.BlockSpec((1, 1, D_MODEL, 2 * D_EXPERT), lambda i, be, nv: (layer, be[i], 0, 0)),
            pl.BlockSpec((1, 1, 1, 2 * D_EXPERT), lambda i, be, nv: (layer, be[i], 0, 0)),
            pl.BlockSpec((1, 1, D_EXPERT, D_MODEL), lambda i, be, nv: (layer, be[i], 0, 0)),
            pl.BlockSpec((1, 1, 1, D_MODEL), lambda i, be, nv: (layer, be[i], 0, 0)),
        ],
        out_specs=pl.BlockSpec((MOE_BLOCK * ROW_SUB, LANES), lambda i, be, nv: (i, 0)),
        scratch_shapes=[
            pltpu.VMEM((D_MODEL, 2 * D_EXPERT), BF16),
            pltpu.VMEM((D_EXPERT, D_MODEL), BF16),
        ],
    )
    return pl.pallas_call(
        _moe_kernel,
        grid_spec=grid_spec,
        out_shape=jax.ShapeDtypeStruct((n_blocks * MOE_BLOCK * ROW_SUB, LANES), F32),
        compiler_params=pltpu.CompilerParams(
            dimension_semantics=("arbitrary",), vmem_limit_bytes=VMEM_LIMIT),
    )(block_expert, n_valid, xs, w_up, b_up.reshape(depth, N_EXPERTS, 1, -1), w_down,
      b_down.reshape(depth, N_EXPERTS, 1, -1))


def _combine_kernel(alpha, start_ref, pos_ref, gate_ref, tbase_ref, tcnt_ref, tbase_n_ref, tcnt_n_ref, y_ref, x_ref,
                    mod_ref, lng_ref, lnb_ref, o_ref, stage, frow, sem):
    i = pl.program_id(0)
    nt = pl.num_programs(0)
    tm = ROW_TILE
    slot = i % 2

    def fetch(cnt_ref, base_ref, s):
        def copy(stage_row, slot_row, n):
            pltpu.make_async_copy(_rows(y_ref, (), slot_row, n), _rows(stage, (s,), stage_row, n),
                                  sem.at[s]).start()
        _expert_runs(cnt_ref, base_ref, start_ref, copy)

    @pl.when(i == 0)
    def _():
        fetch(tcnt_ref, tbase_ref, 0)

    @pl.when(i + 1 < nt)
    def _():
        fetch(tcnt_n_ref, tbase_n_ref, 1 - slot)

    pltpu.make_async_copy(stage.at[slot], stage.at[slot], sem.at[slot]).wait()

    def mix(r, carry):
        acc = None
        for kk in range(TOP_K):
            p = pos_ref[0, kk, r]
            term = gate_ref[0, kk, r] * stage[slot, pl.ds(pl.multiple_of(p * ROW_SUB, ROW_SUB), ROW_SUB), :]
            acc = term if acc is None else acc + term
        frow[pl.ds(pl.multiple_of(r * ROW_SUB, ROW_SUB), ROW_SUB), :] = acc
        return carry
    lax.fori_loop(0, tm, mix, 0, unroll=8)

    m = mod_ref[0]
    f = _load_row_tiles(frow, (), tm)
    o_ref[...] = _layer_norm(alpha * x_ref[...] + m[5:6] * f, lng_ref[...], lnb_ref[...])


def _moe_combine(alpha, route, ys, x1, mod_l, ln_g, ln_b, n_ctx_rows, seq, drop_ctx):
    t = x1.shape[0]
    tm = ROW_TILE
    nt = t // tm
    skip = n_ctx_rows // tm if drop_ctx else 0
    cur = lambda width: pl.BlockSpec((1, 1, width), lambda i, *_: (i, 0, 0), memory_space=pltpu.SMEM)
    per_token = pl.BlockSpec((1, TOP_K, tm), lambda i, *_: (i, 0, 0), memory_space=pltpu.SMEM)
    nxt = lambda width: pl.BlockSpec((1, 1, width), lambda i, *_: (jnp.minimum(i + 1, nt - 1), 0, 0),
                                     memory_space=pltpu.SMEM)
    grid_spec = pltpu.PrefetchScalarGridSpec(
        num_scalar_prefetch=1,
        grid=(nt,),
        in_specs=[
            per_token, per_token, cur(N_EXPERTS), cur(N_EXPERTS), nxt(N_EXPERTS), nxt(N_EXPERTS),
            pl.BlockSpec(memory_space=pl.ANY),
            pl.BlockSpec((tm, D_MODEL), lambda i, *_: (i, 0)),
            pl.BlockSpec((1, 6, D_MODEL), lambda i, *_: (_segment(i * tm, n_ctx_rows, seq), 0, 0)),
            pl.BlockSpec((1, D_MODEL), lambda i, *_: (0, 0)),
            pl.BlockSpec((1, D_MODEL), lambda i, *_: (0, 0)),
        ],
        out_specs=pl.BlockSpec((tm, D_MODEL), lambda i, *_: (jnp.maximum(i - skip, 0), 0)),
        scratch_shapes=[
            pltpu.VMEM((2, tm * TOP_K * ROW_SUB, LANES), F32),
            pltpu.VMEM((tm * ROW_SUB, LANES), F32),
            pltpu.SemaphoreType.DMA((2,)),
        ],
    )
    return pl.pallas_call(
        functools.partial(_combine_kernel, alpha),
        grid_spec=grid_spec,
        out_shape=jax.ShapeDtypeStruct((t - skip * tm, D_MODEL), F32),
        compiler_params=pltpu.CompilerParams(
            dimension_semantics=("arbitrary",), vmem_limit_bytes=VMEM_LIMIT),
    )(route['start'], route['pos'], route['gates'], route['tile_base'], route['tile_cnt'], route['tile_base'],
      route['tile_cnt'], ys, x1, mod_l, ln_g.reshape(1, -1), ln_b.reshape(1, -1))


def kernel(x, c, ctx, c_ctx, w_mod, b_mod, w_in, sgu_ln_g, sgu_ln_b, sgu_w, sgu_b, gla_gate_up, gla_gate_b,
           gla_norm_g, s5_lam_re, s5_lam_im, s5_log_dt, s5_b_re, s5_b_im, s5_c_re, s5_c_im, s5_d, s5_glu_w,
           s5_glu_b, w_out, ln_g, ln_b, router_w, router_b, w_up, b_up, w_down, b_down):
    bsz, seq, d = x.shape
    ctx_len = ctx.shape[1]
    depth = w_in.shape[0]
    alpha = float((2 * depth) ** 0.25)
    n_ctx_rows = bsz * ctx_len
    t = n_ctx_rows + bsz * seq
    assert d == D_MODEL and bsz + 1 <= 8
    assert n_ctx_rows % ROW_TILE == 0 and seq % ROW_TILE == 0
    assert ctx_len % SEQ_TILE == 0 and seq % SEQ_TILE == 0

    xa = jnp.concatenate([ctx.reshape(n_ctx_rows, d), x.reshape(bsz * seq, d)], axis=0)
    cvec = jnp.zeros((8, d), F32).at[0].set(c_ctx).at[1:1 + bsz].set(c)
    mod = _modulation(cvec, w_mod, b_mod).reshape(depth, 8, 6, d)

    o = np.cumsum((0, A_WIDTH, A_WIDTH, B_QK, B_QK, B_WIDTH, B_WIDTH, 2 * GATE_RANK, C_WIDTH))
    w_in_r = jnp.concatenate(
        [w_in[:, :, o[0]:o[6]], w_in[:, :, o[7]:o[8]], w_in[:, :, o[6]:o[7]],
         jnp.zeros((depth, d, N_IN_PAD - int(o[8])), w_in.dtype)], axis=-1).astype(BF16)
    w_out_b = w_out.astype(BF16)

    mats = _s5_matrices(s5_lam_re, s5_lam_im, s5_log_dt, s5_b_re, s5_b_im, s5_c_re, s5_c_im)
    rw = jnp.zeros((depth, D_MODEL, LANES), F32).at[:, :, :N_EXPERTS].set(router_w)
    rw_hi = rw.astype(BF16)
    rw_lo = (rw - rw_hi.astype(F32)).astype(BF16)
    rb = jnp.zeros((depth, 1, LANES), F32).at[:, 0, :N_EXPERTS].set(router_b)

    n_assign = t * TOP_K
    n_blocks = -(-(n_assign + N_EXPERTS * (MOE_BLOCK - 1)) // MOE_BLOCK)

    for l in range(depth):
        mod_l = mod[l]
        proj, u5 = _in_projection(xa, mod_l, w_in_r[l], n_ctx_rows, seq)
        a_out = _spatial_gate(proj, sgu_ln_g[l], sgu_ln_b[l], sgu_w[l], sgu_b[l])
        o_fwd, o_bwd = _gla_sweep(proj, gla_gate_up[l], gla_gate_b[l], bsz, ctx_len, seq)
        y_s5 = _s5_scan(u5, mats, l, bsz, ctx_len, seq)
        x1, h2, logits = _out_projection(alpha, a_out, o_fwd, o_bwd, gla_norm_g[l], y_s5, proj, xa, mod_l, s5_d[l],
                                         s5_glu_w[l], s5_glu_b[l], w_out_b[l], ln_g[l, 0], ln_b[l, 0], rw_hi[l],
                                         rw_lo[l], rb[l], n_ctx_rows, seq)
        route, block_expert, n_valid = _routing(logits, n_blocks)
        xs = _moe_dispatch(h2, route, n_blocks)
        ys = _moe_experts(l, xs, block_expert, n_valid, w_up, b_up, w_down, b_down)
        xa = _moe_combine(alpha, route, ys, x1, mod_l, ln_g[l, 1], ln_b[l, 1], n_ctx_rows, seq,
                          drop_ctx=(l == depth - 1))
    return xa.reshape(bsz, seq, d)
```

```python
import functools

import numpy as np
import jax
import jax.numpy as jnp
from jax import lax
from jax.experimental import pallas as pl
from jax.experimental.pallas import tpu as pltpu

F32 = jnp.float32
BF16 = jnp.bfloat16
HIGHEST = lax.Precision.HIGHEST

D_MODEL = 1024
CHUNK = 128
A_HEADS = 4
A_HEAD_DIM = 64
A_WIDTH = 256
B_HEADS = 4
B_DK = 64
B_DV = 128
B_QK = 256
B_WIDTH = 512
GATE_RANK = 16
GATE_TAU = 16.0
GLA_CHUNK = 64
S5_GROUPS = 16
S5_GROUP_CH = 16
S5_STATE = 64
C_WIDTH = 256
N_EXPERTS = 32
TOP_K = 4
D_EXPERT = 1024
SWIGLU_LIMIT = 7.0
SWIGLU_ALPHA = 1.702
LN_EPS = 1e-5

N_IN_PAD = 2432
COL_GL = 2304
COL_S5 = 2048

ROW_TILE = 512
SEQ_TILE = 256
S5_T = 16
S5_QUARTER = 4
MOE_BLOCK = 512
VMEM_LIMIT = 56 * 1024 * 1024


def _layer_norm(x, g, b):
    mu = jnp.mean(x, axis=-1, keepdims=True)
    xc = x - mu
    var = jnp.mean(xc * xc, axis=-1, keepdims=True)
    return xc * lax.rsqrt(var + LN_EPS) * g + b


LANES = 128
ROW_SUB = D_MODEL // LANES


def _store_row_tiles(ref, lead, val):
    n = val.shape[0]
    for j in range(ROW_SUB):
        ref[lead + (pl.ds(j, n, stride=ROW_SUB), slice(None))] = val[:, j * LANES:(j + 1) * LANES]


def _load_row_tiles(ref, lead, n):
    return jnp.concatenate(
        [ref[lead + (pl.ds(j, n, stride=ROW_SUB), slice(None))] for j in range(ROW_SUB)], axis=1)


def _row_tile(ref, lead, r):
    return ref.at[lead + (pl.ds(pl.multiple_of(r * ROW_SUB, ROW_SUB), ROW_SUB), slice(None))]


def _segment(row0, n_ctx_rows, seq):
    return jnp.where(row0 < n_ctx_rows, 0, 1 + (row0 - n_ctx_rows) // seq)


def _mod_kernel(c_ref, w_ref, b_ref, o_ref):
    c = c_ref[...]
    s = c * jax.nn.sigmoid(c)
    o_ref[0] = jnp.dot(s, w_ref[0], precision=HIGHEST, preferred_element_type=F32) + b_ref[0]


def _modulation(cvec, w_mod, b_mod):
    depth = w_mod.shape[0]
    n6 = w_mod.shape[2]
    tn = 1024
    return pl.pallas_call(
        _mod_kernel,
        grid=(depth, n6 // tn),
        in_specs=[
            pl.BlockSpec((8, D_MODEL), lambda l, j: (0, 0)),
            pl.BlockSpec((1, D_MODEL, tn), lambda l, j: (l, 0, j)),
            pl.BlockSpec((1, 1, tn), lambda l, j: (l, 0, j)),
        ],
        out_specs=pl.BlockSpec((1, 8, tn), lambda l, j: (l, 0, j)),
        out_shape=jax.ShapeDtypeStruct((depth, 8, n6), F32),
        compiler_params=pltpu.CompilerParams(
            dimension_semantics=("arbitrary", "arbitrary"), vmem_limit_bytes=VMEM_LIMIT),
    )(cvec, w_mod, b_mod.reshape(depth, 1, n6))


def _inproj_kernel(x_ref, mod_ref, w_ref, o_ref, u5_ref, s5_scr):
    m = mod_ref[0]
    h = x_ref[...] * (1.0 + m[1:2]) + m[0:1]
    res = jnp.dot(h.astype(BF16), w_ref[...], preferred_element_type=F32)
    o_ref[...] = res.astype(o_ref.dtype)
    n_row = x_ref.shape[0] // S5_T
    per_tile = LANES // S5_GROUP_CH
    for half in range(C_WIDTH // LANES):
        s5_scr[half] = res[:, COL_S5 + half * LANES:COL_S5 + (half + 1) * LANES]
    at_step = [[s5_scr[half, pl.ds(step, n_row, stride=S5_T), :] for half in range(C_WIDTH // LANES)]
               for step in range(S5_T)]
    for g in range(S5_GROUPS):
        lanes = slice((g % per_tile) * S5_GROUP_CH, (g % per_tile + 1) * S5_GROUP_CH)
        u5_ref[g] = jnp.concatenate(
            [at_step[step][g // per_tile][:, lanes] for step in range(S5_T)], axis=1).astype(BF16)


def _in_projection(x, mod_l, w_in_b, n_ctx_rows, seq):
    t = x.shape[0]
    tm = ROW_TILE
    return pl.pallas_call(
        _inproj_kernel,
        grid=(t // tm,),
        in_specs=[
            pl.BlockSpec((tm, D_MODEL), lambda i: (i, 0)),
            pl.BlockSpec((1, 6, D_MODEL), lambda i: (_segment(i * tm, n_ctx_rows, seq), 0, 0)),
            pl.BlockSpec((D_MODEL, N_IN_PAD), lambda i: (0, 0)),
        ],
        out_specs=[
            pl.BlockSpec((tm, N_IN_PAD), lambda i: (i, 0)),
            pl.BlockSpec((S5_GROUPS, tm // S5_T, S5_T * S5_GROUP_CH), lambda i: (0, i, 0)),
        ],
        out_shape=[
            jax.ShapeDtypeStruct((t, N_IN_PAD), BF16),
            jax.ShapeDtypeStruct((S5_GROUPS, t // S5_T, S5_T * S5_GROUP_CH), BF16),
        ],
        scratch_shapes=[pltpu.VMEM((C_WIDTH // LANES, tm, LANES), F32)],
        compiler_params=pltpu.CompilerParams(
            dimension_semantics=("arbitrary",), vmem_limit_bytes=VMEM_LIMIT),
    )(x, mod_l, w_in_b)


def _sgu_kernel(uv_ref, g_ref, b_ref, w_ref, bias_ref, o_ref):
    tm = uv_ref.shape[0]
    u = jax.nn.gelu(uv_ref[:, :A_WIDTH].astype(F32))
    v = _layer_norm(jax.nn.gelu(uv_ref[:, A_WIDTH:].astype(F32)), g_ref[...], b_ref[...]).astype(BF16)
    head = lax.broadcasted_iota(jnp.int32, (1, A_WIDTH), 1) // A_HEAD_DIM
    for c in range(tm // CHUNK):
        rows = slice(c * CHUNK, (c + 1) * CHUNK)
        vc = v[rows]
        acc = bias_ref[...]
        for h in range(A_HEADS):
            r = jnp.dot(w_ref[h], vc, preferred_element_type=F32)
            acc = acc + jnp.where(head == h, r, 0.0)
        o_ref[rows, :] = (u[rows] * acc).astype(BF16)


def _spatial_gate(proj, ln_g, ln_b, w_s, b_s):
    t = proj.shape[0]
    tm = ROW_TILE
    bias = jnp.repeat(b_s.T, A_HEAD_DIM, axis=1)
    return pl.pallas_call(
        _sgu_kernel,
        grid=(t // tm,),
        in_specs=[
            pl.BlockSpec((tm, 2 * A_WIDTH), lambda i: (i, 0)),
            pl.BlockSpec((1, A_WIDTH), lambda i: (0, 0)),
            pl.BlockSpec((1, A_WIDTH), lambda i: (0, 0)),
            pl.BlockSpec((A_HEADS, CHUNK, CHUNK), lambda i: (0, 0, 0)),
            pl.BlockSpec((CHUNK, A_WIDTH), lambda i: (0, 0)),
        ],
        out_specs=pl.BlockSpec((tm, A_WIDTH), lambda i: (i, 0)),
        out_shape=jax.ShapeDtypeStruct((t, A_WIDTH), BF16),
        compiler_params=pltpu.CompilerParams(
            dimension_semantics=("arbitrary",), vmem_limit_bytes=VMEM_LIMIT),
    )(proj, ln_g.reshape(1, -1), ln_b.reshape(1, -1), w_s.astype(BF16), bias)


_NT = (((1,), (1,)), ((), ()))
_TN = (((0,), (0,)), ((), ()))


def _gla_direction(backward, q_ref, k_ref, v_ref, gl_ref, gup_ref, gb_ref, o_ref, s_ref):
    c_len = GLA_CHUNK
    n_rows = q_ref.shape[0]
    n_chunks = n_rows // c_len

    row = lax.broadcasted_iota(jnp.int32, (n_rows, n_rows), 0)
    col = lax.broadcasted_iota(jnp.int32, (n_rows, n_rows), 1)
    same = (row // c_len) == (col // c_len)
    if backward:
        tri = jnp.where(same & (col >= row), 1.0, 0.0).astype(BF16)
        keep = same & (col > row)
        i_last, i_mid = 0, c_len - 1 - c_len // 2
        lo = GATE_RANK
    else:
        tri = jnp.where(same & (col <= row), 1.0, 0.0).astype(BF16)
        keep = same & (col <= row)
        i_last, i_mid = c_len - 1, c_len // 2
        lo = 0

    gl = gl_ref[:, lo:lo + GATE_RANK]
    z = (jnp.dot(gl, gup_ref[0], preferred_element_type=F32) + jnp.dot(gl, gup_ref[1], preferred_element_type=F32)
         + gb_ref[...])
    la = jax.nn.log_sigmoid(z) / GATE_TAU
    l1 = la.astype(BF16)
    r1 = la - l1.astype(F32)
    l2 = r1.astype(BF16)
    l3 = (r1 - l2.astype(F32)).astype(BF16)
    b = (jnp.dot(tri, l1, preferred_element_type=F32) + jnp.dot(tri, l2, preferred_element_type=F32)
         + jnp.dot(tri, l3, preferred_element_type=F32))

    def per_chunk(index):
        return jnp.concatenate(
            [jnp.broadcast_to(b[c * c_len + index:c * c_len + index + 1], (c_len, B_QK))
             for c in range(n_chunks)], axis=0)

    b_last = per_chunk(i_last)
    b_mid = per_chunk(i_mid)
    q = q_ref[...].astype(F32) * (B_DK ** -0.5)
    k = k_ref[...].astype(F32)
    q_mid = (q * jnp.exp(b - b_mid)).astype(BF16)
    k_mid = (k * jnp.exp(b_mid - b)).astype(BF16)
    q_in = (q * jnp.exp(b)).astype(BF16)
    k_out = (k * jnp.exp(b_last - b)).astype(BF16)
    order = range(n_chunks - 1, -1, -1) if backward else range(n_chunks)
    v_all = v_ref[...].astype(BF16)
    intra = []
    for h in range(B_HEADS):
        hk = slice(h * B_DK, (h + 1) * B_DK)
        hv = slice(h * B_DV, (h + 1) * B_DV)
        sc = lax.dot_general(q_mid[:, hk], k_mid[:, hk], _NT, preferred_element_type=F32)
        sc = jnp.where(keep, sc, 0.0).astype(BF16)
        intra.append(jnp.dot(sc, v_all[:, hv], preferred_element_type=F32))
    o_intra = jnp.concatenate(intra, axis=1)
    own = (lax.broadcasted_iota(jnp.int32, (B_WIDTH, B_QK), 0) // B_DV
           == lax.broadcasted_iota(jnp.int32, (B_WIDTH, B_QK), 1) // B_DK)
    state = s_ref[...]
    for c in order:
        rows = slice(c * c_len, (c + 1) * c_len)
        o_ref[rows, :] = o_intra[rows] + lax.dot_general(q_in[rows], state.astype(BF16), _NT,
                                                         preferred_element_type=F32)
        decay = jnp.exp(b[c * c_len + i_last:c * c_len + i_last + 1])
        update = lax.dot_general(v_all[rows], k_out[rows], _TN, preferred_element_type=F32)
        state = jnp.where(own, state * decay + update, 0.0)
    s_ref[...] = state


def _gla_kernel(qf, kf, vf, glf, qb, kb, vb, glb, gup_ref, gb_ref, of_ref, ob_ref, s_ref):
    @pl.when(pl.program_id(1) == 0)
    def _():
        s_ref[...] = jnp.zeros_like(s_ref)

    _gla_direction(False, qf, kf, vf, glf, gup_ref.at[0], gb_ref.at[0], of_ref, s_ref.at[0])
    _gla_direction(True, qb, kb, vb, glb, gup_ref.at[1], gb_ref.at[1], ob_ref, s_ref.at[1])


def _gla_block(backward, bsz, nctx_blk, nlat_blk, b, j):
    if backward:
        ctx_i = b * nctx_blk + (nctx_blk - 1 - j)
        lat_i = bsz * nctx_blk + b * nlat_blk + (nlat_blk - 1 - (j - nctx_blk))
    else:
        ctx_i = b * nctx_blk + j
        lat_i = bsz * nctx_blk + b * nlat_blk + (j - nctx_blk)
    return jnp.where(j < nctx_blk, ctx_i, lat_i)


def _gla_sweep(proj, gate_up, gate_b, bsz, ctx_len, seq):
    t = proj.shape[0]
    r = SEQ_TILE
    nctx_blk, nlat_blk = ctx_len // r, seq // r
    up_hi = gate_up.astype(BF16)
    in_specs = []
    for backward in (False, True):
        blk = functools.partial(_gla_block, backward, bsz, nctx_blk, nlat_blk)
        in_specs += [
            pl.BlockSpec((r, B_QK), lambda b, j, blk=blk: (blk(b, j), 2)),
            pl.BlockSpec((r, B_QK), lambda b, j, blk=blk: (blk(b, j), 3)),
            pl.BlockSpec((r, B_WIDTH), lambda b, j, blk=blk: (blk(b, j), 2)),
            pl.BlockSpec((r, 128), lambda b, j, blk=blk: (blk(b, j), COL_GL // 128)),
        ]
    in_specs += [
        pl.BlockSpec((2, 2, GATE_RANK, B_QK), lambda b, j: (0, 0, 0, 0)),
        pl.BlockSpec((2, 1, B_QK), lambda b, j: (0, 0, 0)),
    ]
    fwd = functools.partial(_gla_block, False, bsz, nctx_blk, nlat_blk)
    bwd = functools.partial(_gla_block, True, bsz, nctx_blk, nlat_blk)
    return pl.pallas_call(
        _gla_kernel,
        grid=(bsz, nctx_blk + nlat_blk),
        in_specs=in_specs,
        out_specs=[
            pl.BlockSpec((r, B_WIDTH), lambda b, j: (fwd(b, j), 0)),
            pl.BlockSpec((r, B_WIDTH), lambda b, j: (bwd(b, j), 0)),
        ],
        out_shape=[jax.ShapeDtypeStruct((t, B_WIDTH), F32), jax.ShapeDtypeStruct((t, B_WIDTH), F32)],
        scratch_shapes=[pltpu.VMEM((2, B_WIDTH, B_QK), F32)],
        compiler_params=pltpu.CompilerParams(
            dimension_semantics=("arbitrary", "arbitrary"), vmem_limit_bytes=VMEM_LIMIT),
    )(*([proj] * 8), jnp.stack([up_hi, (gate_up - up_hi.astype(F32)).astype(BF16)], axis=1),
      gate_b.reshape(2, 1, -1))


def _s5_matrices(lam_re, lam_im, log_dt, b_re, b_im, c_re, c_im):
    tc, hh, width = S5_T, S5_GROUP_CH, S5_T * S5_GROUP_CH
    lam = lax.complex(lam_re.astype(F32), lam_im.astype(F32))
    dt = jnp.exp(log_dt.astype(F32))
    bm = lax.complex(b_re.astype(F32), b_im.astype(F32))
    cm = lax.complex(c_re.astype(F32), c_im.astype(F32))
    ldt = lam * dt[..., None]
    lam_bar = jnp.exp(ldt)
    b_bar = ((lam_bar - 1.0) / lam)[..., None] * bm
    steps = jnp.arange(tc + 1, dtype=F32)
    pw = jnp.exp(ldt[:, :, None] * steps[None, None, :, None, None])
    n_l, g_n = lam.shape[0], lam.shape[2]

    kern = jnp.real(jnp.einsum('ldgip,ldtgp,ldgpj->ldgjti', cm, pw[:, :, :tc], b_bar))
    k_f = kern[:, 0].reshape(n_l, g_n, hh, width)
    k_b = kern[:, 1, :, :, ::-1].reshape(n_l, g_n, hh, width)
    zero = jnp.zeros_like(k_f)
    wide_f = jnp.concatenate([zero, k_f], axis=-1)
    wide_b = jnp.concatenate([k_b, zero], axis=-1)
    mt = jnp.stack([wide_f[..., width - s * hh:2 * width - s * hh]
                    + wide_b[..., (tc - 1 - s) * hh:(tc - 1 - s) * hh + width] for s in range(tc)], axis=2)
    mt = mt.reshape(n_l, g_n, width, width)

    b_t = jnp.swapaxes(b_bar, -1, -2)
    pw_in = jnp.stack([pw[:, 0, :tc][:, ::-1], pw[:, 1, :tc]], axis=1)
    w = jnp.swapaxes(pw_in, 2, 3)[:, :, :, :, None, :] * b_t[:, :, :, None, :, :]
    w = w.reshape(n_l, 2, g_n, width, S5_STATE)
    re, im = jnp.real(w), jnp.imag(w)
    qt = jnp.concatenate([re[:, 0], im[:, 0], im[:, 0], re[:, 0], re[:, 1], im[:, 1], im[:, 1], re[:, 1]], axis=-1)

    pw_out = jnp.stack([pw[:, 0, 1:], pw[:, 1, 1:][:, ::-1]], axis=1)
    pw_out = jnp.repeat(jnp.transpose(pw_out, (0, 1, 3, 4, 2)), hh, axis=-1)
    c_t = jnp.tile(jnp.swapaxes(cm, -1, -2), (1, 1, 1, 1, tc))
    w = c_t * pw_out
    pt = jnp.concatenate([jnp.real(w[:, 0]), -jnp.imag(w[:, 0]), jnp.real(w[:, 1]), -jnp.imag(w[:, 1])], axis=2)

    a = pw[:, :, tc]
    ar, ai = jnp.real(a), jnp.imag(a)
    a1 = jnp.concatenate([ar, ar], axis=-1).reshape(n_l, 2, -1)
    a2 = jnp.concatenate([-ai, ai], axis=-1).reshape(n_l, 2, -1)
    a3 = jnp.concatenate([ai, -ai], axis=-1).reshape(n_l, 2, -1)
    ac = jnp.stack([a1, a2, a3], axis=2)
    return mt.astype(BF16), qt.astype(BF16), pt.astype(BF16), ac.reshape(n_l, 2, 3, 1, -1)


def _s5_kernel(uc_ref, ul_ref, mt_ref, qt_ref, pt_ref, ac_ref, yc_ref, yl_ref, ef, esf, eb, esb):
    nq = uc_ref.shape[0]
    nc_ctx, nc_lat = uc_ref.shape[1], ul_ref.shape[1]
    n = nc_ctx + nc_lat
    w = 2 * S5_STATE
    for gi in range(nq):
        lanes = slice(gi * w, (gi + 1) * w)
        rc = jnp.dot(uc_ref[gi], qt_ref[0, gi], preferred_element_type=F32)
        rl = jnp.dot(ul_ref[gi], qt_ref[0, gi], preferred_element_type=F32)
        ef[0:nc_ctx, lanes] = rc[:, 0:w]
        esf[0:nc_ctx, lanes] = rc[:, w:2 * w]
        ef[nc_ctx:n, lanes] = rl[:, 0:w]
        esf[nc_ctx:n, lanes] = rl[:, w:2 * w]
        eb[0:nc_lat, lanes] = rl[:, 2 * w:3 * w]
        esb[0:nc_lat, lanes] = rl[:, 3 * w:4 * w]
        eb[nc_lat:n, lanes] = rc[:, 2 * w:3 * w]
        esb[nc_lat:n, lanes] = rc[:, 3 * w:4 * w]

    a1f, a2f, a3f = ac_ref[0, 0, 0], ac_ref[0, 0, 1], ac_ref[0, 0, 2]
    a1b, a2b, a3b = ac_ref[0, 1, 0], ac_ref[0, 1, 1], ac_ref[0, 1, 2]

    def body(i, carry):
        hf, hsf, hb, hsb = carry
        rf = pl.ds(i, 1)
        rb = pl.ds(n - 1 - i, 1)
        e_f, es_f = ef[rf, :], esf[rf, :]
        e_b, es_b = eb[rb, :], esb[rb, :]
        ef[rf, :] = hf
        eb[rb, :] = hb
        return (a1f * hf + a2f * hsf + e_f, a1f * hsf + a3f * hf + es_f,
                a1b * hb + a2b * hsb + e_b, a1b * hsb + a3b * hb + es_b)

    zero = jnp.zeros((1, nq * w), F32)
    lax.fori_loop(0, n, body, (zero, zero, zero, zero))

    for gi in range(nq):
        lanes = slice(gi * w, (gi + 1) * w)
        hc = jnp.concatenate([ef[0:nc_ctx, lanes], eb[nc_lat:n, lanes]], axis=1).astype(BF16)
        hl = jnp.concatenate([ef[nc_ctx:n, lanes], eb[0:nc_lat, lanes]], axis=1).astype(BF16)
        yc_ref[gi] = (jnp.dot(uc_ref[gi], mt_ref[0, gi], preferred_element_type=F32)
                      + jnp.dot(hc, pt_ref[0, gi], preferred_element_type=F32))
        yl_ref[gi] = (jnp.dot(ul_ref[gi], mt_ref[0, gi], preferred_element_type=F32)
                      + jnp.dot(hl, pt_ref[0, gi], preferred_element_type=F32))


def _s5_scan(u5, mats, layer, bsz, ctx_len, seq):
    mt, qt, pt, ac = mats
    g_n, tc, hh = S5_GROUPS, S5_T, S5_GROUP_CH
    n_ctx_rows = bsz * ctx_len
    uc, ul = u5[:, :n_ctx_rows // tc], u5[:, n_ctx_rows // tc:]
    nc_ctx, nc_lat = ctx_len // tc, seq // tc
    nq = S5_QUARTER
    wq = nq * 2 * S5_STATE
    n = nc_ctx + nc_lat
    yc, yl = pl.pallas_call(
        _s5_kernel,
        grid=(bsz, g_n // nq),
        in_specs=[
            pl.BlockSpec((nq, nc_ctx, tc * hh), lambda b, qi: (qi, b, 0)),
            pl.BlockSpec((nq, nc_lat, tc * hh), lambda b, qi: (qi, b, 0)),
            pl.BlockSpec((1, nq, tc * hh, tc * hh), lambda b, qi: (layer, qi, 0, 0)),
            pl.BlockSpec((1, nq, tc * hh, 8 * S5_STATE), lambda b, qi: (layer, qi, 0, 0)),
            pl.BlockSpec((1, nq, 4 * S5_STATE, tc * hh), lambda b, qi: (layer, qi, 0, 0)),
            pl.BlockSpec((1, 2, 3, 1, wq), lambda b, qi: (layer, 0, 0, 0, qi)),
        ],
        out_specs=[
            pl.BlockSpec((nq, nc_ctx, tc * hh), lambda b, qi: (qi, b, 0)),
            pl.BlockSpec((nq, nc_lat, tc * hh), lambda b, qi: (qi, b, 0)),
        ],
        out_shape=[
            jax.ShapeDtypeStruct((g_n, bsz * nc_ctx, tc * hh), F32),
            jax.ShapeDtypeStruct((g_n, bsz * nc_lat, tc * hh), F32),
        ],
        scratch_shapes=[pltpu.VMEM((n, wq), F32) for _ in range(4)],
        compiler_params=pltpu.CompilerParams(
            dimension_semantics=("arbitrary", "arbitrary"), vmem_limit_bytes=VMEM_LIMIT),
    )(uc, ul, mt, qt, pt, ac)

    return jnp.concatenate([yc, yl], axis=1)


def _outproj_kernel(alpha, a_ref, of_ref, ob_ref, g_ref, ng_ref, y_ref, u_ref, x_ref, mod_ref, d_ref, gw_ref,
                    gb_ref, wo_ref, lng_ref, lnb_ref, rwh_ref, rwl_ref, rb_ref, x1_ref, h2_ref, lg_ref, y_scr):
    m = mod_ref[0]
    n_row = y_ref.shape[1]
    for step in range(S5_T):
        lanes = slice(step * S5_GROUP_CH, (step + 1) * S5_GROUP_CH)
        row = jnp.concatenate([y_ref[g][:, lanes] for g in range(S5_GROUPS)], axis=1)
        for half in range(C_WIDTH // LANES):
            y_scr[half, pl.ds(step, n_row, stride=S5_T), :] = row[:, half * LANES:(half + 1) * LANES]
    y_nat = jnp.concatenate([y_scr[half] for half in range(C_WIDTH // LANES)], axis=1)
    heads = []
    for h in range(B_HEADS):
        hv = slice(h * B_DV, (h + 1) * B_DV)
        o = of_ref[:, hv] + ob_ref[:, hv]
        heads.append(o * lax.rsqrt(jnp.mean(o * o, axis=-1, keepdims=True) + LN_EPS))
    gate = g_ref[...].astype(F32)
    gla = (jnp.concatenate(heads, axis=1) * ng_ref[...] * (gate * jax.nn.sigmoid(gate))).astype(BF16)
    y = jax.nn.gelu(y_nat + d_ref[...] * u_ref[...].astype(F32))
    s5 = y * jax.nn.sigmoid(jnp.dot(y.astype(BF16), gw_ref[...], preferred_element_type=F32) + gb_ref[...])
    mix = (jnp.dot(a_ref[...], wo_ref[0:A_WIDTH, :], preferred_element_type=F32)
           + jnp.dot(gla, wo_ref[A_WIDTH:A_WIDTH + B_WIDTH, :], preferred_element_type=F32)
           + jnp.dot(s5.astype(BF16), wo_ref[A_WIDTH + B_WIDTH:, :], preferred_element_type=F32))
    x1 = _layer_norm(alpha * x_ref[...] + m[2:3] * mix, lng_ref[...], lnb_ref[...])
    x1_ref[...] = x1
    h2 = x1 * (1.0 + m[4:5]) + m[3:4]
    _store_row_tiles(h2_ref, (), h2)
    h_hi = h2.astype(BF16)
    h_lo = (h2 - h_hi.astype(F32)).astype(BF16)
    lg_ref[...] = (jnp.dot(h_hi, rwh_ref[...], preferred_element_type=F32)
                   + jnp.dot(h_lo, rwh_ref[...], preferred_element_type=F32)
                   + jnp.dot(h_hi, rwl_ref[...], preferred_element_type=F32) + rb_ref[...])


def _out_projection(alpha, a_out, o_fwd, o_bwd, norm_g, y_s5, proj, x, mod_l, s5_d, glu_w, glu_b, w_out_b, ln_g,
                    ln_b, rw_hi, rw_lo, rb, n_ctx_rows, seq):
    t = x.shape[0]
    tm = ROW_TILE
    row = lambda i: (i, 0)
    fixed = lambda i: (0, 0)
    return pl.pallas_call(
        functools.partial(_outproj_kernel, alpha),
        grid=(t // tm,),
        in_specs=[
            pl.BlockSpec((tm, A_WIDTH), row),
            pl.BlockSpec((tm, B_WIDTH), row),
            pl.BlockSpec((tm, B_WIDTH), row),
            pl.BlockSpec((tm, B_WIDTH), lambda i: (i, 3)),
            pl.BlockSpec((1, B_WIDTH), fixed),
            pl.BlockSpec((S5_GROUPS, tm // S5_T, S5_T * S5_GROUP_CH), lambda i: (0, i, 0)),
            pl.BlockSpec((tm, C_WIDTH), lambda i: (i, COL_S5 // C_WIDTH)),
            pl.BlockSpec((tm, D_MODEL), row),
            pl.BlockSpec((1, 6, D_MODEL), lambda i: (_segment(i * tm, n_ctx_rows, seq), 0, 0)),
            pl.BlockSpec((1, C_WIDTH), fixed),
            pl.BlockSpec((C_WIDTH, C_WIDTH), fixed),
            pl.BlockSpec((1, C_WIDTH), fixed),
            pl.BlockSpec((D_MODEL, D_MODEL), fixed),
            pl.BlockSpec((1, D_MODEL), fixed),
            pl.BlockSpec((1, D_MODEL), fixed),
            pl.BlockSpec((D_MODEL, 128), fixed),
            pl.BlockSpec((D_MODEL, 128), fixed),
            pl.BlockSpec((1, 128), fixed),
        ],
        out_specs=[
            pl.BlockSpec((tm, D_MODEL), row),
            pl.BlockSpec((tm * ROW_SUB, LANES), row),
            pl.BlockSpec((tm, 128), row),
        ],
        out_shape=[
            jax.ShapeDtypeStruct((t, D_MODEL), F32),
            jax.ShapeDtypeStruct((t * ROW_SUB, LANES), F32),
            jax.ShapeDtypeStruct((t, 128), F32),
        ],
        scratch_shapes=[pltpu.VMEM((C_WIDTH // LANES, tm, LANES), F32)],
        compiler_params=pltpu.CompilerParams(
            dimension_semantics=("arbitrary",), vmem_limit_bytes=VMEM_LIMIT),
    )(a_out, o_fwd, o_bwd, proj, norm_g.reshape(1, -1), y_s5, proj, x, mod_l, s5_d.reshape(1, -1), glu_w.astype(BF16), glu_b.reshape(1, -1),
      w_out_b, ln_g.reshape(1, -1), ln_b.reshape(1, -1), rw_hi, rw_lo, rb)


def _route_kernel(lg_ref, gate_ref, pos_ref, tbase_ref, tcnt_ref, cnt_ref, base, before, below):
    tm = lg_ref.shape[0]

    @pl.when(pl.program_id(0) == 0)
    def _():
        base[...] = jnp.zeros_like(base)
        r = lax.broadcasted_iota(jnp.int32, (tm, tm), 0)
        c = lax.broadcasted_iota(jnp.int32, (tm, tm), 1)
        before[...] = jnp.where(r < c, 1.0, 0.0).astype(BF16)
        r = lax.broadcasted_iota(jnp.int32, (N_EXPERTS, N_EXPERTS), 0)
        c = lax.broadcasted_iota(jnp.int32, (N_EXPERTS, N_EXPERTS), 1)
        below[...] = jnp.where(c < r, 1.0, 0.0)

    logit = jnp.transpose(lg_ref[...])[:N_EXPERTS]
    eid = lax.broadcasted_iota(jnp.int32, (N_EXPERTS, tm), 0)
    vals, hots = [], []
    work = logit
    for kk in range(TOP_K):
        m = jnp.max(work, axis=0, keepdims=True)
        ix = jnp.min(jnp.where(work == m, eid, N_EXPERTS), axis=0, keepdims=True)
        hot = eid == ix
        vals.append(m)
        hots.append(hot)
        work = jnp.where(hot, -jnp.inf, work)
    ex = [jnp.exp(v - vals[0]) for v in vals]
    den = ex[0] + ex[1] + ex[2] + ex[3]
    member = jnp.zeros((N_EXPERTS, tm), F32)
    for kk in range(TOP_K):
        gate_ref[kk:kk + 1, :] = ex[kk] / den
        member = member + jnp.where(hots[kk], 1.0, 0.0)
    tile_cnt = jnp.broadcast_to(jnp.sum(member, axis=1, keepdims=True), (N_EXPERTS, LANES))
    group_off = jnp.dot(below[...], tile_cnt, precision=HIGHEST, preferred_element_type=F32)[:, 0:1]
    in_group = jnp.dot(member.astype(BF16), before[...], preferred_element_type=F32)
    for kk in range(TOP_K):
        pos_ref[kk:kk + 1, :] = jnp.sum(jnp.where(hots[kk], group_off + in_group, 0.0), axis=0,
                                        keepdims=True).astype(jnp.int32)
    tbase_ref[...] = jnp.broadcast_to(base[...], tbase_ref.shape)
    tcnt_ref[...] = tile_cnt
    total = base[...] + tile_cnt[:, 0:1]
    base[...] = total
    cnt_ref[...] = jnp.broadcast_to(total, cnt_ref.shape)


def _routing(logits, n_blocks):
    t = logits.shape[0]
    tm = ROW_TILE
    nt = t // tm
    gates, pos, tbase, tcnt, cnt = pl.pallas_call(
        _route_kernel,
        grid=(nt,),
        in_specs=[pl.BlockSpec((tm, LANES), lambda i: (i, 0))],
        out_specs=[
            pl.BlockSpec((TOP_K, tm), lambda i: (0, i)),
            pl.BlockSpec((TOP_K, tm), lambda i: (0, i)),
            pl.BlockSpec((N_EXPERTS, LANES), lambda i: (i, 0)),
            pl.BlockSpec((N_EXPERTS, LANES), lambda i: (i, 0)),
            pl.BlockSpec((N_EXPERTS, LANES), lambda i: (0, 0)),
        ],
        out_shape=[
            jax.ShapeDtypeStruct((TOP_K, t), F32),
            jax.ShapeDtypeStruct((TOP_K, t), jnp.int32),
            jax.ShapeDtypeStruct((nt * N_EXPERTS, LANES), F32),
            jax.ShapeDtypeStruct((nt * N_EXPERTS, LANES), F32),
            jax.ShapeDtypeStruct((N_EXPERTS, LANES), F32),
        ],
        scratch_shapes=[pltpu.VMEM((N_EXPERTS, 1), F32), pltpu.VMEM((tm, tm), BF16),
                        pltpu.VMEM((N_EXPERTS, N_EXPERTS), F32)],
        compiler_params=pltpu.CompilerParams(
            dimension_semantics=("arbitrary",), vmem_limit_bytes=VMEM_LIMIT),
    )(logits)
    counts = cnt[:, 0].astype(jnp.int32)
    padded = (counts + MOE_BLOCK - 1) // MOE_BLOCK * MOE_BLOCK
    padded_end = jnp.cumsum(padded)
    padded_start = (padded_end - padded).astype(jnp.int32)
    first_slot = jnp.arange(n_blocks, dtype=jnp.int32) * MOE_BLOCK
    block_expert = jnp.minimum(jnp.sum((padded_end[None, :] <= first_slot[:, None]).astype(jnp.int32), axis=1),
                               N_EXPERTS - 1).astype(jnp.int32)
    n_valid = (padded_end[-1] // MOE_BLOCK).astype(jnp.int32).reshape(1)
    pad_lo = (padded_start + counts).astype(jnp.int32)
    route = dict(
        gates=gates.T.reshape(nt, 1, tm * TOP_K),
        pos=pos.T.reshape(nt, 1, tm * TOP_K),
        tile_base=tbase[:, 0].astype(jnp.int32).reshape(nt, 1, N_EXPERTS),
        tile_cnt=tcnt[:, 0].astype(jnp.int32).reshape(nt, 1, N_EXPERTS),
        start=padded_start, pad_lo=pad_lo, pad_hi=padded_end.astype(jnp.int32))
    return route, block_expert, n_valid


def _expert_runs(tile_cnt_ref, tile_base_ref, start_ref, copy):
    off = 0
    for e in range(N_EXPERTS):
        n = tile_cnt_ref[0, 0, e]
        slot0 = start_ref[e] + tile_base_ref[0, 0, e]
        for bit in range(ROW_TILE.bit_length() - 1, -1, -1):
            size = 1 << bit
            done = (n >> (bit + 1)) << (bit + 1)

            @pl.when(((n >> bit) & 1) == 1)
            def _(off=off, done=done, slot0=slot0, size=size, queue=(e + bit) % 2):
                copy(off + done, slot0 + done, size, queue)
        off = off + n


def _rows(ref, lead, row0, n):
    return ref.at[lead + (pl.ds(pl.multiple_of(row0 * ROW_SUB, ROW_SUB), n * ROW_SUB), slice(None))]


def _dispatch_kernel(start_ref, lo_ref, hi_ref, pos_ref, tbase_ref, tcnt_ref, h_ref, o_ref, stage, zbuf, sem, zsem):
    i = pl.program_id(0)
    nt = pl.num_programs(0)
    tm = ROW_TILE
    slot = i % 2
    blk_rows = MOE_BLOCK * ROW_SUB
    n_blocks = o_ref.shape[0] // blk_rows

    def wait_stage(s):
        pltpu.make_async_copy(stage.at[s], stage.at[s], sem.at[s]).wait()

    @pl.when(i >= 2)
    def _():
        wait_stage(slot)

    def place(r, carry):
        row = h_ref[pl.ds(pl.multiple_of(r * ROW_SUB, ROW_SUB), ROW_SUB), :]
        for kk in range(TOP_K):
            p = pos_ref[0, 0, r * TOP_K + kk]
            stage[slot, pl.ds(pl.multiple_of(p * ROW_SUB, ROW_SUB), ROW_SUB), :] = row
        return carry
    lax.fori_loop(0, tm, place, 0, unroll=8)

    def copy(stage_row, slot_row, n, queue):
        pltpu.make_async_copy(_rows(stage, (slot,), stage_row, n), _rows(o_ref, (), slot_row, n),
                              sem.at[slot]).start(priority=queue)
    _expert_runs(tcnt_ref, tbase_ref, start_ref, copy)

    @pl.when(i == nt - 1)
    def _():
        wait_stage(slot)

        @pl.when(i >= 1)
        def _():
            wait_stage(1 - slot)
        zbuf[...] = jnp.zeros_like(zbuf)
        zrow = zbuf.at[pl.ds(0, ROW_SUB), :]
        for e in range(N_EXPERTS):
            def fill(s, carry):
                pltpu.make_async_copy(zrow, _row_tile(o_ref, (), s), zsem).start()
                return carry
            lax.fori_loop(lo_ref[e], hi_ref[e], fill, 0)
        for e in range(N_EXPERTS):
            def drain(s, carry):
                pltpu.make_async_copy(zrow, zrow, zsem).wait()
                return carry
            lax.fori_loop(lo_ref[e], hi_ref[e], drain, 0)
        used = hi_ref[N_EXPERTS - 1] // MOE_BLOCK

        def fill_block(j, carry):
            rows = pl.ds(pl.multiple_of(j * blk_rows, blk_rows), blk_rows)
            pltpu.make_async_copy(zbuf, o_ref.at[rows, :], zsem).start()
            return carry
        lax.fori_loop(used, n_blocks, fill_block, 0)

        def drain_block(j, carry):
            pltpu.make_async_copy(zbuf, zbuf, zsem).wait()
            return carry
        lax.fori_loop(used, n_blocks, drain_block, 0)


def _moe_dispatch(h2t, route, n_blocks):
    nt = route['pos'].shape[0]
    tm = ROW_TILE
    smem = lambda width: pl.BlockSpec((1, 1, width), lambda i, *_: (i, 0, 0), memory_space=pltpu.SMEM)
    grid_spec = pltpu.PrefetchScalarGridSpec(
        num_scalar_prefetch=3,
        grid=(nt,),
        in_specs=[
            smem(tm * TOP_K), smem(N_EXPERTS), smem(N_EXPERTS),
            pl.BlockSpec((tm * ROW_SUB, LANES), lambda i, *_: (i, 0)),
        ],
        out_specs=pl.BlockSpec(memory_space=pl.ANY),
        scratch_shapes=[
            pltpu.VMEM((2, tm * TOP_K * ROW_SUB, LANES), F32),
            pltpu.VMEM((MOE_BLOCK * ROW_SUB, LANES), F32),
            pltpu.SemaphoreType.DMA((2,)),
            pltpu.SemaphoreType.DMA(()),
        ],
    )
    return pl.pallas_call(
        _dispatch_kernel,
        grid_spec=grid_spec,
        out_shape=jax.ShapeDtypeStruct((n_blocks * MOE_BLOCK * ROW_SUB, LANES), F32),
        compiler_params=pltpu.CompilerParams(
            dimension_semantics=("arbitrary",), vmem_limit_bytes=VMEM_LIMIT),
    )(route['start'], route['pad_lo'], route['pad_hi'], route['pos'], route['tile_base'], route['tile_cnt'], h2t)


def _moe_kernel(be_ref, nv_ref, x_ref, wu_ref, bu_ref, wd_ref, bd_ref, o_ref, wu_b, wd_b):
    i = pl.program_id(0)
    n_valid = nv_ref[0]

    @pl.when(i < n_valid)
    def _():
        first = jnp.logical_or(i == 0, be_ref[i] != be_ref[jnp.maximum(i - 1, 0)])

        @pl.when(first)
        def _():
            rows = 64

            def cast(r, carry):
                rs = pl.ds(pl.multiple_of(r * rows, rows), rows)
                wu_b[rs, :] = wu_ref[0, 0, rs, :].astype(BF16)
                wd_b[rs, :] = wd_ref[0, 0, rs, :].astype(BF16)
                return carry
            lax.fori_loop(0, D_MODEL // rows, cast, 0)

        x = _load_row_tiles(x_ref, (), MOE_BLOCK).astype(BF16)
        acc = jnp.zeros((MOE_BLOCK, D_MODEL), F32) + bd_ref[0, 0]
        cw = 512
        for jc in range(D_EXPERT // cw):
            cg = slice(jc * cw, (jc + 1) * cw)
            cl = slice(D_EXPERT + jc * cw, D_EXPERT + (jc + 1) * cw)
            ug = jnp.dot(x, wu_b[:, cg], preferred_element_type=F32) + bu_ref[0, 0, :, cg]
            ul = jnp.dot(x, wu_b[:, cl], preferred_element_type=F32) + bu_ref[0, 0, :, cl]
            xg = jnp.minimum(ug, SWIGLU_LIMIT)
            xl = jnp.clip(ul, -SWIGLU_LIMIT, SWIGLU_LIMIT)
            act = xg * jax.nn.sigmoid(SWIGLU_ALPHA * xg) * (xl + 1.0)
            acc = acc + jnp.dot(act.astype(BF16), wd_b[cg, :], preferred_element_type=F32)
        _store_row_tiles(o_ref, (), acc)

    @pl.when(i >= n_valid)
    def _():
        o_ref[...] = jnp.zeros_like(o_ref)


def _moe_experts(layer, xs, block_expert, n_valid, w_up, b_up, w_down, b_down):
    n_blocks = block_expert.shape[0]
    depth = w_up.shape[0]
    grid_spec = pltpu.PrefetchScalarGridSpec(
        num_scalar_prefetch=2,
        grid=(n_blocks,),
        in_specs=[
            pl.BlockSpec((MOE_BLOCK * ROW_SUB, LANES),
                         lambda i, be, nv: (jnp.minimum(i, jnp.maximum(nv[0] - 1, 0)), 0)),
            pl.BlockSpec((1, 1, D_MODEL, 2 * D_EXPERT), lambda i, be, nv: (layer, be[i], 0, 0)),
            pl.BlockSpec((1, 1, 1, 2 * D_EXPERT), lambda i, be, nv: (layer, be[i], 0, 0)),
            pl.BlockSpec((1, 1, D_EXPERT, D_MODEL), lambda i, be, nv: (layer, be[i], 0, 0)),
            pl.BlockSpec((1, 1, 1, D_MODEL), lambda i, be, nv: (layer, be[i], 0, 0)),
        ],
        out_specs=pl.BlockSpec((MOE_BLOCK * ROW_SUB, LANES), lambda i, be, nv: (i, 0)),
        scratch_shapes=[
            pltpu.VMEM((D_MODEL, 2 * D_EXPERT), BF16),
            pltpu.VMEM((D_EXPERT, D_MODEL), BF16),
        ],
    )
    return pl.pallas_call(
        _moe_kernel,
        grid_spec=grid_spec,
        out_shape=jax.ShapeDtypeStruct((n_blocks * MOE_BLOCK * ROW_SUB, LANES), F32),
        compiler_params=pltpu.CompilerParams(
            dimension_semantics=("arbitrary",), vmem_limit_bytes=VMEM_LIMIT),
    )(block_expert, n_valid, xs, w_up, b_up.reshape(depth, N_EXPERTS, 1, -1), w_down,
      b_down.reshape(depth, N_EXPERTS, 1, -1))


def _combine_kernel(alpha, start_ref, pos_ref, gate_ref, tbase_ref, tcnt_ref, tbase_n_ref, tcnt_n_ref, y_ref, x_ref,
                    mod_ref, lng_ref, lnb_ref, o_ref, stage, frow, sem):
    i = pl.program_id(0)
    nt = pl.num_programs(0)
    tm = ROW_TILE
    slot = i % 2

    def fetch(cnt_ref, base_ref, s):
        def copy(stage_row, slot_row, n, queue):
            pltpu.make_async_copy(_rows(y_ref, (), slot_row, n), _rows(stage, (s,), stage_row, n),
                                  sem.at[s]).start(priority=queue)
        _expert_runs(cnt_ref, base_ref, start_ref, copy)

    @pl.when(i == 0)
    def _():
        fetch(tcnt_ref, tbase_ref, 0)

    @pl.when(i + 1 < nt)
    def _():
        fetch(tcnt_n_ref, tbase_n_ref, 1 - slot)

    pltpu.make_async_copy(stage.at[slot], stage.at[slot], sem.at[slot]).wait()

    def mix(r, carry):
        acc = None
        for kk in range(TOP_K):
            p = pos_ref[0, 0, r * TOP_K + kk]
            term = gate_ref[0, 0, r * TOP_K + kk] * stage[slot, pl.ds(pl.multiple_of(p * ROW_SUB, ROW_SUB), ROW_SUB), :]
            acc = term if acc is None else acc + term
        frow[pl.ds(pl.multiple_of(r * ROW_SUB, ROW_SUB), ROW_SUB), :] = acc
        return carry
    lax.fori_loop(0, tm, mix, 0, unroll=8)

    m = mod_ref[0]
    f = _load_row_tiles(frow, (), tm)
    o_ref[...] = _layer_norm(alpha * x_ref[...] + m[5:6] * f, lng_ref[...], lnb_ref[...])


def _moe_combine(alpha, route, ys, x1, mod_l, ln_g, ln_b, n_ctx_rows, seq, drop_ctx):
    t = x1.shape[0]
    tm = ROW_TILE
    nt = t // tm
    skip = n_ctx_rows // tm if drop_ctx else 0
    cur = lambda width: pl.BlockSpec((1, 1, width), lambda i, *_: (i, 0, 0), memory_space=pltpu.SMEM)
    nxt = lambda width: pl.BlockSpec((1, 1, width), lambda i, *_: (jnp.minimum(i + 1, nt - 1), 0, 0),
                                     memory_space=pltpu.SMEM)
    grid_spec = pltpu.PrefetchScalarGridSpec(
        num_scalar_prefetch=1,
        grid=(nt,),
        in_specs=[
            cur(tm * TOP_K), cur(tm * TOP_K), cur(N_EXPERTS), cur(N_EXPERTS), nxt(N_EXPERTS), nxt(N_EXPERTS),
            pl.BlockSpec(memory_space=pl.ANY),
            pl.BlockSpec((tm, D_MODEL), lambda i, *_: (i, 0)),
            pl.BlockSpec((1, 6, D_MODEL), lambda i, *_: (_segment(i * tm, n_ctx_rows, seq), 0, 0)),
            pl.BlockSpec((1, D_MODEL), lambda i, *_: (0, 0)),
            pl.BlockSpec((1, D_MODEL), lambda i, *_: (0, 0)),
        ],
        out_specs=pl.BlockSpec((tm, D_MODEL), lambda i, *_: (jnp.maximum(i - skip, 0), 0)),
        scratch_shapes=[
            pltpu.VMEM((2, tm * TOP_K * ROW_SUB, LANES), F32),
            pltpu.VMEM((tm * ROW_SUB, LANES), F32),
            pltpu.SemaphoreType.DMA((2,)),
        ],
    )
    return pl.pallas_call(
        functools.partial(_combine_kernel, alpha),
        grid_spec=grid_spec,
        out_shape=jax.ShapeDtypeStruct((t - skip * tm, D_MODEL), F32),
        compiler_params=pltpu.CompilerParams(
            dimension_semantics=("arbitrary",), vmem_limit_bytes=VMEM_LIMIT),
    )(route['start'], route['pos'], route['gates'], route['tile_base'], route['tile_cnt'], route['tile_base'],
      route['tile_cnt'], ys, x1, mod_l, ln_g.reshape(1, -1), ln_b.reshape(1, -1))


def kernel(x, c, ctx, c_ctx, w_mod, b_mod, w_in, sgu_ln_g, sgu_ln_b, sgu_w, sgu_b, gla_gate_up, gla_gate_b,
           gla_norm_g, s5_lam_re, s5_lam_im, s5_log_dt, s5_b_re, s5_b_im, s5_c_re, s5_c_im, s5_d, s5_glu_w,
           s5_glu_b, w_out, ln_g, ln_b, router_w, router_b, w_up, b_up, w_down, b_down):
    bsz, seq, d = x.shape
    ctx_len = ctx.shape[1]
    depth = w_in.shape[0]
    alpha = float((2 * depth) ** 0.25)
    n_ctx_rows = bsz * ctx_len
    t = n_ctx_rows + bsz * seq
    assert d == D_MODEL and bsz + 1 <= 8
    assert n_ctx_rows % ROW_TILE == 0 and seq % ROW_TILE == 0
    assert ctx_len % SEQ_TILE == 0 and seq % SEQ_TILE == 0

    xa = jnp.concatenate([ctx.reshape(n_ctx_rows, d), x.reshape(bsz * seq, d)], axis=0)
    cvec = jnp.zeros((8, d), F32).at[0].set(c_ctx).at[1:1 + bsz].set(c)
    mod = _modulation(cvec, w_mod, b_mod).reshape(depth, 8, 6, d)

    o = np.cumsum((0, A_WIDTH, A_WIDTH, B_QK, B_QK, B_WIDTH, B_WIDTH, 2 * GATE_RANK, C_WIDTH))
    w_in_r = jnp.concatenate(
        [w_in[:, :, o[0]:o[6]], w_in[:, :, o[7]:o[8]], w_in[:, :, o[6]:o[7]],
         jnp.zeros((depth, d, N_IN_PAD - int(o[8])), w_in.dtype)], axis=-1).astype(BF16)
    w_out_b = w_out.astype(BF16)

    mats = _s5_matrices(s5_lam_re, s5_lam_im, s5_log_dt, s5_b_re, s5_b_im, s5_c_re, s5_c_im)
    rw = jnp.zeros((depth, D_MODEL, LANES), F32).at[:, :, :N_EXPERTS].set(router_w)
    rw_hi = rw.astype(BF16)
    rw_lo = (rw - rw_hi.astype(F32)).astype(BF16)
    rb = jnp.zeros((depth, 1, LANES), F32).at[:, 0, :N_EXPERTS].set(router_b)

    n_assign = t * TOP_K
    n_blocks = -(-(n_assign + N_EXPERTS * (MOE_BLOCK - 1)) // MOE_BLOCK)

    for l in range(depth):
        mod_l = mod[l]
        proj, u5 = _in_projection(xa, mod_l, w_in_r[l], n_ctx_rows, seq)
        a_out = _spatial_gate(proj, sgu_ln_g[l], sgu_ln_b[l], sgu_w[l], sgu_b[l])
        o_fwd, o_bwd = _gla_sweep(proj, gla_gate_up[l], gla_gate_b[l], bsz, ctx_len, seq)
        y_s5 = _s5_scan(u5, mats, l, bsz, ctx_len, seq)
        x1, h2, logits = _out_projection(alpha, a_out, o_fwd, o_bwd, gla_norm_g[l], y_s5, proj, xa, mod_l, s5_d[l],
                                         s5_glu_w[l], s5_glu_b[l], w_out_b[l], ln_g[l, 0], ln_b[l, 0], rw_hi[l],
                                         rw_lo[l], rb[l], n_ctx_rows, seq)
        route, block_expert, n_valid = _routing(logits, n_blocks)
        xs = _moe_dispatch(h2, route, n_blocks)
        ys = _moe_experts(l, xs, block_expert, n_valid, w_up, b_up, w_down, b_down)
        xa = _moe_combine(alpha, route, ys, x1, mod_l, ln_g[l, 1], ln_b[l, 1], n_ctx_rows, seq,
                          drop_ctx=(l == depth - 1))
    return xa.reshape(bsz, seq, d)
```
